```python
import math, functools
import jax, jax.numpy as jnp
from jax import lax
import numpy as np

D_MODEL = 1024
BATCH = 8
SEQ = 2048
DEPTH = 4
DEC_BATCH = 32
DEC_SEQ = 32
PAST_LEN = 2048

CHUNK = 64
HEAD_DIM = 64
A_WIDTH = 256
A_GROUPS = 4
A_GROUP_DIM = A_WIDTH // A_GROUPS
A_CHUNK = 128
B_HEADS = 8
B_KV_HEADS = 2
B_GROUP = B_HEADS // B_KV_HEADS
B_WIDTH = B_HEADS * HEAD_DIM
B_KV_WIDTH = B_KV_HEADS * HEAD_DIM
B_WINDOW = 128
B_PREV = B_WINDOW // CHUNK
C_HEADS = 4
C_WIDTH = C_HEADS * HEAD_DIM
C_PREV = 8
C_REACH = C_PREV * CHUNK
C_REL_CLIP = 128
T5_BUCKETS = 32
T5_MAX_DIST = 128

MIX_WIDTH = A_WIDTH + B_WIDTH + C_WIDTH
PROJ_SIZES = [A_WIDTH, A_WIDTH, A_WIDTH,
              B_WIDTH, B_KV_WIDTH, B_KV_WIDTH, B_WIDTH,
              C_WIDTH, C_WIDTH, C_WIDTH, C_WIDTH]
PROJ_SPLITS = [int(s) for s in np.cumsum(PROJ_SIZES)[:-1]]
IN_WIDTH = int(sum(PROJ_SIZES))
RMS_EPS = 1e-6
NEG_INF = -1e30

kernel_name = "hybrid_streaming_encoder_step"


def rmsnorm(x, g):
    x32 = x.astype(jnp.float32)
    y = x32 * lax.rsqrt(jnp.mean(x32 * x32, axis=-1, keepdims=True) + RMS_EPS)
    return (y * g.astype(jnp.float32)).astype(x.dtype)


def t5_bucket(rel):
    half = T5_BUCKETS // 2
    max_exact = half // 2
    ret = jnp.where(rel > 0, half, 0)
    n = jnp.abs(rel)
    nf = jnp.maximum(n, 1).astype(jnp.float32)
    large = max_exact + (jnp.log(nf / max_exact) / math.log(T5_MAX_DIST / max_exact)
                         * (half - max_exact)).astype(jnp.int32)
    large = jnp.minimum(large, half - 1)
    return ret + jnp.where(n < max_exact, n, large)


def t5_bias_block(table, rel):
    b = table[t5_bucket(rel)].astype(jnp.float32)
    tq, l = rel.shape
    return jnp.transpose(b, (2, 0, 1)).reshape(B_KV_HEADS, B_GROUP, tq, l)


def c_bias_block(table, rel):
    idx = jnp.clip(rel, -C_REL_CLIP, C_REL_CLIP) + C_REL_CLIP
    b = table[idx].astype(jnp.float32)
    return jnp.transpose(b, (2, 0, 1))[:, None]


def band_rel(tq, length, n_past):
    return jnp.arange(length)[None, :] - n_past - jnp.arange(tq)[:, None]


def band_gather(x, n_prev):
    s = x.shape[1]
    n = s // CHUNK
    xp = jnp.pad(x, ((0, 0), (n_prev * CHUNK, 0), (0, 0), (0, 0)))
    idx = jnp.arange(n)[:, None] * CHUNK + jnp.arange((n_prev + 1) * CHUNK)[None, :]
    return xp[:, idx]


def band_valid(n, n_prev):
    length = (n_prev + 1) * CHUNK
    return (jnp.arange(n)[:, None] * CHUNK - n_prev * CHUNK + jnp.arange(length)[None, :]) >= 0


def block_attention(q, k, v, bias, valid, sinks):
    s = jnp.einsum('bnqhgd,bnlhd->bnhgql', q, k).astype(jnp.float32) * (HEAD_DIM ** -0.5) + bias
    if valid is not None:
        s = jnp.where(valid[None, :, None, None, None, :], s, NEG_INF)
    if sinks is None:
        p = jax.nn.softmax(s, axis=-1)
    else:
        sink = sinks.astype(jnp.float32).reshape(bias.shape[0], bias.shape[1], 1, 1)
        m = jnp.maximum(jnp.max(s, axis=-1, keepdims=True), sink)
        e = jnp.exp(s - m)
        p = e / (jnp.sum(e, axis=-1, keepdims=True) + jnp.exp(sink - m))
    return jnp.einsum('bnhgql,bnlhd->bnqhgd', p.astype(v.dtype), v)


def band_attention_prompt(q, k, v, n_prev, kv_heads, bias_fn, sinks, keep):
    bn, s, _ = q.shape
    n = s // CHUNK
    group = q.shape[-1] // (kv_heads * HEAD_DIM)
    qb = q.reshape(bn, n, CHUNK, kv_heads, group, HEAD_DIM)
    kh = k.reshape(bn, s, kv_heads, HEAD_DIM)
    vh = v.reshape(bn, s, kv_heads, HEAD_DIM)
    length = (n_prev + 1) * CHUNK
    bias = bias_fn(band_rel(CHUNK, length, n_prev * CHUNK))
    o = block_attention(qb, band_gather(kh, n_prev), band_gather(vh, n_prev),
                        bias, band_valid(n, n_prev), sinks)
    keep = min(keep, s)
    return o.reshape(bn, s, -1), kh[:, s - keep:], vh[:, s - keep:]


def band_attention_sample(q, k, v, cache_k, cache_v, kv_heads, bias_fn, sinks):
    bn, t, _ = q.shape
    group = q.shape[-1] // (kv_heads * HEAD_DIM)
    qb = q.reshape(bn, 1, t, kv_heads, group, HEAD_DIM)
    kn = k.reshape(bn, t, kv_heads, HEAD_DIM)
    vn = v.reshape(bn, t, kv_heads, HEAD_DIM)
    kk = jnp.concatenate([cache_k.astype(kn.dtype), kn], axis=1)[:, None]
    vv = jnp.concatenate([cache_v.astype(vn.dtype), vn], axis=1)[:, None]
    lc = cache_k.shape[1]
    bias = bias_fn(band_rel(t, lc + t, lc))
    o = block_attention(qb, kk, vv, bias, None, sinks)
    return o.reshape(bn, t, -1), kn, vn


def mixer_a(au, av, az, g_v, ws, bs, chunk_len):
    bn, s, _ = au.shape
    n = s // chunk_len
    vn = rmsnorm(av, g_v)
    shp = (bn, n, chunk_len, A_GROUPS, A_GROUP_DIM)
    w = jnp.tril(ws[:, :chunk_len, :chunk_len])
    mix = (jnp.einsum('gts,bnsgd->bntgd', w, vn.reshape(shp))
           + jnp.transpose(bs[:, :chunk_len])[None, None, :, :, None])
    y = (au.reshape(shp) * mix).reshape(bn, s, A_WIDTH)
    return y * jax.nn.silu(az), vn


def mixer_inputs(x, g_pre, w_in):
    h = rmsnorm(x, g_pre)
    return jnp.split(h @ w_in, PROJ_SPLITS, axis=-1)


def mixer_output(x, ya, yb, yc, w_out, g_post):
    y = jnp.concatenate([ya, yb, yc], axis=-1) @ w_out
    return x + rmsnorm(y, g_post)


def setup_inputs(seed: int = 0) -> dict:
    key = jax.random.key(seed)
    ks = jax.random.split(key, 18)
    b_keep = min(B_WINDOW, PAST_LEN)
    c_keep = min(C_REACH, PAST_LEN)
    nrm = jax.random.normal
    f32 = jnp.float32
    return {
        "x_prompt": nrm(ks[0], (BATCH, SEQ, D_MODEL), f32),
        "x_sample": nrm(ks[1], (DEC_BATCH, DEC_SEQ, D_MODEL), f32),
        "cache_b_k": nrm(ks[2], (DEPTH, DEC_BATCH, b_keep, B_KV_HEADS, HEAD_DIM), f32),
        "cache_b_v": nrm(ks[3], (DEPTH, DEC_BATCH, b_keep, B_KV_HEADS, HEAD_DIM), f32),
        "cache_c_k": nrm(ks[4], (DEPTH, DEC_BATCH, c_keep, C_HEADS, HEAD_DIM), f32),
        "cache_c_v": nrm(ks[5], (DEPTH, DEC_BATCH, c_keep, C_HEADS, HEAD_DIM), f32),
        "g_pre": 1.0 + 0.05 * nrm(ks[6], (DEPTH, D_MODEL), f32),
        "g_post": 1.0 + 0.05 * nrm(ks[7], (DEPTH, D_MODEL), f32),
        "w_in": nrm(ks[8], (DEPTH, D_MODEL, IN_WIDTH), f32) * D_MODEL ** -0.5,
        "w_out": nrm(ks[9], (DEPTH, MIX_WIDTH, D_MODEL), f32) * MIX_WIDTH ** -0.5,
        "a_norm_g": 1.0 + 0.05 * nrm(ks[10], (DEPTH, A_WIDTH), f32),
        "a_ws": nrm(ks[11], (DEPTH, A_GROUPS, A_CHUNK, A_CHUNK), f32) * A_CHUNK ** -0.5,
        "a_bs": 1.0 + 0.05 * nrm(ks[12], (DEPTH, A_GROUPS, A_CHUNK), f32),
        "b_sinks": nrm(ks[13], (DEPTH, B_HEADS), f32),
        "c_rel_bias": 0.1 * nrm(ks[14], (DEPTH, 2 * C_REL_CLIP + 1, C_HEADS), f32),
        "t5_bias": 0.1 * nrm(ks[15], (T5_BUCKETS, B_HEADS), f32),
    }


def reference(x_prompt, x_sample, cache_b_k, cache_b_v, cache_c_k, cache_c_v,
              g_pre, g_post, w_in, w_out, a_norm_g, a_ws, a_bs, b_sinks,
              c_rel_bias, t5_bias):
    t5_fn = functools.partial(t5_bias_block, t5_bias)
    xp, xs = x_prompt, x_sample
    bkp, bvp, ckp, cvp = [], [], [], []
    bks, bvs, cks, cvs, avs = [], [], [], [], []
    for l in range(DEPTH):
        c_fn = functools.partial(c_bias_block, c_rel_bias[l])
        au, av, az, bq, bk, bv, bz, cq, ck, cv, cz = mixer_inputs(xp, g_pre[l], w_in[l])
        ya, _ = mixer_a(au, av, az, a_norm_g[l], a_ws[l], a_bs[l], A_CHUNK)
        yb, kb, vb = band_attention_prompt(bq, bk, bv, B_PREV, B_KV_HEADS, t5_fn, b_sinks[l], B_WINDOW)
        yc, kc, vc = band_attention_prompt(cq, ck, cv, C_PREV, C_HEADS, c_fn, None, C_REACH)
        xp = mixer_output(xp, ya, yb * jax.nn.silu(bz), yc * jax.nn.silu(cz), w_out[l], g_post[l])
        bkp.append(kb); bvp.append(vb); ckp.append(kc); cvp.append(vc)
        au, av, az, bq, bk, bv, bz, cq, ck, cv, cz = mixer_inputs(xs, g_pre[l], w_in[l])
        ya, va = mixer_a(au, av, az, a_norm_g[l], a_ws[l], a_bs[l], xs.shape[1])
        yb, kb, vb = band_attention_sample(bq, bk, bv, cache_b_k[l], cache_b_v[l], B_KV_HEADS, t5_fn, b_sinks[l])
        yc, kc, vc = band_attention_sample(cq, ck, cv, cache_c_k[l], cache_c_v[l], C_HEADS, c_fn, None)
        xs = mixer_output(xs, ya, yb * jax.nn.silu(bz), yc * jax.nn.silu(cz), w_out[l], g_post[l])
        bks.append(kb); bvs.append(vb); cks.append(kc); cvs.append(vc); avs.append(va)
    return (xp, xs,
            jnp.stack(bkp), jnp.stack(bvp), jnp.stack(ckp), jnp.stack(cvp),
            jnp.stack(bks), jnp.stack(bvs), jnp.stack(cks), jnp.stack(cvs), jnp.stack(avs))
```

```python
import functools
import math

import numpy as np
import jax
import jax.numpy as jnp
from jax import lax
from jax.experimental import pallas as pl
from jax.experimental.pallas import tpu as pltpu

bf16 = jnp.bfloat16
f32 = jnp.float32

HEAD_DIM = 64
CHUNK = 64
A_WIDTH, A_GROUPS, A_CHUNK = 256, 4, 128
A_GD = A_WIDTH // A_GROUPS
B_HEADS, B_KV = 8, 2
B_GROUP = B_HEADS // B_KV
B_WIDTH, B_KVW = B_HEADS * HEAD_DIM, B_KV * HEAD_DIM
B_PREV = 2
C_HEADS = 4
C_WIDTH = C_HEADS * HEAD_DIM
C_PREV = 8
C_CLIP = 128
T5_BUCKETS, T5_MAX_DIST = 32, 128
RMS_EPS = 1e-6
NEG_INF = -1e30
QK_SCALE = HEAD_DIM ** -0.5

LANE = 128
PAIR = 2 * CHUNK
B_WIN = (B_PREV + 2) * CHUNK
C_WIN = (C_PREV + 2) * CHUNK
B_KEEP = B_PREV * CHUNK
C_KEEP = C_PREV * CHUNK
SEQ_TILE = 512
SAMPLE_STREAMS_PER_BLOCK = 4
VMEM_LIMIT = 56 * 1024 * 1024

_IN_SIZES = [A_WIDTH, A_WIDTH, A_WIDTH, B_WIDTH, B_KVW, B_KVW, B_WIDTH, C_WIDTH, C_WIDTH, C_WIDTH, C_WIDTH]
_IN_OFF = [int(v) for v in np.cumsum([0] + _IN_SIZES)]
(I_AU, I_AV, I_AZ, I_BQ, I_BK, I_BV, I_BZ, I_CQ, I_CK, I_CV, I_CZ, I_END) = _IN_OFF
T_AU, T_AV, T_AZ, T_BQ = 0, 256, 512, 768
T_BV, T_BZ, T_CQ = 1280, 1408, 1920
T_CV, T_CZ, T_END = 2176, 2432, 2688
N_BK, N_CK, N_BV, N_CV, N_END = 0, 128, 384, 512, 768
Y_A, Y_B, Y_C = 0, A_WIDTH, A_WIDTH + B_WIDTH

_NT = (((1,), (1,)), ((), ()))
_TN = (((0,), (0,)), ((), ()))


def _t5_bucket(rel):
    half = T5_BUCKETS // 2
    max_exact = half // 2
    ret = jnp.where(rel > 0, half, 0)
    n = jnp.abs(rel)
    nf = jnp.maximum(n, 1).astype(jnp.float32)
    large = max_exact + (jnp.log(nf / max_exact) / math.log(T5_MAX_DIST / max_exact)
                         * (half - max_exact)).astype(jnp.int32)
    large = jnp.minimum(large, half - 1)
    return ret + jnp.where(n < max_exact, n, large)


def _lut_kernel(tab_ref, idx_ref, out_ref, *, n_entries, n_heads, ranges, kmins):
    layer = pl.program_id(0)
    neg = jnp.full((CHUNK, LANE), NEG_INF, f32)
    for rb, (lo, hi) in enumerate(ranges):
        rows = slice(rb * CHUNK, (rb + 1) * CHUNK)
        idx = idx_ref[rows, :]
        for h in range(n_heads):
            if hi < lo:
                base = neg
            else:
                def body(e, acc, h=h, idx=idx):
                    return jnp.where(idx == e, tab_ref[(layer * n_entries + e) * n_heads + h], acc)
                base = lax.fori_loop(lo, hi + 1, body, neg)
            for v, kmin in enumerate(kmins):
                out_ref[0, v, rows, h * LANE:(h + 1) * LANE] = neg if rb < kmin else base


def _build_bias(table, idx, *, ranges, kmins):
    n_layers, n_entries, n_heads = table.shape
    n_rows = idx.shape[0]
    kern = functools.partial(_lut_kernel, n_entries=n_entries, n_heads=n_heads,
                             ranges=tuple(ranges), kmins=tuple(kmins))
    return pl.pallas_call(
        kern,
        grid=(n_layers,),
        in_specs=[pl.BlockSpec(memory_space=pltpu.SMEM),
                  pl.BlockSpec((n_rows, LANE), lambda l: (0, 0))],
        out_specs=pl.BlockSpec((1, len(kmins), n_rows, n_heads * LANE), lambda l: (l, 0, 0, 0)),
        out_shape=jax.ShapeDtypeStruct((n_layers, len(kmins), n_rows, n_heads * LANE), f32),
        compiler_params=pltpu.CompilerParams(dimension_semantics=("arbitrary",),
                                             vmem_limit_bytes=VMEM_LIMIT),
        name="bias_lut",
    )(table.reshape(-1), idx)


def _static_ranges(idx_np):
    out = []
    for rb in range(idx_np.shape[0] // CHUNK):
        blk = idx_np[rb * CHUNK:(rb + 1) * CHUNK]
        good = blk[blk >= 0]
        out.append((int(good.min()), int(good.max())) if good.size else (0, -1))
    return out


def _prompt_maps():
    i = np.arange(PAIR)[None, :]
    qi = i // CHUNK
    jb = np.arange(B_WIN)[:, None]
    rel_b = jb - B_KEEP - i
    ok_b = (jb // CHUNK >= qi) & (jb // CHUNK <= qi + B_PREV)
    jc = np.arange(C_WIN)[:, None]
    rel_c = jc - C_KEEP - i
    ok_c = (jc // CHUNK >= qi) & (jc // CHUNK <= qi + C_PREV)
    idx_c = np.where(ok_c, np.clip(rel_c, -C_CLIP, C_CLIP) + C_CLIP, -1).astype(np.int32)
    return rel_b.astype(np.int32), ok_b, idx_c


def _sample_maps(lc_b, lc_c, t_new):
    ns = SAMPLE_STREAMS_PER_BLOCK
    c = np.arange(LANE)[None, :]
    sq, i = c // t_new, c % t_new

    def one(lc):
        r = np.arange(ns * lc + LANE)[:, None]
        is_new = r >= ns * lc
        sk = np.where(is_new, (r - ns * lc) // t_new, r // lc)
        j = np.where(is_new, lc + (r - ns * lc) % t_new, r % lc)
        return (j - lc - i).astype(np.int32), (sk == sq)

    rel_b, ok_b = one(lc_b)
    rel_c, ok_c = one(lc_c)
    idx_c = np.where(ok_c, np.clip(rel_c, -C_CLIP, C_CLIP) + C_CLIP, -1).astype(np.int32)
    return rel_b, ok_b, idx_c


def _rms_rows(x, g):
    ms = jnp.mean(x * x, axis=-1, keepdims=True)
    return x * lax.rsqrt(ms + RMS_EPS) * g


def _rms_cols(xT, g_col):
    ms = jnp.mean(xT * xT, axis=0, keepdims=True)
    return xT * lax.rsqrt(ms + RMS_EPS) * g_col


def _qblockdiag(qa, qb, lower):
    z = jnp.zeros((HEAD_DIM, LANE), bf16)
    if lower is None:
        return jnp.concatenate([jnp.concatenate([qa, z], axis=1),
                                jnp.concatenate([z, qb], axis=1)], axis=0)
    top = jnp.concatenate([qa, qb], axis=1)
    zz = jnp.zeros((HEAD_DIM, 2 * LANE), bf16)
    return jnp.concatenate([top, zz] if lower == 0 else [zz, top], axis=0)


def _softmax_pv(s, sink, vwin):
    m = jnp.max(s, axis=0, keepdims=True)
    if sink is not None:
        m = jnp.maximum(m, sink)
    e = jnp.exp(s - m)
    den = jnp.sum(e, axis=0, keepdims=True)
    if sink is not None:
        den = den + jnp.exp(sink - m)
    o = jnp.dot(vwin, e.astype(bf16), preferred_element_type=f32)
    return o / den


def _mixers_bc(pt_sc, yT_sc, cols, kwin_b, vwinT_b, bias_b, sink_ref, kwin_c, vwinT_c, bias_c):
    for cb in range(B_HEADS // 2):
        ha, hb = 2 * cb, 2 * cb + 1
        kv = ha // B_GROUP
        qa = (pt_sc[T_BQ + ha * HEAD_DIM:T_BQ + (ha + 1) * HEAD_DIM, cols] * QK_SCALE).astype(bf16)
        qb = (pt_sc[T_BQ + hb * HEAD_DIM:T_BQ + (hb + 1) * HEAD_DIM, cols] * QK_SCALE).astype(bf16)
        s = jnp.dot(kwin_b, _qblockdiag(qa, qb, kv), preferred_element_type=f32)
        s = s + bias_b[:, cb * 2 * LANE:(cb + 1) * 2 * LANE]
        sink = sink_ref[:, cb * 2 * LANE:(cb + 1) * 2 * LANE]
        o = _softmax_pv(s, sink, vwinT_b[kv * HEAD_DIM:(kv + 1) * HEAD_DIM, :])
        for j, hh in enumerate((ha, hb)):
            gate = jax.nn.silu(pt_sc[T_BZ + hh * HEAD_DIM:T_BZ + (hh + 1) * HEAD_DIM, cols])
            yT_sc[Y_B + hh * HEAD_DIM:Y_B + (hh + 1) * HEAD_DIM, cols] = (
                o[:, j * LANE:(j + 1) * LANE] * gate).astype(bf16)
    for cb in range(C_HEADS // 2):
        ha, hb = 2 * cb, 2 * cb + 1
        qa = (pt_sc[T_CQ + ha * HEAD_DIM:T_CQ + (ha + 1) * HEAD_DIM, cols] * QK_SCALE).astype(bf16)
        qb = (pt_sc[T_CQ + hb * HEAD_DIM:T_CQ + (hb + 1) * HEAD_DIM, cols] * QK_SCALE).astype(bf16)
        s = jnp.dot(kwin_c[:, cb * LANE:(cb + 1) * LANE], _qblockdiag(qa, qb, None),
                    preferred_element_type=f32)
        s = s + bias_c[:, cb * 2 * LANE:(cb + 1) * 2 * LANE]
        o = _softmax_pv(s, None, vwinT_c[cb * LANE:(cb + 1) * LANE, :])
        for j, hh in enumerate((ha, hb)):
            gate = jax.nn.silu(pt_sc[T_CZ + hh * HEAD_DIM:T_CZ + (hh + 1) * HEAD_DIM, cols])
            yT_sc[Y_C + hh * HEAD_DIM:Y_C + (hh + 1) * HEAD_DIM, cols] = (
                o[j * HEAD_DIM:(j + 1) * HEAD_DIM, j * LANE:(j + 1) * LANE] * gate).astype(bf16)


def _finish(x, yT_sc, wout_ref, gpost_ref):
    out = lax.dot_general(yT_sc[...], wout_ref[...], _TN, preferred_element_type=f32)
    return x + _rms_rows(out, gpost_ref[...])


def _prompt_kernel(x_ref, gpre_ref, gpost_ref, wT_ref, wnat_ref, wout_ref, ang_ref, awsT_ref, abs_ref,
                   sink_ref, biasb_ref, biasc_ref,
                   xo_ref, bk_ref, bv_ref, ck_ref, cv_ref,
                   pt_sc, kb_ring, kc_ring, vbT_ring, vcT_ring, yT_sc):
    t = pl.program_id(1)
    ts = SEQ_TILE

    @pl.when(t == 0)
    def _():
        kb_ring[0:B_KEEP, :] = jnp.zeros((B_KEEP, B_KVW), bf16)
        kc_ring[0:C_KEEP, :] = jnp.zeros((C_KEEP, C_WIDTH), bf16)
        vbT_ring[:, 0:B_KEEP] = jnp.zeros((B_KVW, B_KEEP), bf16)
        vcT_ring[:, 0:C_KEEP] = jnp.zeros((C_WIDTH, C_KEEP), bf16)

    x = x_ref[0]
    h = _rms_rows(x, gpre_ref[...]).astype(bf16)
    pt_sc[...] = lax.dot_general(wT_ref[...], h, _NT, preferred_element_type=f32)
    kn = jnp.dot(h, wnat_ref[:, N_BK:N_BV], preferred_element_type=f32)
    kb_ring[B_KEEP:, :] = kn[:, N_BK:N_CK].astype(bf16)
    kc_ring[C_KEEP:, :] = kn[:, N_CK:N_BV].astype(bf16)
    vbT_ring[:, B_KEEP:] = pt_sc[T_BV:T_BV + B_KVW, :].astype(bf16)
    vcT_ring[:, C_KEEP:] = pt_sc[T_CV:T_CV + C_WIDTH, :].astype(bf16)

    @pl.when(t == pl.num_programs(1) - 1)
    def _():
        vnat = jnp.dot(h, wnat_ref[:, N_BV:N_END], preferred_element_type=f32)
        bk_ref[0] = kn[ts - B_KEEP:, N_BK:N_CK]
        ck_ref[0] = kn[ts - C_KEEP:, N_CK:N_BV]
        bv_ref[0] = vnat[ts - B_KEEP:, 0:B_KVW]
        cv_ref[0] = vnat[ts - C_KEEP:, B_KVW:]

    vn = _rms_cols(pt_sc[T_AV:T_AV + A_WIDTH, :], ang_ref[...]).astype(bf16)
    n_ac = ts // A_CHUNK
    row = lax.broadcasted_iota(jnp.int32, (A_CHUNK, A_CHUNK), 0)
    col = lax.broadcasted_iota(jnp.int32, (A_CHUNK, A_CHUNK), 1)
    for g in range(A_GROUPS):
        rows = slice(g * A_GD, (g + 1) * A_GD)
        wgt = jnp.where(row <= col, awsT_ref[g], 0.0).astype(bf16)
        lhs = jnp.concatenate([vn[rows, c * A_CHUNK:(c + 1) * A_CHUNK] for c in range(n_ac)], axis=0)
        mix = jnp.dot(lhs, wgt, preferred_element_type=f32) + abs_ref[g]
        for c in range(n_ac):
            cols = slice(c * A_CHUNK, (c + 1) * A_CHUNK)
            u = pt_sc[T_AU + g * A_GD:T_AU + (g + 1) * A_GD, cols]
            z = pt_sc[T_AZ + g * A_GD:T_AZ + (g + 1) * A_GD, cols]
            yT_sc[Y_A + g * A_GD:Y_A + (g + 1) * A_GD, cols] = (
                u * mix[c * A_GD:(c + 1) * A_GD, :] * jax.nn.silu(z)).astype(bf16)

    def pair_body(p, carry):
        pg = t * (ts // PAIR) + p
        start = pl.multiple_of(p * PAIR, PAIR)
        _mixers_bc(
            pt_sc, yT_sc, pl.ds(start, PAIR),
            kb_ring[pl.ds(start, B_WIN), :], vbT_ring[:, pl.ds(start, B_WIN)],
            biasb_ref.at[jnp.minimum(pg, 1)], sink_ref,
            kc_ring[pl.ds(start, C_WIN), :], vcT_ring[:, pl.ds(start, C_WIN)],
            biasc_ref.at[jnp.minimum(pg, C_PREV // 2)])
        return carry

    lax.fori_loop(0, ts // PAIR, pair_body, 0)

    xo_ref[0] = _finish(x, yT_sc, wout_ref, gpost_ref)

    kb_ring[0:B_KEEP, :] = kb_ring[ts:ts + B_KEEP, :]
    kc_ring[0:C_KEEP, :] = kc_ring[ts:ts + C_KEEP, :]
    vbT_ring[:, 0:B_KEEP] = vbT_ring[:, ts:ts + B_KEEP]
    vcT_ring[:, 0:C_KEEP] = vcT_ring[:, ts:ts + C_KEEP]


def _const_spec(shape):
    nd = len(shape)
    return pl.BlockSpec(shape, lambda *_: (0,) * nd, pipeline_mode=pl.Buffered(1))


def _prompt_layer(x, gpre, gpost, wT, wnat, wout, ang, awsT, abs_, sink, biasb, biasc):
    nb, seq, d = x.shape
    ts = SEQ_TILE
    assert seq % ts == 0 and ts == C_KEEP and d == wT.shape[1]
    outs = pl.pallas_call(
        _prompt_kernel,
        grid=(nb, seq // ts),
        in_specs=[
            pl.BlockSpec((1, ts, d), lambda b, t: (b, t, 0)),
            _const_spec(gpre.shape), _const_spec(gpost.shape),
            _const_spec(wT.shape), _const_spec(wnat.shape), _const_spec(wout.shape),
            _const_spec(ang.shape), _const_spec(awsT.shape), _const_spec(abs_.shape),
            _const_spec(sink.shape), _const_spec(biasb.shape), _const_spec(biasc.shape),
        ],
        out_specs=[
            pl.BlockSpec((1, ts, d), lambda b, t: (b, t, 0)),
            pl.BlockSpec((1, B_KEEP, B_KVW), lambda b, t: (b, 0, 0)),
            pl.BlockSpec((1, B_KEEP, B_KVW), lambda b, t: (b, 0, 0)),
            pl.BlockSpec((1, C_KEEP, C_WIDTH), lambda b, t: (b, 0, 0)),
            pl.BlockSpec((1, C_KEEP, C_WIDTH), lambda b, t: (b, 0, 0)),
        ],
        out_shape=[
            jax.ShapeDtypeStruct((nb, seq, d), f32),
            jax.ShapeDtypeStruct((nb, B_KEEP, B_KVW), f32),
            jax.ShapeDtypeStruct((nb, B_KEEP, B_KVW), f32),
            jax.ShapeDtypeStruct((nb, C_KEEP, C_WIDTH), f32),
            jax.ShapeDtypeStruct((nb, C_KEEP, C_WIDTH), f32),
        ],
        scratch_shapes=[
            pltpu.VMEM((T_END, ts), f32),
            pltpu.VMEM((B_KEEP + ts, B_KVW), bf16),
            pltpu.VMEM((C_KEEP + ts, C_WIDTH), bf16),
            pltpu.VMEM((B_KVW, B_KEEP + ts), bf16),
            pltpu.VMEM((C_WIDTH, C_KEEP + ts), bf16),
            pltpu.VMEM((A_WIDTH + B_WIDTH + C_WIDTH, ts), bf16),
        ],
        compiler_params=pltpu.CompilerParams(dimension_semantics=("arbitrary", "arbitrary"),
                                             vmem_limit_bytes=VMEM_LIMIT),
        name="prompt_layer",
    )(x, gpre, gpost, wT, wnat, wout, ang, awsT, abs_, sink, biasb, biasc)
    return outs


def _sample_kernel(x_ref, cbk_ref, cbv_ref, cck_ref, ccv_ref, gpre_ref, gpost_ref, wT_ref, wnat_ref,
                   wout_ref, ang_ref, abd_ref, akeep_ref, abias_ref, sink_ref, biasb_ref, biasc_ref,
                   xo_ref, bk_ref, bv_ref, ck_ref, cv_ref, av_ref,
                   pt_sc, yT_sc):
    ns = SAMPLE_STREAMS_PER_BLOCK
    x = x_ref[...]
    ntok = x.shape[0]
    h = _rms_rows(x, gpre_ref[...]).astype(bf16)
    pt_sc[...] = lax.dot_general(wT_ref[...], h, _NT, preferred_element_type=f32)
    kvn = jnp.dot(h, wnat_ref[...], preferred_element_type=f32)
    bk_ref[...] = kvn[:, N_BK:N_CK]
    ck_ref[...] = kvn[:, N_CK:N_BV]
    bv_ref[...] = kvn[:, N_BV:N_CV]
    cv_ref[...] = kvn[:, N_CV:N_END]

    vn = _rms_cols(pt_sc[T_AV:T_AV + A_WIDTH, :], ang_ref[...])
    av_ref[...] = vn.T
    vnb = vn.astype(bf16)
    keep = akeep_ref[...] > 0.0
    for g in range(A_GROUPS):
        rows = slice(g * A_GD, (g + 1) * A_GD)
        wgt = jnp.where(keep, abd_ref[g], 0.0).astype(bf16)
        mix = jnp.dot(vnb[rows, :], wgt, preferred_element_type=f32) + abias_ref[g]
        u = pt_sc[T_AU + g * A_GD:T_AU + (g + 1) * A_GD, :]
        z = pt_sc[T_AZ + g * A_GD:T_AZ + (g + 1) * A_GD, :]
        yT_sc[Y_A + g * A_GD:Y_A + (g + 1) * A_GD, :] = (u * mix * jax.nn.silu(z)).astype(bf16)

    lc_b, lc_c = cbk_ref.shape[1], cck_ref.shape[1]
    for blk in range(ntok // LANE):
        cols = slice(blk * LANE, (blk + 1) * LANE)
        ss = slice(blk * ns, (blk + 1) * ns)
        kwin_b = jnp.concatenate([cbk_ref[ss].reshape(ns * lc_b, B_KVW).astype(bf16),
                                  kvn[cols, N_BK:N_CK].astype(bf16)], axis=0)
        vwinT_b = jnp.concatenate([cbv_ref[ss].reshape(ns * lc_b, B_KVW).T.astype(bf16),
                                   pt_sc[T_BV:T_BV + B_KVW, cols].astype(bf16)], axis=1)
        kwin_c = jnp.concatenate([cck_ref[ss].reshape(ns * lc_c, C_WIDTH).astype(bf16),
                                  kvn[cols, N_CK:N_BV].astype(bf16)], axis=0)
        vwinT_c = jnp.concatenate([ccv_ref[ss].reshape(ns * lc_c, C_WIDTH).T.astype(bf16),
                                   pt_sc[T_CV:T_CV + C_WIDTH, cols].astype(bf16)], axis=1)
        _mixers_bc(pt_sc, yT_sc, cols, kwin_b, vwinT_b, biasb_ref, sink_ref,
                   kwin_c, vwinT_c, biasc_ref)

    xo_ref[...] = _finish(x, yT_sc, wout_ref, gpost_ref)


def _sample_layer(x, cbk, cbv, cck, ccv, gpre, gpost, wT, wnat, wout, ang, abd, akeep, abias, sink,
                  biasb, biasc, *, t_new, tok_per_step):
    ntok, d = x.shape
    spb = tok_per_step // t_new
    lc_b, lc_c = cbk.shape[1], cck.shape[1]

    def rows(width):
        return pl.BlockSpec((tok_per_step, width), lambda i: (i, 0))

    def cache(lc, width):
        return pl.BlockSpec((spb, lc, width), lambda i: (i, 0, 0))

    outs = pl.pallas_call(
        _sample_kernel,
        grid=(ntok // tok_per_step,),
        in_specs=[
            rows(d), cache(lc_b, B_KVW), cache(lc_b, B_KVW), cache(lc_c, C_WIDTH), cache(lc_c, C_WIDTH),
            _const_spec(gpre.shape), _const_spec(gpost.shape),
            _const_spec(wT.shape), _const_spec(wnat.shape), _const_spec(wout.shape),
            _const_spec(ang.shape), _const_spec(abd.shape), _const_spec(akeep.shape), _const_spec(abias.shape),
            _const_spec(sink.shape), _const_spec(biasb.shape), _const_spec(biasc.shape),
        ],
        out_specs=[rows(d), rows(B_KVW), rows(B_KVW), rows(C_WIDTH), rows(C_WIDTH), rows(A_WIDTH)],
        out_shape=[
            jax.ShapeDtypeStruct((ntok, d), f32),
            jax.ShapeDtypeStruct((ntok, B_KVW), f32), jax.ShapeDtypeStruct((ntok, B_KVW), f32),
            jax.ShapeDtypeStruct((ntok, C_WIDTH), f32), jax.ShapeDtypeStruct((ntok, C_WIDTH), f32),
            jax.ShapeDtypeStruct((ntok, A_WIDTH), f32),
        ],
        scratch_shapes=[
            pltpu.VMEM((T_END, tok_per_step), f32),
            pltpu.VMEM((A_WIDTH + B_WIDTH + C_WIDTH, tok_per_step), bf16),
        ],
        compiler_params=pltpu.CompilerParams(dimension_semantics=("arbitrary",),
                                             vmem_limit_bytes=VMEM_LIMIT),
        name="sample_layer",
    )(x, cbk, cbv, cck, ccv, gpre, gpost, wT, wnat, wout, ang, abd, akeep, abias, sink, biasb, biasc)
    return outs


def kernel(x_prompt, x_sample, cache_b_k, cache_b_v, cache_c_k, cache_c_v, g_pre, g_post, w_in, w_out,
           a_norm_g, a_ws, a_bs, b_sinks, c_rel_bias, t5_bias):
    depth = w_in.shape[0]
    nb, seq, d = x_prompt.shape
    ns_all, t_new, _ = x_sample.shape
    lc_b, lc_c = cache_b_k.shape[2], cache_c_k.shape[2]
    assert lc_b == B_KEEP and lc_c == C_KEEP and LANE % t_new == 0
    assert SAMPLE_STREAMS_PER_BLOCK * t_new == LANE

    rel_pb, ok_pb, idx_pc = _prompt_maps()
    rel_sb, ok_sb, idx_sc = _sample_maps(lc_b, lc_c, t_new)
    idx_pb = jnp.where(ok_pb, _t5_bucket(jnp.asarray(rel_pb)), -1).astype(jnp.int32)
    idx_sb = jnp.where(ok_sb, _t5_bucket(jnp.asarray(rel_sb)), -1).astype(jnp.int32)
    t5_tab = t5_bias[None]
    full_t5 = [(0, T5_BUCKETS - 1)]
    bias_pb = _build_bias(t5_tab, idx_pb, ranges=full_t5 * (B_WIN // CHUNK), kmins=(B_PREV, 0))[0]
    bias_sb = _build_bias(t5_tab, idx_sb, ranges=full_t5 * (idx_sb.shape[0] // CHUNK), kmins=(0,))[0, 0]
    bias_pc = _build_bias(c_rel_bias, jnp.asarray(idx_pc), ranges=_static_ranges(idx_pc),
                          kmins=tuple(max(C_PREV - 2 * v, 0) for v in range(C_PREV // 2 + 1)))
    bias_sc = _build_bias(c_rel_bias, jnp.asarray(idx_sc), ranges=_static_ranges(idx_sc), kmins=(0,))[:, 0]

    wT = jnp.concatenate([w_in[:, :, I_AU:I_BK], w_in[:, :, I_BV:I_CK], w_in[:, :, I_CV:I_END]],
                         axis=2).transpose(0, 2, 1).astype(bf16)
    wnat = jnp.concatenate([w_in[:, :, I_BK:I_BV], w_in[:, :, I_CK:I_CV],
                            w_in[:, :, I_BV:I_BZ], w_in[:, :, I_CV:I_CZ]], axis=2).astype(bf16)
    wout = w_out.astype(bf16)
    gpre = g_pre[:, None, :]
    gpost = g_post[:, None, :]
    ang = a_norm_g[:, :, None]
    awsT = a_ws.transpose(0, 1, 3, 2)
    abs_p = a_bs[:, :, None, :]
    sink = jnp.repeat(b_sinks, LANE, axis=1)[:, None, :]
    tok_per_step = 2 * LANE
    reps = tok_per_step // t_new
    abd = jnp.tile(awsT[:, :, :t_new, :t_new], (1, 1, reps, reps))
    abias_s = jnp.tile(a_bs[:, :, None, :t_new], (1, 1, 1, reps))
    r_ = np.arange(tok_per_step)
    akeep = jnp.asarray(((r_[:, None] // t_new == r_[None, :] // t_new)
                         & (r_[:, None] % t_new <= r_[None, :] % t_new)).astype(np.float32))

    xp = x_prompt
    xs = x_sample.reshape(ns_all * t_new, d)
    cbk = cache_b_k.reshape(depth, ns_all, lc_b, B_KVW)
    cbv = cache_b_v.reshape(depth, ns_all, lc_b, B_KVW)
    cck = cache_c_k.reshape(depth, ns_all, lc_c, C_WIDTH)
    ccv = cache_c_v.reshape(depth, ns_all, lc_c, C_WIDTH)

    pk, pv, pck, pcv = [], [], [], []
    sk, sv, sck, scv, sav = [], [], [], [], []
    for l in range(depth):
        xp, bk, bv, ck, cv = _prompt_layer(xp, gpre[l], gpost[l], wT[l], wnat[l], wout[l], ang[l],
                                           awsT[l], abs_p[l], sink[l], bias_pb, bias_pc[l])
        pk.append(bk); pv.append(bv); pck.append(ck); pcv.append(cv)
        xs, bk, bv, ck, cv, av = _sample_layer(
            xs, cbk[l], cbv[l], cck[l], ccv[l], gpre[l], gpost[l], wT[l], wnat[l], wout[l], ang[l],
            abd[l], akeep, abias_s[l], sink[l], bias_sb, bias_sc[l], t_new=t_new, tok_per_step=tok_per_step)
        sk.append(bk); sv.append(bv); sck.append(ck); scv.append(cv); sav.append(av)

    def stk(xs_, lead, heads):
        return jnp.stack(xs_).reshape(depth, lead, -1, heads, HEAD_DIM)

    return (xp, xs.reshape(ns_all, t_new, d),
            stk(pk, nb, B_KV), stk(pv, nb, B_KV), stk(pck, nb, C_HEADS), stk(pcv, nb, C_HEADS),
            stk(sk, ns_all, B_KV), stk(sv, ns_all, B_KV), stk(sck, ns_all, C_HEADS), stk(scv, ns_all, C_HEADS),
            jnp.stack(sav).reshape(depth, ns_all, t_new, A_WIDTH))
```

```python
import functools
import math

import numpy as np
import jax
import jax.numpy as jnp
from jax import lax
from jax.experimental import pallas as pl
from jax.experimental.pallas import tpu as pltpu

bf16 = jnp.bfloat16
f32 = jnp.float32

HEAD_DIM = 64
CHUNK = 64
A_WIDTH, A_GROUPS, A_CHUNK = 256, 4, 128
A_GD = A_WIDTH // A_GROUPS
B_HEADS, B_KV = 8, 2
B_GROUP = B_HEADS // B_KV
B_WIDTH, B_KVW = B_HEADS * HEAD_DIM, B_KV * HEAD_DIM
B_PREV = 2
C_HEADS = 4
C_WIDTH = C_HEADS * HEAD_DIM
C_PREV = 8
C_CLIP = 128
T5_BUCKETS, T5_MAX_DIST = 32, 128
RMS_EPS = 1e-6
NEG_INF = -1e30
QK_SCALE = HEAD_DIM ** -0.5

LANE = 128
MXU_N = 256
PAIR = 2 * CHUNK
B_WIN = (B_PREV + 2) * CHUNK
C_WIN = (C_PREV + 2) * CHUNK
B_KEEP = B_PREV * CHUNK
C_KEEP = C_PREV * CHUNK
SEQ_TILE = 512
SAMPLE_STREAMS_PER_BLOCK = 4
VMEM_LIMIT = 56 * 1024 * 1024
LUT_UNROLL = 8

_IN_SIZES = [A_WIDTH, A_WIDTH, A_WIDTH, B_WIDTH, B_KVW, B_KVW, B_WIDTH, C_WIDTH, C_WIDTH, C_WIDTH, C_WIDTH]
_IN_OFF = [int(v) for v in np.cumsum([0] + _IN_SIZES)]
(T_AU, T_AV, T_AZ, T_BQ, T_BK, T_BV, T_BZ, T_CQ, T_CK, T_CV, T_CZ, T_END) = _IN_OFF
_T_SECTIONS = ((T_AU, T_BK), (T_BV, T_CK), (T_CV, T_END))
Y_A, Y_B, Y_C = 0, A_WIDTH, A_WIDTH + B_WIDTH
assert T_BK % MXU_N == 0 and T_CK % MXU_N == 0 and T_CV % MXU_N == 0

_NT = (((1,), (1,)), ((), ()))
_TN = (((0,), (0,)), ((), ()))


def _t5_bucket(rel):
    half = T5_BUCKETS // 2
    max_exact = half // 2
    ret = jnp.where(rel > 0, half, 0)
    n = jnp.abs(rel)
    nf = jnp.maximum(n, 1).astype(jnp.float32)
    large = max_exact + (jnp.log(nf / max_exact) / math.log(T5_MAX_DIST / max_exact)
                         * (half - max_exact)).astype(jnp.int32)
    large = jnp.minimum(large, half - 1)
    return ret + jnp.where(n < max_exact, n, large)


def _lut_kernel(tab_ref, idx_ref, out_ref, *, n_entries, n_heads, ranges, kmins):
    layer = pl.program_id(0)
    neg = jnp.full((CHUNK, LANE), NEG_INF, f32)
    hgroup = min(n_heads, 4)
    for rb, (lo, hi) in enumerate(ranges):
        rows = slice(rb * CHUNK, (rb + 1) * CHUNK)
        idx = idx_ref[rows, :]
        for h0 in range(0, n_heads, hgroup):
            if hi < lo:
                base = (neg,) * hgroup
            else:
                def body(e, accs, h0=h0, idx=idx):
                    hit = idx == e
                    off = (layer * n_entries + e) * n_heads + h0
                    return tuple(jnp.where(hit, tab_ref[off + j], a) for j, a in enumerate(accs))
                base = lax.fori_loop(lo, hi + 1, body, (neg,) * hgroup,
                                     unroll=min(LUT_UNROLL, hi + 1 - lo))
            for j in range(hgroup):
                for v, kmin in enumerate(kmins):
                    out_ref[0, v, rows, (h0 + j) * LANE:(h0 + j + 1) * LANE] = neg if rb < kmin else base[j]


def _build_bias(table, idx, *, ranges, kmins):
    n_layers, n_entries, n_heads = table.shape
    n_rows = idx.shape[0]
    kern = functools.partial(_lut_kernel, n_entries=n_entries, n_heads=n_heads,
                             ranges=tuple(ranges), kmins=tuple(kmins))
    return pl.pallas_call(
        kern,
        grid=(n_layers,),
        in_specs=[pl.BlockSpec(memory_space=pltpu.SMEM),
                  pl.BlockSpec((n_rows, LANE), lambda l: (0, 0))],
        out_specs=pl.BlockSpec((1, len(kmins), n_rows, n_heads * LANE), lambda l: (l, 0, 0, 0)),
        out_shape=jax.ShapeDtypeStruct((n_layers, len(kmins), n_rows, n_heads * LANE), f32),
        compiler_params=pltpu.CompilerParams(dimension_semantics=("arbitrary",),
                                             vmem_limit_bytes=VMEM_LIMIT),
        name="bias_lut",
    )(table.reshape(-1), idx)


def _static_ranges(idx_np):
    out = []
    for rb in range(idx_np.shape[0] // CHUNK):
        blk = idx_np[rb * CHUNK:(rb + 1) * CHUNK]
        good = blk[blk >= 0]
        out.append((int(good.min()), int(good.max())) if good.size else (0, -1))
    return out


def _prompt_maps():
    i = np.arange(PAIR)[None, :]
    qi = i // CHUNK
    jb = np.arange(B_WIN)[:, None]
    rel_b = jb - B_KEEP - i
    ok_b = (jb // CHUNK >= qi) & (jb // CHUNK <= qi + B_PREV)
    jc = np.arange(C_WIN)[:, None]
    rel_c = jc - C_KEEP - i
    ok_c = (jc // CHUNK >= qi) & (jc // CHUNK <= qi + C_PREV)
    idx_c = np.where(ok_c, np.clip(rel_c, -C_CLIP, C_CLIP) + C_CLIP, -1).astype(np.int32)
    return rel_b.astype(np.int32), ok_b, idx_c


def _sample_maps(lc_b, lc_c, t_new):
    ns = SAMPLE_STREAMS_PER_BLOCK
    c = np.arange(LANE)[None, :]
    sq, i = c // t_new, c % t_new

    def one(lc):
        r = np.arange(ns * lc + LANE)[:, None]
        is_new = r >= ns * lc
        sk = np.where(is_new, (r - ns * lc) // t_new, r // lc)
        j = np.where(is_new, lc + (r - ns * lc) % t_new, r % lc)
        return (j - lc - i).astype(np.int32), (sk == sq)

    rel_b, ok_b = one(lc_b)
    rel_c, ok_c = one(lc_c)
    idx_c = np.where(ok_c, np.clip(rel_c, -C_CLIP, C_CLIP) + C_CLIP, -1).astype(np.int32)
    return rel_b, ok_b, idx_c


def _rms_rows(x, g):
    ms = jnp.mean(x * x, axis=-1, keepdims=True)
    return x * lax.rsqrt(ms + RMS_EPS) * g


def _rms_cols(xT, g_col):
    ms = jnp.mean(xT * xT, axis=0, keepdims=True)
    return xT * lax.rsqrt(ms + RMS_EPS) * g_col


def _project_T(pt_sc, wT_ref, h):
    for r0, r1 in _T_SECTIONS:
        pt_sc[r0:r1, :] = lax.dot_general(wT_ref[r0:r1, :], h, _NT, preferred_element_type=f32)


def _qblockdiag(qa, qb, lower):
    z = jnp.zeros((HEAD_DIM, LANE), bf16)
    if lower is None:
        return jnp.concatenate([jnp.concatenate([qa, z], axis=1),
                                jnp.concatenate([z, qb], axis=1)], axis=0)
    top = jnp.concatenate([qa, qb], axis=1)
    zz = jnp.zeros((HEAD_DIM, 2 * LANE), bf16)
    return jnp.concatenate([top, zz] if lower == 0 else [zz, top], axis=0)


def _softmax_pv(s, sink, vwin):
    m = jnp.max(s, axis=0, keepdims=True)
    if sink is not None:
        m = jnp.maximum(m, sink)
    e = jnp.exp(s - m)
    den = jnp.sum(e, axis=0, keepdims=True)
    if sink is not None:
        den = den + jnp.exp(sink - m)
    o = jnp.dot(vwin, e.astype(bf16), preferred_element_type=f32)
    return o / den


def _mixers_bc(pt_sc, yT_sc, cols, kwin_b, vwinT_b, bias_b, sink_ref, kwin_c, vwinT_c, bias_c):
    for cb in range(B_HEADS // 2):
        ha, hb = 2 * cb, 2 * cb + 1
        kv = ha // B_GROUP
        qa = (pt_sc[T_BQ + ha * HEAD_DIM:T_BQ + (ha + 1) * HEAD_DIM, cols] * QK_SCALE).astype(bf16)
        qb = (pt_sc[T_BQ + hb * HEAD_DIM:T_BQ + (hb + 1) * HEAD_DIM, cols] * QK_SCALE).astype(bf16)
        s = jnp.dot(kwin_b, _qblockdiag(qa, qb, kv), preferred_element_type=f32)
        s = s + bias_b[:, cb * 2 * LANE:(cb + 1) * 2 * LANE]
        sink = sink_ref[:, cb * 2 * LANE:(cb + 1) * 2 * LANE]
        o = _softmax_pv(s, sink, vwinT_b[kv * HEAD_DIM:(kv + 1) * HEAD_DIM, :])
        for j, hh in enumerate((ha, hb)):
            gate = jax.nn.silu(pt_sc[T_BZ + hh * HEAD_DIM:T_BZ + (hh + 1) * HEAD_DIM, cols])
            yT_sc[Y_B + hh * HEAD_DIM:Y_B + (hh + 1) * HEAD_DIM, cols] = (
                o[:, j * LANE:(j + 1) * LANE] * gate).astype(bf16)
    for cb in range(C_HEADS // 2):
        ha, hb = 2 * cb, 2 * cb + 1
        qa = (pt_sc[T_CQ + ha * HEAD_DIM:T_CQ + (ha + 1) * HEAD_DIM, cols] * QK_SCALE).astype(bf16)
        qb = (pt_sc[T_CQ + hb * HEAD_DIM:T_CQ + (hb + 1) * HEAD_DIM, cols] * QK_SCALE).astype(bf16)
        s = jnp.dot(kwin_c[:, cb * LANE:(cb + 1) * LANE], _qblockdiag(qa, qb, None),
                    preferred_element_type=f32)
        s = s + bias_c[:, cb * 2 * LANE:(cb + 1) * 2 * LANE]
        o = _softmax_pv(s, None, vwinT_c[cb * LANE:(cb + 1) * LANE, :])
        for j, hh in enumerate((ha, hb)):
            gate = jax.nn.silu(pt_sc[T_CZ + hh * HEAD_DIM:T_CZ + (hh + 1) * HEAD_DIM, cols])
            yT_sc[Y_C + hh * HEAD_DIM:Y_C + (hh + 1) * HEAD_DIM, cols] = (
                o[j * HEAD_DIM:(j + 1) * HEAD_DIM, j * LANE:(j + 1) * LANE] * gate).astype(bf16)


def _finish(x, yT_sc, wout_ref, gpost_ref):
    out = lax.dot_general(yT_sc[...], wout_ref[...], _TN, preferred_element_type=f32)
    return x + _rms_rows(out, gpost_ref[...])


def _layer_spec(arr, layer):
    shape = arr.shape[1:]
    return pl.BlockSpec((None,) + shape, lambda *_: (layer,) + (0,) * len(shape),
                        pipeline_mode=pl.Buffered(1))


def _wcol_spec(w, layer, col0, width=MXU_N):
    assert col0 % width == 0
    return pl.BlockSpec((None, w.shape[1], width), lambda *_: (layer, 0, col0 // width),
                        pipeline_mode=pl.Buffered(1))


def _const_spec(arr):
    return pl.BlockSpec(arr.shape, lambda *_: (0,) * arr.ndim, pipeline_mode=pl.Buffered(1))


def _prompt_kernel(x_ref, gpre_ref, gpost_ref, wT_ref, wbkv_ref, wck_ref, wcv_ref, wout_ref,
                   ang_ref, awsT_ref, abs_ref, sink_ref, biasb_ref, biasc_ref,
                   xo_ref, bk_ref, bv_ref, ck_ref, cv_ref,
                   pt_sc, kb_ring, kc_ring, vbT_ring, vcT_ring, yT_sc):
    t = pl.program_id(1)
    ts = SEQ_TILE

    @pl.when(t == 0)
    def _():
        kb_ring[0:B_KEEP, :] = jnp.zeros((B_KEEP, B_KVW), bf16)
        kc_ring[0:C_KEEP, :] = jnp.zeros((C_KEEP, C_WIDTH), bf16)
        vbT_ring[:, 0:B_KEEP] = jnp.zeros((B_KVW, B_KEEP), bf16)
        vcT_ring[:, 0:C_KEEP] = jnp.zeros((C_WIDTH, C_KEEP), bf16)

    x = x_ref[0]
    h = _rms_rows(x, gpre_ref[...]).astype(bf16)
    _project_T(pt_sc, wT_ref, h)
    kn_b = jnp.dot(h, wbkv_ref[:, 0:B_KVW], preferred_element_type=f32)
    kn_c = jnp.dot(h, wck_ref[...], preferred_element_type=f32)
    kb_ring[B_KEEP:, :] = kn_b.astype(bf16)
    kc_ring[C_KEEP:, :] = kn_c.astype(bf16)
    vbT_ring[:, B_KEEP:] = pt_sc[T_BV:T_BV + B_KVW, :].astype(bf16)
    vcT_ring[:, C_KEEP:] = pt_sc[T_CV:T_CV + C_WIDTH, :].astype(bf16)

    @pl.when(t == pl.num_programs(1) - 1)
    def _():
        bk_ref[0] = kn_b[ts - B_KEEP:, :]
        ck_ref[0] = kn_c[ts - C_KEEP:, :]
        bv_ref[0] = jnp.dot(h[ts - B_KEEP:, :], wbkv_ref[:, B_KVW:], preferred_element_type=f32)
        cv_ref[0] = jnp.dot(h[ts - C_KEEP:, :], wcv_ref[...], preferred_element_type=f32)

    vn = _rms_cols(pt_sc[T_AV:T_AV + A_WIDTH, :], ang_ref[...]).astype(bf16)
    n_ac = ts // A_CHUNK
    row = lax.broadcasted_iota(jnp.int32, (A_CHUNK, A_CHUNK), 0)
    col = lax.broadcasted_iota(jnp.int32, (A_CHUNK, A_CHUNK), 1)
    for g in range(A_GROUPS):
        rows = slice(g * A_GD, (g + 1) * A_GD)
        wgt = jnp.where(row <= col, awsT_ref[g], 0.0).astype(bf16)
        lhs = jnp.concatenate([vn[rows, c * A_CHUNK:(c + 1) * A_CHUNK] for c in range(n_ac)], axis=0)
        mix = jnp.dot(lhs, wgt, preferred_element_type=f32) + abs_ref[g]
        for c in range(n_ac):
            cols = slice(c * A_CHUNK, (c + 1) * A_CHUNK)
            u = pt_sc[T_AU + g * A_GD:T_AU + (g + 1) * A_GD, cols]
            z = pt_sc[T_AZ + g * A_GD:T_AZ + (g + 1) * A_GD, cols]
            yT_sc[Y_A + g * A_GD:Y_A + (g + 1) * A_GD, cols] = (
                u * mix[c * A_GD:(c + 1) * A_GD, :] * jax.nn.silu(z)).astype(bf16)

    def pair_body(p, carry):
        pg = t * (ts // PAIR) + p
        start = pl.multiple_of(p * PAIR, PAIR)
        _mixers_bc(
            pt_sc, yT_sc, pl.ds(start, PAIR),
            kb_ring[pl.ds(start, B_WIN), :], vbT_ring[:, pl.ds(start, B_WIN)],
            biasb_ref.at[jnp.minimum(pg, 1)], sink_ref,
            kc_ring[pl.ds(start, C_WIN), :], vcT_ring[:, pl.ds(start, C_WIN)],
            biasc_ref.at[jnp.minimum(pg, C_PREV // 2)])
        return carry

    lax.fori_loop(0, ts // PAIR, pair_body, 0)

    xo_ref[0] = _finish(x, yT_sc, wout_ref, gpost_ref)

    kb_ring[0:B_KEEP, :] = kb_ring[ts:ts + B_KEEP, :]
    kc_ring[0:C_KEEP, :] = kc_ring[ts:ts + C_KEEP, :]
    vbT_ring[:, 0:B_KEEP] = vbT_ring[:, ts:ts + B_KEEP]
    vcT_ring[:, 0:C_KEEP] = vcT_ring[:, ts:ts + C_KEEP]


def _prompt_layer(layer, x, p):
    nb, seq, d = x.shape
    ts = SEQ_TILE
    assert seq % ts == 0 and ts == C_KEEP
    kv_spec = lambda keep, width: pl.BlockSpec((1, keep, width), lambda b, t: (b, 0, 0))
    return pl.pallas_call(
        _prompt_kernel,
        grid=(nb, seq // ts),
        in_specs=[
            pl.BlockSpec((1, ts, d), lambda b, t: (b, t, 0)),
            _layer_spec(p["gpre"], layer), _layer_spec(p["gpost"], layer),
            _layer_spec(p["wT"], layer),
            _wcol_spec(p["w"], layer, T_BK), _wcol_spec(p["w"], layer, T_CK), _wcol_spec(p["w"], layer, T_CV),
            _layer_spec(p["wout"], layer),
            _layer_spec(p["ang"], layer), _layer_spec(p["awsT"], layer), _layer_spec(p["abs"], layer),
            _layer_spec(p["sink"], layer), _const_spec(p["bias_pb"]), _layer_spec(p["bias_pc"], layer),
        ],
        out_specs=[
            pl.BlockSpec((1, ts, d), lambda b, t: (b, t, 0)),
            kv_spec(B_KEEP, B_KVW), kv_spec(B_KEEP, B_KVW), kv_spec(C_KEEP, C_WIDTH), kv_spec(C_KEEP, C_WIDTH),
        ],
        out_shape=[
            jax.ShapeDtypeStruct((nb, seq, d), f32),
            jax.ShapeDtypeStruct((nb, B_KEEP, B_KVW), f32),
            jax.ShapeDtypeStruct((nb, B_KEEP, B_KVW), f32),
            jax.ShapeDtypeStruct((nb, C_KEEP, C_WIDTH), f32),
            jax.ShapeDtypeStruct((nb, C_KEEP, C_WIDTH), f32),
        ],
        scratch_shapes=[
            pltpu.VMEM((T_END, ts), f32),
            pltpu.VMEM((B_KEEP + ts, B_KVW), bf16),
            pltpu.VMEM((C_KEEP + ts, C_WIDTH), bf16),
            pltpu.VMEM((B_KVW, B_KEEP + ts), bf16),
            pltpu.VMEM((C_WIDTH, C_KEEP + ts), bf16),
            pltpu.VMEM((A_WIDTH + B_WIDTH + C_WIDTH, ts), bf16),
        ],
        compiler_params=pltpu.CompilerParams(dimension_semantics=("arbitrary", "arbitrary"),
                                             vmem_limit_bytes=VMEM_LIMIT),
        name="prompt_layer",
    )(x, p["gpre"], p["gpost"], p["wT"], p["w"], p["w"], p["w"], p["wout"], p["ang"], p["awsT"],
      p["abs"], p["sink"], p["bias_pb"], p["bias_pc"])


def _sample_kernel(x_ref, cbk_ref, cbv_ref, cck_ref, ccv_ref, gpre_ref, gpost_ref, wT_ref,
                   wbkv_ref, wck_ref, wcv_ref, wout_ref, ang_ref, abd_ref, akeep_ref, abias_ref,
                   sink_ref, biasb_ref, biasc_ref,
                   xo_ref, bk_ref, bv_ref, ck_ref, cv_ref, av_ref,
                   pt_sc, yT_sc):
    ns = SAMPLE_STREAMS_PER_BLOCK
    x = x_ref[...]
    ntok = x.shape[0]
    h = _rms_rows(x, gpre_ref[...]).astype(bf16)
    _project_T(pt_sc, wT_ref, h)
    kvn_b = jnp.dot(h, wbkv_ref[...], preferred_element_type=f32)
    kn_c = jnp.dot(h, wck_ref[...], preferred_element_type=f32)
    bk_ref[...] = kvn_b[:, 0:B_KVW]
    bv_ref[...] = kvn_b[:, B_KVW:]
    ck_ref[...] = kn_c
    cv_ref[...] = jnp.dot(h, wcv_ref[...], preferred_element_type=f32)

    vn = _rms_cols(pt_sc[T_AV:T_AV + A_WIDTH, :], ang_ref[...])
    av_ref[...] = vn.T
    vnb = vn.astype(bf16)
    keep = akeep_ref[...] > 0.0
    for g in range(A_GROUPS):
        rows = slice(g * A_GD, (g + 1) * A_GD)
        wgt = jnp.where(keep, abd_ref[g], 0.0).astype(bf16)
        mix = jnp.dot(vnb[rows, :], wgt, preferred_element_type=f32) + abias_ref[g]
        u = pt_sc[T_AU + g * A_GD:T_AU + (g + 1) * A_GD, :]
        z = pt_sc[T_AZ + g * A_GD:T_AZ + (g + 1) * A_GD, :]
        yT_sc[Y_A + g * A_GD:Y_A + (g + 1) * A_GD, :] = (u * mix * jax.nn.silu(z)).astype(bf16)

    lc_b, lc_c = cbk_ref.shape[1], cck_ref.shape[1]
    for blk in range(ntok // LANE):
        cols = slice(blk * LANE, (blk + 1) * LANE)
        ss = slice(blk * ns, (blk + 1) * ns)
        kwin_b = jnp.concatenate([cbk_ref[ss].reshape(ns * lc_b, B_KVW).astype(bf16),
                                  kvn_b[cols, 0:B_KVW].astype(bf16)], axis=0)
        vwinT_b = jnp.concatenate([cbv_ref[ss].reshape(ns * lc_b, B_KVW).T.astype(bf16),
                                   pt_sc[T_BV:T_BV + B_KVW, cols].astype(bf16)], axis=1)
        kwin_c = jnp.concatenate([cck_ref[ss].reshape(ns * lc_c, C_WIDTH).astype(bf16),
                                  kn_c[cols, :].astype(bf16)], axis=0)
        vwinT_c = jnp.concatenate([ccv_ref[ss].reshape(ns * lc_c, C_WIDTH).T.astype(bf16),
                                   pt_sc[T_CV:T_CV + C_WIDTH, cols].astype(bf16)], axis=1)
        _mixers_bc(pt_sc, yT_sc, cols, kwin_b, vwinT_b, biasb_ref, sink_ref,
                   kwin_c, vwinT_c, biasc_ref)

    xo_ref[...] = _finish(x, yT_sc, wout_ref, gpost_ref)


def _sample_layer(layer, x, caches, p, *, t_new, tok_per_step):
    ntok, d = x.shape
    spb = tok_per_step // t_new
    cbk, cbv, cck, ccv = caches

    def rows(width):
        return pl.BlockSpec((tok_per_step, width), lambda i: (i, 0))

    def cache(c):
        return pl.BlockSpec((None, spb) + c.shape[2:], lambda i: (layer, i, 0, 0))

    return pl.pallas_call(
        _sample_kernel,
        grid=(ntok // tok_per_step,),
        in_specs=[
            rows(d), cache(cbk), cache(cbv), cache(cck), cache(ccv),
            _layer_spec(p["gpre"], layer), _layer_spec(p["gpost"], layer),
            _layer_spec(p["wT"], layer),
            _wcol_spec(p["w"], layer, T_BK), _wcol_spec(p["w"], layer, T_CK), _wcol_spec(p["w"], layer, T_CV),
            _layer_spec(p["wout"], layer),
            _layer_spec(p["ang"], layer), _layer_spec(p["abd"], layer), _const_spec(p["akeep"]),
            _layer_spec(p["abias_s"], layer),
            _layer_spec(p["sink"], layer), _const_spec(p["bias_sb"]), _layer_spec(p["bias_sc"], layer),
        ],
        out_specs=[rows(d), rows(B_KVW), rows(B_KVW), rows(C_WIDTH), rows(C_WIDTH), rows(A_WIDTH)],
        out_shape=[
            jax.ShapeDtypeStruct((ntok, d), f32),
            jax.ShapeDtypeStruct((ntok, B_KVW), f32), jax.ShapeDtypeStruct((ntok, B_KVW), f32),
            jax.ShapeDtypeStruct((ntok, C_WIDTH), f32), jax.ShapeDtypeStruct((ntok, C_WIDTH), f32),
            jax.ShapeDtypeStruct((ntok, A_WIDTH), f32),
        ],
        scratch_shapes=[
            pltpu.VMEM((T_END, tok_per_step), f32),
            pltpu.VMEM((A_WIDTH + B_WIDTH + C_WIDTH, tok_per_step), bf16),
        ],
        compiler_params=pltpu.CompilerParams(dimension_semantics=("arbitrary",),
                                             vmem_limit_bytes=VMEM_LIMIT),
        name="sample_layer",
    )(x, cbk, cbv, cck, ccv, p["gpre"], p["gpost"], p["wT"], p["w"], p["w"], p["w"], p["wout"],
      p["ang"], p["abd"], p["akeep"], p["abias_s"], p["sink"], p["bias_sb"], p["bias_sc"])


def kernel(x_prompt, x_sample, cache_b_k, cache_b_v, cache_c_k, cache_c_v, g_pre, g_post, w_in, w_out,
           a_norm_g, a_ws, a_bs, b_sinks, c_rel_bias, t5_bias):
    depth = w_in.shape[0]
    nb, seq, d = x_prompt.shape
    ns_all, t_new, _ = x_sample.shape
    lc_b, lc_c = cache_b_k.shape[2], cache_c_k.shape[2]
    assert lc_b == B_KEEP and lc_c == C_KEEP and LANE % t_new == 0
    assert SAMPLE_STREAMS_PER_BLOCK * t_new == LANE
    tok_per_step = 2 * LANE
    reps = tok_per_step // t_new

    rel_pb, ok_pb, idx_pc = _prompt_maps()
    rel_sb, ok_sb, idx_sc = _sample_maps(lc_b, lc_c, t_new)
    idx_pb = jnp.where(ok_pb, _t5_bucket(jnp.asarray(rel_pb)), -1).astype(jnp.int32)
    idx_sb = jnp.where(ok_sb, _t5_bucket(jnp.asarray(rel_sb)), -1).astype(jnp.int32)
    t5_tab = t5_bias[None]
    full_t5 = [(0, T5_BUCKETS - 1)]
    r_ = np.arange(tok_per_step)
    awsT = a_ws.transpose(0, 1, 3, 2)
    w_bf = w_in.astype(bf16)

    p = dict(
        bias_pb=_build_bias(t5_tab, idx_pb, ranges=full_t5 * (B_WIN // CHUNK), kmins=(B_PREV, 0))[0],
        bias_sb=_build_bias(t5_tab, idx_sb, ranges=full_t5 * (idx_sb.shape[0] // CHUNK), kmins=(0,))[0, 0],
        bias_pc=_build_bias(c_rel_bias, jnp.asarray(idx_pc), ranges=_static_ranges(idx_pc),
                            kmins=tuple(max(C_PREV - 2 * v, 0) for v in range(C_PREV // 2 + 1))),
        bias_sc=_build_bias(c_rel_bias, jnp.asarray(idx_sc), ranges=_static_ranges(idx_sc),
                            kmins=(0,))[:, 0],
        w=w_bf,
        wT=w_bf.transpose(0, 2, 1),
        wout=w_out.astype(bf16),
        gpre=g_pre[:, None, :],
        gpost=g_post[:, None, :],
        ang=a_norm_g[:, :, None],
        awsT=awsT,
        abs=a_bs[:, :, None, :],
        sink=jnp.repeat(b_sinks, LANE, axis=1)[:, None, :],
        abd=jnp.tile(awsT[:, :, :t_new, :t_new], (1, 1, reps, reps)),
        abias_s=jnp.tile(a_bs[:, :, None, :t_new], (1, 1, 1, reps)),
        akeep=jnp.asarray(((r_[:, None] // t_new == r_[None, :] // t_new)
                           & (r_[:, None] % t_new <= r_[None, :] % t_new)).astype(np.float32)),
    )

    xp = x_prompt
    xs = x_sample.reshape(ns_all * t_new, d)
    caches = (cache_b_k.reshape(depth, ns_all, lc_b, B_KVW), cache_b_v.reshape(depth, ns_all, lc_b, B_KVW),
              cache_c_k.reshape(depth, ns_all, lc_c, C_WIDTH), cache_c_v.reshape(depth, ns_all, lc_c, C_WIDTH))

    pk, pv, pck, pcv = [], [], [], []
    sk, sv, sck, scv, sav = [], [], [], [], []
    for l in range(depth):
        xp, bk, bv, ck, cv = _prompt_layer(l, xp, p)
        pk.append(bk); pv.append(bv); pck.append(ck); pcv.append(cv)
        xs, bk, bv, ck, cv, av = _sample_layer(l, xs, caches, p, t_new=t_new, tok_per_step=tok_per_step)
        sk.append(bk); sv.append(bv); sck.append(ck); scv.append(cv); sav.append(av)

    def stk(xs_, lead, heads):
        return jnp.stack(xs_).reshape(depth, lead, -1, heads, HEAD_DIM)

    return (xp, xs.reshape(ns_all, t_new, d),
            stk(pk, nb, B_KV), stk(pv, nb, B_KV), stk(pck, nb, C_HEADS), stk(pcv, nb, C_HEADS),
            stk(sk, ns_all, B_KV), stk(sv, ns_all, B_KV), stk(sck, ns_all, C_HEADS), stk(scv, ns_all, C_HEADS),
            jnp.stack(sav).reshape(depth, ns_all, t_new, A_WIDTH))
```

```python
import functools
import math

import numpy as np
import jax
import jax.numpy as jnp
from jax import lax
from jax.experimental import pallas as pl
from jax.experimental.pallas import tpu as pltpu

bf16 = jnp.bfloat16
f32 = jnp.float32

HEAD_DIM = 64
CHUNK = 64
A_WIDTH, A_GROUPS, A_CHUNK = 256, 4, 128
A_GD = A_WIDTH // A_GROUPS
B_HEADS, B_KV = 8, 2
B_GROUP = B_HEADS // B_KV
B_WIDTH, B_KVW = B_HEADS * HEAD_DIM, B_KV * HEAD_DIM
B_PREV = 2
C_HEADS = 4
C_WIDTH = C_HEADS * HEAD_DIM
C_PREV = 8
C_CLIP = 128
T5_BUCKETS, T5_MAX_DIST = 32, 128
RMS_EPS = 1e-6
NEG_INF = -1e30
QK_SCALE = HEAD_DIM ** -0.5
LOG2E = math.log2(math.e)

LANE = 128
MXU_N = 256
PAIR = 2 * CHUNK
B_WIN = (B_PREV + 2) * CHUNK
C_WIN = (C_PREV + 2) * CHUNK
B_KEEP = B_PREV * CHUNK
C_KEEP = C_PREV * CHUNK
ONES_ROWS = 16
VB_STRIDE = HEAD_DIM + ONES_ROWS
VC_STRIDE = 2 * HEAD_DIM + ONES_ROWS
SEQ_TILE = 512
SAMPLE_STREAMS_PER_BLOCK = 4
VMEM_LIMIT = 56 * 1024 * 1024
LUT_UNROLL = 8

_IN_SIZES = [A_WIDTH, A_WIDTH, A_WIDTH, B_WIDTH, B_KVW, B_KVW, B_WIDTH, C_WIDTH, C_WIDTH, C_WIDTH, C_WIDTH]
_IN_OFF = [int(v) for v in np.cumsum([0] + _IN_SIZES)]
(T_AU, T_AV, T_AZ, T_BQ, T_BK, T_BV, T_BZ, T_CQ, T_CK, T_CV, T_CZ, T_END) = _IN_OFF
_T_SECTIONS = ((T_AU, T_BK), (T_BV, T_CK), (T_CV, T_END))
Y_A, Y_B, Y_C = 0, A_WIDTH, A_WIDTH + B_WIDTH
assert T_BK % MXU_N == 0 and T_CK % MXU_N == 0 and T_CV % MXU_N == 0

_NT = (((1,), (1,)), ((), ()))
_TN = (((0,), (0,)), ((), ()))


def _t5_bucket(rel):
    half = T5_BUCKETS // 2
    max_exact = half // 2
    ret = jnp.where(rel > 0, half, 0)
    n = jnp.abs(rel)
    nf = jnp.maximum(n, 1).astype(jnp.float32)
    large = max_exact + (jnp.log(nf / max_exact) / math.log(T5_MAX_DIST / max_exact)
                         * (half - max_exact)).astype(jnp.int32)
    large = jnp.minimum(large, half - 1)
    return ret + jnp.where(n < max_exact, n, large)


def _lut_kernel(tab_ref, idx_ref, out_ref, *, n_entries, n_heads, ranges, kmins):
    layer = pl.program_id(0)
    neg = jnp.full((CHUNK, LANE), NEG_INF, f32)
    hgroup = min(n_heads, 4)
    for rb, (lo, hi) in enumerate(ranges):
        rows = slice(rb * CHUNK, (rb + 1) * CHUNK)
        idx = idx_ref[rows, :]
        for h0 in range(0, n_heads, hgroup):
            if hi < lo:
                base = (neg,) * hgroup
            else:
                def body(e, accs, h0=h0, idx=idx):
                    hit = idx == e
                    off = (layer * n_entries + e) * n_heads + h0
                    return tuple(jnp.where(hit, tab_ref[off + j], a) for j, a in enumerate(accs))
                base = lax.fori_loop(lo, hi + 1, body, (neg,) * hgroup,
                                     unroll=min(LUT_UNROLL, hi + 1 - lo))
            for j in range(hgroup):
                scaled = base[j] * LOG2E
                for v, kmin in enumerate(kmins):
                    out_ref[0, v, rows, (h0 + j) * LANE:(h0 + j + 1) * LANE] = neg if rb < kmin else scaled


def _build_bias(table, idx, *, ranges, kmins):
    n_layers, n_entries, n_heads = table.shape
    n_rows = idx.shape[0]
    kern = functools.partial(_lut_kernel, n_entries=n_entries, n_heads=n_heads,
                             ranges=tuple(ranges), kmins=tuple(kmins))
    return pl.pallas_call(
        kern,
        grid=(n_layers,),
        in_specs=[pl.BlockSpec(memory_space=pltpu.SMEM),
                  pl.BlockSpec((n_rows, LANE), lambda l: (0, 0))],
        out_specs=pl.BlockSpec((1, len(kmins), n_rows, n_heads * LANE), lambda l: (l, 0, 0, 0)),
        out_shape=jax.ShapeDtypeStruct((n_layers, len(kmins), n_rows, n_heads * LANE), f32),
        compiler_params=pltpu.CompilerParams(dimension_semantics=("arbitrary",),
                                             vmem_limit_bytes=VMEM_LIMIT),
        name="bias_lut",
    )(table.reshape(-1), idx)


def _static_ranges(idx_np):
    out = []
    for rb in range(idx_np.shape[0] // CHUNK):
        blk = idx_np[rb * CHUNK:(rb + 1) * CHUNK]
        good = blk[blk >= 0]
        out.append((int(good.min()), int(good.max())) if good.size else (0, -1))
    return out


def _prompt_maps():
    i = np.arange(PAIR)[None, :]
    qi = i // CHUNK
    jb = np.arange(B_WIN)[:, None]
    rel_b = jb - B_KEEP - i
    ok_b = (jb // CHUNK >= qi) & (jb // CHUNK <= qi + B_PREV)
    jc = np.arange(C_WIN)[:, None]
    rel_c = jc - C_KEEP - i
    ok_c = (jc // CHUNK >= qi) & (jc // CHUNK <= qi + C_PREV)
    idx_c = np.where(ok_c, np.clip(rel_c, -C_CLIP, C_CLIP) + C_CLIP, -1).astype(np.int32)
    return rel_b.astype(np.int32), ok_b, idx_c


def _sample_maps(lc_b, lc_c, t_new):
    ns = SAMPLE_STREAMS_PER_BLOCK
    c = np.arange(LANE)[None, :]
    sq, i = c // t_new, c % t_new

    def one(lc):
        r = np.arange(ns * lc + LANE)[:, None]
        is_new = r >= ns * lc
        sk = np.where(is_new, (r - ns * lc) // t_new, r // lc)
        j = np.where(is_new, lc + (r - ns * lc) % t_new, r % lc)
        return (j - lc - i).astype(np.int32), (sk == sq)

    rel_b, ok_b = one(lc_b)
    rel_c, ok_c = one(lc_c)
    idx_c = np.where(ok_c, np.clip(rel_c, -C_CLIP, C_CLIP) + C_CLIP, -1).astype(np.int32)
    return rel_b, ok_b, idx_c


def _rms_rows(x, g):
    ms = jnp.mean(x * x, axis=-1, keepdims=True)
    return x * lax.rsqrt(ms + RMS_EPS) * g


def _rms_cols(xT, g_col):
    ms = jnp.mean(xT * xT, axis=0, keepdims=True)
    return xT * lax.rsqrt(ms + RMS_EPS) * g_col


def _project_T(pt_sc, wT_ref, h):
    for r0, r1 in _T_SECTIONS:
        pt_sc[r0:r1, :] = lax.dot_general(wT_ref[r0:r1, :], h, _NT, preferred_element_type=f32)


def _qblockdiag(qa, qb, lower):
    z = jnp.zeros((HEAD_DIM, LANE), bf16)
    if lower is None:
        return jnp.concatenate([jnp.concatenate([qa, z], axis=1),
                                jnp.concatenate([z, qb], axis=1)], axis=0)
    top = jnp.concatenate([qa, qb], axis=1)
    zz = jnp.zeros((HEAD_DIM, 2 * LANE), bf16)
    return jnp.concatenate([top, zz] if lower == 0 else [zz, top], axis=0)


def _with_ones(vT, group):
    ones = jnp.ones((ONES_ROWS, vT.shape[1]), vT.dtype)
    parts = []
    for r in range(0, vT.shape[0], group):
        parts += [vT[r:r + group, :], ones]
    return jnp.concatenate(parts, axis=0)


def _softmax_pv(s, sink, vwin):
    m = jnp.max(s, axis=0, keepdims=True)
    if sink is not None:
        m = jnp.maximum(m, sink)
    e = jnp.exp2(s - m)
    o = jnp.dot(vwin, e.astype(bf16), preferred_element_type=f32)
    nv = vwin.shape[0] - ONES_ROWS
    den = o[nv:nv + 1, :]
    if sink is not None:
        den = den + jnp.exp2(sink - m)
    return o[:nv, :] * (1.0 / den)


def _mixers_bc(pt_sc, yT_sc, cols, kwin_b, vwinT_b, bias_b, sink_ref, kwin_c, vwinT_c, bias_c):
    def head_q(t0, hh):
        return (pt_sc[t0 + hh * HEAD_DIM:t0 + (hh + 1) * HEAD_DIM, cols] * (QK_SCALE * LOG2E)).astype(bf16)

    s_b, s_c = [], []
    for cb in range(B_HEADS // 2):
        ha, hb = 2 * cb, 2 * cb + 1
        qbd = _qblockdiag(head_q(T_BQ, ha), head_q(T_BQ, hb), ha // B_GROUP)
        s = jnp.dot(kwin_b, qbd, preferred_element_type=f32)
        s_b.append(s + bias_b[:, cb * 2 * LANE:(cb + 1) * 2 * LANE])
    for cb in range(C_HEADS // 2):
        ha, hb = 2 * cb, 2 * cb + 1
        qbd = _qblockdiag(head_q(T_CQ, ha), head_q(T_CQ, hb), None)
        s = jnp.dot(kwin_c[:, cb * LANE:(cb + 1) * LANE], qbd, preferred_element_type=f32)
        s_c.append(s + bias_c[:, cb * 2 * LANE:(cb + 1) * 2 * LANE])

    for cb in range(B_HEADS // 2):
        ha, hb = 2 * cb, 2 * cb + 1
        kv = ha // B_GROUP
        sink = sink_ref[:, cb * 2 * LANE:(cb + 1) * 2 * LANE] * LOG2E
        o = _softmax_pv(s_b[cb], sink, vwinT_b[kv * VB_STRIDE:(kv + 1) * VB_STRIDE, :])
        for j, hh in enumerate((ha, hb)):
            gate = jax.nn.silu(pt_sc[T_BZ + hh * HEAD_DIM:T_BZ + (hh + 1) * HEAD_DIM, cols])
            yT_sc[Y_B + hh * HEAD_DIM:Y_B + (hh + 1) * HEAD_DIM, cols] = (
                o[:, j * LANE:(j + 1) * LANE] * gate).astype(bf16)
    for cb in range(C_HEADS // 2):
        ha, hb = 2 * cb, 2 * cb + 1
        o = _softmax_pv(s_c[cb], None, vwinT_c[cb * VC_STRIDE:(cb + 1) * VC_STRIDE, :])
        for j, hh in enumerate((ha, hb)):
            gate = jax.nn.silu(pt_sc[T_CZ + hh * HEAD_DIM:T_CZ + (hh + 1) * HEAD_DIM, cols])
            yT_sc[Y_C + hh * HEAD_DIM:Y_C + (hh + 1) * HEAD_DIM, cols] = (
                o[j * HEAD_DIM:(j + 1) * HEAD_DIM, j * LANE:(j + 1) * LANE] * gate).astype(bf16)


def _finish(x, yT_sc, wout_ref, gpost_ref):
    out = lax.dot_general(yT_sc[...], wout_ref[...], _TN, preferred_element_type=f32)
    return x + _rms_rows(out, gpost_ref[...])


def _layer_spec(arr, layer):
    shape = arr.shape[1:]
    return pl.BlockSpec((None,) + shape, lambda *_: (layer,) + (0,) * len(shape),
                        pipeline_mode=pl.Buffered(1))


def _wcol_spec(w, layer, col0, width=MXU_N):
    assert col0 % width == 0
    return pl.BlockSpec((None, w.shape[1], width), lambda *_: (layer, 0, col0 // width),
                        pipeline_mode=pl.Buffered(1))


def _const_spec(arr):
    return pl.BlockSpec(arr.shape, lambda *_: (0,) * arr.ndim, pipeline_mode=pl.Buffered(1))


def _prompt_kernel(x_ref, gpre_ref, gpost_ref, wT_ref, wbkv_ref, wck_ref, wcv_ref, wout_ref,
                   ang_ref, awsT_ref, abs_ref, sink_ref, biasb_ref, biasc_ref,
                   xo_ref, bk_ref, bv_ref, ck_ref, cv_ref,
                   pt_sc, kb_ring, kc_ring, vbT_ring, vcT_ring, yT_sc):
    t = pl.program_id(1)
    ts = SEQ_TILE

    @pl.when(t == 0)
    def _():
        kb_ring[0:B_KEEP, :] = jnp.zeros((B_KEEP, B_KVW), bf16)
        kc_ring[0:C_KEEP, :] = jnp.zeros((C_KEEP, C_WIDTH), bf16)
        vbT_ring[:, 0:B_KEEP] = jnp.zeros((B_KV * VB_STRIDE, B_KEEP), bf16)
        vcT_ring[:, 0:C_KEEP] = jnp.zeros((C_HEADS // 2 * VC_STRIDE, C_KEEP), bf16)
        for g in range(B_KV):
            vbT_ring[g * VB_STRIDE + HEAD_DIM:(g + 1) * VB_STRIDE, :] = jnp.ones((ONES_ROWS, B_KEEP + ts), bf16)
        for g in range(C_HEADS // 2):
            vcT_ring[g * VC_STRIDE + 2 * HEAD_DIM:(g + 1) * VC_STRIDE, :] = jnp.ones((ONES_ROWS, C_KEEP + ts), bf16)

    x = x_ref[0]
    h = _rms_rows(x, gpre_ref[...]).astype(bf16)
    _project_T(pt_sc, wT_ref, h)
    kn_b = jnp.dot(h, wbkv_ref[:, 0:B_KVW], preferred_element_type=f32)
    kn_c = jnp.dot(h, wck_ref[...], preferred_element_type=f32)
    kb_ring[B_KEEP:, :] = kn_b.astype(bf16)
    kc_ring[C_KEEP:, :] = kn_c.astype(bf16)
    for g in range(B_KV):
        vbT_ring[g * VB_STRIDE:g * VB_STRIDE + HEAD_DIM, B_KEEP:] = (
            pt_sc[T_BV + g * HEAD_DIM:T_BV + (g + 1) * HEAD_DIM, :].astype(bf16))
    for g in range(C_HEADS // 2):
        vcT_ring[g * VC_STRIDE:g * VC_STRIDE + 2 * HEAD_DIM, C_KEEP:] = (
            pt_sc[T_CV + g * 2 * HEAD_DIM:T_CV + (g + 1) * 2 * HEAD_DIM, :].astype(bf16))

    @pl.when(t == pl.num_programs(1) - 1)
    def _():
        bk_ref[0] = kn_b[ts - B_KEEP:, :]
        ck_ref[0] = kn_c[ts - C_KEEP:, :]
        bv_ref[0] = jnp.dot(h[ts - B_KEEP:, :], wbkv_ref[:, B_KVW:], preferred_element_type=f32)
        cv_ref[0] = jnp.dot(h[ts - C_KEEP:, :], wcv_ref[...], preferred_element_type=f32)

    vn = _rms_cols(pt_sc[T_AV:T_AV + A_WIDTH, :], ang_ref[...]).astype(bf16)
    n_ac = ts // A_CHUNK
    row = lax.broadcasted_iota(jnp.int32, (A_CHUNK, A_CHUNK), 0)
    col = lax.broadcasted_iota(jnp.int32, (A_CHUNK, A_CHUNK), 1)
    for g in range(A_GROUPS):
        rows = slice(g * A_GD, (g + 1) * A_GD)
        wgt = jnp.where(row <= col, awsT_ref[g], 0.0).astype(bf16)
        lhs = jnp.concatenate([vn[rows, c * A_CHUNK:(c + 1) * A_CHUNK] for c in range(n_ac)], axis=0)
        mix = jnp.dot(lhs, wgt, preferred_element_type=f32) + abs_ref[g]
        for c in range(n_ac):
            cols = slice(c * A_CHUNK, (c + 1) * A_CHUNK)
            u = pt_sc[T_AU + g * A_GD:T_AU + (g + 1) * A_GD, cols]
            z = pt_sc[T_AZ + g * A_GD:T_AZ + (g + 1) * A_GD, cols]
            yT_sc[Y_A + g * A_GD:Y_A + (g + 1) * A_GD, cols] = (
                u * mix[c * A_GD:(c + 1) * A_GD, :] * jax.nn.silu(z)).astype(bf16)

    def pair_body(p, carry):
        pg = t * (ts // PAIR) + p
        start = pl.multiple_of(p * PAIR, PAIR)
        _mixers_bc(
            pt_sc, yT_sc, pl.ds(start, PAIR),
            kb_ring[pl.ds(start, B_WIN), :], vbT_ring[:, pl.ds(start, B_WIN)],
            biasb_ref.at[jnp.minimum(pg, 1)], sink_ref,
            kc_ring[pl.ds(start, C_WIN), :], vcT_ring[:, pl.ds(start, C_WIN)],
            biasc_ref.at[jnp.minimum(pg, C_PREV // 2)])
        return carry

    lax.fori_loop(0, ts // PAIR, pair_body, 0)

    xo_ref[0] = _finish(x, yT_sc, wout_ref, gpost_ref)

    kb_ring[0:B_KEEP, :] = kb_ring[ts:ts + B_KEEP, :]
    kc_ring[0:C_KEEP, :] = kc_ring[ts:ts + C_KEEP, :]
    vbT_ring[:, 0:B_KEEP] = vbT_ring[:, ts:ts + B_KEEP]
    vcT_ring[:, 0:C_KEEP] = vcT_ring[:, ts:ts + C_KEEP]


def _prompt_layer(layer, x, p):
    nb, seq, d = x.shape
    ts = SEQ_TILE
    assert seq % ts == 0 and ts == C_KEEP
    kv_spec = lambda keep, width: pl.BlockSpec((1, keep, width), lambda b, t: (b, 0, 0))
    return pl.pallas_call(
        _prompt_kernel,
        grid=(nb, seq // ts),
        in_specs=[
            pl.BlockSpec((1, ts, d), lambda b, t: (b, t, 0)),
            _layer_spec(p["gpre"], layer), _layer_spec(p["gpost"], layer),
            _layer_spec(p["wT"], layer),
            _wcol_spec(p["w"], layer, T_BK), _wcol_spec(p["w"], layer, T_CK), _wcol_spec(p["w"], layer, T_CV),
            _layer_spec(p["wout"], layer),
            _layer_spec(p["ang"], layer), _layer_spec(p["awsT"], layer), _layer_spec(p["abs"], layer),
            _layer_spec(p["sink"], layer), _const_spec(p["bias_pb"]), _layer_spec(p["bias_pc"], layer),
        ],
        out_specs=[
            pl.BlockSpec((1, ts, d), lambda b, t: (b, t, 0)),
            kv_spec(B_KEEP, B_KVW), kv_spec(B_KEEP, B_KVW), kv_spec(C_KEEP, C_WIDTH), kv_spec(C_KEEP, C_WIDTH),
        ],
        out_shape=[
            jax.ShapeDtypeStruct((nb, seq, d), f32),
            jax.ShapeDtypeStruct((nb, B_KEEP, B_KVW), f32),
            jax.ShapeDtypeStruct((nb, B_KEEP, B_KVW), f32),
            jax.ShapeDtypeStruct((nb, C_KEEP, C_WIDTH), f32),
            jax.ShapeDtypeStruct((nb, C_KEEP, C_WIDTH), f32),
        ],
        scratch_shapes=[
            pltpu.VMEM((T_END, ts), f32),
            pltpu.VMEM((B_KEEP + ts, B_KVW), bf16),
            pltpu.VMEM((C_KEEP + ts, C_WIDTH), bf16),
            pltpu.VMEM((B_KV * VB_STRIDE, B_KEEP + ts), bf16),
            pltpu.VMEM((C_HEADS // 2 * VC_STRIDE, C_KEEP + ts), bf16),
            pltpu.VMEM((A_WIDTH + B_WIDTH + C_WIDTH, ts), bf16),
        ],
        compiler_params=pltpu.CompilerParams(dimension_semantics=("arbitrary", "arbitrary"),
                                             vmem_limit_bytes=VMEM_LIMIT),
        name="prompt_layer",
    )(x, p["gpre"], p["gpost"], p["wT"], p["w"], p["w"], p["w"], p["wout"], p["ang"], p["awsT"],
      p["abs"], p["sink"], p["bias_pb"], p["bias_pc"])


def _sample_kernel(x_ref, cbk_ref, cbv_ref, cck_ref, ccv_ref, gpre_ref, gpost_ref, wT_ref,
                   wbkv_ref, wck_ref, wcv_ref, wout_ref, ang_ref, abd_ref, akeep_ref, abias_ref,
                   sink_ref, biasb_ref, biasc_ref,
                   xo_ref, bk_ref, bv_ref, ck_ref, cv_ref, av_ref,
                   pt_sc, yT_sc):
    ns = SAMPLE_STREAMS_PER_BLOCK
    x = x_ref[...]
    ntok = x.shape[0]
    h = _rms_rows(x, gpre_ref[...]).astype(bf16)
    _project_T(pt_sc, wT_ref, h)
    kvn_b = jnp.dot(h, wbkv_ref[...], preferred_element_type=f32)
    kn_c = jnp.dot(h, wck_ref[...], preferred_element_type=f32)
    bk_ref[...] = kvn_b[:, 0:B_KVW]
    bv_ref[...] = kvn_b[:, B_KVW:]
    ck_ref[...] = kn_c
    cv_ref[...] = jnp.dot(h, wcv_ref[...], preferred_element_type=f32)

    vn = _rms_cols(pt_sc[T_AV:T_AV + A_WIDTH, :], ang_ref[...])
    av_ref[...] = vn.T
    vnb = vn.astype(bf16)
    keep = akeep_ref[...] > 0.0
    for g in range(A_GROUPS):
        rows = slice(g * A_GD, (g + 1) * A_GD)
        wgt = jnp.where(keep, abd_ref[g], 0.0).astype(bf16)
        mix = jnp.dot(vnb[rows, :], wgt, preferred_element_type=f32) + abias_ref[g]
        u = pt_sc[T_AU + g * A_GD:T_AU + (g + 1) * A_GD, :]
        z = pt_sc[T_AZ + g * A_GD:T_AZ + (g + 1) * A_GD, :]
        yT_sc[Y_A + g * A_GD:Y_A + (g + 1) * A_GD, :] = (u * mix * jax.nn.silu(z)).astype(bf16)

    lc_b, lc_c = cbk_ref.shape[1], cck_ref.shape[1]
    for blk in range(ntok // LANE):
        cols = slice(blk * LANE, (blk + 1) * LANE)
        ss = slice(blk * ns, (blk + 1) * ns)
        kwin_b = jnp.concatenate([cbk_ref[ss].reshape(ns * lc_b, B_KVW).astype(bf16),
                                  kvn_b[cols, 0:B_KVW].astype(bf16)], axis=0)
        vwinT_b = jnp.concatenate([cbv_ref[ss].reshape(ns * lc_b, B_KVW).T.astype(bf16),
                                   pt_sc[T_BV:T_BV + B_KVW, cols].astype(bf16)], axis=1)
        kwin_c = jnp.concatenate([cck_ref[ss].reshape(ns * lc_c, C_WIDTH).astype(bf16),
                                  kn_c[cols, :].astype(bf16)], axis=0)
        vwinT_c = jnp.concatenate([ccv_ref[ss].reshape(ns * lc_c, C_WIDTH).T.astype(bf16),
                                   pt_sc[T_CV:T_CV + C_WIDTH, cols].astype(bf16)], axis=1)
        _mixers_bc(pt_sc, yT_sc, cols, kwin_b, _with_ones(vwinT_b, HEAD_DIM), biasb_ref, sink_ref,
                   kwin_c, _with_ones(vwinT_c, 2 * HEAD_DIM), biasc_ref)

    xo_ref[...] = _finish(x, yT_sc, wout_ref, gpost_ref)


def _sample_layer(layer, x, caches, p, *, t_new, tok_per_step):
    ntok, d = x.shape
    spb = tok_per_step // t_new
    cbk, cbv, cck, ccv = caches

    def rows(width):
        return pl.BlockSpec((tok_per_step, width), lambda i: (i, 0))

    def cache(c):
        return pl.BlockSpec((None, spb) + c.shape[2:], lambda i: (layer, i, 0, 0))

    return pl.pallas_call(
        _sample_kernel,
        grid=(ntok // tok_per_step,),
        in_specs=[
            rows(d), cache(cbk), cache(cbv), cache(cck), cache(ccv),
            _layer_spec(p["gpre"], layer), _layer_spec(p["gpost"], layer),
            _layer_spec(p["wT"], layer),
            _wcol_spec(p["w"], layer, T_BK), _wcol_spec(p["w"], layer, T_CK), _wcol_spec(p["w"], layer, T_CV),
            _layer_spec(p["wout"], layer),
            _layer_spec(p["ang"], layer), _layer_spec(p["abd"], layer), _const_spec(p["akeep"]),
            _layer_spec(p["abias_s"], layer),
            _layer_spec(p["sink"], layer), _const_spec(p["bias_sb"]), _layer_spec(p["bias_sc"], layer),
        ],
        out_specs=[rows(d), rows(B_KVW), rows(B_KVW), rows(C_WIDTH), rows(C_WIDTH), rows(A_WIDTH)],
        out_shape=[
            jax.ShapeDtypeStruct((ntok, d), f32),
            jax.ShapeDtypeStruct((ntok, B_KVW), f32), jax.ShapeDtypeStruct((ntok, B_KVW), f32),
            jax.ShapeDtypeStruct((ntok, C_WIDTH), f32), jax.ShapeDtypeStruct((ntok, C_WIDTH), f32),
            jax.ShapeDtypeStruct((ntok, A_WIDTH), f32),
        ],
        scratch_shapes=[
            pltpu.VMEM((T_END, tok_per_step), f32),
            pltpu.VMEM((A_WIDTH + B_WIDTH + C_WIDTH, tok_per_step), bf16),
        ],
        compiler_params=pltpu.CompilerParams(dimension_semantics=("arbitrary",),
                                             vmem_limit_bytes=VMEM_LIMIT),
        name="sample_layer",
    )(x, cbk, cbv, cck, ccv, p["gpre"], p["gpost"], p["wT"], p["w"], p["w"], p["w"], p["wout"],
      p["ang"], p["abd"], p["akeep"], p["abias_s"], p["sink"], p["bias_sb"], p["bias_sc"])


def kernel(x_prompt, x_sample, cache_b_k, cache_b_v, cache_c_k, cache_c_v, g_pre, g_post, w_in, w_out,
           a_norm_g, a_ws, a_bs, b_sinks, c_rel_bias, t5_bias):
    depth = w_in.shape[0]
    nb, seq, d = x_prompt.shape
    ns_all, t_new, _ = x_sample.shape
    lc_b, lc_c = cache_b_k.shape[2], cache_c_k.shape[2]
    assert lc_b == B_KEEP and lc_c == C_KEEP and LANE % t_new == 0
    assert SAMPLE_STREAMS_PER_BLOCK * t_new == LANE
    tok_per_step = 2 * LANE
    reps = tok_per_step // t_new

    rel_pb, ok_pb, idx_pc = _prompt_maps()
    rel_sb, ok_sb, idx_sc = _sample_maps(lc_b, lc_c, t_new)
    idx_pb = jnp.where(ok_pb, _t5_bucket(jnp.asarray(rel_pb)), -1).astype(jnp.int32)
    idx_sb = jnp.where(ok_sb, _t5_bucket(jnp.asarray(rel_sb)), -1).astype(jnp.int32)
    t5_tab = t5_bias[None]
    full_t5 = [(0, T5_BUCKETS - 1)]
    r_ = np.arange(tok_per_step)
    awsT = a_ws.transpose(0, 1, 3, 2)
    w_bf = w_in.astype(bf16)

    p = dict(
        bias_pb=_build_bias(t5_tab, idx_pb, ranges=full_t5 * (B_WIN // CHUNK), kmins=(B_PREV, 0))[0],
        bias_sb=_build_bias(t5_tab, idx_sb, ranges=full_t5 * (idx_sb.shape[0] // CHUNK), kmins=(0,))[0, 0],
        bias_pc=_build_bias(c_rel_bias, jnp.asarray(idx_pc), ranges=_static_ranges(idx_pc),
                            kmins=tuple(max(C_PREV - 2 * v, 0) for v in range(C_PREV // 2 + 1))),
        bias_sc=_build_bias(c_rel_bias, jnp.asarray(idx_sc), ranges=_static_ranges(idx_sc),
                            kmins=(0,))[:, 0],
        w=w_bf,
        wT=w_bf.transpose(0, 2, 1),
        wout=w_out.astype(bf16),
        gpre=g_pre[:, None, :],
        gpost=g_post[:, None, :],
        ang=a_norm_g[:, :, None],
        awsT=awsT,
        abs=a_bs[:, :, None, :],
        sink=jnp.repeat(b_sinks, LANE, axis=1)[:, None, :],
        abd=jnp.tile(awsT[:, :, :t_new, :t_new], (1, 1, reps, reps)),
        abias_s=jnp.tile(a_bs[:, :, None, :t_new], (1, 1, 1, reps)),
        akeep=jnp.asarray(((r_[:, None] // t_new == r_[None, :] // t_new)
                           & (r_[:, None] % t_new <= r_[None, :] % t_new)).astype(np.float32)),
    )

    xp = x_prompt
    xs = x_sample.reshape(ns_all * t_new, d)
    caches = (cache_b_k.reshape(depth, ns_all, lc_b, B_KVW), cache_b_v.reshape(depth, ns_all, lc_b, B_KVW),
              cache_c_k.reshape(depth, ns_all, lc_c, C_WIDTH), cache_c_v.reshape(depth, ns_all, lc_c, C_WIDTH))

    pk, pv, pck, pcv = [], [], [], []
    sk, sv, sck, scv, sav = [], [], [], [], []
    for l in range(depth):
        xp, bk, bv, ck, cv = _prompt_layer(l, xp, p)
        pk.append(bk); pv.append(bv); pck.append(ck); pcv.append(cv)
        xs, bk, bv, ck, cv, av = _sample_layer(l, xs, caches, p, t_new=t_new, tok_per_step=tok_per_step)
        sk.append(bk); sv.append(bv); sck.append(ck); scv.append(cv); sav.append(av)

    def stk(xs_, lead, heads):
        return jnp.stack(xs_).reshape(depth, lead, -1, heads, HEAD_DIM)

    return (xp, xs.reshape(ns_all, t_new, d),
            stk(pk, nb, B_KV), stk(pv, nb, B_KV), stk(pck, nb, C_HEADS), stk(pcv, nb, C_HEADS),
            stk(sk, ns_all, B_KV), stk(sv, ns_all, B_KV), stk(sck, ns_all, C_HEADS), stk(scv, ns_all, C_HEADS),
            jnp.stack(sav).reshape(depth, ns_all, t_new, A_WIDTH))
```

```python
import functools
import math

import numpy as np
import jax
import jax.numpy as jnp
from jax import lax
from jax.experimental import pallas as pl
from jax.experimental.pallas import tpu as pltpu

bf16 = jnp.bfloat16
f32 = jnp.float32

HEAD_DIM = 64
CHUNK = 64
A_WIDTH, A_GROUPS, A_CHUNK = 256, 4, 128
A_GD = A_WIDTH // A_GROUPS
B_HEADS, B_KV = 8, 2
B_GROUP = B_HEADS // B_KV
B_WIDTH, B_KVW = B_HEADS * HEAD_DIM, B_KV * HEAD_DIM
B_PREV = 2
C_HEADS = 4
C_WIDTH = C_HEADS * HEAD_DIM
C_PREV = 8
C_CLIP = 128
T5_BUCKETS, T5_MAX_DIST = 32, 128
RMS_EPS = 1e-6
NEG_INF = -1e30
QK_SCALE = HEAD_DIM ** -0.5
LOG2E = math.log2(math.e)

LANE = 128
MXU_N = 256
PAIR = 2 * CHUNK
B_WIN = (B_PREV + 2) * CHUNK
C_WIN = (C_PREV + 2) * CHUNK
B_KEEP = B_PREV * CHUNK
C_KEEP = C_PREV * CHUNK
ONES_ROWS = 16
VB_STRIDE = HEAD_DIM + ONES_ROWS
VC_STRIDE = 2 * HEAD_DIM + ONES_ROWS
SEQ_TILE = 512
SAMPLE_STREAMS_PER_BLOCK = 4
VMEM_LIMIT = 56 * 1024 * 1024
LUT_UNROLL = 8

_IN_SIZES = [A_WIDTH, A_WIDTH, A_WIDTH, B_WIDTH, B_KVW, B_KVW, B_WIDTH, C_WIDTH, C_WIDTH, C_WIDTH, C_WIDTH]
_IN_OFF = [int(v) for v in np.cumsum([0] + _IN_SIZES)]
(T_AU, T_AV, T_AZ, T_BQ, T_BK, T_BV, T_BZ, T_CQ, T_CK, T_CV, T_CZ, T_END) = _IN_OFF
_T_SECTIONS = ((T_AU, T_BK), (T_BV, T_CK), (T_CV, T_END))
Y_A, Y_B, Y_C = 0, A_WIDTH, A_WIDTH + B_WIDTH
W_BKV, W_CK, W_CV = 0, 2 * B_KVW, 2 * B_KVW + C_WIDTH
assert 2 * B_KVW == MXU_N and C_WIDTH == MXU_N

_NT = (((1,), (1,)), ((), ()))
_TN = (((0,), (0,)), ((), ()))


def _t5_bucket(rel):
    half = T5_BUCKETS // 2
    max_exact = half // 2
    ret = jnp.where(rel > 0, half, 0)
    n = jnp.abs(rel)
    nf = jnp.maximum(n, 1).astype(jnp.float32)
    large = max_exact + (jnp.log(nf / max_exact) / math.log(T5_MAX_DIST / max_exact)
                         * (half - max_exact)).astype(jnp.int32)
    large = jnp.minimum(large, half - 1)
    return ret + jnp.where(n < max_exact, n, large)


def _lut_kernel(tab_ref, idx_ref, out_ref, *, n_entries, n_heads, ranges, kmins):
    layer = pl.program_id(0)
    neg = jnp.full((CHUNK, LANE), NEG_INF, f32)
    hgroup = min(n_heads, 4)
    for rb, (lo, hi) in enumerate(ranges):
        rows = slice(rb * CHUNK, (rb + 1) * CHUNK)
        idx = idx_ref[rows, :]
        for h0 in range(0, n_heads, hgroup):
            if hi < lo:
                base = (neg,) * hgroup
            else:
                def body(e, accs, h0=h0, idx=idx):
                    hit = idx == e
                    off = (layer * n_entries + e) * n_heads + h0
                    return tuple(jnp.where(hit, tab_ref[off + j], a) for j, a in enumerate(accs))
                base = lax.fori_loop(lo, hi + 1, body, (neg,) * hgroup,
                                     unroll=min(LUT_UNROLL, hi + 1 - lo))
            for j in range(hgroup):
                scaled = base[j] * LOG2E
                for v, kmin in enumerate(kmins):
                    out_ref[0, v, rows, (h0 + j) * LANE:(h0 + j + 1) * LANE] = neg if rb < kmin else scaled


def _build_bias(table, idx, *, ranges, kmins):
    n_layers, n_entries, n_heads = table.shape
    n_rows = idx.shape[0]
    kern = functools.partial(_lut_kernel, n_entries=n_entries, n_heads=n_heads,
                             ranges=tuple(ranges), kmins=tuple(kmins))
    return pl.pallas_call(
        kern,
        grid=(n_layers,),
        in_specs=[pl.BlockSpec(memory_space=pltpu.SMEM),
                  pl.BlockSpec((n_rows, LANE), lambda l: (0, 0))],
        out_specs=pl.BlockSpec((1, len(kmins), n_rows, n_heads * LANE), lambda l: (l, 0, 0, 0)),
        out_shape=jax.ShapeDtypeStruct((n_layers, len(kmins), n_rows, n_heads * LANE), f32),
        compiler_params=pltpu.CompilerParams(dimension_semantics=("arbitrary",),
                                             vmem_limit_bytes=VMEM_LIMIT),
        name="bias_lut",
    )(table.reshape(-1), idx)


def _static_ranges(idx_np):
    out = []
    for rb in range(idx_np.shape[0] // CHUNK):
        blk = idx_np[rb * CHUNK:(rb + 1) * CHUNK]
        good = blk[blk >= 0]
        out.append((int(good.min()), int(good.max())) if good.size else (0, -1))
    return out


def _prompt_maps():
    i = np.arange(PAIR)[None, :]
    qi = i // CHUNK
    jb = np.arange(B_WIN)[:, None]
    rel_b = jb - B_KEEP - i
    ok_b = (jb // CHUNK >= qi) & (jb // CHUNK <= qi + B_PREV)
    jc = np.arange(C_WIN)[:, None]
    rel_c = jc - C_KEEP - i
    ok_c = (jc // CHUNK >= qi) & (jc // CHUNK <= qi + C_PREV)
    idx_c = np.where(ok_c, np.clip(rel_c, -C_CLIP, C_CLIP) + C_CLIP, -1).astype(np.int32)
    return rel_b.astype(np.int32), ok_b, idx_c


def _sample_maps(lc_b, lc_c, t_new):
    ns = SAMPLE_STREAMS_PER_BLOCK
    c = np.arange(LANE)[None, :]
    sq, i = c // t_new, c % t_new

    def one(lc):
        r = np.arange(ns * lc + LANE)[:, None]
        is_new = r >= ns * lc
        sk = np.where(is_new, (r - ns * lc) // t_new, r // lc)
        j = np.where(is_new, lc + (r - ns * lc) % t_new, r % lc)
        return (j - lc - i).astype(np.int32), (sk == sq)

    rel_b, ok_b = one(lc_b)
    rel_c, ok_c = one(lc_c)
    idx_c = np.where(ok_c, np.clip(rel_c, -C_CLIP, C_CLIP) + C_CLIP, -1).astype(np.int32)
    return rel_b, ok_b, idx_c


def _rms_rows(x, g):
    ms = jnp.mean(x * x, axis=-1, keepdims=True)
    return x * lax.rsqrt(ms + RMS_EPS) * g


def _rms_cols(xT, g_col):
    ms = jnp.mean(xT * xT, axis=0, keepdims=True)
    return xT * lax.rsqrt(ms + RMS_EPS) * g_col


def _project_T(pt_sc, wT_ref, h):
    for r0, r1 in _T_SECTIONS:
        pt_sc[r0:r1, :] = lax.dot_general(wT_ref[r0:r1, :], h, _NT, preferred_element_type=f32)


def _qblockdiag(qa, qb, lower):
    z = jnp.zeros((HEAD_DIM, LANE), bf16)
    if lower is None:
        return jnp.concatenate([jnp.concatenate([qa, z], axis=1),
                                jnp.concatenate([z, qb], axis=1)], axis=0)
    top = jnp.concatenate([qa, qb], axis=1)
    zz = jnp.zeros((HEAD_DIM, 2 * LANE), bf16)
    return jnp.concatenate([top, zz] if lower == 0 else [zz, top], axis=0)


def _with_ones(vT, group):
    ones = jnp.ones((ONES_ROWS, vT.shape[1]), vT.dtype)
    parts = []
    for r in range(0, vT.shape[0], group):
        parts += [vT[r:r + group, :], ones]
    return jnp.concatenate(parts, axis=0)


def _softmax_pv(s, sink, vwin):
    m = jnp.max(s, axis=0, keepdims=True)
    if sink is not None:
        m = jnp.maximum(m, sink)
    e = jnp.exp2(s - m)
    o = jnp.dot(vwin, e.astype(bf16), preferred_element_type=f32)
    nv = vwin.shape[0] - ONES_ROWS
    den = o[nv:nv + 1, :]
    if sink is not None:
        den = den + jnp.exp2(sink - m)
    return o[:nv, :] * (1.0 / den)


def _mixers_bc(pt_sc, yT_sc, cols, kwin_b, vwinT_b, bias_b, sink_ref, kwin_c, vwinT_c, bias_c):
    def head_q(t0, hh):
        return (pt_sc[t0 + hh * HEAD_DIM:t0 + (hh + 1) * HEAD_DIM, cols] * (QK_SCALE * LOG2E)).astype(bf16)

    s_b, s_c = [], []
    for cb in range(B_HEADS // 2):
        ha, hb = 2 * cb, 2 * cb + 1
        qbd = _qblockdiag(head_q(T_BQ, ha), head_q(T_BQ, hb), ha // B_GROUP)
        s = jnp.dot(kwin_b, qbd, preferred_element_type=f32)
        s_b.append(s + bias_b[:, cb * 2 * LANE:(cb + 1) * 2 * LANE])
    for cb in range(C_HEADS // 2):
        ha, hb = 2 * cb, 2 * cb + 1
        qbd = _qblockdiag(head_q(T_CQ, ha), head_q(T_CQ, hb), None)
        s = jnp.dot(kwin_c[:, cb * LANE:(cb + 1) * LANE], qbd, preferred_element_type=f32)
        s_c.append(s + bias_c[:, cb * 2 * LANE:(cb + 1) * 2 * LANE])

    for cb in range(B_HEADS // 2):
        ha, hb = 2 * cb, 2 * cb + 1
        kv = ha // B_GROUP
        sink = sink_ref[:, cb * 2 * LANE:(cb + 1) * 2 * LANE] * LOG2E
        o = _softmax_pv(s_b[cb], sink, vwinT_b[kv * VB_STRIDE:(kv + 1) * VB_STRIDE, :])
        for j, hh in enumerate((ha, hb)):
            gate = jax.nn.silu(pt_sc[T_BZ + hh * HEAD_DIM:T_BZ + (hh + 1) * HEAD_DIM, cols])
            yT_sc[Y_B + hh * HEAD_DIM:Y_B + (hh + 1) * HEAD_DIM, cols] = (
                o[:, j * LANE:(j + 1) * LANE] * gate).astype(bf16)
    for cb in range(C_HEADS // 2):
        ha, hb = 2 * cb, 2 * cb + 1
        o = _softmax_pv(s_c[cb], None, vwinT_c[cb * VC_STRIDE:(cb + 1) * VC_STRIDE, :])
        for j, hh in enumerate((ha, hb)):
            gate = jax.nn.silu(pt_sc[T_CZ + hh * HEAD_DIM:T_CZ + (hh + 1) * HEAD_DIM, cols])
            yT_sc[Y_C + hh * HEAD_DIM:Y_C + (hh + 1) * HEAD_DIM, cols] = (
                o[j * HEAD_DIM:(j + 1) * HEAD_DIM, j * LANE:(j + 1) * LANE] * gate).astype(bf16)


def _finish(x, yT_sc, wout_ref, gpost_ref):
    out = lax.dot_general(yT_sc[...], wout_ref[...], _TN, preferred_element_type=f32)
    return x + _rms_rows(out, gpost_ref[...])


def _layer_spec(arr, layer):
    shape = arr.shape[1:]
    return pl.BlockSpec((None,) + shape, lambda *_: (layer,) + (0,) * len(shape),
                        pipeline_mode=pl.Buffered(1))


def _wcol_spec(w, layer, col0, width=MXU_N):
    assert col0 % width == 0
    return pl.BlockSpec((None, w.shape[1], width), lambda *_: (layer, 0, col0 // width),
                        pipeline_mode=pl.Buffered(1))


def _const_spec(arr):
    return pl.BlockSpec(arr.shape, lambda *_: (0,) * arr.ndim, pipeline_mode=pl.Buffered(1))


def _prompt_kernel(x_ref, gpre_ref, gpost_ref, wT_ref, wbkv_ref, wck_ref, wcv_ref, wout_ref,
                   ang_ref, awsT_ref, abs_ref, sink_ref, biasb_ref, biasc_ref,
                   xo_ref, bk_ref, bv_ref, ck_ref, cv_ref,
                   pt_sc, kb_ring, kc_ring, vbT_ring, vcT_ring, yT_sc):
    t = pl.program_id(1)
    ts = SEQ_TILE

    @pl.when(t == 0)
    def _():
        kb_ring[0:B_KEEP, :] = jnp.zeros((B_KEEP, B_KVW), bf16)
        kc_ring[0:C_KEEP, :] = jnp.zeros((C_KEEP, C_WIDTH), bf16)
        vbT_ring[:, 0:B_KEEP] = jnp.zeros((B_KV * VB_STRIDE, B_KEEP), bf16)
        vcT_ring[:, 0:C_KEEP] = jnp.zeros((C_HEADS // 2 * VC_STRIDE, C_KEEP), bf16)
        for g in range(B_KV):
            vbT_ring[g * VB_STRIDE + HEAD_DIM:(g + 1) * VB_STRIDE, :] = jnp.ones((ONES_ROWS, B_KEEP + ts), bf16)
        for g in range(C_HEADS // 2):
            vcT_ring[g * VC_STRIDE + 2 * HEAD_DIM:(g + 1) * VC_STRIDE, :] = jnp.ones((ONES_ROWS, C_KEEP + ts), bf16)

    x = x_ref[0]
    h = _rms_rows(x, gpre_ref[...]).astype(bf16)
    _project_T(pt_sc, wT_ref, h)
    kn_b = jnp.dot(h, wbkv_ref[:, 0:B_KVW], preferred_element_type=f32)
    kn_c = jnp.dot(h, wck_ref[...], preferred_element_type=f32)
    kb_ring[B_KEEP:, :] = kn_b.astype(bf16)
    kc_ring[C_KEEP:, :] = kn_c.astype(bf16)
    for g in range(B_KV):
        vbT_ring[g * VB_STRIDE:g * VB_STRIDE + HEAD_DIM, B_KEEP:] = (
            pt_sc[T_BV + g * HEAD_DIM:T_BV + (g + 1) * HEAD_DIM, :].astype(bf16))
    for g in range(C_HEADS // 2):
        vcT_ring[g * VC_STRIDE:g * VC_STRIDE + 2 * HEAD_DIM, C_KEEP:] = (
            pt_sc[T_CV + g * 2 * HEAD_DIM:T_CV + (g + 1) * 2 * HEAD_DIM, :].astype(bf16))

    @pl.when(t == pl.num_programs(1) - 1)
    def _():
        bk_ref[0] = kn_b[ts - B_KEEP:, :]
        ck_ref[0] = kn_c[ts - C_KEEP:, :]
        bv_ref[0] = jnp.dot(h[ts - B_KEEP:, :], wbkv_ref[:, B_KVW:], preferred_element_type=f32)
        cv_ref[0] = jnp.dot(h[ts - C_KEEP:, :], wcv_ref[...], preferred_element_type=f32)

    vn = _rms_cols(pt_sc[T_AV:T_AV + A_WIDTH, :], ang_ref[...]).astype(bf16)
    n_ac = ts // A_CHUNK
    row = lax.broadcasted_iota(jnp.int32, (A_CHUNK, A_CHUNK), 0)
    col = lax.broadcasted_iota(jnp.int32, (A_CHUNK, A_CHUNK), 1)
    for g in range(A_GROUPS):
        rows = slice(g * A_GD, (g + 1) * A_GD)
        wgt = jnp.where(row <= col, awsT_ref[g], 0.0).astype(bf16)
        lhs = jnp.concatenate([vn[rows, c * A_CHUNK:(c + 1) * A_CHUNK] for c in range(n_ac)], axis=0)
        mix = jnp.dot(lhs, wgt, preferred_element_type=f32) + abs_ref[g]
        for c in range(n_ac):
            cols = slice(c * A_CHUNK, (c + 1) * A_CHUNK)
            u = pt_sc[T_AU + g * A_GD:T_AU + (g + 1) * A_GD, cols]
            z = pt_sc[T_AZ + g * A_GD:T_AZ + (g + 1) * A_GD, cols]
            yT_sc[Y_A + g * A_GD:Y_A + (g + 1) * A_GD, cols] = (
                u * mix[c * A_GD:(c + 1) * A_GD, :] * jax.nn.silu(z)).astype(bf16)

    def pair_body(p, carry):
        pg = t * (ts // PAIR) + p
        start = pl.multiple_of(p * PAIR, PAIR)
        _mixers_bc(
            pt_sc, yT_sc, pl.ds(start, PAIR),
            kb_ring[pl.ds(start, B_WIN), :], vbT_ring[:, pl.ds(start, B_WIN)],
            biasb_ref.at[jnp.minimum(pg, 1)], sink_ref,
            kc_ring[pl.ds(start, C_WIN), :], vcT_ring[:, pl.ds(start, C_WIN)],
            biasc_ref.at[jnp.minimum(pg, C_PREV // 2)])
        return carry

    lax.fori_loop(0, ts // PAIR, pair_body, 0)

    xo_ref[0] = _finish(x, yT_sc, wout_ref, gpost_ref)

    kb_ring[0:B_KEEP, :] = kb_ring[ts:ts + B_KEEP, :]
    kc_ring[0:C_KEEP, :] = kc_ring[ts:ts + C_KEEP, :]
    vbT_ring[:, 0:B_KEEP] = vbT_ring[:, ts:ts + B_KEEP]
    vcT_ring[:, 0:C_KEEP] = vcT_ring[:, ts:ts + C_KEEP]


def _prompt_layer(layer, x, p):
    nb, seq, d = x.shape
    ts = SEQ_TILE
    assert seq % ts == 0 and ts == C_KEEP
    kv_spec = lambda keep, width: pl.BlockSpec((1, keep, width), lambda b, t: (b, 0, 0))
    return pl.pallas_call(
        _prompt_kernel,
        grid=(nb, seq // ts),
        in_specs=[
            pl.BlockSpec((1, ts, d), lambda b, t: (b, t, 0)),
            _layer_spec(p["gpre"], layer), _layer_spec(p["gpost"], layer),
            _layer_spec(p["wT"], layer),
            _wcol_spec(p["w"], layer, W_BKV), _wcol_spec(p["w"], layer, W_CK), _wcol_spec(p["w"], layer, W_CV),
            _layer_spec(p["wout"], layer),
            _layer_spec(p["ang"], layer), _layer_spec(p["awsT"], layer), _layer_spec(p["abs"], layer),
            _layer_spec(p["sink"], layer), _const_spec(p["bias_pb"]), _layer_spec(p["bias_pc"], layer),
        ],
        out_specs=[
            pl.BlockSpec((1, ts, d), lambda b, t: (b, t, 0)),
            kv_spec(B_KEEP, B_KVW), kv_spec(B_KEEP, B_KVW), kv_spec(C_KEEP, C_WIDTH), kv_spec(C_KEEP, C_WIDTH),
        ],
        out_shape=[
            jax.ShapeDtypeStruct((nb, seq, d), f32),
            jax.ShapeDtypeStruct((nb, B_KEEP, B_KVW), f32),
            jax.ShapeDtypeStruct((nb, B_KEEP, B_KVW), f32),
            jax.ShapeDtypeStruct((nb, C_KEEP, C_WIDTH), f32),
            jax.ShapeDtypeStruct((nb, C_KEEP, C_WIDTH), f32),
        ],
        scratch_shapes=[
            pltpu.VMEM((T_END, ts), f32),
            pltpu.VMEM((B_KEEP + ts, B_KVW), bf16),
            pltpu.VMEM((C_KEEP + ts, C_WIDTH), bf16),
            pltpu.VMEM((B_KV * VB_STRIDE, B_KEEP + ts), bf16),
            pltpu.VMEM((C_HEADS // 2 * VC_STRIDE, C_KEEP + ts), bf16),
            pltpu.VMEM((A_WIDTH + B_WIDTH + C_WIDTH, ts), bf16),
        ],
        compiler_params=pltpu.CompilerParams(dimension_semantics=("arbitrary", "arbitrary"),
                                             vmem_limit_bytes=VMEM_LIMIT),
        name="prompt_layer",
    )(x, p["gpre"], p["gpost"], p["wT"], p["w"], p["w"], p["w"], p["wout"], p["ang"], p["awsT"],
      p["abs"], p["sink"], p["bias_pb"], p["bias_pc"])


def _sample_kernel(x_ref, cbk_ref, cbv_ref, cck_ref, ccv_ref, gpre_ref, gpost_ref, wT_ref,
                   wbkv_ref, wck_ref, wcv_ref, wout_ref, ang_ref, aws_ref, arep_ref, akeep_ref, abias_ref,
                   sink_ref, biasb_ref, biasc_ref,
                   xo_ref, bk_ref, bv_ref, ck_ref, cv_ref, av_ref,
                   pt_sc, yT_sc):
    ns = SAMPLE_STREAMS_PER_BLOCK
    x = x_ref[...]
    ntok = x.shape[0]
    h = _rms_rows(x, gpre_ref[...]).astype(bf16)
    _project_T(pt_sc, wT_ref, h)
    kvn_b = jnp.dot(h, wbkv_ref[...], preferred_element_type=f32)
    kn_c = jnp.dot(h, wck_ref[...], preferred_element_type=f32)
    bk_ref[...] = kvn_b[:, 0:B_KVW]
    bv_ref[...] = kvn_b[:, B_KVW:]
    ck_ref[...] = kn_c
    cv_ref[...] = jnp.dot(h, wcv_ref[...], preferred_element_type=f32)

    vn = _rms_cols(pt_sc[T_AV:T_AV + A_WIDTH, :], ang_ref[...])
    av_ref[...] = vn.T
    vnb = vn.astype(bf16)
    keep = akeep_ref[...] > 0.0
    rep = arep_ref[...]
    for g in range(A_GROUPS):
        rows = slice(g * A_GD, (g + 1) * A_GD)
        tiled = lax.dot_general(jnp.dot(rep, aws_ref[g].astype(bf16), preferred_element_type=f32).astype(bf16),
                                rep, _NT, preferred_element_type=f32)
        wgt = jnp.where(keep, tiled, 0.0).astype(bf16)
        mix = jnp.dot(vnb[rows, :], wgt, preferred_element_type=f32) + abias_ref[g]
        u = pt_sc[T_AU + g * A_GD:T_AU + (g + 1) * A_GD, :]
        z = pt_sc[T_AZ + g * A_GD:T_AZ + (g + 1) * A_GD, :]
        yT_sc[Y_A + g * A_GD:Y_A + (g + 1) * A_GD, :] = (u * mix * jax.nn.silu(z)).astype(bf16)

    for blk in range(ntok // LANE):
        cols = slice(blk * LANE, (blk + 1) * LANE)
        streams = range(blk * ns, (blk + 1) * ns)
        kwin_b = jnp.concatenate([cbk_ref[s].T.astype(bf16) for s in streams]
                                 + [kvn_b[cols, 0:B_KVW].astype(bf16)], axis=0)
        vwinT_b = jnp.concatenate([cbv_ref[s].astype(bf16) for s in streams]
                                  + [pt_sc[T_BV:T_BV + B_KVW, cols].astype(bf16)], axis=1)
        kwin_c = jnp.concatenate([cck_ref[s].T.astype(bf16) for s in streams]
                                 + [kn_c[cols, :].astype(bf16)], axis=0)
        vwinT_c = jnp.concatenate([ccv_ref[s].astype(bf16) for s in streams]
                                  + [pt_sc[T_CV:T_CV + C_WIDTH, cols].astype(bf16)], axis=1)
        _mixers_bc(pt_sc, yT_sc, cols, kwin_b, _with_ones(vwinT_b, HEAD_DIM), biasb_ref, sink_ref,
                   kwin_c, _with_ones(vwinT_c, 2 * HEAD_DIM), biasc_ref)

    xo_ref[...] = _finish(x, yT_sc, wout_ref, gpost_ref)


def _sample_layer(layer, x, caches, p, *, t_new, tok_per_step):
    ntok, d = x.shape
    spb = tok_per_step // t_new
    cbk, cbv, cck, ccv = caches

    def rows(width):
        return pl.BlockSpec((tok_per_step, width), lambda i: (i, 0))

    def cache(c):
        return pl.BlockSpec((None, spb) + c.shape[2:], lambda i: (layer, i, 0, 0))

    return pl.pallas_call(
        _sample_kernel,
        grid=(ntok // tok_per_step,),
        in_specs=[
            rows(d), cache(cbk), cache(cbv), cache(cck), cache(ccv),
            _layer_spec(p["gpre"], layer), _layer_spec(p["gpost"], layer),
            _layer_spec(p["wT"], layer),
            _wcol_spec(p["w"], layer, W_BKV), _wcol_spec(p["w"], layer, W_CK), _wcol_spec(p["w"], layer, W_CV),
            _layer_spec(p["wout"], layer),
            _layer_spec(p["ang"], layer), _layer_spec(p["aws_s"], layer), _const_spec(p["arep"]),
            _const_spec(p["akeep"]), _layer_spec(p["abias_s"], layer),
            _layer_spec(p["sink"], layer), _const_spec(p["bias_sb"]), _layer_spec(p["bias_sc"], layer),
        ],
        out_specs=[rows(d), rows(B_KVW), rows(B_KVW), rows(C_WIDTH), rows(C_WIDTH), rows(A_WIDTH)],
        out_shape=[
            jax.ShapeDtypeStruct((ntok, d), f32),
            jax.ShapeDtypeStruct((ntok, B_KVW), f32), jax.ShapeDtypeStruct((ntok, B_KVW), f32),
            jax.ShapeDtypeStruct((ntok, C_WIDTH), f32), jax.ShapeDtypeStruct((ntok, C_WIDTH), f32),
            jax.ShapeDtypeStruct((ntok, A_WIDTH), f32),
        ],
        scratch_shapes=[
            pltpu.VMEM((T_END, tok_per_step), f32),
            pltpu.VMEM((A_WIDTH + B_WIDTH + C_WIDTH, tok_per_step), bf16),
        ],
        compiler_params=pltpu.CompilerParams(dimension_semantics=("arbitrary",),
                                             vmem_limit_bytes=VMEM_LIMIT),
        name="sample_layer",
    )(x, cbk, cbv, cck, ccv, p["gpre"], p["gpost"], p["wT"], p["w"], p["w"], p["w"], p["wout"],
      p["ang"], p["aws_s"], p["arep"], p["akeep"], p["abias_s"], p["sink"], p["bias_sb"], p["bias_sc"])


def kernel(x_prompt, x_sample, cache_b_k, cache_b_v, cache_c_k, cache_c_v, g_pre, g_post, w_in, w_out,
           a_norm_g, a_ws, a_bs, b_sinks, c_rel_bias, t5_bias):
    depth = w_in.shape[0]
    nb, seq, d = x_prompt.shape
    ns_all, t_new, _ = x_sample.shape
    lc_b, lc_c = cache_b_k.shape[2], cache_c_k.shape[2]
    assert lc_b == B_KEEP and lc_c == C_KEEP and LANE % t_new == 0
    assert SAMPLE_STREAMS_PER_BLOCK * t_new == LANE
    tok_per_step = 2 * LANE
    reps = tok_per_step // t_new

    rel_pb, ok_pb, idx_pc = _prompt_maps()
    rel_sb, ok_sb, idx_sc = _sample_maps(lc_b, lc_c, t_new)
    idx_pb = jnp.where(ok_pb, _t5_bucket(jnp.asarray(rel_pb)), -1).astype(jnp.int32)
    idx_sb = jnp.where(ok_sb, _t5_bucket(jnp.asarray(rel_sb)), -1).astype(jnp.int32)
    t5_tab = t5_bias[None]
    full_t5 = [(0, T5_BUCKETS - 1)]
    r_ = np.arange(tok_per_step)
    awsT = a_ws.transpose(0, 1, 3, 2)

    p = dict(
        bias_pb=_build_bias(t5_tab, idx_pb, ranges=full_t5 * (B_WIN // CHUNK), kmins=(B_PREV, 0))[0],
        bias_sb=_build_bias(t5_tab, idx_sb, ranges=full_t5 * (idx_sb.shape[0] // CHUNK), kmins=(0,))[0, 0],
        bias_pc=_build_bias(c_rel_bias, jnp.asarray(idx_pc), ranges=_static_ranges(idx_pc),
                            kmins=tuple(max(C_PREV - 2 * v, 0) for v in range(C_PREV // 2 + 1))),
        bias_sc=_build_bias(c_rel_bias, jnp.asarray(idx_sc), ranges=_static_ranges(idx_sc),
                            kmins=(0,))[:, 0],
        w=jnp.concatenate([w_in[:, :, T_BK:T_BZ], w_in[:, :, T_CK:T_CZ]], axis=2).astype(bf16),
        wT=w_in.astype(bf16).transpose(0, 2, 1),
        wout=w_out.astype(bf16),
        gpre=g_pre[:, None, :],
        gpost=g_post[:, None, :],
        ang=a_norm_g[:, :, None],
        awsT=awsT,
        abs=a_bs[:, :, None, :],
        sink=jnp.repeat(b_sinks, LANE, axis=1)[:, None, :],
        aws_s=awsT[:, :, :t_new, :t_new],
        arep=jnp.asarray((r_[:, None] % t_new == np.arange(t_new)[None, :]).astype(np.float32), bf16),
        abias_s=jnp.tile(a_bs[:, :, None, :t_new], (1, 1, 1, reps)),
        akeep=jnp.asarray(((r_[:, None] // t_new == r_[None, :] // t_new)
                           & (r_[:, None] % t_new <= r_[None, :] % t_new)).astype(np.float32)),
    )

    xp = x_prompt
    xs = x_sample.reshape(ns_all * t_new, d)
    caches = tuple(c.transpose(0, 1, 3, 4, 2).reshape(depth, ns_all, -1, c.shape[2])
                   for c in (cache_b_k, cache_b_v, cache_c_k, cache_c_v))

    pk, pv, pck, pcv = [], [], [], []
    sk, sv, sck, scv, sav = [], [], [], [], []
    for l in range(depth):
        xp, bk, bv, ck, cv = _prompt_layer(l, xp, p)
        pk.append(bk); pv.append(bv); pck.append(ck); pcv.append(cv)
        xs, bk, bv, ck, cv, av = _sample_layer(l, xs, caches, p, t_new=t_new, tok_per_step=tok_per_step)
        sk.append(bk); sv.append(bv); sck.append(ck); scv.append(cv); sav.append(av)

    def stk(xs_, lead, heads):
        return jnp.stack(xs_).reshape(depth, lead, -1, heads, HEAD_DIM)

    return (xp, xs.reshape(ns_all, t_new, d),
            stk(pk, nb, B_KV), stk(pv, nb, B_KV), stk(pck, nb, C_HEADS), stk(pcv, nb, C_HEADS),
            stk(sk, ns_all, B_KV), stk(sv, ns_all, B_KV), stk(sck, ns_all, C_HEADS), stk(scv, ns_all, C_HEADS),
            jnp.stack(sav).reshape(depth, ns_all, t_new, A_WIDTH))
```

```python
import functools
import math

import numpy as np
import jax
import jax.numpy as jnp
from jax import lax
from jax.experimental import pallas as pl
from jax.experimental.pallas import tpu as pltpu

bf16 = jnp.bfloat16
f32 = jnp.float32

HEAD_DIM = 64
CHUNK = 64
A_WIDTH, A_GROUPS, A_CHUNK = 256, 4, 128
A_GD = A_WIDTH // A_GROUPS
B_HEADS, B_KV = 8, 2
B_GROUP = B_HEADS // B_KV
B_WIDTH, B_KVW = B_HEADS * HEAD_DIM, B_KV * HEAD_DIM
B_PREV = 2
C_HEADS = 4
C_WIDTH = C_HEADS * HEAD_DIM
C_PREV = 8
C_CLIP = 128
T5_BUCKETS, T5_MAX_DIST = 32, 128
RMS_EPS = 1e-6
NEG_INF = -1e30
QK_SCALE = HEAD_DIM ** -0.5
LOG2E = math.log2(math.e)

LANE = 128
MXU_N = 256
PAIR = 2 * CHUNK
B_WIN = (B_PREV + 2) * CHUNK
C_WIN = (C_PREV + 2) * CHUNK
B_KEEP = B_PREV * CHUNK
C_KEEP = C_PREV * CHUNK
ONES_ROWS = 16
VB_STRIDE = HEAD_DIM + ONES_ROWS
VC_STRIDE = 2 * HEAD_DIM + ONES_ROWS
SEQ_TILE = 512
SAMPLE_STREAMS_PER_BLOCK = 4
VMEM_LIMIT = 56 * 1024 * 1024
LUT_UNROLL = 8

_IN_SIZES = [A_WIDTH, A_WIDTH, A_WIDTH, B_WIDTH, B_KVW, B_KVW, B_WIDTH, C_WIDTH, C_WIDTH, C_WIDTH, C_WIDTH]
_IN_OFF = [int(v) for v in np.cumsum([0] + _IN_SIZES)]
(T_AU, T_AV, T_AZ, T_BQ, T_BK, T_BV, T_BZ, T_CQ, T_CK, T_CV, T_CZ, T_END) = _IN_OFF
_T_SECTIONS = ((T_AU, T_BK), (T_BV, T_CK), (T_CV, T_END))
Y_A, Y_B, Y_C = 0, A_WIDTH, A_WIDTH + B_WIDTH

_NT = (((1,), (1,)), ((), ()))
_TN = (((0,), (0,)), ((), ()))


def _t5_bucket(rel):
    half = T5_BUCKETS // 2
    max_exact = half // 2
    ret = jnp.where(rel > 0, half, 0)
    n = jnp.abs(rel)
    nf = jnp.maximum(n, 1).astype(jnp.float32)
    large = max_exact + (jnp.log(nf / max_exact) / math.log(T5_MAX_DIST / max_exact)
                         * (half - max_exact)).astype(jnp.int32)
    large = jnp.minimum(large, half - 1)
    return ret + jnp.where(n < max_exact, n, large)


def _lut_kernel(tab_ref, idx_ref, out_ref, *, n_entries, n_heads, ranges, kmins):
    layer = pl.program_id(0)
    neg = jnp.full((CHUNK, LANE), NEG_INF, f32)
    hgroup = min(n_heads, 4)
    for rb, (lo, hi) in enumerate(ranges):
        rows = slice(rb * CHUNK, (rb + 1) * CHUNK)
        idx = idx_ref[rows, :]
        for h0 in range(0, n_heads, hgroup):
            if hi < lo:
                base = (neg,) * hgroup
            else:
                def body(e, accs, h0=h0, idx=idx):
                    hit = idx == e
                    off = (layer * n_entries + e) * n_heads + h0
                    return tuple(jnp.where(hit, tab_ref[off + j], a) for j, a in enumerate(accs))
                base = lax.fori_loop(lo, hi + 1, body, (neg,) * hgroup,
                                     unroll=min(LUT_UNROLL, hi + 1 - lo))
            for j in range(hgroup):
                scaled = base[j] * LOG2E
                for v, kmin in enumerate(kmins):
                    out_ref[0, v, rows, (h0 + j) * LANE:(h0 + j + 1) * LANE] = neg if rb < kmin else scaled


def _build_bias(table, idx, *, ranges, kmins):
    n_layers, n_entries, n_heads = table.shape
    n_rows = idx.shape[0]
    kern = functools.partial(_lut_kernel, n_entries=n_entries, n_heads=n_heads,
                             ranges=tuple(ranges), kmins=tuple(kmins))
    return pl.pallas_call(
        kern,
        grid=(n_layers,),
        in_specs=[pl.BlockSpec(memory_space=pltpu.SMEM),
                  pl.BlockSpec((n_rows, LANE), lambda l: (0, 0))],
        out_specs=pl.BlockSpec((1, len(kmins), n_rows, n_heads * LANE), lambda l: (l, 0, 0, 0)),
        out_shape=jax.ShapeDtypeStruct((n_layers, len(kmins), n_rows, n_heads * LANE), f32),
        compiler_params=pltpu.CompilerParams(dimension_semantics=("arbitrary",),
                                             vmem_limit_bytes=VMEM_LIMIT),
        name="bias_lut",
    )(table.reshape(-1), idx)


def _static_ranges(idx_np):
    out = []
    for rb in range(idx_np.shape[0] // CHUNK):
        blk = idx_np[rb * CHUNK:(rb + 1) * CHUNK]
        good = blk[blk >= 0]
        out.append((int(good.min()), int(good.max())) if good.size else (0, -1))
    return out


def _prompt_maps():
    i = np.arange(PAIR)[None, :]
    qi = i // CHUNK
    jb = np.arange(B_WIN)[:, None]
    rel_b = jb - B_KEEP - i
    ok_b = (jb // CHUNK >= qi) & (jb // CHUNK <= qi + B_PREV)
    jc = np.arange(C_WIN)[:, None]
    rel_c = jc - C_KEEP - i
    ok_c = (jc // CHUNK >= qi) & (jc // CHUNK <= qi + C_PREV)
    idx_c = np.where(ok_c, np.clip(rel_c, -C_CLIP, C_CLIP) + C_CLIP, -1).astype(np.int32)
    return rel_b.astype(np.int32), ok_b, idx_c


def _sample_maps(lc_b, lc_c, t_new):
    ns = SAMPLE_STREAMS_PER_BLOCK
    c = np.arange(LANE)[None, :]
    sq, i = c // t_new, c % t_new

    def one(lc):
        r = np.arange(ns * lc + LANE)[:, None]
        is_new = r >= ns * lc
        sk = np.where(is_new, (r - ns * lc) // t_new, r // lc)
        j = np.where(is_new, lc + (r - ns * lc) % t_new, r % lc)
        return (j - lc - i).astype(np.int32), (sk == sq)

    rel_b, ok_b = one(lc_b)
    rel_c, ok_c = one(lc_c)
    idx_c = np.where(ok_c, np.clip(rel_c, -C_CLIP, C_CLIP) + C_CLIP, -1).astype(np.int32)
    return rel_b, ok_b, idx_c


def _rms_rows(x, g):
    ms = jnp.mean(x * x, axis=-1, keepdims=True)
    return x * lax.rsqrt(ms + RMS_EPS) * g


def _rms_cols(xT, g_col):
    ms = jnp.mean(xT * xT, axis=0, keepdims=True)
    return xT * lax.rsqrt(ms + RMS_EPS) * g_col


def _project_T(pt_sc, wT_ref, h):
    for r0, r1 in _T_SECTIONS:
        pt_sc[r0:r1, :] = lax.dot_general(wT_ref[r0:r1, :], h, _NT, preferred_element_type=f32)


def _qblockdiag(qa, qb, lower):
    z = jnp.zeros((HEAD_DIM, LANE), bf16)
    if lower is None:
        return jnp.concatenate([jnp.concatenate([qa, z], axis=1),
                                jnp.concatenate([z, qb], axis=1)], axis=0)
    top = jnp.concatenate([qa, qb], axis=1)
    zz = jnp.zeros((HEAD_DIM, 2 * LANE), bf16)
    return jnp.concatenate([top, zz] if lower == 0 else [zz, top], axis=0)


def _with_ones(vT, group):
    ones = jnp.ones((ONES_ROWS, vT.shape[1]), vT.dtype)
    parts = []
    for r in range(0, vT.shape[0], group):
        parts += [vT[r:r + group, :], ones]
    return jnp.concatenate(parts, axis=0)


def _softmax_pv(s, sink, vwin):
    m = jnp.max(s, axis=0, keepdims=True)
    if sink is not None:
        m = jnp.maximum(m, sink)
    e = jnp.exp2(s - m)
    o = jnp.dot(vwin, e.astype(bf16), preferred_element_type=f32)
    nv = vwin.shape[0] - ONES_ROWS
    den = o[nv:nv + 1, :]
    if sink is not None:
        den = den + jnp.exp2(sink - m)
    return o[:nv, :] * (1.0 / den)


def _mixers_bc(pt_sc, yT_sc, cols, kwin_b, vwinT_b, bias_b, sink_ref, kwin_c, vwinT_c, bias_c):
    def head_q(t0, hh):
        return (pt_sc[t0 + hh * HEAD_DIM:t0 + (hh + 1) * HEAD_DIM, cols] * (QK_SCALE * LOG2E)).astype(bf16)

    s_b, s_c = [], []
    for cb in range(B_HEADS // 2):
        ha, hb = 2 * cb, 2 * cb + 1
        qbd = _qblockdiag(head_q(T_BQ, ha), head_q(T_BQ, hb), ha // B_GROUP)
        s = jnp.dot(kwin_b, qbd, preferred_element_type=f32)
        s_b.append(s + bias_b[:, cb * 2 * LANE:(cb + 1) * 2 * LANE])
    for cb in range(C_HEADS // 2):
        ha, hb = 2 * cb, 2 * cb + 1
        qbd = _qblockdiag(head_q(T_CQ, ha), head_q(T_CQ, hb), None)
        s = jnp.dot(kwin_c[:, cb * LANE:(cb + 1) * LANE], qbd, preferred_element_type=f32)
        s_c.append(s + bias_c[:, cb * 2 * LANE:(cb + 1) * 2 * LANE])

    for cb in range(B_HEADS // 2):
        ha, hb = 2 * cb, 2 * cb + 1
        kv = ha // B_GROUP
        sink = sink_ref[:, cb * 2 * LANE:(cb + 1) * 2 * LANE] * LOG2E
        o = _softmax_pv(s_b[cb], sink, vwinT_b[kv * VB_STRIDE:(kv + 1) * VB_STRIDE, :])
        for j, hh in enumerate((ha, hb)):
            gate = jax.nn.silu(pt_sc[T_BZ + hh * HEAD_DIM:T_BZ + (hh + 1) * HEAD_DIM, cols])
            yT_sc[Y_B + hh * HEAD_DIM:Y_B + (hh + 1) * HEAD_DIM, cols] = (
                o[:, j * LANE:(j + 1) * LANE] * gate).astype(bf16)
    for cb in range(C_HEADS // 2):
        ha, hb = 2 * cb, 2 * cb + 1
        o = _softmax_pv(s_c[cb], None, vwinT_c[cb * VC_STRIDE:(cb + 1) * VC_STRIDE, :])
        for j, hh in enumerate((ha, hb)):
            gate = jax.nn.silu(pt_sc[T_CZ + hh * HEAD_DIM:T_CZ + (hh + 1) * HEAD_DIM, cols])
            yT_sc[Y_C + hh * HEAD_DIM:Y_C + (hh + 1) * HEAD_DIM, cols] = (
                o[j * HEAD_DIM:(j + 1) * HEAD_DIM, j * LANE:(j + 1) * LANE] * gate).astype(bf16)


def _finish(x, yT_sc, wout_ref, gpost_ref):
    out = lax.dot_general(yT_sc[...], wout_ref[...], _TN, preferred_element_type=f32)
    return x + _rms_rows(out, gpost_ref[...])


def _layer_spec(arr, layer):
    shape = arr.shape[1:]
    return pl.BlockSpec((None,) + shape, lambda *_: (layer,) + (0,) * len(shape),
                        pipeline_mode=pl.Buffered(1))


def _const_spec(arr):
    return pl.BlockSpec(arr.shape, lambda *_: (0,) * arr.ndim, pipeline_mode=pl.Buffered(1))


def _prompt_kernel(x_ref, gpre_ref, gpost_ref, wT_ref, wout_ref,
                   ang_ref, awsT_ref, abs_ref, sink_ref, biasb_ref, biasc_ref,
                   xo_ref, bk_ref, bv_ref, ck_ref, cv_ref,
                   pt_sc, kb_ring, kc_ring, vbT_ring, vcT_ring, yT_sc):
    t = pl.program_id(1)
    ts = SEQ_TILE

    @pl.when(t == 0)
    def _():
        kb_ring[0:B_KEEP, :] = jnp.zeros((B_KEEP, B_KVW), bf16)
        kc_ring[0:C_KEEP, :] = jnp.zeros((C_KEEP, C_WIDTH), bf16)
        vbT_ring[:, 0:B_KEEP] = jnp.zeros((B_KV * VB_STRIDE, B_KEEP), bf16)
        vcT_ring[:, 0:C_KEEP] = jnp.zeros((C_HEADS // 2 * VC_STRIDE, C_KEEP), bf16)
        for g in range(B_KV):
            vbT_ring[g * VB_STRIDE + HEAD_DIM:(g + 1) * VB_STRIDE, :] = jnp.ones((ONES_ROWS, B_KEEP + ts), bf16)
        for g in range(C_HEADS // 2):
            vcT_ring[g * VC_STRIDE + 2 * HEAD_DIM:(g + 1) * VC_STRIDE, :] = jnp.ones((ONES_ROWS, C_KEEP + ts), bf16)

    x = x_ref[0]
    h = _rms_rows(x, gpre_ref[...]).astype(bf16)
    _project_T(pt_sc, wT_ref, h)
    kb_ring[B_KEEP:, :] = lax.dot_general(h, wT_ref[T_BK:T_BV, :], _NT,
                                          preferred_element_type=f32).astype(bf16)
    kc_ring[C_KEEP:, :] = lax.dot_general(h, wT_ref[T_CK:T_CV, :], _NT,
                                          preferred_element_type=f32).astype(bf16)
    for g in range(B_KV):
        vbT_ring[g * VB_STRIDE:g * VB_STRIDE + HEAD_DIM, B_KEEP:] = (
            pt_sc[T_BV + g * HEAD_DIM:T_BV + (g + 1) * HEAD_DIM, :].astype(bf16))
    for g in range(C_HEADS // 2):
        vcT_ring[g * VC_STRIDE:g * VC_STRIDE + 2 * HEAD_DIM, C_KEEP:] = (
            pt_sc[T_CV + g * 2 * HEAD_DIM:T_CV + (g + 1) * 2 * HEAD_DIM, :].astype(bf16))

    @pl.when(t == pl.num_programs(1) - 1)
    def _():
        bk_ref[0] = lax.dot_general(wT_ref[T_BK:T_BV, :], h[ts - B_KEEP:, :], _NT, preferred_element_type=f32)
        ck_ref[0] = lax.dot_general(wT_ref[T_CK:T_CV, :], h[ts - C_KEEP:, :], _NT, preferred_element_type=f32)
        bv_ref[0] = pt_sc[T_BV:T_BZ, ts - B_KEEP:]
        cv_ref[0] = pt_sc[T_CV:T_CZ, ts - C_KEEP:]

    vn = _rms_cols(pt_sc[T_AV:T_AV + A_WIDTH, :], ang_ref[...]).astype(bf16)
    n_ac = ts // A_CHUNK
    row = lax.broadcasted_iota(jnp.int32, (A_CHUNK, A_CHUNK), 0)
    col = lax.broadcasted_iota(jnp.int32, (A_CHUNK, A_CHUNK), 1)
    for g in range(A_GROUPS):
        rows = slice(g * A_GD, (g + 1) * A_GD)
        wgt = jnp.where(row <= col, awsT_ref[g], 0.0).astype(bf16)
        lhs = jnp.concatenate([vn[rows, c * A_CHUNK:(c + 1) * A_CHUNK] for c in range(n_ac)], axis=0)
        mix = jnp.dot(lhs, wgt, preferred_element_type=f32) + abs_ref[g]
        for c in range(n_ac):
            cols = slice(c * A_CHUNK, (c + 1) * A_CHUNK)
            u = pt_sc[T_AU + g * A_GD:T_AU + (g + 1) * A_GD, cols]
            z = pt_sc[T_AZ + g * A_GD:T_AZ + (g + 1) * A_GD, cols]
            yT_sc[Y_A + g * A_GD:Y_A + (g + 1) * A_GD, cols] = (
                u * mix[c * A_GD:(c + 1) * A_GD, :] * jax.nn.silu(z)).astype(bf16)

    def pair_body(p, carry):
        pg = t * (ts // PAIR) + p
        start = pl.multiple_of(p * PAIR, PAIR)
        _mixers_bc(
            pt_sc, yT_sc, pl.ds(start, PAIR),
            kb_ring[pl.ds(start, B_WIN), :], vbT_ring[:, pl.ds(start, B_WIN)],
            biasb_ref.at[jnp.minimum(pg, 1)], sink_ref,
            kc_ring[pl.ds(start, C_WIN), :], vcT_ring[:, pl.ds(start, C_WIN)],
            biasc_ref.at[jnp.minimum(pg, C_PREV // 2)])
        return carry

    lax.fori_loop(0, ts // PAIR, pair_body, 0)

    xo_ref[0] = _finish(x, yT_sc, wout_ref, gpost_ref)

    kb_ring[0:B_KEEP, :] = kb_ring[ts:ts + B_KEEP, :]
    kc_ring[0:C_KEEP, :] = kc_ring[ts:ts + C_KEEP, :]
    vbT_ring[:, 0:B_KEEP] = vbT_ring[:, ts:ts + B_KEEP]
    vcT_ring[:, 0:C_KEEP] = vcT_ring[:, ts:ts + C_KEEP]


def _prompt_layer(layer, x, p):
    nb, seq, d = x.shape
    ts = SEQ_TILE
    assert seq % ts == 0 and ts == C_KEEP
    kv_spec = lambda keep, width: pl.BlockSpec((1, width, keep), lambda b, t: (b, 0, 0))
    return pl.pallas_call(
        _prompt_kernel,
        grid=(nb, seq // ts),
        in_specs=[
            pl.BlockSpec((1, ts, d), lambda b, t: (b, t, 0)),
            _layer_spec(p["gpre"], layer), _layer_spec(p["gpost"], layer),
            _layer_spec(p["wT"], layer),
            _layer_spec(p["wout"], layer),
            _layer_spec(p["ang"], layer), _layer_spec(p["awsT"], layer), _layer_spec(p["abs"], layer),
            _layer_spec(p["sink"], layer), _const_spec(p["bias_pb"]), _layer_spec(p["bias_pc"], layer),
        ],
        out_specs=[
            pl.BlockSpec((1, ts, d), lambda b, t: (b, t, 0)),
            kv_spec(B_KEEP, B_KVW), kv_spec(B_KEEP, B_KVW), kv_spec(C_KEEP, C_WIDTH), kv_spec(C_KEEP, C_WIDTH),
        ],
        out_shape=[
            jax.ShapeDtypeStruct((nb, seq, d), f32),
            jax.ShapeDtypeStruct((nb, B_KVW, B_KEEP), f32),
            jax.ShapeDtypeStruct((nb, B_KVW, B_KEEP), f32),
            jax.ShapeDtypeStruct((nb, C_WIDTH, C_KEEP), f32),
            jax.ShapeDtypeStruct((nb, C_WIDTH, C_KEEP), f32),
        ],
        scratch_shapes=[
            pltpu.VMEM((T_END, ts), f32),
            pltpu.VMEM((B_KEEP + ts, B_KVW), bf16),
            pltpu.VMEM((C_KEEP + ts, C_WIDTH), bf16),
            pltpu.VMEM((B_KV * VB_STRIDE, B_KEEP + ts), bf16),
            pltpu.VMEM((C_HEADS // 2 * VC_STRIDE, C_KEEP + ts), bf16),
            pltpu.VMEM((A_WIDTH + B_WIDTH + C_WIDTH, ts), bf16),
        ],
        compiler_params=pltpu.CompilerParams(dimension_semantics=("arbitrary", "arbitrary"),
                                             vmem_limit_bytes=VMEM_LIMIT),
        name="prompt_layer",
    )(x, p["gpre"], p["gpost"], p["wT"], p["wout"], p["ang"], p["awsT"],
      p["abs"], p["sink"], p["bias_pb"], p["bias_pc"])


def _sample_kernel(x_ref, cbk_ref, cbv_ref, cck_ref, ccv_ref, gpre_ref, gpost_ref, wT_ref,
                   wout_ref, ang_ref, aws_ref, arep_ref, akeep_ref, abias_ref,
                   sink_ref, biasb_ref, biasc_ref,
                   xo_ref, bk_ref, bv_ref, ck_ref, cv_ref, av_ref,
                   pt_sc, yT_sc):
    ns = SAMPLE_STREAMS_PER_BLOCK
    x = x_ref[...]
    ntok = x.shape[0]
    h = _rms_rows(x, gpre_ref[...]).astype(bf16)
    _project_T(pt_sc, wT_ref, h)
    kvn_b = lax.dot_general(h, wT_ref[T_BK:T_BZ, :], _NT, preferred_element_type=f32)
    kvn_c = lax.dot_general(h, wT_ref[T_CK:T_CZ, :], _NT, preferred_element_type=f32)
    kn_c = kvn_c[:, 0:C_WIDTH]
    bk_ref[...] = kvn_b[:, 0:B_KVW]
    bv_ref[...] = kvn_b[:, B_KVW:]
    ck_ref[...] = kn_c
    cv_ref[...] = kvn_c[:, C_WIDTH:]

    vn = _rms_cols(pt_sc[T_AV:T_AV + A_WIDTH, :], ang_ref[...])
    av_ref[...] = vn.T
    vnb = vn.astype(bf16)
    keep = akeep_ref[...] > 0.0
    rep = arep_ref[...]
    for g in range(A_GROUPS):
        rows = slice(g * A_GD, (g + 1) * A_GD)
        tiled = lax.dot_general(jnp.dot(rep, aws_ref[g].astype(bf16), preferred_element_type=f32).astype(bf16),
                                rep, _NT, preferred_element_type=f32)
        wgt = jnp.where(keep, tiled, 0.0).astype(bf16)
        mix = jnp.dot(vnb[rows, :], wgt, preferred_element_type=f32) + abias_ref[g]
        u = pt_sc[T_AU + g * A_GD:T_AU + (g + 1) * A_GD, :]
        z = pt_sc[T_AZ + g * A_GD:T_AZ + (g + 1) * A_GD, :]
        yT_sc[Y_A + g * A_GD:Y_A + (g + 1) * A_GD, :] = (u * mix * jax.nn.silu(z)).astype(bf16)

    for blk in range(ntok // LANE):
        cols = slice(blk * LANE, (blk + 1) * LANE)
        streams = range(blk * ns, (blk + 1) * ns)
        kwin_b = jnp.concatenate([cbk_ref[s].T.astype(bf16) for s in streams]
                                 + [kvn_b[cols, 0:B_KVW].astype(bf16)], axis=0)
        vwinT_b = jnp.concatenate([cbv_ref[s].astype(bf16) for s in streams]
                                  + [pt_sc[T_BV:T_BV + B_KVW, cols].astype(bf16)], axis=1)
        kwin_c = jnp.concatenate([cck_ref[s].T.astype(bf16) for s in streams]
                                 + [kn_c[cols, :].astype(bf16)], axis=0)
        vwinT_c = jnp.concatenate([ccv_ref[s].astype(bf16) for s in streams]
                                  + [pt_sc[T_CV:T_CV + C_WIDTH, cols].astype(bf16)], axis=1)
        _mixers_bc(pt_sc, yT_sc, cols, kwin_b, _with_ones(vwinT_b, HEAD_DIM), biasb_ref, sink_ref,
                   kwin_c, _with_ones(vwinT_c, 2 * HEAD_DIM), biasc_ref)

    xo_ref[...] = _finish(x, yT_sc, wout_ref, gpost_ref)


def _sample_layer(layer, x, caches, p, *, t_new, tok_per_step):
    ntok, d = x.shape
    spb = tok_per_step // t_new
    cbk, cbv, cck, ccv = caches

    def rows(width):
        return pl.BlockSpec((tok_per_step, width), lambda i: (i, 0))

    def cache(c):
        return pl.BlockSpec((None, spb) + c.shape[2:], lambda i: (layer, i, 0, 0))

    return pl.pallas_call(
        _sample_kernel,
        grid=(ntok // tok_per_step,),
        in_specs=[
            rows(d), cache(cbk), cache(cbv), cache(cck), cache(ccv),
            _layer_spec(p["gpre"], layer), _layer_spec(p["gpost"], layer),
            _layer_spec(p["wT"], layer),
            _layer_spec(p["wout"], layer),
            _layer_spec(p["ang"], layer), _layer_spec(p["aws_s"], layer), _const_spec(p["arep"]),
            _const_spec(p["akeep"]), _layer_spec(p["abias_s"], layer),
            _layer_spec(p["sink"], layer), _const_spec(p["bias_sb"]), _layer_spec(p["bias_sc"], layer),
        ],
        out_specs=[rows(d), rows(B_KVW), rows(B_KVW), rows(C_WIDTH), rows(C_WIDTH), rows(A_WIDTH)],
        out_shape=[
            jax.ShapeDtypeStruct((ntok, d), f32),
            jax.ShapeDtypeStruct((ntok, B_KVW), f32), jax.ShapeDtypeStruct((ntok, B_KVW), f32),
            jax.ShapeDtypeStruct((ntok, C_WIDTH), f32), jax.ShapeDtypeStruct((ntok, C_WIDTH), f32),
            jax.ShapeDtypeStruct((ntok, A_WIDTH), f32),
        ],
        scratch_shapes=[
            pltpu.VMEM((T_END, tok_per_step), f32),
            pltpu.VMEM((A_WIDTH + B_WIDTH + C_WIDTH, tok_per_step), bf16),
        ],
        compiler_params=pltpu.CompilerParams(dimension_semantics=("arbitrary",),
                                             vmem_limit_bytes=VMEM_LIMIT),
        name="sample_layer",
    )(x, cbk, cbv, cck, ccv, p["gpre"], p["gpost"], p["wT"], p["wout"],
      p["ang"], p["aws_s"], p["arep"], p["akeep"], p["abias_s"], p["sink"], p["bias_sb"], p["bias_sc"])


def kernel(x_prompt, x_sample, cache_b_k, cache_b_v, cache_c_k, cache_c_v, g_pre, g_post, w_in, w_out,
           a_norm_g, a_ws, a_bs, b_sinks, c_rel_bias, t5_bias):
    depth = w_in.shape[0]
    nb, seq, d = x_prompt.shape
    ns_all, t_new, _ = x_sample.shape
    lc_b, lc_c = cache_b_k.shape[2], cache_c_k.shape[2]
    assert lc_b == B_KEEP and lc_c == C_KEEP and LANE % t_new == 0
    assert SAMPLE_STREAMS_PER_BLOCK * t_new == LANE
    tok_per_step = 2 * LANE
    reps = tok_per_step // t_new

    rel_pb, ok_pb, idx_pc = _prompt_maps()
    rel_sb, ok_sb, idx_sc = _sample_maps(lc_b, lc_c, t_new)
    rel_lo = int(min(rel_pb.min(), rel_sb.min()))
    rel_hi = int(max(rel_pb.max(), rel_sb.max()))
    t5_rel = t5_bias[_t5_bucket(jnp.arange(rel_lo, rel_hi + 1))][None]
    idx_pb = np.where(ok_pb, rel_pb - rel_lo, -1).astype(np.int32)
    idx_sb = np.where(ok_sb, rel_sb - rel_lo, -1).astype(np.int32)
    r_ = np.arange(tok_per_step)
    awsT = a_ws.transpose(0, 1, 3, 2)

    p = dict(
        bias_pb=_build_bias(t5_rel, jnp.asarray(idx_pb), ranges=_static_ranges(idx_pb), kmins=(B_PREV, 0))[0],
        bias_sb=_build_bias(t5_rel, jnp.asarray(idx_sb), ranges=_static_ranges(idx_sb), kmins=(0,))[0, 0],
        bias_pc=_build_bias(c_rel_bias, jnp.asarray(idx_pc), ranges=_static_ranges(idx_pc),
                            kmins=tuple(max(C_PREV - 2 * v, 0) for v in range(C_PREV // 2 + 1))),
        bias_sc=_build_bias(c_rel_bias, jnp.asarray(idx_sc), ranges=_static_ranges(idx_sc),
                            kmins=(0,))[:, 0],
        wT=w_in.astype(bf16).transpose(0, 2, 1),
        wout=w_out.astype(bf16),
        gpre=g_pre[:, None, :],
        gpost=g_post[:, None, :],
        ang=a_norm_g[:, :, None],
        awsT=awsT,
        abs=a_bs[:, :, None, :],
        sink=jnp.repeat(b_sinks, LANE, axis=1)[:, None, :],
        aws_s=awsT[:, :, :t_new, :t_new],
        arep=jnp.asarray((r_[:, None] % t_new == np.arange(t_new)[None, :]).astype(np.float32), bf16),
        abias_s=jnp.tile(a_bs[:, :, None, :t_new], (1, 1, 1, reps)),
        akeep=jnp.asarray(((r_[:, None] // t_new == r_[None, :] // t_new)
                           & (r_[:, None] % t_new <= r_[None, :] % t_new)).astype(np.float32)),
    )

    xp = x_prompt
    xs = x_sample.reshape(ns_all * t_new, d)
    caches = tuple(c.transpose(0, 1, 3, 4, 2).reshape(depth, ns_all, -1, c.shape[2])
                   for c in (cache_b_k, cache_b_v, cache_c_k, cache_c_v))

    pk, pv, pck, pcv = [], [], [], []
    sk, sv, sck, scv, sav = [], [], [], [], []
    for l in range(depth):
        xp, bk, bv, ck, cv = _prompt_layer(l, xp, p)
        pk.append(bk); pv.append(bv); pck.append(ck); pcv.append(cv)
        xs, bk, bv, ck, cv, av = _sample_layer(l, xs, caches, p, t_new=t_new, tok_per_step=tok_per_step)
        sk.append(bk); sv.append(bv); sck.append(ck); scv.append(cv); sav.append(av)

    def stk(xs_, lead, heads):
        return jnp.stack(xs_).reshape(depth, lead, -1, heads, HEAD_DIM)

    def stk_fm(xs_, heads):
        a = jnp.stack(xs_)
        return a.reshape(depth, nb, heads, HEAD_DIM, a.shape[-1]).transpose(0, 1, 4, 2, 3)

    return (xp, xs.reshape(ns_all, t_new, d),
            stk_fm(pk, B_KV), stk_fm(pv, B_KV), stk_fm(pck, C_HEADS), stk_fm(pcv, C_HEADS),
            stk(sk, ns_all, B_KV), stk(sv, ns_all, B_KV), stk(sck, ns_all, C_HEADS), stk(scv, ns_all, C_HEADS),
            jnp.stack(sav).reshape(depth, ns_all, t_new, A_WIDTH))
```

```python
import functools
import math

import numpy as np
import jax
import jax.numpy as jnp
from jax import lax
from jax.experimental import pallas as pl
from jax.experimental.pallas import tpu as pltpu

bf16 = jnp.bfloat16
f32 = jnp.float32

HEAD_DIM = 64
CHUNK = 64
A_WIDTH, A_GROUPS, A_CHUNK = 256, 4, 128
A_GD = A_WIDTH // A_GROUPS
B_HEADS, B_KV = 8, 2
B_GROUP = B_HEADS // B_KV
B_WIDTH, B_KVW = B_HEADS * HEAD_DIM, B_KV * HEAD_DIM
B_PREV = 2
C_HEADS = 4
C_WIDTH = C_HEADS * HEAD_DIM
C_PREV = 8
C_CLIP = 128
T5_BUCKETS, T5_MAX_DIST = 32, 128
RMS_EPS = 1e-6
NEG_INF = -1e30
QK_SCALE = HEAD_DIM ** -0.5
LOG2E = math.log2(math.e)

LANE = 128
PAIR = 2 * CHUNK
B_WIN = (B_PREV + 2) * CHUNK
C_WIN = (C_PREV + 2) * CHUNK
B_KEEP = B_PREV * CHUNK
C_KEEP = C_PREV * CHUNK
ONES_ROWS = 16
VB_STRIDE = HEAD_DIM + ONES_ROWS
VC_STRIDE = 2 * HEAD_DIM + ONES_ROWS
SEQ_TILE = 512
SAMPLE_STREAMS_PER_BLOCK = 4
VMEM_LIMIT = 56 * 1024 * 1024
LUT_UNROLL = 8

_IN_SIZES = [A_WIDTH, A_WIDTH, A_WIDTH, B_WIDTH, B_KVW, B_KVW, B_WIDTH, C_WIDTH, C_WIDTH, C_WIDTH, C_WIDTH]
_IN_OFF = [int(v) for v in np.cumsum([0] + _IN_SIZES)]
(T_AU, T_AV, T_AZ, T_BQ, T_BK, T_BV, T_BZ, T_CQ, T_CK, T_CV, T_CZ, T_END) = _IN_OFF
_T_SECTIONS = ((T_AU, T_BK), (T_BV, T_CK), (T_CV, T_END))
Y_A, Y_B, Y_C = 0, A_WIDTH, A_WIDTH + B_WIDTH

_NT = (((1,), (1,)), ((), ()))
_TN = (((0,), (0,)), ((), ()))


def _t5_bucket(rel):
    half = T5_BUCKETS // 2
    max_exact = half // 2
    ret = jnp.where(rel > 0, half, 0)
    n = jnp.abs(rel)
    nf = jnp.maximum(n, 1).astype(jnp.float32)
    large = max_exact + (jnp.log(nf / max_exact) / math.log(T5_MAX_DIST / max_exact)
                         * (half - max_exact)).astype(jnp.int32)
    large = jnp.minimum(large, half - 1)
    return ret + jnp.where(n < max_exact, n, large)


def _lut_kernel(tab_ref, idx_ref, out_ref, *, n_entries, n_heads, ranges, kmins):
    layer = pl.program_id(0)
    neg = jnp.full((CHUNK, LANE), NEG_INF, f32)
    hgroup = min(n_heads, 4)
    for rb, (lo, hi) in enumerate(ranges):
        rows = slice(rb * CHUNK, (rb + 1) * CHUNK)
        idx = idx_ref[rows, :]
        for h0 in range(0, n_heads, hgroup):
            if hi < lo:
                base = (neg,) * hgroup
            else:
                def body(e, accs, h0=h0, idx=idx):
                    hit = idx == e
                    off = (layer * n_entries + e) * n_heads + h0
                    return tuple(jnp.where(hit, tab_ref[off + j], a) for j, a in enumerate(accs))
                base = lax.fori_loop(lo, hi + 1, body, (neg,) * hgroup,
                                     unroll=min(LUT_UNROLL, hi + 1 - lo))
            for j in range(hgroup):
                scaled = base[j] * LOG2E
                for v, kmin in enumerate(kmins):
                    out_ref[0, v, rows, (h0 + j) * LANE:(h0 + j + 1) * LANE] = neg if rb < kmin else scaled


def _build_bias(table, idx, *, ranges, kmins):
    n_layers, n_entries, n_heads = table.shape
    n_rows = idx.shape[0]
    kern = functools.partial(_lut_kernel, n_entries=n_entries, n_heads=n_heads,
                             ranges=tuple(ranges), kmins=tuple(kmins))
    return pl.pallas_call(
        kern,
        grid=(n_layers,),
        in_specs=[pl.BlockSpec(memory_space=pltpu.SMEM),
                  pl.BlockSpec((n_rows, LANE), lambda l: (0, 0))],
        out_specs=pl.BlockSpec((1, len(kmins), n_rows, n_heads * LANE), lambda l: (l, 0, 0, 0)),
        out_shape=jax.ShapeDtypeStruct((n_layers, len(kmins), n_rows, n_heads * LANE), f32),
        compiler_params=pltpu.CompilerParams(dimension_semantics=("arbitrary",),
                                             vmem_limit_bytes=VMEM_LIMIT),
        name="bias_lut",
    )(table.reshape(-1), idx)


def _static_ranges(idx_np):
    out = []
    for rb in range(idx_np.shape[0] // CHUNK):
        blk = idx_np[rb * CHUNK:(rb + 1) * CHUNK]
        good = blk[blk >= 0]
        out.append((int(good.min()), int(good.max())) if good.size else (0, -1))
    return out


def _prompt_maps():
    i = np.arange(PAIR)[None, :]
    qi = i // CHUNK
    jb = np.arange(B_WIN)[:, None]
    rel_b = jb - B_KEEP - i
    ok_b = (jb // CHUNK >= qi) & (jb // CHUNK <= qi + B_PREV)
    jc = np.arange(C_WIN)[:, None]
    rel_c = jc - C_KEEP - i
    ok_c = (jc // CHUNK >= qi) & (jc // CHUNK <= qi + C_PREV)
    idx_c = np.where(ok_c, np.clip(rel_c, -C_CLIP, C_CLIP) + C_CLIP, -1).astype(np.int32)
    return rel_b.astype(np.int32), ok_b, idx_c


def _sample_maps(lc_b, lc_c, t_new):
    ns = SAMPLE_STREAMS_PER_BLOCK
    c = np.arange(LANE)[None, :]
    sq, i = c // t_new, c % t_new

    def one(lc):
        r = np.arange(ns * lc + LANE)[:, None]
        is_new = r >= ns * lc
        sk = np.where(is_new, (r - ns * lc) // t_new, r // lc)
        j = np.where(is_new, lc + (r - ns * lc) % t_new, r % lc)
        return (j - lc - i).astype(np.int32), (sk == sq)

    rel_b, ok_b = one(lc_b)
    rel_c, ok_c = one(lc_c)
    idx_c = np.where(ok_c, np.clip(rel_c, -C_CLIP, C_CLIP) + C_CLIP, -1).astype(np.int32)
    return rel_b, ok_b, idx_c


def _rms_rows(x, g):
    ms = jnp.mean(x * x, axis=-1, keepdims=True)
    return x * lax.rsqrt(ms + RMS_EPS) * g


def _rms_cols(xT, g_col):
    ms = jnp.mean(xT * xT, axis=0, keepdims=True)
    return xT * lax.rsqrt(ms + RMS_EPS) * g_col


def _project_T(pt_sc, wT_ref, h):
    for r0, r1 in _T_SECTIONS:
        pt_sc[r0:r1, :] = lax.dot_general(wT_ref[r0:r1, :], h, _NT, preferred_element_type=f32)


def _qblockdiag(qa, qb, lower):
    z = jnp.zeros((HEAD_DIM, LANE), bf16)
    if lower is None:
        return jnp.concatenate([jnp.concatenate([qa, z], axis=1),
                                jnp.concatenate([z, qb], axis=1)], axis=0)
    top = jnp.concatenate([qa, qb], axis=1)
    zz = jnp.zeros((HEAD_DIM, 2 * LANE), bf16)
    return jnp.concatenate([top, zz] if lower == 0 else [zz, top], axis=0)


def _with_ones(vT, group):
    ones = jnp.ones((ONES_ROWS, vT.shape[1]), vT.dtype)
    parts = []
    for r in range(0, vT.shape[0], group):
        parts += [vT[r:r + group, :], ones]
    return jnp.concatenate(parts, axis=0)


def _softmax_pv(s, sink, vwin):
    m = jnp.max(s, axis=0, keepdims=True)
    if sink is not None:
        m = jnp.maximum(m, sink)
    e = jnp.exp2(s - m)
    o = jnp.dot(vwin, e.astype(bf16), preferred_element_type=f32)
    nv = vwin.shape[0] - ONES_ROWS
    den = o[nv:nv + 1, :]
    if sink is not None:
        den = den + jnp.exp2(sink - m)
    return o[:nv, :] * (1.0 / den)


def _scores(pt_sc, cols, kwin_b, bias_b, kwin_c, bias_c):
    def head_q(t0, hh):
        return (pt_sc[t0 + hh * HEAD_DIM:t0 + (hh + 1) * HEAD_DIM, cols] * (QK_SCALE * LOG2E)).astype(bf16)

    s_b, s_c = [], []
    for cb in range(B_HEADS // 2):
        ha, hb = 2 * cb, 2 * cb + 1
        qbd = _qblockdiag(head_q(T_BQ, ha), head_q(T_BQ, hb), ha // B_GROUP)
        s = jnp.dot(kwin_b, qbd, preferred_element_type=f32)
        s_b.append(s + bias_b[:, cb * 2 * LANE:(cb + 1) * 2 * LANE])
    for cb in range(C_HEADS // 2):
        ha, hb = 2 * cb, 2 * cb + 1
        qbd = _qblockdiag(head_q(T_CQ, ha), head_q(T_CQ, hb), None)
        s = jnp.dot(kwin_c[:, cb * LANE:(cb + 1) * LANE], qbd, preferred_element_type=f32)
        s_c.append(s + bias_c[:, cb * 2 * LANE:(cb + 1) * 2 * LANE])
    return s_b, s_c


def _attend(pt_sc, yT_sc, cols, s_b, s_c, vwinT_b, vwinT_c, sink_ref):
    for cb in range(B_HEADS // 2):
        ha, hb = 2 * cb, 2 * cb + 1
        kv = ha // B_GROUP
        sink = sink_ref[:, cb * 2 * LANE:(cb + 1) * 2 * LANE] * LOG2E
        o = _softmax_pv(s_b[cb], sink, vwinT_b[kv * VB_STRIDE:(kv + 1) * VB_STRIDE, :])
        for j, hh in enumerate((ha, hb)):
            gate = pt_sc[T_BZ + hh * HEAD_DIM:T_BZ + (hh + 1) * HEAD_DIM, cols]
            yT_sc[Y_B + hh * HEAD_DIM:Y_B + (hh + 1) * HEAD_DIM, cols] = (
                o[:, j * LANE:(j + 1) * LANE] * gate).astype(bf16)
    for cb in range(C_HEADS // 2):
        ha, hb = 2 * cb, 2 * cb + 1
        o = _softmax_pv(s_c[cb], None, vwinT_c[cb * VC_STRIDE:(cb + 1) * VC_STRIDE, :])
        for j, hh in enumerate((ha, hb)):
            gate = pt_sc[T_CZ + hh * HEAD_DIM:T_CZ + (hh + 1) * HEAD_DIM, cols]
            yT_sc[Y_C + hh * HEAD_DIM:Y_C + (hh + 1) * HEAD_DIM, cols] = (
                o[j * HEAD_DIM:(j + 1) * HEAD_DIM, j * LANE:(j + 1) * LANE] * gate).astype(bf16)


def _gate_rows(pt_sc):
    for r0, r1 in ((T_BZ, T_CQ), (T_CZ, T_END)):
        pt_sc[r0:r1, :] = jax.nn.silu(pt_sc[r0:r1, :])


def _finish(x, yT_sc, wout_ref, gpost_ref):
    out = lax.dot_general(yT_sc[...], wout_ref[...], _TN, preferred_element_type=f32)
    return x + _rms_rows(out, gpost_ref[...])


def _layer_spec(arr, layer):
    shape = arr.shape[1:]
    return pl.BlockSpec((None,) + shape, lambda *_: (layer,) + (0,) * len(shape),
                        pipeline_mode=pl.Buffered(1))


def _const_spec(arr):
    return pl.BlockSpec(arr.shape, lambda *_: (0,) * arr.ndim, pipeline_mode=pl.Buffered(1))


def _prompt_kernel(x_ref, gpre_ref, gpost_ref, wT_ref, wout_ref,
                   ang_ref, awsT_ref, abs_ref, sink_ref, biasb_ref, biasc_ref,
                   xo_ref, bk_ref, bv_ref, ck_ref, cv_ref,
                   pt_sc, kb_ring, kc_ring, vbT_ring, vcT_ring, yT_sc, sb_even, sc_even, sb_odd, sc_odd):
    t = pl.program_id(1)
    ts = SEQ_TILE

    @pl.when(t == 0)
    def _():
        kb_ring[0:B_KEEP, :] = jnp.zeros((B_KEEP, B_KVW), bf16)
        kc_ring[0:C_KEEP, :] = jnp.zeros((C_KEEP, C_WIDTH), bf16)
        vbT_ring[:, 0:B_KEEP] = jnp.zeros((B_KV * VB_STRIDE, B_KEEP), bf16)
        vcT_ring[:, 0:C_KEEP] = jnp.zeros((C_HEADS // 2 * VC_STRIDE, C_KEEP), bf16)
        for g in range(B_KV):
            vbT_ring[g * VB_STRIDE + HEAD_DIM:(g + 1) * VB_STRIDE, :] = jnp.ones((ONES_ROWS, B_KEEP + ts), bf16)
        for g in range(C_HEADS // 2):
            vcT_ring[g * VC_STRIDE + 2 * HEAD_DIM:(g + 1) * VC_STRIDE, :] = jnp.ones((ONES_ROWS, C_KEEP + ts), bf16)

    x = x_ref[0]
    h = _rms_rows(x, gpre_ref[...]).astype(bf16)
    _project_T(pt_sc, wT_ref, h)
    kb_ring[B_KEEP:, :] = lax.dot_general(h, wT_ref[T_BK:T_BV, :], _NT,
                                          preferred_element_type=f32).astype(bf16)
    kc_ring[C_KEEP:, :] = lax.dot_general(h, wT_ref[T_CK:T_CV, :], _NT,
                                          preferred_element_type=f32).astype(bf16)
    for g in range(B_KV):
        vbT_ring[g * VB_STRIDE:g * VB_STRIDE + HEAD_DIM, B_KEEP:] = (
            pt_sc[T_BV + g * HEAD_DIM:T_BV + (g + 1) * HEAD_DIM, :].astype(bf16))
    for g in range(C_HEADS // 2):
        vcT_ring[g * VC_STRIDE:g * VC_STRIDE + 2 * HEAD_DIM, C_KEEP:] = (
            pt_sc[T_CV + g * 2 * HEAD_DIM:T_CV + (g + 1) * 2 * HEAD_DIM, :].astype(bf16))

    _gate_rows(pt_sc)

    @pl.when(t == pl.num_programs(1) - 1)
    def _():
        bk_ref[0] = lax.dot_general(wT_ref[T_BK:T_BV, :], h[ts - B_KEEP:, :], _NT, preferred_element_type=f32)
        ck_ref[0] = lax.dot_general(wT_ref[T_CK:T_CV, :], h[ts - C_KEEP:, :], _NT, preferred_element_type=f32)
        bv_ref[0] = pt_sc[T_BV:T_BZ, ts - B_KEEP:]
        cv_ref[0] = pt_sc[T_CV:T_CZ, ts - C_KEEP:]

    vn = _rms_cols(pt_sc[T_AV:T_AV + A_WIDTH, :], ang_ref[...]).astype(bf16)
    n_ac = ts // A_CHUNK
    row = lax.broadcasted_iota(jnp.int32, (A_CHUNK, A_CHUNK), 0)
    col = lax.broadcasted_iota(jnp.int32, (A_CHUNK, A_CHUNK), 1)
    for g in range(A_GROUPS):
        rows = slice(g * A_GD, (g + 1) * A_GD)
        wgt = jnp.where(row <= col, awsT_ref[g], 0.0).astype(bf16)
        lhs = jnp.concatenate([vn[rows, c * A_CHUNK:(c + 1) * A_CHUNK] for c in range(n_ac)], axis=0)
        mix = jnp.dot(lhs, wgt, preferred_element_type=f32) + abs_ref[g]
        for c in range(n_ac):
            cols = slice(c * A_CHUNK, (c + 1) * A_CHUNK)
            u = pt_sc[T_AU + g * A_GD:T_AU + (g + 1) * A_GD, cols]
            z = pt_sc[T_AZ + g * A_GD:T_AZ + (g + 1) * A_GD, cols]
            yT_sc[Y_A + g * A_GD:Y_A + (g + 1) * A_GD, cols] = (
                u * mix[c * A_GD:(c + 1) * A_GD, :] * jax.nn.silu(z)).astype(bf16)

    n_pairs = ts // PAIR

    def scores_into(p, sb_ref, sc_ref):
        pg = t * n_pairs + p
        start = pl.multiple_of(p * PAIR, PAIR)
        s_b, s_c = _scores(pt_sc, pl.ds(start, PAIR),
                           kb_ring[pl.ds(start, B_WIN), :], biasb_ref.at[jnp.minimum(pg, 1)],
                           kc_ring[pl.ds(start, C_WIN), :], biasc_ref.at[jnp.minimum(pg, C_PREV // 2)])
        for cb, v in enumerate(s_b):
            sb_ref[cb] = v
        for cb, v in enumerate(s_c):
            sc_ref[cb] = v

    def attend_from(p, sb_ref, sc_ref):
        start = pl.multiple_of(p * PAIR, PAIR)
        _attend(pt_sc, yT_sc, pl.ds(start, PAIR),
                [sb_ref[cb] for cb in range(B_HEADS // 2)], [sc_ref[cb] for cb in range(C_HEADS // 2)],
                vbT_ring[:, pl.ds(start, B_WIN)], vcT_ring[:, pl.ds(start, C_WIN)], sink_ref)

    scores_into(0, sb_even, sc_even)

    def two_pairs(q, carry):
        p = 2 * q
        scores_into(p + 1, sb_odd, sc_odd)
        attend_from(p, sb_even, sc_even)
        scores_into(jnp.minimum(p + 2, n_pairs - 1), sb_even, sc_even)
        attend_from(p + 1, sb_odd, sc_odd)
        return carry

    lax.fori_loop(0, n_pairs // 2, two_pairs, 0)

    xo_ref[0] = _finish(x, yT_sc, wout_ref, gpost_ref)

    kb_ring[0:B_KEEP, :] = kb_ring[ts:ts + B_KEEP, :]
    kc_ring[0:C_KEEP, :] = kc_ring[ts:ts + C_KEEP, :]
    vbT_ring[:, 0:B_KEEP] = vbT_ring[:, ts:ts + B_KEEP]
    vcT_ring[:, 0:C_KEEP] = vcT_ring[:, ts:ts + C_KEEP]


def _prompt_layer(layer, x, p):
    nb, seq, d = x.shape
    ts = SEQ_TILE
    assert seq % ts == 0 and ts == C_KEEP
    kv_spec = lambda keep, width: pl.BlockSpec((1, width, keep), lambda b, t: (b, 0, 0))
    return pl.pallas_call(
        _prompt_kernel,
        grid=(nb, seq // ts),
        in_specs=[
            pl.BlockSpec((1, ts, d), lambda b, t: (b, t, 0)),
            _layer_spec(p["gpre"], layer), _layer_spec(p["gpost"], layer),
            _layer_spec(p["wT"], layer),
            _layer_spec(p["wout"], layer),
            _layer_spec(p["ang"], layer), _layer_spec(p["awsT"], layer), _layer_spec(p["abs"], layer),
            _layer_spec(p["sink"], layer), _const_spec(p["bias_pb"]), _layer_spec(p["bias_pc"], layer),
        ],
        out_specs=[
            pl.BlockSpec((1, ts, d), lambda b, t: (b, t, 0)),
            kv_spec(B_KEEP, B_KVW), kv_spec(B_KEEP, B_KVW), kv_spec(C_KEEP, C_WIDTH), kv_spec(C_KEEP, C_WIDTH),
        ],
        out_shape=[
            jax.ShapeDtypeStruct((nb, seq, d), f32),
            jax.ShapeDtypeStruct((nb, B_KVW, B_KEEP), f32),
            jax.ShapeDtypeStruct((nb, B_KVW, B_KEEP), f32),
            jax.ShapeDtypeStruct((nb, C_WIDTH, C_KEEP), f32),
            jax.ShapeDtypeStruct((nb, C_WIDTH, C_KEEP), f32),
        ],
        scratch_shapes=[
            pltpu.VMEM((T_END, ts), f32),
            pltpu.VMEM((B_KEEP + ts, B_KVW), bf16),
            pltpu.VMEM((C_KEEP + ts, C_WIDTH), bf16),
            pltpu.VMEM((B_KV * VB_STRIDE, B_KEEP + ts), bf16),
            pltpu.VMEM((C_HEADS // 2 * VC_STRIDE, C_KEEP + ts), bf16),
            pltpu.VMEM((A_WIDTH + B_WIDTH + C_WIDTH, ts), bf16),
            pltpu.VMEM((B_HEADS // 2, B_WIN, 2 * LANE), f32), pltpu.VMEM((C_HEADS // 2, C_WIN, 2 * LANE), f32),
            pltpu.VMEM((B_HEADS // 2, B_WIN, 2 * LANE), f32), pltpu.VMEM((C_HEADS // 2, C_WIN, 2 * LANE), f32),
        ],
        compiler_params=pltpu.CompilerParams(dimension_semantics=("arbitrary", "arbitrary"),
                                             vmem_limit_bytes=VMEM_LIMIT),
        name="prompt_layer",
    )(x, p["gpre"], p["gpost"], p["wT"], p["wout"], p["ang"], p["awsT"],
      p["abs"], p["sink"], p["bias_pb"], p["bias_pc"])


def _sample_kernel(x_ref, cbk_ref, cbv_ref, cck_ref, ccv_ref, gpre_ref, gpost_ref, wT_ref,
                   wout_ref, ang_ref, aws_ref, arep_ref, akeep_ref, abias_ref,
                   sink_ref, biasb_ref, biasc_ref,
                   xo_ref, bk_ref, bv_ref, ck_ref, cv_ref, av_ref,
                   pt_sc, yT_sc):
    ns = SAMPLE_STREAMS_PER_BLOCK
    x = x_ref[...]
    ntok = x.shape[0]
    h = _rms_rows(x, gpre_ref[...]).astype(bf16)
    _project_T(pt_sc, wT_ref, h)
    kvn_b = lax.dot_general(h, wT_ref[T_BK:T_BZ, :], _NT, preferred_element_type=f32)
    kvn_c = lax.dot_general(h, wT_ref[T_CK:T_CZ, :], _NT, preferred_element_type=f32)
    kn_c = kvn_c[:, 0:C_WIDTH]
    bk_ref[...] = kvn_b[:, 0:B_KVW]
    bv_ref[...] = kvn_b[:, B_KVW:]
    ck_ref[...] = kn_c
    cv_ref[...] = kvn_c[:, C_WIDTH:]

    vn = _rms_cols(pt_sc[T_AV:T_AV + A_WIDTH, :], ang_ref[...])
    av_ref[...] = vn.T
    vnb = vn.astype(bf16)
    keep = akeep_ref[...] > 0.0
    rep = arep_ref[...]
    for g in range(A_GROUPS):
        rows = slice(g * A_GD, (g + 1) * A_GD)
        tiled = lax.dot_general(jnp.dot(rep, aws_ref[g].astype(bf16), preferred_element_type=f32).astype(bf16),
                                rep, _NT, preferred_element_type=f32)
        wgt = jnp.where(keep, tiled, 0.0).astype(bf16)
        mix = jnp.dot(vnb[rows, :], wgt, preferred_element_type=f32) + abias_ref[g]
        u = pt_sc[T_AU + g * A_GD:T_AU + (g + 1) * A_GD, :]
        z = pt_sc[T_AZ + g * A_GD:T_AZ + (g + 1) * A_GD, :]
        yT_sc[Y_A + g * A_GD:Y_A + (g + 1) * A_GD, :] = (u * mix * jax.nn.silu(z)).astype(bf16)

    _gate_rows(pt_sc)
    for blk in range(ntok // LANE):
        cols = slice(blk * LANE, (blk + 1) * LANE)
        streams = range(blk * ns, (blk + 1) * ns)
        kwin_b = jnp.concatenate([cbk_ref[s].T.astype(bf16) for s in streams]
                                 + [kvn_b[cols, 0:B_KVW].astype(bf16)], axis=0)
        vwinT_b = jnp.concatenate([cbv_ref[s].astype(bf16) for s in streams]
                                  + [pt_sc[T_BV:T_BV + B_KVW, cols].astype(bf16)], axis=1)
        kwin_c = jnp.concatenate([cck_ref[s].T.astype(bf16) for s in streams]
                                 + [kn_c[cols, :].astype(bf16)], axis=0)
        vwinT_c = jnp.concatenate([ccv_ref[s].astype(bf16) for s in streams]
                                  + [pt_sc[T_CV:T_CV + C_WIDTH, cols].astype(bf16)], axis=1)
        s_b, s_c = _scores(pt_sc, cols, kwin_b, biasb_ref, kwin_c, biasc_ref)
        _attend(pt_sc, yT_sc, cols, s_b, s_c, _with_ones(vwinT_b, HEAD_DIM), _with_ones(vwinT_c, 2 * HEAD_DIM),
                sink_ref)

    xo_ref[...] = _finish(x, yT_sc, wout_ref, gpost_ref)


def _sample_layer(layer, x, caches, p, *, t_new, tok_per_step):
    ntok, d = x.shape
    spb = tok_per_step // t_new
    cbk, cbv, cck, ccv = caches

    def rows(width):
        return pl.BlockSpec((tok_per_step, width), lambda i: (i, 0))

    def cache(c):
        return pl.BlockSpec((None, spb) + c.shape[2:], lambda i: (layer, i, 0, 0))

    return pl.pallas_call(
        _sample_kernel,
        grid=(ntok // tok_per_step,),
        in_specs=[
            rows(d), cache(cbk), cache(cbv), cache(cck), cache(ccv),
            _layer_spec(p["gpre"], layer), _layer_spec(p["gpost"], layer),
            _layer_spec(p["wT"], layer),
            _layer_spec(p["wout"], layer),
            _layer_spec(p["ang"], layer), _layer_spec(p["aws_s"], layer), _const_spec(p["arep"]),
            _const_spec(p["akeep"]), _layer_spec(p["abias_s"], layer),
            _layer_spec(p["sink"], layer), _const_spec(p["bias_sb"]), _layer_spec(p["bias_sc"], layer),
        ],
        out_specs=[rows(d), rows(B_KVW), rows(B_KVW), rows(C_WIDTH), rows(C_WIDTH), rows(A_WIDTH)],
        out_shape=[
            jax.ShapeDtypeStruct((ntok, d), f32),
            jax.ShapeDtypeStruct((ntok, B_KVW), f32), jax.ShapeDtypeStruct((ntok, B_KVW), f32),
            jax.ShapeDtypeStruct((ntok, C_WIDTH), f32), jax.ShapeDtypeStruct((ntok, C_WIDTH), f32),
            jax.ShapeDtypeStruct((ntok, A_WIDTH), f32),
        ],
        scratch_shapes=[
            pltpu.VMEM((T_END, tok_per_step), f32),
            pltpu.VMEM((A_WIDTH + B_WIDTH + C_WIDTH, tok_per_step), bf16),
        ],
        compiler_params=pltpu.CompilerParams(dimension_semantics=("arbitrary",),
                                             vmem_limit_bytes=VMEM_LIMIT),
        name="sample_layer",
    )(x, cbk, cbv, cck, ccv, p["gpre"], p["gpost"], p["wT"], p["wout"],
      p["ang"], p["aws_s"], p["arep"], p["akeep"], p["abias_s"], p["sink"], p["bias_sb"], p["bias_sc"])


def kernel(x_prompt, x_sample, cache_b_k, cache_b_v, cache_c_k, cache_c_v, g_pre, g_post, w_in, w_out,
           a_norm_g, a_ws, a_bs, b_sinks, c_rel_bias, t5_bias):
    depth = w_in.shape[0]
    nb, seq, d = x_prompt.shape
    ns_all, t_new, _ = x_sample.shape
    lc_b, lc_c = cache_b_k.shape[2], cache_c_k.shape[2]
    assert lc_b == B_KEEP and lc_c == C_KEEP and LANE % t_new == 0
    assert SAMPLE_STREAMS_PER_BLOCK * t_new == LANE
    tok_per_step = 2 * LANE
    reps = tok_per_step // t_new

    rel_pb, ok_pb, idx_pc = _prompt_maps()
    rel_sb, ok_sb, idx_sc = _sample_maps(lc_b, lc_c, t_new)
    rel_lo = int(min(rel_pb.min(), rel_sb.min()))
    rel_hi = int(max(rel_pb.max(), rel_sb.max()))
    t5_rel = t5_bias[_t5_bucket(jnp.arange(rel_lo, rel_hi + 1))][None]
    idx_pb = np.where(ok_pb, rel_pb - rel_lo, -1).astype(np.int32)
    idx_sb = np.where(ok_sb, rel_sb - rel_lo, -1).astype(np.int32)
    r_ = np.arange(tok_per_step)
    awsT = a_ws.transpose(0, 1, 3, 2)

    p = dict(
        bias_pb=_build_bias(t5_rel, jnp.asarray(idx_pb), ranges=_static_ranges(idx_pb), kmins=(B_PREV, 0))[0],
        bias_sb=_build_bias(t5_rel, jnp.asarray(idx_sb), ranges=_static_ranges(idx_sb), kmins=(0,))[0, 0],
        bias_pc=_build_bias(c_rel_bias, jnp.asarray(idx_pc), ranges=_static_ranges(idx_pc),
                            kmins=tuple(max(C_PREV - 2 * v, 0) for v in range(C_PREV // 2 + 1))),
        bias_sc=_build_bias(c_rel_bias, jnp.asarray(idx_sc), ranges=_static_ranges(idx_sc),
                            kmins=(0,))[:, 0],
        wT=w_in.astype(bf16).transpose(0, 2, 1),
        wout=w_out.astype(bf16),
        gpre=g_pre[:, None, :],
        gpost=g_post[:, None, :],
        ang=a_norm_g[:, :, None],
        awsT=awsT,
        abs=a_bs[:, :, None, :],
        sink=jnp.repeat(b_sinks, LANE, axis=1)[:, None, :],
        aws_s=awsT[:, :, :t_new, :t_new],
        arep=jnp.asarray((r_[:, None] % t_new == np.arange(t_new)[None, :]).astype(np.float32), bf16),
        abias_s=jnp.tile(a_bs[:, :, None, :t_new], (1, 1, 1, reps)),
        akeep=jnp.asarray(((r_[:, None] // t_new == r_[None, :] // t_new)
                           & (r_[:, None] % t_new <= r_[None, :] % t_new)).astype(np.float32)),
    )

    xp = x_prompt
    xs = x_sample.reshape(ns_all * t_new, d)
    caches = tuple(c.transpose(0, 1, 3, 4, 2).reshape(depth, ns_all, -1, c.shape[2])
                   for c in (cache_b_k, cache_b_v, cache_c_k, cache_c_v))

    pk, pv, pck, pcv = [], [], [], []
    sk, sv, sck, scv, sav = [], [], [], [], []
    for l in range(depth):
        xp, bk, bv, ck, cv = _prompt_layer(l, xp, p)
        pk.append(bk); pv.append(bv); pck.append(ck); pcv.append(cv)
        xs, bk, bv, ck, cv, av = _sample_layer(l, xs, caches, p, t_new=t_new, tok_per_step=tok_per_step)
        sk.append(bk); sv.append(bv); sck.append(ck); scv.append(cv); sav.append(av)

    def stk(xs_, lead, heads):
        return jnp.stack(xs_).reshape(depth, lead, -1, heads, HEAD_DIM)

    def stk_fm(xs_, heads):
        a = jnp.stack(xs_)
        return a.reshape(depth, nb, heads, HEAD_DIM, a.shape[-1]).transpose(0, 1, 4, 2, 3)

    return (xp, xs.reshape(ns_all, t_new, d),
            stk_fm(pk, B_KV), stk_fm(pv, B_KV), stk_fm(pck, C_HEADS), stk_fm(pcv, C_HEADS),
            stk(sk, ns_all, B_KV), stk(sv, ns_all, B_KV), stk(sck, ns_all, C_HEADS), stk(scv, ns_all, C_HEADS),
            jnp.stack(sav).reshape(depth, ns_all, t_new, A_WIDTH))
```

```python
import functools
import math

import numpy as np
import jax
import jax.numpy as jnp
from jax import lax
from jax.experimental import pallas as pl
from jax.experimental.pallas import tpu as pltpu

bf16 = jnp.bfloat16
f32 = jnp.float32

HEAD_DIM = 64
CHUNK = 64
A_WIDTH, A_GROUPS, A_CHUNK = 256, 4, 128
A_GD = A_WIDTH // A_GROUPS
B_HEADS, B_KV = 8, 2
B_GROUP = B_HEADS // B_KV
B_WIDTH, B_KVW = B_HEADS * HEAD_DIM, B_KV * HEAD_DIM
B_PREV = 2
C_HEADS = 4
C_WIDTH = C_HEADS * HEAD_DIM
C_PREV = 8
C_CLIP = 128
T5_BUCKETS, T5_MAX_DIST = 32, 128
RMS_EPS = 1e-6
NEG_INF = -1e30
QK_SCALE = HEAD_DIM ** -0.5
LOG2E = math.log2(math.e)

LANE = 128
PAIR = 2 * CHUNK
B_WIN = (B_PREV + 2) * CHUNK
C_WIN = (C_PREV + 2) * CHUNK
B_KEEP = B_PREV * CHUNK
C_KEEP = C_PREV * CHUNK
ONES_ROWS = 16
VB_STRIDE = HEAD_DIM + ONES_ROWS
VC_STRIDE = 2 * HEAD_DIM + ONES_ROWS
SEQ_TILE = 512
SAMPLE_STREAMS_PER_BLOCK = 4
VMEM_LIMIT = 56 * 1024 * 1024
LUT_UNROLL = 8

_IN_SIZES = [A_WIDTH, A_WIDTH, A_WIDTH, B_WIDTH, B_KVW, B_KVW, B_WIDTH, C_WIDTH, C_WIDTH, C_WIDTH, C_WIDTH]
_IN_OFF = [int(v) for v in np.cumsum([0] + _IN_SIZES)]
(T_AU, T_AV, T_AZ, T_BQ, T_BK, T_BV, T_BZ, T_CQ, T_CK, T_CV, T_CZ, T_END) = _IN_OFF
_T_SECTIONS = ((T_AU, T_BK), (T_BV, T_CK), (T_CV, T_END))
Y_A, Y_B, Y_C = 0, A_WIDTH, A_WIDTH + B_WIDTH

_NT = (((1,), (1,)), ((), ()))
_TN = (((0,), (0,)), ((), ()))


def _t5_bucket(rel):
    half = T5_BUCKETS // 2
    max_exact = half // 2
    ret = jnp.where(rel > 0, half, 0)
    n = jnp.abs(rel)
    nf = jnp.maximum(n, 1).astype(jnp.float32)
    large = max_exact + (jnp.log(nf / max_exact) / math.log(T5_MAX_DIST / max_exact)
                         * (half - max_exact)).astype(jnp.int32)
    large = jnp.minimum(large, half - 1)
    return ret + jnp.where(n < max_exact, n, large)


def _lut_kernel(tab_ref, idx_ref, out_ref, *, n_entries, n_heads, ranges, kmins):
    layer = pl.program_id(0)
    neg = jnp.full((CHUNK, LANE), NEG_INF, f32)
    hgroup = min(n_heads, 4)
    for rb, (lo, hi) in enumerate(ranges):
        rows = slice(rb * CHUNK, (rb + 1) * CHUNK)
        idx = idx_ref[rows, :]
        for h0 in range(0, n_heads, hgroup):
            if hi < lo:
                base = (neg,) * hgroup
            else:
                def body(e, accs, h0=h0, idx=idx):
                    hit = idx == e
                    off = (layer * n_entries + e) * n_heads + h0
                    return tuple(jnp.where(hit, tab_ref[off + j], a) for j, a in enumerate(accs))
                base = lax.fori_loop(lo, hi + 1, body, (neg,) * hgroup,
                                     unroll=min(LUT_UNROLL, hi + 1 - lo))
            for j in range(hgroup):
                scaled = base[j] * LOG2E
                for v, kmin in enumerate(kmins):
                    out_ref[0, v, rows, (h0 + j) * LANE:(h0 + j + 1) * LANE] = neg if rb < kmin else scaled


def _build_bias(table, idx, *, ranges, kmins):
    n_layers, n_entries, n_heads = table.shape
    n_rows = idx.shape[0]
    kern = functools.partial(_lut_kernel, n_entries=n_entries, n_heads=n_heads,
                             ranges=tuple(ranges), kmins=tuple(kmins))
    return pl.pallas_call(
        kern,
        grid=(n_layers,),
        in_specs=[pl.BlockSpec(memory_space=pltpu.SMEM),
                  pl.BlockSpec((n_rows, LANE), lambda l: (0, 0))],
        out_specs=pl.BlockSpec((1, len(kmins), n_rows, n_heads * LANE), lambda l: (l, 0, 0, 0)),
        out_shape=jax.ShapeDtypeStruct((n_layers, len(kmins), n_rows, n_heads * LANE), f32),
        compiler_params=pltpu.CompilerParams(dimension_semantics=("arbitrary",),
                                             vmem_limit_bytes=VMEM_LIMIT),
        name="bias_lut",
    )(table.reshape(-1), idx)


def _static_ranges(idx_np):
    out = []
    for rb in range(idx_np.shape[0] // CHUNK):
        blk = idx_np[rb * CHUNK:(rb + 1) * CHUNK]
        good = blk[blk >= 0]
        out.append((int(good.min()), int(good.max())) if good.size else (0, -1))
    return out


def _prompt_maps():
    i = np.arange(PAIR)[None, :]
    qi = i // CHUNK
    jb = np.arange(B_WIN)[:, None]
    rel_b = jb - B_KEEP - i
    ok_b = (jb // CHUNK >= qi) & (jb // CHUNK <= qi + B_PREV)
    jc = np.arange(C_WIN)[:, None]
    rel_c = jc - C_KEEP - i
    ok_c = (jc // CHUNK >= qi) & (jc // CHUNK <= qi + C_PREV)
    idx_c = np.where(ok_c, np.clip(rel_c, -C_CLIP, C_CLIP) + C_CLIP, -1).astype(np.int32)
    return rel_b.astype(np.int32), ok_b, idx_c


def _sample_maps(lc_b, lc_c, t_new):
    ns = SAMPLE_STREAMS_PER_BLOCK
    c = np.arange(LANE)[None, :]
    sq, i = c // t_new, c % t_new

    def one(lc):
        r = np.arange(ns * lc + LANE)[:, None]
        is_new = r >= ns * lc
        sk = np.where(is_new, (r - ns * lc) // t_new, r // lc)
        j = np.where(is_new, lc + (r - ns * lc) % t_new, r % lc)
        return (j - lc - i).astype(np.int32), (sk == sq)

    rel_b, ok_b = one(lc_b)
    rel_c, ok_c = one(lc_c)
    idx_c = np.where(ok_c, np.clip(rel_c, -C_CLIP, C_CLIP) + C_CLIP, -1).astype(np.int32)
    return rel_b, ok_b, idx_c


def _rms_rows(x, g):
    ms = jnp.mean(x * x, axis=-1, keepdims=True)
    return x * lax.rsqrt(ms + RMS_EPS) * g


def _rms_cols(xT, g_col):
    ms = jnp.mean(xT * xT, axis=0, keepdims=True)
    return xT * lax.rsqrt(ms + RMS_EPS) * g_col


def _project_T(pt_sc, wT_ref, h):
    for r0, r1 in _T_SECTIONS:
        pt_sc[r0:r1, :] = lax.dot_general(wT_ref[r0:r1, :], h, _NT, preferred_element_type=f32)


def _qblockdiag(qa, qb, lower):
    z = jnp.zeros((HEAD_DIM, LANE), bf16)
    if lower is None:
        return jnp.concatenate([jnp.concatenate([qa, z], axis=1),
                                jnp.concatenate([z, qb], axis=1)], axis=0)
    top = jnp.concatenate([qa, qb], axis=1)
    zz = jnp.zeros((HEAD_DIM, 2 * LANE), bf16)
    return jnp.concatenate([top, zz] if lower == 0 else [zz, top], axis=0)


def _with_ones(vT, group):
    ones = jnp.ones((ONES_ROWS, vT.shape[1]), vT.dtype)
    parts = []
    for r in range(0, vT.shape[0], group):
        parts += [vT[r:r + group, :], ones]
    return jnp.concatenate(parts, axis=0)


def _softmax_pv(s_m, sink, vwin):
    s, m = s_m
    if sink is not None:
        m = jnp.maximum(m, sink)
    e = jnp.exp2(s - m)
    o = jnp.dot(vwin, e.astype(bf16), preferred_element_type=f32)
    nv = vwin.shape[0] - ONES_ROWS
    den = o[nv:nv + 1, :]
    if sink is not None:
        den = den + jnp.exp2(sink - m)
    return o[:nv, :] * (1.0 / den)


def _scores(pt_sc, cols, kwin_b, bias_b, kwin_c, bias_c):
    def head_q(t0, hh):
        return (pt_sc[t0 + hh * HEAD_DIM:t0 + (hh + 1) * HEAD_DIM, cols] * (QK_SCALE * LOG2E)).astype(bf16)

    def with_max(s):
        return s, jnp.max(s, axis=0, keepdims=True)

    s_b, s_c = [], []
    for cb in range(B_HEADS // 2):
        ha, hb = 2 * cb, 2 * cb + 1
        qbd = _qblockdiag(head_q(T_BQ, ha), head_q(T_BQ, hb), ha // B_GROUP)
        s = jnp.dot(kwin_b, qbd, preferred_element_type=f32)
        s_b.append(with_max(s + bias_b[:, cb * 2 * LANE:(cb + 1) * 2 * LANE]))
    for cb in range(C_HEADS // 2):
        ha, hb = 2 * cb, 2 * cb + 1
        qbd = _qblockdiag(head_q(T_CQ, ha), head_q(T_CQ, hb), None)
        s = jnp.dot(kwin_c[:, cb * LANE:(cb + 1) * LANE], qbd, preferred_element_type=f32)
        s_c.append(with_max(s + bias_c[:, cb * 2 * LANE:(cb + 1) * 2 * LANE]))
    return s_b, s_c


def _attend(pt_sc, yT_sc, cols, s_b, s_c, vwinT_b, vwinT_c, sink_ref):
    for cb in range(B_HEADS // 2):
        ha, hb = 2 * cb, 2 * cb + 1
        kv = ha // B_GROUP
        sink = sink_ref[:, cb * 2 * LANE:(cb + 1) * 2 * LANE] * LOG2E
        o = _softmax_pv(s_b[cb], sink, vwinT_b[kv * VB_STRIDE:(kv + 1) * VB_STRIDE, :])
        for j, hh in enumerate((ha, hb)):
            gate = pt_sc[T_BZ + hh * HEAD_DIM:T_BZ + (hh + 1) * HEAD_DIM, cols]
            yT_sc[Y_B + hh * HEAD_DIM:Y_B + (hh + 1) * HEAD_DIM, cols] = (
                o[:, j * LANE:(j + 1) * LANE] * gate).astype(bf16)
    for cb in range(C_HEADS // 2):
        ha, hb = 2 * cb, 2 * cb + 1
        o = _softmax_pv(s_c[cb], None, vwinT_c[cb * VC_STRIDE:(cb + 1) * VC_STRIDE, :])
        for j, hh in enumerate((ha, hb)):
            gate = pt_sc[T_CZ + hh * HEAD_DIM:T_CZ + (hh + 1) * HEAD_DIM, cols]
            yT_sc[Y_C + hh * HEAD_DIM:Y_C + (hh + 1) * HEAD_DIM, cols] = (
                o[j * HEAD_DIM:(j + 1) * HEAD_DIM, j * LANE:(j + 1) * LANE] * gate).astype(bf16)


def _gate_rows(pt_sc):
    for r0, r1 in ((T_BZ, T_CQ), (T_CZ, T_END)):
        pt_sc[r0:r1, :] = jax.nn.silu(pt_sc[r0:r1, :])


def _finish(x, yT_sc, wout_ref, gpost_ref):
    out = lax.dot_general(yT_sc[...], wout_ref[...], _TN, preferred_element_type=f32)
    return x + _rms_rows(out, gpost_ref[...])


def _layer_spec(arr, layer):
    shape = arr.shape[1:]
    return pl.BlockSpec((None,) + shape, lambda *_: (layer,) + (0,) * len(shape),
                        pipeline_mode=pl.Buffered(1))


def _const_spec(arr):
    return pl.BlockSpec(arr.shape, lambda *_: (0,) * arr.ndim, pipeline_mode=pl.Buffered(1))


def _prompt_kernel(x_ref, gpre_ref, gpost_ref, wT_ref, wout_ref,
                   ang_ref, awsT_ref, abs_ref, sink_ref, biasb_ref, biasc_ref,
                   xo_ref, bk_ref, bv_ref, ck_ref, cv_ref,
                   pt_sc, kb_ring, kc_ring, vbT_ring, vcT_ring, yT_sc,
                   sb_even, mb_even, sc_even, mc_even, sb_odd, mb_odd, sc_odd, mc_odd):
    t = pl.program_id(1)
    ts = SEQ_TILE

    @pl.when(t == 0)
    def _():
        kb_ring[0:B_KEEP, :] = jnp.zeros((B_KEEP, B_KVW), bf16)
        kc_ring[0:C_KEEP, :] = jnp.zeros((C_KEEP, C_WIDTH), bf16)
        vbT_ring[:, 0:B_KEEP] = jnp.zeros((B_KV * VB_STRIDE, B_KEEP), bf16)
        vcT_ring[:, 0:C_KEEP] = jnp.zeros((C_HEADS // 2 * VC_STRIDE, C_KEEP), bf16)
        for g in range(B_KV):
            vbT_ring[g * VB_STRIDE + HEAD_DIM:(g + 1) * VB_STRIDE, :] = jnp.ones((ONES_ROWS, B_KEEP + ts), bf16)
        for g in range(C_HEADS // 2):
            vcT_ring[g * VC_STRIDE + 2 * HEAD_DIM:(g + 1) * VC_STRIDE, :] = jnp.ones((ONES_ROWS, C_KEEP + ts), bf16)

    x = x_ref[0]
    h = _rms_rows(x, gpre_ref[...]).astype(bf16)
    _project_T(pt_sc, wT_ref, h)
    kb_ring[B_KEEP:, :] = lax.dot_general(h, wT_ref[T_BK:T_BV, :], _NT,
                                          preferred_element_type=f32).astype(bf16)
    kc_ring[C_KEEP:, :] = lax.dot_general(h, wT_ref[T_CK:T_CV, :], _NT,
                                          preferred_element_type=f32).astype(bf16)
    for g in range(B_KV):
        vbT_ring[g * VB_STRIDE:g * VB_STRIDE + HEAD_DIM, B_KEEP:] = (
            pt_sc[T_BV + g * HEAD_DIM:T_BV + (g + 1) * HEAD_DIM, :].astype(bf16))
    for g in range(C_HEADS // 2):
        vcT_ring[g * VC_STRIDE:g * VC_STRIDE + 2 * HEAD_DIM, C_KEEP:] = (
            pt_sc[T_CV + g * 2 * HEAD_DIM:T_CV + (g + 1) * 2 * HEAD_DIM, :].astype(bf16))

    _gate_rows(pt_sc)

    @pl.when(t == pl.num_programs(1) - 1)
    def _():
        bk_ref[0] = lax.dot_general(wT_ref[T_BK:T_BV, :], h[ts - B_KEEP:, :], _NT, preferred_element_type=f32)
        ck_ref[0] = lax.dot_general(wT_ref[T_CK:T_CV, :], h[ts - C_KEEP:, :], _NT, preferred_element_type=f32)
        bv_ref[0] = pt_sc[T_BV:T_BZ, ts - B_KEEP:]
        cv_ref[0] = pt_sc[T_CV:T_CZ, ts - C_KEEP:]

    vn = _rms_cols(pt_sc[T_AV:T_AV + A_WIDTH, :], ang_ref[...]).astype(bf16)
    n_ac = ts // A_CHUNK
    row = lax.broadcasted_iota(jnp.int32, (A_CHUNK, A_CHUNK), 0)
    col = lax.broadcasted_iota(jnp.int32, (A_CHUNK, A_CHUNK), 1)
    for g in range(A_GROUPS):
        rows = slice(g * A_GD, (g + 1) * A_GD)
        wgt = jnp.where(row <= col, awsT_ref[g], 0.0).astype(bf16)
        lhs = jnp.concatenate([vn[rows, c * A_CHUNK:(c + 1) * A_CHUNK] for c in range(n_ac)], axis=0)
        mix = jnp.dot(lhs, wgt, preferred_element_type=f32) + abs_ref[g]
        for c in range(n_ac):
            cols = slice(c * A_CHUNK, (c + 1) * A_CHUNK)
            u = pt_sc[T_AU + g * A_GD:T_AU + (g + 1) * A_GD, cols]
            z = pt_sc[T_AZ + g * A_GD:T_AZ + (g + 1) * A_GD, cols]
            yT_sc[Y_A + g * A_GD:Y_A + (g + 1) * A_GD, cols] = (
                u * mix[c * A_GD:(c + 1) * A_GD, :] * jax.nn.silu(z)).astype(bf16)

    n_pairs = ts // PAIR

    even, odd = (sb_even, mb_even, sc_even, mc_even), (sb_odd, mb_odd, sc_odd, mc_odd)

    def scores_into(p, bufs):
        sb_ref, mb_ref, sc_ref, mc_ref = bufs
        pg = t * n_pairs + p
        start = pl.multiple_of(p * PAIR, PAIR)
        s_b, s_c = _scores(pt_sc, pl.ds(start, PAIR),
                           kb_ring[pl.ds(start, B_WIN), :], biasb_ref.at[jnp.minimum(pg, 1)],
                           kc_ring[pl.ds(start, C_WIN), :], biasc_ref.at[jnp.minimum(pg, C_PREV // 2)])
        for cb, (v, m) in enumerate(s_b):
            sb_ref[cb] = v
            mb_ref[cb] = m
        for cb, (v, m) in enumerate(s_c):
            sc_ref[cb] = v
            mc_ref[cb] = m

    def attend_from(p, bufs):
        sb_ref, mb_ref, sc_ref, mc_ref = bufs
        start = pl.multiple_of(p * PAIR, PAIR)
        _attend(pt_sc, yT_sc, pl.ds(start, PAIR),
                [(sb_ref[cb], mb_ref[cb]) for cb in range(B_HEADS // 2)],
                [(sc_ref[cb], mc_ref[cb]) for cb in range(C_HEADS // 2)],
                vbT_ring[:, pl.ds(start, B_WIN)], vcT_ring[:, pl.ds(start, C_WIN)], sink_ref)

    scores_into(0, even)

    def two_pairs(q, carry):
        p = 2 * q
        scores_into(p + 1, odd)
        attend_from(p, even)
        scores_into(jnp.minimum(p + 2, n_pairs - 1), even)
        attend_from(p + 1, odd)
        return carry

    lax.fori_loop(0, n_pairs // 2, two_pairs, 0)

    xo_ref[0] = _finish(x, yT_sc, wout_ref, gpost_ref)

    kb_ring[0:B_KEEP, :] = kb_ring[ts:ts + B_KEEP, :]
    kc_ring[0:C_KEEP, :] = kc_ring[ts:ts + C_KEEP, :]
    vbT_ring[:, 0:B_KEEP] = vbT_ring[:, ts:ts + B_KEEP]
    vcT_ring[:, 0:C_KEEP] = vcT_ring[:, ts:ts + C_KEEP]


def _prompt_layer(layer, x, p):
    nb, seq, d = x.shape
    ts = SEQ_TILE
    assert seq % ts == 0 and ts == C_KEEP
    kv_spec = lambda keep, width: pl.BlockSpec((1, width, keep), lambda b, t: (b, 0, 0))
    return pl.pallas_call(
        _prompt_kernel,
        grid=(nb, seq // ts),
        in_specs=[
            pl.BlockSpec((1, ts, d), lambda b, t: (b, t, 0)),
            _layer_spec(p["gpre"], layer), _layer_spec(p["gpost"], layer),
            _layer_spec(p["wT"], layer),
            _layer_spec(p["wout"], layer),
            _layer_spec(p["ang"], layer), _layer_spec(p["awsT"], layer), _layer_spec(p["abs"], layer),
            _layer_spec(p["sink"], layer), _const_spec(p["bias_pb"]), _layer_spec(p["bias_pc"], layer),
        ],
        out_specs=[
            pl.BlockSpec((1, ts, d), lambda b, t: (b, t, 0)),
            kv_spec(B_KEEP, B_KVW), kv_spec(B_KEEP, B_KVW), kv_spec(C_KEEP, C_WIDTH), kv_spec(C_KEEP, C_WIDTH),
        ],
        out_shape=[
            jax.ShapeDtypeStruct((nb, seq, d), f32),
            jax.ShapeDtypeStruct((nb, B_KVW, B_KEEP), f32),
            jax.ShapeDtypeStruct((nb, B_KVW, B_KEEP), f32),
            jax.ShapeDtypeStruct((nb, C_WIDTH, C_KEEP), f32),
            jax.ShapeDtypeStruct((nb, C_WIDTH, C_KEEP), f32),
        ],
        scratch_shapes=[
            pltpu.VMEM((T_END, ts), f32),
            pltpu.VMEM((B_KEEP + ts, B_KVW), bf16),
            pltpu.VMEM((C_KEEP + ts, C_WIDTH), bf16),
            pltpu.VMEM((B_KV * VB_STRIDE, B_KEEP + ts), bf16),
            pltpu.VMEM((C_HEADS // 2 * VC_STRIDE, C_KEEP + ts), bf16),
            pltpu.VMEM((A_WIDTH + B_WIDTH + C_WIDTH, ts), bf16),
        ] + 2 * [
            pltpu.VMEM((B_HEADS // 2, B_WIN, 2 * LANE), f32), pltpu.VMEM((B_HEADS // 2, 1, 2 * LANE), f32),
            pltpu.VMEM((C_HEADS // 2, C_WIN, 2 * LANE), f32), pltpu.VMEM((C_HEADS // 2, 1, 2 * LANE), f32),
        ],
        compiler_params=pltpu.CompilerParams(dimension_semantics=("arbitrary", "arbitrary"),
                                             vmem_limit_bytes=VMEM_LIMIT),
        name="prompt_layer",
    )(x, p["gpre"], p["gpost"], p["wT"], p["wout"], p["ang"], p["awsT"],
      p["abs"], p["sink"], p["bias_pb"], p["bias_pc"])


def _sample_kernel(x_ref, cbk_ref, cbv_ref, cck_ref, ccv_ref, gpre_ref, gpost_ref, wT_ref,
                   wout_ref, ang_ref, aws_ref, arep_ref, akeep_ref, abias_ref,
                   sink_ref, biasb_ref, biasc_ref,
                   xo_ref, bk_ref, bv_ref, ck_ref, cv_ref, av_ref,
                   pt_sc, yT_sc):
    ns = SAMPLE_STREAMS_PER_BLOCK
    x = x_ref[...]
    ntok = x.shape[0]
    h = _rms_rows(x, gpre_ref[...]).astype(bf16)
    _project_T(pt_sc, wT_ref, h)
    kvn_b = lax.dot_general(h, wT_ref[T_BK:T_BZ, :], _NT, preferred_element_type=f32)
    kvn_c = lax.dot_general(h, wT_ref[T_CK:T_CZ, :], _NT, preferred_element_type=f32)
    kn_c = kvn_c[:, 0:C_WIDTH]
    bk_ref[...] = kvn_b[:, 0:B_KVW]
    bv_ref[...] = kvn_b[:, B_KVW:]
    ck_ref[...] = kn_c
    cv_ref[...] = kvn_c[:, C_WIDTH:]

    vn = _rms_cols(pt_sc[T_AV:T_AV + A_WIDTH, :], ang_ref[...])
    av_ref[...] = vn.T
    vnb = vn.astype(bf16)
    keep = akeep_ref[...] > 0.0
    rep = arep_ref[...]
    for g in range(A_GROUPS):
        rows = slice(g * A_GD, (g + 1) * A_GD)
        tiled = lax.dot_general(jnp.dot(rep, aws_ref[g].astype(bf16), preferred_element_type=f32).astype(bf16),
                                rep, _NT, preferred_element_type=f32)
        wgt = jnp.where(keep, tiled, 0.0).astype(bf16)
        mix = jnp.dot(vnb[rows, :], wgt, preferred_element_type=f32) + abias_ref[g]
        u = pt_sc[T_AU + g * A_GD:T_AU + (g + 1) * A_GD, :]
        z = pt_sc[T_AZ + g * A_GD:T_AZ + (g + 1) * A_GD, :]
        yT_sc[Y_A + g * A_GD:Y_A + (g + 1) * A_GD, :] = (u * mix * jax.nn.silu(z)).astype(bf16)

    _gate_rows(pt_sc)
    for blk in range(ntok // LANE):
        cols = slice(blk * LANE, (blk + 1) * LANE)
        streams = range(blk * ns, (blk + 1) * ns)
        kwin_b = jnp.concatenate([cbk_ref[s].T.astype(bf16) for s in streams]
                                 + [kvn_b[cols, 0:B_KVW].astype(bf16)], axis=0)
        vwinT_b = jnp.concatenate([cbv_ref[s].astype(bf16) for s in streams]
                                  + [pt_sc[T_BV:T_BV + B_KVW, cols].astype(bf16)], axis=1)
        kwin_c = jnp.concatenate([cck_ref[s].T.astype(bf16) for s in streams]
                                 + [kn_c[cols, :].astype(bf16)], axis=0)
        vwinT_c = jnp.concatenate([ccv_ref[s].astype(bf16) for s in streams]
                                  + [pt_sc[T_CV:T_CV + C_WIDTH, cols].astype(bf16)], axis=1)
        s_b, s_c = _scores(pt_sc, cols, kwin_b, biasb_ref, kwin_c, biasc_ref)
        _attend(pt_sc, yT_sc, cols, s_b, s_c, _with_ones(vwinT_b, HEAD_DIM), _with_ones(vwinT_c, 2 * HEAD_DIM),
                sink_ref)

    xo_ref[...] = _finish(x, yT_sc, wout_ref, gpost_ref)


def _sample_layer(layer, x, caches, p, *, t_new, tok_per_step):
    ntok, d = x.shape
    spb = tok_per_step // t_new
    cbk, cbv, cck, ccv = caches

    def rows(width):
        return pl.BlockSpec((tok_per_step, width), lambda i: (i, 0))

    def cache(c):
        return pl.BlockSpec((None, spb) + c.shape[2:], lambda i: (layer, i, 0, 0))

    return pl.pallas_call(
        _sample_kernel,
        grid=(ntok // tok_per_step,),
        in_specs=[
            rows(d), cache(cbk), cache(cbv), cache(cck), cache(ccv),
            _layer_spec(p["gpre"], layer), _layer_spec(p["gpost"], layer),
            _layer_spec(p["wT"], layer),
            _layer_spec(p["wout"], layer),
            _layer_spec(p["ang"], layer), _layer_spec(p["aws_s"], layer), _const_spec(p["arep"]),
            _const_spec(p["akeep"]), _layer_spec(p["abias_s"], layer),
            _layer_spec(p["sink"], layer), _const_spec(p["bias_sb"]), _layer_spec(p["bias_sc"], layer),
        ],
        out_specs=[rows(d), rows(B_KVW), rows(B_KVW), rows(C_WIDTH), rows(C_WIDTH), rows(A_WIDTH)],
        out_shape=[
            jax.ShapeDtypeStruct((ntok, d), f32),
            jax.ShapeDtypeStruct((ntok, B_KVW), f32), jax.ShapeDtypeStruct((ntok, B_KVW), f32),
            jax.ShapeDtypeStruct((ntok, C_WIDTH), f32), jax.ShapeDtypeStruct((ntok, C_WIDTH), f32),
            jax.ShapeDtypeStruct((ntok, A_WIDTH), f32),
        ],
        scratch_shapes=[
            pltpu.VMEM((T_END, tok_per_step), f32),
            pltpu.VMEM((A_WIDTH + B_WIDTH + C_WIDTH, tok_per_step), bf16),
        ],
        compiler_params=pltpu.CompilerParams(dimension_semantics=("arbitrary",),
                                             vmem_limit_bytes=VMEM_LIMIT),
        name="sample_layer",
    )(x, cbk, cbv, cck, ccv, p["gpre"], p["gpost"], p["wT"], p["wout"],
      p["ang"], p["aws_s"], p["arep"], p["akeep"], p["abias_s"], p["sink"], p["bias_sb"], p["bias_sc"])


def kernel(x_prompt, x_sample, cache_b_k, cache_b_v, cache_c_k, cache_c_v, g_pre, g_post, w_in, w_out,
           a_norm_g, a_ws, a_bs, b_sinks, c_rel_bias, t5_bias):
    depth = w_in.shape[0]
    nb, seq, d = x_prompt.shape
    ns_all, t_new, _ = x_sample.shape
    lc_b, lc_c = cache_b_k.shape[2], cache_c_k.shape[2]
    assert lc_b == B_KEEP and lc_c == C_KEEP and LANE % t_new == 0
    assert SAMPLE_STREAMS_PER_BLOCK * t_new == LANE
    tok_per_step = 2 * LANE
    reps = tok_per_step // t_new

    rel_pb, ok_pb, idx_pc = _prompt_maps()
    rel_sb, ok_sb, idx_sc = _sample_maps(lc_b, lc_c, t_new)
    rel_lo = int(min(rel_pb.min(), rel_sb.min()))
    rel_hi = int(max(rel_pb.max(), rel_sb.max()))
    t5_rel = t5_bias[_t5_bucket(jnp.arange(rel_lo, rel_hi + 1))][None]
    idx_pb = np.where(ok_pb, rel_pb - rel_lo, -1).astype(np.int32)
    idx_sb = np.where(ok_sb, rel_sb - rel_lo, -1).astype(np.int32)
    r_ = np.arange(tok_per_step)
    awsT = a_ws.transpose(0, 1, 3, 2)

    p = dict(
        bias_pb=_build_bias(t5_rel, jnp.asarray(idx_pb), ranges=_static_ranges(idx_pb), kmins=(B_PREV, 0))[0],
        bias_sb=_build_bias(t5_rel, jnp.asarray(idx_sb), ranges=_static_ranges(idx_sb), kmins=(0,))[0, 0],
        bias_pc=_build_bias(c_rel_bias, jnp.asarray(idx_pc), ranges=_static_ranges(idx_pc),
                            kmins=tuple(max(C_PREV - 2 * v, 0) for v in range(C_PREV // 2 + 1))),
        bias_sc=_build_bias(c_rel_bias, jnp.asarray(idx_sc), ranges=_static_ranges(idx_sc),
                            kmins=(0,))[:, 0],
        wT=w_in.astype(bf16).transpose(0, 2, 1),
        wout=w_out.astype(bf16),
        gpre=g_pre[:, None, :],
        gpost=g_post[:, None, :],
        ang=a_norm_g[:, :, None],
        awsT=awsT,
        abs=a_bs[:, :, None, :],
        sink=jnp.repeat(b_sinks, LANE, axis=1)[:, None, :],
        aws_s=awsT[:, :, :t_new, :t_new],
        arep=jnp.asarray((r_[:, None] % t_new == np.arange(t_new)[None, :]).astype(np.float32), bf16),
        abias_s=jnp.tile(a_bs[:, :, None, :t_new], (1, 1, 1, reps)),
        akeep=jnp.asarray(((r_[:, None] // t_new == r_[None, :] // t_new)
                           & (r_[:, None] % t_new <= r_[None, :] % t_new)).astype(np.float32)),
    )

    xp = x_prompt
    xs = x_sample.reshape(ns_all * t_new, d)
    caches = tuple(c.transpose(0, 1, 3, 4, 2).reshape(depth, ns_all, -1, c.shape[2])
                   for c in (cache_b_k, cache_b_v, cache_c_k, cache_c_v))

    pk, pv, pck, pcv = [], [], [], []
    sk, sv, sck, scv, sav = [], [], [], [], []
    for l in range(depth):
        xp, bk, bv, ck, cv = _prompt_layer(l, xp, p)
        pk.append(bk); pv.append(bv); pck.append(ck); pcv.append(cv)
        xs, bk, bv, ck, cv, av = _sample_layer(l, xs, caches, p, t_new=t_new, tok_per_step=tok_per_step)
        sk.append(bk); sv.append(bv); sck.append(ck); scv.append(cv); sav.append(av)

    def stk(xs_, lead, heads):
        return jnp.stack(xs_).reshape(depth, lead, -1, heads, HEAD_DIM)

    def stk_fm(xs_, heads):
        a = jnp.stack(xs_)
        return a.reshape(depth, nb, heads, HEAD_DIM, a.shape[-1]).transpose(0, 1, 4, 2, 3)

    return (xp, xs.reshape(ns_all, t_new, d),
            stk_fm(pk, B_KV), stk_fm(pv, B_KV), stk_fm(pck, C_HEADS), stk_fm(pcv, C_HEADS),
            stk(sk, ns_all, B_KV), stk(sv, ns_all, B_KV), stk(sck, ns_all, C_HEADS), stk(scv, ns_all, C_HEADS),
            jnp.stack(sav).reshape(depth, ns_all, t_new, A_WIDTH))
```

```python
import functools
import math

import numpy as np
import jax
import jax.numpy as jnp
from jax import lax
from jax.experimental import pallas as pl
from jax.experimental.pallas import tpu as pltpu

bf16 = jnp.bfloat16
f32 = jnp.float32

HEAD_DIM = 64
CHUNK = 64
A_WIDTH, A_GROUPS, A_CHUNK = 256, 4, 128
A_GD = A_WIDTH // A_GROUPS
B_HEADS, B_KV = 8, 2
B_GROUP = B_HEADS // B_KV
B_WIDTH, B_KVW = B_HEADS * HEAD_DIM, B_KV * HEAD_DIM
B_PREV = 2
C_HEADS = 4
C_WIDTH = C_HEADS * HEAD_DIM
C_PREV = 8
C_CLIP = 128
T5_BUCKETS, T5_MAX_DIST = 32, 128
RMS_EPS = 1e-6
NEG_INF = -1e30
QK_SCALE = HEAD_DIM ** -0.5
LOG2E = math.log2(math.e)

LANE = 128
PAIR = 2 * CHUNK
B_WIN = (B_PREV + 2) * CHUNK
C_WIN = (C_PREV + 2) * CHUNK
B_KEEP = B_PREV * CHUNK
C_KEEP = C_PREV * CHUNK
ONES_ROWS = 16
VB_STRIDE = HEAD_DIM + ONES_ROWS
VC_STRIDE = 2 * HEAD_DIM + ONES_ROWS
SEQ_TILE = 512
SAMPLE_STREAMS_PER_BLOCK = 4
VMEM_LIMIT = 56 * 1024 * 1024
LUT_UNROLL = 8

_IN_SIZES = [A_WIDTH, A_WIDTH, A_WIDTH, B_WIDTH, B_KVW, B_KVW, B_WIDTH, C_WIDTH, C_WIDTH, C_WIDTH, C_WIDTH]
_IN_OFF = [int(v) for v in np.cumsum([0] + _IN_SIZES)]
(T_AU, T_AV, T_AZ, T_BQ, T_BK, T_BV, T_BZ, T_CQ, T_CK, T_CV, T_CZ, T_END) = _IN_OFF
_T_SECTIONS = ((T_AU, T_BK), (T_BV, T_CK), (T_CV, T_END))
Y_A, Y_B, Y_C = 0, A_WIDTH, A_WIDTH + B_WIDTH

_NT = (((1,), (1,)), ((), ()))
_TN = (((0,), (0,)), ((), ()))


def _t5_bucket(rel):
    half = T5_BUCKETS // 2
    max_exact = half // 2
    ret = jnp.where(rel > 0, half, 0)
    n = jnp.abs(rel)
    nf = jnp.maximum(n, 1).astype(jnp.float32)
    large = max_exact + (jnp.log(nf / max_exact) / math.log(T5_MAX_DIST / max_exact)
                         * (half - max_exact)).astype(jnp.int32)
    large = jnp.minimum(large, half - 1)
    return ret + jnp.where(n < max_exact, n, large)


def _lut_kernel(tab_ref, idx_ref, out_ref, *, n_entries, n_heads, ranges, kmins):
    layer = pl.program_id(0)
    neg = jnp.full((CHUNK, LANE), NEG_INF, f32)
    hgroup = min(n_heads, 4)
    for rb, (lo, hi) in enumerate(ranges):
        rows = slice(rb * CHUNK, (rb + 1) * CHUNK)
        idx = idx_ref[rows, :]
        for h0 in range(0, n_heads, hgroup):
            if hi < lo:
                base = (neg,) * hgroup
            else:
                def body(e, accs, h0=h0, idx=idx):
                    hit = idx == e
                    off = (layer * n_entries + e) * n_heads + h0
                    return tuple(jnp.where(hit, tab_ref[off + j], a) for j, a in enumerate(accs))
                base = lax.fori_loop(lo, hi + 1, body, (neg,) * hgroup,
                                     unroll=min(LUT_UNROLL, hi + 1 - lo))
            for j in range(hgroup):
                scaled = base[j] * LOG2E
                for v, kmin in enumerate(kmins):
                    out_ref[0, v, rows, (h0 + j) * LANE:(h0 + j + 1) * LANE] = neg if rb < kmin else scaled


def _build_bias(table, idx, *, ranges, kmins):
    n_layers, n_entries, n_heads = table.shape
    n_rows = idx.shape[0]
    kern = functools.partial(_lut_kernel, n_entries=n_entries, n_heads=n_heads,
                             ranges=tuple(ranges), kmins=tuple(kmins))
    return pl.pallas_call(
        kern,
        grid=(n_layers,),
        in_specs=[pl.BlockSpec(memory_space=pltpu.SMEM),
                  pl.BlockSpec((n_rows, LANE), lambda l: (0, 0))],
        out_specs=pl.BlockSpec((1, len(kmins), n_rows, n_heads * LANE), lambda l: (l, 0, 0, 0)),
        out_shape=jax.ShapeDtypeStruct((n_layers, len(kmins), n_rows, n_heads * LANE), f32),
        compiler_params=pltpu.CompilerParams(dimension_semantics=("arbitrary",),
                                             vmem_limit_bytes=VMEM_LIMIT),
        name="bias_lut",
    )(table.reshape(-1), idx)


def _wT_kernel(w_ref, o_ref):
    o_ref[...] = w_ref[...].T.astype(bf16)


def _transpose_weights(w, cols=4 * LANE):
    n_layers, k, n = w.shape
    return pl.pallas_call(
        _wT_kernel,
        grid=(n_layers, n // cols),
        in_specs=[pl.BlockSpec((None, k, cols), lambda l, c: (l, 0, c))],
        out_specs=pl.BlockSpec((None, cols, k), lambda l, c: (l, c, 0)),
        out_shape=jax.ShapeDtypeStruct((n_layers, n, k), bf16),
        compiler_params=pltpu.CompilerParams(dimension_semantics=("arbitrary", "arbitrary"),
                                             vmem_limit_bytes=VMEM_LIMIT),
        name="weights_T",
    )(w)


def _static_ranges(idx_np):
    out = []
    for rb in range(idx_np.shape[0] // CHUNK):
        blk = idx_np[rb * CHUNK:(rb + 1) * CHUNK]
        good = blk[blk >= 0]
        out.append((int(good.min()), int(good.max())) if good.size else (0, -1))
    return out


def _prompt_maps():
    i = np.arange(PAIR)[None, :]
    qi = i // CHUNK
    jb = np.arange(B_WIN)[:, None]
    rel_b = jb - B_KEEP - i
    ok_b = (jb // CHUNK >= qi) & (jb // CHUNK <= qi + B_PREV)
    jc = np.arange(C_WIN)[:, None]
    rel_c = jc - C_KEEP - i
    ok_c = (jc // CHUNK >= qi) & (jc // CHUNK <= qi + C_PREV)
    idx_c = np.where(ok_c, np.clip(rel_c, -C_CLIP, C_CLIP) + C_CLIP, -1).astype(np.int32)
    return rel_b.astype(np.int32), ok_b, idx_c


def _sample_maps(lc_b, lc_c, t_new):
    ns = SAMPLE_STREAMS_PER_BLOCK
    c = np.arange(LANE)[None, :]
    sq, i = c // t_new, c % t_new

    def one(lc):
        r = np.arange(ns * lc + LANE)[:, None]
        is_new = r >= ns * lc
        sk = np.where(is_new, (r - ns * lc) // t_new, r // lc)
        j = np.where(is_new, lc + (r - ns * lc) % t_new, r % lc)
        return (j - lc - i).astype(np.int32), (sk == sq)

    rel_b, ok_b = one(lc_b)
    rel_c, ok_c = one(lc_c)
    idx_c = np.where(ok_c, np.clip(rel_c, -C_CLIP, C_CLIP) + C_CLIP, -1).astype(np.int32)
    return rel_b, ok_b, idx_c


def _rms_rows(x, g):
    ms = jnp.mean(x * x, axis=-1, keepdims=True)
    return x * lax.rsqrt(ms + RMS_EPS) * g


def _rms_cols(xT, g_col):
    ms = jnp.mean(xT * xT, axis=0, keepdims=True)
    return xT * lax.rsqrt(ms + RMS_EPS) * g_col


def _project_T(pt_sc, wT_ref, h):
    for r0, r1 in _T_SECTIONS:
        pt_sc[r0:r1, :] = lax.dot_general(wT_ref[r0:r1, :], h, _NT, preferred_element_type=f32)


def _qblockdiag(qa, qb, lower):
    z = jnp.zeros((HEAD_DIM, LANE), bf16)
    if lower is None:
        return jnp.concatenate([jnp.concatenate([qa, z], axis=1),
                                jnp.concatenate([z, qb], axis=1)], axis=0)
    top = jnp.concatenate([qa, qb], axis=1)
    zz = jnp.zeros((HEAD_DIM, 2 * LANE), bf16)
    return jnp.concatenate([top, zz] if lower == 0 else [zz, top], axis=0)


def _with_ones(vT, group):
    ones = jnp.ones((ONES_ROWS, vT.shape[1]), vT.dtype)
    parts = []
    for r in range(0, vT.shape[0], group):
        parts += [vT[r:r + group, :], ones]
    return jnp.concatenate(parts, axis=0)


def _softmax_pv(s_m, sink, vwin):
    s, m = s_m
    if sink is not None:
        m = jnp.maximum(m, sink)
    e = jnp.exp2(s - m)
    o = jnp.dot(vwin, e.astype(bf16), preferred_element_type=f32)
    nv = vwin.shape[0] - ONES_ROWS
    den = o[nv:nv + 1, :]
    if sink is not None:
        den = den + jnp.exp2(sink - m)
    return o[:nv, :] * (1.0 / den)


def _scores(pt_sc, cols, kwin_b, bias_b, kwin_c, bias_c):
    def head_q(t0, hh):
        return (pt_sc[t0 + hh * HEAD_DIM:t0 + (hh + 1) * HEAD_DIM, cols] * (QK_SCALE * LOG2E)).astype(bf16)

    def with_max(s):
        return s, jnp.max(s, axis=0, keepdims=True)

    s_b, s_c = [], []
    for cb in range(B_HEADS // 2):
        ha, hb = 2 * cb, 2 * cb + 1
        qbd = _qblockdiag(head_q(T_BQ, ha), head_q(T_BQ, hb), ha // B_GROUP)
        s = jnp.dot(kwin_b, qbd, preferred_element_type=f32)
        s_b.append(with_max(s + bias_b[:, cb * 2 * LANE:(cb + 1) * 2 * LANE]))
    for cb in range(C_HEADS // 2):
        ha, hb = 2 * cb, 2 * cb + 1
        qbd = _qblockdiag(head_q(T_CQ, ha), head_q(T_CQ, hb), None)
        s = jnp.dot(kwin_c[:, cb * LANE:(cb + 1) * LANE], qbd, preferred_element_type=f32)
        s_c.append(with_max(s + bias_c[:, cb * 2 * LANE:(cb + 1) * 2 * LANE]))
    return s_b, s_c


def _attend(pt_sc, yT_sc, cols, s_b, s_c, vwinT_b, vwinT_c, sink_ref):
    for cb in range(B_HEADS // 2):
        ha, hb = 2 * cb, 2 * cb + 1
        kv = ha // B_GROUP
        sink = sink_ref[:, cb * 2 * LANE:(cb + 1) * 2 * LANE] * LOG2E
        o = _softmax_pv(s_b[cb], sink, vwinT_b[kv * VB_STRIDE:(kv + 1) * VB_STRIDE, :])
        for j, hh in enumerate((ha, hb)):
            gate = pt_sc[T_BZ + hh * HEAD_DIM:T_BZ + (hh + 1) * HEAD_DIM, cols]
            yT_sc[Y_B + hh * HEAD_DIM:Y_B + (hh + 1) * HEAD_DIM, cols] = (
                o[:, j * LANE:(j + 1) * LANE] * gate).astype(bf16)
    for cb in range(C_HEADS // 2):
        ha, hb = 2 * cb, 2 * cb + 1
        o = _softmax_pv(s_c[cb], None, vwinT_c[cb * VC_STRIDE:(cb + 1) * VC_STRIDE, :])
        for j, hh in enumerate((ha, hb)):
            gate = pt_sc[T_CZ + hh * HEAD_DIM:T_CZ + (hh + 1) * HEAD_DIM, cols]
            yT_sc[Y_C + hh * HEAD_DIM:Y_C + (hh + 1) * HEAD_DIM, cols] = (
                o[j * HEAD_DIM:(j + 1) * HEAD_DIM, j * LANE:(j + 1) * LANE] * gate).astype(bf16)


def _gate_rows(pt_sc):
    for r0, r1 in ((T_BZ, T_CQ), (T_CZ, T_END)):
        pt_sc[r0:r1, :] = jax.nn.silu(pt_sc[r0:r1, :])


def _finish(x, yT_sc, wout_ref, gpost_ref):
    out = lax.dot_general(yT_sc[...], wout_ref[...], _TN, preferred_element_type=f32)
    return x + _rms_rows(out, gpost_ref[...])


def _layer_spec(arr, layer):
    shape = arr.shape[1:]
    return pl.BlockSpec((None,) + shape, lambda *_: (layer,) + (0,) * len(shape),
                        pipeline_mode=pl.Buffered(1))


def _const_spec(arr):
    return pl.BlockSpec(arr.shape, lambda *_: (0,) * arr.ndim, pipeline_mode=pl.Buffered(1))


def _prompt_kernel(x_ref, gpre_ref, gpost_ref, wT_ref, wout_ref,
                   ang_ref, awsT_ref, abs_ref, sink_ref, biasb_ref, biasc_ref,
                   xo_ref, bk_ref, bv_ref, ck_ref, cv_ref,
                   pt_sc, kb_ring, kc_ring, vbT_ring, vcT_ring, yT_sc,
                   sb_even, mb_even, sc_even, mc_even, sb_odd, mb_odd, sc_odd, mc_odd):
    t = pl.program_id(1)
    ts = SEQ_TILE

    @pl.when(t == 0)
    def _():
        kb_ring[0:B_KEEP, :] = jnp.zeros((B_KEEP, B_KVW), bf16)
        kc_ring[0:C_KEEP, :] = jnp.zeros((C_KEEP, C_WIDTH), bf16)
        vbT_ring[:, 0:B_KEEP] = jnp.zeros((B_KV * VB_STRIDE, B_KEEP), bf16)
        vcT_ring[:, 0:C_KEEP] = jnp.zeros((C_HEADS // 2 * VC_STRIDE, C_KEEP), bf16)
        for g in range(B_KV):
            vbT_ring[g * VB_STRIDE + HEAD_DIM:(g + 1) * VB_STRIDE, :] = jnp.ones((ONES_ROWS, B_KEEP + ts), bf16)
        for g in range(C_HEADS // 2):
            vcT_ring[g * VC_STRIDE + 2 * HEAD_DIM:(g + 1) * VC_STRIDE, :] = jnp.ones((ONES_ROWS, C_KEEP + ts), bf16)

    x = x_ref[0]
    h = _rms_rows(x, gpre_ref[...]).astype(bf16)
    _project_T(pt_sc, wT_ref, h)
    kb_ring[B_KEEP:, :] = lax.dot_general(h, wT_ref[T_BK:T_BV, :], _NT,
                                          preferred_element_type=f32).astype(bf16)
    kc_ring[C_KEEP:, :] = lax.dot_general(h, wT_ref[T_CK:T_CV, :], _NT,
                                          preferred_element_type=f32).astype(bf16)
    for g in range(B_KV):
        vbT_ring[g * VB_STRIDE:g * VB_STRIDE + HEAD_DIM, B_KEEP:] = (
            pt_sc[T_BV + g * HEAD_DIM:T_BV + (g + 1) * HEAD_DIM, :].astype(bf16))
    for g in range(C_HEADS // 2):
        vcT_ring[g * VC_STRIDE:g * VC_STRIDE + 2 * HEAD_DIM, C_KEEP:] = (
            pt_sc[T_CV + g * 2 * HEAD_DIM:T_CV + (g + 1) * 2 * HEAD_DIM, :].astype(bf16))

    _gate_rows(pt_sc)

    @pl.when(t == pl.num_programs(1) - 1)
    def _():
        bk_ref[0] = lax.dot_general(wT_ref[T_BK:T_BV, :], h[ts - B_KEEP:, :], _NT, preferred_element_type=f32)
        ck_ref[0] = lax.dot_general(wT_ref[T_CK:T_CV, :], h[ts - C_KEEP:, :], _NT, preferred_element_type=f32)
        bv_ref[0] = pt_sc[T_BV:T_BZ, ts - B_KEEP:]
        cv_ref[0] = pt_sc[T_CV:T_CZ, ts - C_KEEP:]

    vn = _rms_cols(pt_sc[T_AV:T_AV + A_WIDTH, :], ang_ref[...]).astype(bf16)
    n_ac = ts // A_CHUNK
    row = lax.broadcasted_iota(jnp.int32, (A_CHUNK, A_CHUNK), 0)
    col = lax.broadcasted_iota(jnp.int32, (A_CHUNK, A_CHUNK), 1)
    for g in range(A_GROUPS):
        rows = slice(g * A_GD, (g + 1) * A_GD)
        wgt = jnp.where(row <= col, awsT_ref[g], 0.0).astype(bf16)
        lhs = jnp.concatenate([vn[rows, c * A_CHUNK:(c + 1) * A_CHUNK] for c in range(n_ac)], axis=0)
        mix = jnp.dot(lhs, wgt, preferred_element_type=f32) + abs_ref[g]
        for c in range(n_ac):
            cols = slice(c * A_CHUNK, (c + 1) * A_CHUNK)
            u = pt_sc[T_AU + g * A_GD:T_AU + (g + 1) * A_GD, cols]
            z = pt_sc[T_AZ + g * A_GD:T_AZ + (g + 1) * A_GD, cols]
            yT_sc[Y_A + g * A_GD:Y_A + (g + 1) * A_GD, cols] = (
                u * mix[c * A_GD:(c + 1) * A_GD, :] * jax.nn.silu(z)).astype(bf16)

    n_pairs = ts // PAIR

    even, odd = (sb_even, mb_even, sc_even, mc_even), (sb_odd, mb_odd, sc_odd, mc_odd)

    def scores_into(p, bufs):
        sb_ref, mb_ref, sc_ref, mc_ref = bufs
        pg = t * n_pairs + p
        start = p * PAIR if isinstance(p, int) else pl.multiple_of(p * PAIR, PAIR)
        s_b, s_c = _scores(pt_sc, pl.ds(start, PAIR),
                           kb_ring[pl.ds(start, B_WIN), :], biasb_ref.at[jnp.minimum(pg, 1)],
                           kc_ring[pl.ds(start, C_WIN), :], biasc_ref.at[jnp.minimum(pg, C_PREV // 2)])
        for cb, (v, m) in enumerate(s_b):
            sb_ref[cb] = v
            mb_ref[cb] = m
        for cb, (v, m) in enumerate(s_c):
            sc_ref[cb] = v
            mc_ref[cb] = m

    def attend_from(p, bufs):
        sb_ref, mb_ref, sc_ref, mc_ref = bufs
        start = p * PAIR if isinstance(p, int) else pl.multiple_of(p * PAIR, PAIR)
        _attend(pt_sc, yT_sc, pl.ds(start, PAIR),
                [(sb_ref[cb], mb_ref[cb]) for cb in range(B_HEADS // 2)],
                [(sc_ref[cb], mc_ref[cb]) for cb in range(C_HEADS // 2)],
                vbT_ring[:, pl.ds(start, B_WIN)], vcT_ring[:, pl.ds(start, C_WIN)], sink_ref)

    scores_into(0, even)

    def two_pairs(q, carry):
        p = 2 * q
        scores_into(p + 1, odd)
        attend_from(p, even)
        scores_into(jnp.minimum(p + 2, n_pairs - 1), even)
        attend_from(p + 1, odd)
        return carry

    lax.fori_loop(0, n_pairs // 2, two_pairs, 0)

    xo_ref[0] = _finish(x, yT_sc, wout_ref, gpost_ref)

    kb_ring[0:B_KEEP, :] = kb_ring[ts:ts + B_KEEP, :]
    kc_ring[0:C_KEEP, :] = kc_ring[ts:ts + C_KEEP, :]
    vbT_ring[:, 0:B_KEEP] = vbT_ring[:, ts:ts + B_KEEP]
    vcT_ring[:, 0:C_KEEP] = vcT_ring[:, ts:ts + C_KEEP]


def _prompt_layer(layer, x, p):
    nb, seq, d = x.shape
    ts = SEQ_TILE
    assert seq % ts == 0 and ts == C_KEEP
    kv_spec = lambda keep, width: pl.BlockSpec((1, width, keep), lambda b, t: (b, 0, 0))
    return pl.pallas_call(
        _prompt_kernel,
        grid=(nb, seq // ts),
        in_specs=[
            pl.BlockSpec((1, ts, d), lambda b, t: (b, t, 0)),
            _layer_spec(p["gpre"], layer), _layer_spec(p["gpost"], layer),
            _layer_spec(p["wT"], layer),
            _layer_spec(p["wout"], layer),
            _layer_spec(p["ang"], layer), _layer_spec(p["awsT"], layer), _layer_spec(p["abs"], layer),
            _layer_spec(p["sink"], layer), _const_spec(p["bias_pb"]), _layer_spec(p["bias_pc"], layer),
        ],
        out_specs=[
            pl.BlockSpec((1, ts, d), lambda b, t: (b, t, 0)),
            kv_spec(B_KEEP, B_KVW), kv_spec(B_KEEP, B_KVW), kv_spec(C_KEEP, C_WIDTH), kv_spec(C_KEEP, C_WIDTH),
        ],
        out_shape=[
            jax.ShapeDtypeStruct((nb, seq, d), f32),
            jax.ShapeDtypeStruct((nb, B_KVW, B_KEEP), f32),
            jax.ShapeDtypeStruct((nb, B_KVW, B_KEEP), f32),
            jax.ShapeDtypeStruct((nb, C_WIDTH, C_KEEP), f32),
            jax.ShapeDtypeStruct((nb, C_WIDTH, C_KEEP), f32),
        ],
        scratch_shapes=[
            pltpu.VMEM((T_END, ts), f32),
            pltpu.VMEM((B_KEEP + ts, B_KVW), bf16),
            pltpu.VMEM((C_KEEP + ts, C_WIDTH), bf16),
            pltpu.VMEM((B_KV * VB_STRIDE, B_KEEP + ts), bf16),
            pltpu.VMEM((C_HEADS // 2 * VC_STRIDE, C_KEEP + ts), bf16),
            pltpu.VMEM((A_WIDTH + B_WIDTH + C_WIDTH, ts), bf16),
        ] + 2 * [
            pltpu.VMEM((B_HEADS // 2, B_WIN, 2 * LANE), f32), pltpu.VMEM((B_HEADS // 2, 1, 2 * LANE), f32),
            pltpu.VMEM((C_HEADS // 2, C_WIN, 2 * LANE), f32), pltpu.VMEM((C_HEADS // 2, 1, 2 * LANE), f32),
        ],
        compiler_params=pltpu.CompilerParams(dimension_semantics=("arbitrary", "arbitrary"),
                                             vmem_limit_bytes=VMEM_LIMIT),
        name="prompt_layer",
    )(x, p["gpre"], p["gpost"], p["wT"], p["wout"], p["ang"], p["awsT"],
      p["abs"], p["sink"], p["bias_pb"], p["bias_pc"])


def _sample_kernel(x_ref, cbk_ref, cbv_ref, cck_ref, ccv_ref, gpre_ref, gpost_ref, wT_ref,
                   wout_ref, ang_ref, aws_ref, arep_ref, akeep_ref, abias_ref,
                   sink_ref, biasb_ref, biasc_ref,
                   xo_ref, bk_ref, bv_ref, ck_ref, cv_ref, av_ref,
                   pt_sc, yT_sc):
    ns = SAMPLE_STREAMS_PER_BLOCK
    x = x_ref[...]
    ntok = x.shape[0]
    h = _rms_rows(x, gpre_ref[...]).astype(bf16)
    _project_T(pt_sc, wT_ref, h)
    kvn_b = lax.dot_general(h, wT_ref[T_BK:T_BZ, :], _NT, preferred_element_type=f32)
    kvn_c = lax.dot_general(h, wT_ref[T_CK:T_CZ, :], _NT, preferred_element_type=f32)
    kn_c = kvn_c[:, 0:C_WIDTH]
    bk_ref[...] = kvn_b[:, 0:B_KVW]
    bv_ref[...] = kvn_b[:, B_KVW:]
    ck_ref[...] = kn_c
    cv_ref[...] = kvn_c[:, C_WIDTH:]

    vn = _rms_cols(pt_sc[T_AV:T_AV + A_WIDTH, :], ang_ref[...])
    av_ref[...] = vn.T
    vnb = vn.astype(bf16)
    keep = akeep_ref[...] > 0.0
    rep = arep_ref[...]
    for g in range(A_GROUPS):
        rows = slice(g * A_GD, (g + 1) * A_GD)
        tiled = lax.dot_general(jnp.dot(rep, aws_ref[g].astype(bf16), preferred_element_type=f32).astype(bf16),
                                rep, _NT, preferred_element_type=f32)
        wgt = jnp.where(keep, tiled, 0.0).astype(bf16)
        mix = jnp.dot(vnb[rows, :], wgt, preferred_element_type=f32) + abias_ref[g]
        u = pt_sc[T_AU + g * A_GD:T_AU + (g + 1) * A_GD, :]
        z = pt_sc[T_AZ + g * A_GD:T_AZ + (g + 1) * A_GD, :]
        yT_sc[Y_A + g * A_GD:Y_A + (g + 1) * A_GD, :] = (u * mix * jax.nn.silu(z)).astype(bf16)

    _gate_rows(pt_sc)
    for blk in range(ntok // LANE):
        cols = slice(blk * LANE, (blk + 1) * LANE)
        streams = range(blk * ns, (blk + 1) * ns)
        kwin_b = jnp.concatenate([cbk_ref[s].T.astype(bf16) for s in streams]
                                 + [kvn_b[cols, 0:B_KVW].astype(bf16)], axis=0)
        vwinT_b = jnp.concatenate([cbv_ref[s].astype(bf16) for s in streams]
                                  + [pt_sc[T_BV:T_BV + B_KVW, cols].astype(bf16)], axis=1)
        kwin_c = jnp.concatenate([cck_ref[s].T.astype(bf16) for s in streams]
                                 + [kn_c[cols, :].astype(bf16)], axis=0)
        vwinT_c = jnp.concatenate([ccv_ref[s].astype(bf16) for s in streams]
                                  + [pt_sc[T_CV:T_CV + C_WIDTH, cols].astype(bf16)], axis=1)
        s_b, s_c = _scores(pt_sc, cols, kwin_b, biasb_ref, kwin_c, biasc_ref)
        _attend(pt_sc, yT_sc, cols, s_b, s_c, _with_ones(vwinT_b, HEAD_DIM), _with_ones(vwinT_c, 2 * HEAD_DIM),
                sink_ref)

    xo_ref[...] = _finish(x, yT_sc, wout_ref, gpost_ref)


def _sample_layer(layer, x, caches, p, *, t_new, tok_per_step):
    ntok, d = x.shape
    spb = tok_per_step // t_new
    cbk, cbv, cck, ccv = caches

    def rows(width):
        return pl.BlockSpec((tok_per_step, width), lambda i: (i, 0))

    def cache(c):
        return pl.BlockSpec((None, spb) + c.shape[2:], lambda i: (layer, i, 0, 0))

    return pl.pallas_call(
        _sample_kernel,
        grid=(ntok // tok_per_step,),
        in_specs=[
            rows(d), cache(cbk), cache(cbv), cache(cck), cache(ccv),
            _layer_spec(p["gpre"], layer), _layer_spec(p["gpost"], layer),
            _layer_spec(p["wT"], layer),
            _layer_spec(p["wout"], layer),
            _layer_spec(p["ang"], layer), _layer_spec(p["aws_s"], layer), _const_spec(p["arep"]),
            _const_spec(p["akeep"]), _layer_spec(p["abias_s"], layer),
            _layer_spec(p["sink"], layer), _const_spec(p["bias_sb"]), _layer_spec(p["bias_sc"], layer),
        ],
        out_specs=[rows(d), rows(B_KVW), rows(B_KVW), rows(C_WIDTH), rows(C_WIDTH), rows(A_WIDTH)],
        out_shape=[
            jax.ShapeDtypeStruct((ntok, d), f32),
            jax.ShapeDtypeStruct((ntok, B_KVW), f32), jax.ShapeDtypeStruct((ntok, B_KVW), f32),
            jax.ShapeDtypeStruct((ntok, C_WIDTH), f32), jax.ShapeDtypeStruct((ntok, C_WIDTH), f32),
            jax.ShapeDtypeStruct((ntok, A_WIDTH), f32),
        ],
        scratch_shapes=[
            pltpu.VMEM((T_END, tok_per_step), f32),
            pltpu.VMEM((A_WIDTH + B_WIDTH + C_WIDTH, tok_per_step), bf16),
        ],
        compiler_params=pltpu.CompilerParams(dimension_semantics=("arbitrary",),
                                             vmem_limit_bytes=VMEM_LIMIT),
        name="sample_layer",
    )(x, cbk, cbv, cck, ccv, p["gpre"], p["gpost"], p["wT"], p["wout"],
      p["ang"], p["aws_s"], p["arep"], p["akeep"], p["abias_s"], p["sink"], p["bias_sb"], p["bias_sc"])


def kernel(x_prompt, x_sample, cache_b_k, cache_b_v, cache_c_k, cache_c_v, g_pre, g_post, w_in, w_out,
           a_norm_g, a_ws, a_bs, b_sinks, c_rel_bias, t5_bias):
    depth = w_in.shape[0]
    nb, seq, d = x_prompt.shape
    ns_all, t_new, _ = x_sample.shape
    lc_b, lc_c = cache_b_k.shape[2], cache_c_k.shape[2]
    assert lc_b == B_KEEP and lc_c == C_KEEP and LANE % t_new == 0
    assert SAMPLE_STREAMS_PER_BLOCK * t_new == LANE
    tok_per_step = 2 * LANE
    reps = tok_per_step // t_new

    rel_pb, ok_pb, idx_pc = _prompt_maps()
    rel_sb, ok_sb, idx_sc = _sample_maps(lc_b, lc_c, t_new)
    rel_lo = int(min(rel_pb.min(), rel_sb.min()))
    rel_hi = int(max(rel_pb.max(), rel_sb.max()))
    t5_rel = t5_bias[_t5_bucket(jnp.arange(rel_lo, rel_hi + 1))][None]
    idx_pb = np.where(ok_pb, rel_pb - rel_lo, -1).astype(np.int32)
    idx_sb = np.where(ok_sb, rel_sb - rel_lo, -1).astype(np.int32)
    r_ = np.arange(tok_per_step)
    awsT = a_ws.transpose(0, 1, 3, 2)

    p = dict(
        bias_pb=_build_bias(t5_rel, jnp.asarray(idx_pb), ranges=_static_ranges(idx_pb), kmins=(B_PREV, 0))[0],
        bias_sb=_build_bias(t5_rel, jnp.asarray(idx_sb), ranges=_static_ranges(idx_sb), kmins=(0,))[0, 0],
        bias_pc=_build_bias(c_rel_bias, jnp.asarray(idx_pc), ranges=_static_ranges(idx_pc),
                            kmins=tuple(max(C_PREV - 2 * v, 0) for v in range(C_PREV // 2 + 1))),
        bias_sc=_build_bias(c_rel_bias, jnp.asarray(idx_sc), ranges=_static_ranges(idx_sc),
                            kmins=(0,))[:, 0],
        wT=_transpose_weights(w_in),
        wout=w_out.astype(bf16),
        gpre=g_pre[:, None, :],
        gpost=g_post[:, None, :],
        ang=a_norm_g[:, :, None],
        awsT=awsT,
        abs=a_bs[:, :, None, :],
        sink=jnp.repeat(b_sinks, LANE, axis=1)[:, None, :],
        aws_s=awsT[:, :, :t_new, :t_new],
        arep=jnp.asarray((r_[:, None] % t_new == np.arange(t_new)[None, :]).astype(np.float32), bf16),
        abias_s=jnp.tile(a_bs[:, :, None, :t_new], (1, 1, 1, reps)),
        akeep=jnp.asarray(((r_[:, None] // t_new == r_[None, :] // t_new)
                           & (r_[:, None] % t_new <= r_[None, :] % t_new)).astype(np.float32)),
    )

    xp = x_prompt
    xs = x_sample.reshape(ns_all * t_new, d)
    caches = tuple(c.transpose(0, 1, 3, 4, 2).reshape(depth, ns_all, -1, c.shape[2])
                   for c in (cache_b_k, cache_b_v, cache_c_k, cache_c_v))

    pk, pv, pck, pcv = [], [], [], []
    sk, sv, sck, scv, sav = [], [], [], [], []
    for l in range(depth):
        xp, bk, bv, ck, cv = _prompt_layer(l, xp, p)
        pk.append(bk); pv.append(bv); pck.append(ck); pcv.append(cv)
        xs, bk, bv, ck, cv, av = _sample_layer(l, xs, caches, p, t_new=t_new, tok_per_step=tok_per_step)
        sk.append(bk); sv.append(bv); sck.append(ck); scv.append(cv); sav.append(av)

    def stk(xs_, lead, heads):
        return jnp.stack(xs_).reshape(depth, lead, -1, heads, HEAD_DIM)

    def stk_fm(xs_, heads):
        a = jnp.stack(xs_)
        return a.reshape(depth, nb, heads, HEAD_DIM, a.shape[-1]).transpose(0, 1, 4, 2, 3)

    return (xp, xs.reshape(ns_all, t_new, d),
            stk_fm(pk, B_KV), stk_fm(pv, B_KV), stk_fm(pck, C_HEADS), stk_fm(pcv, C_HEADS),
            stk(sk, ns_all, B_KV), stk(sv, ns_all, B_KV), stk(sck, ns_all, C_HEADS), stk(scv, ns_all, C_HEADS),
            jnp.stack(sav).reshape(depth, ns_all, t_new, A_WIDTH))
```

```python
import functools
import math

import numpy as np
import jax
import jax.numpy as jnp
from jax import lax
from jax.experimental import pallas as pl
from jax.experimental.pallas import tpu as pltpu

bf16 = jnp.bfloat16
f32 = jnp.float32

HEAD_DIM = 64
CHUNK = 64
A_WIDTH, A_GROUPS, A_CHUNK = 256, 4, 128
A_GD = A_WIDTH // A_GROUPS
B_HEADS, B_KV = 8, 2
B_GROUP = B_HEADS // B_KV
B_WIDTH, B_KVW = B_HEADS * HEAD_DIM, B_KV * HEAD_DIM
B_PREV = 2
C_HEADS = 4
C_WIDTH = C_HEADS * HEAD_DIM
C_PREV = 8
C_CLIP = 128
T5_BUCKETS, T5_MAX_DIST = 32, 128
RMS_EPS = 1e-6
NEG_INF = -1e30
QK_SCALE = HEAD_DIM ** -0.5
LOG2E = math.log2(math.e)

LANE = 128
PAIR = 2 * CHUNK
B_WIN = (B_PREV + 2) * CHUNK
C_WIN = (C_PREV + 2) * CHUNK
B_KEEP = B_PREV * CHUNK
C_KEEP = C_PREV * CHUNK
ONES_ROWS = 16
VB_STRIDE = HEAD_DIM + ONES_ROWS
VC_STRIDE = 2 * HEAD_DIM + ONES_ROWS
SEQ_TILE = 512
SAMPLE_STREAMS_PER_BLOCK = 4
VMEM_LIMIT = 56 * 1024 * 1024
TOEPLITZ_UNROLL = 17

_IN_SIZES = [A_WIDTH, A_WIDTH, A_WIDTH, B_WIDTH, B_KVW, B_KVW, B_WIDTH, C_WIDTH, C_WIDTH, C_WIDTH, C_WIDTH]
_IN_OFF = [int(v) for v in np.cumsum([0] + _IN_SIZES)]
(T_AU, T_AV, T_AZ, T_BQ, T_BK, T_BV, T_BZ, T_CQ, T_CK, T_CV, T_CZ, T_END) = _IN_OFF
_T_SECTIONS = ((T_AU, T_BK), (T_BV, T_CK), (T_CV, T_END))
Y_A, Y_B, Y_C = 0, A_WIDTH, A_WIDTH + B_WIDTH

_NT = (((1,), (1,)), ((), ()))
_TN = (((0,), (0,)), ((), ()))


def _t5_bucket(rel):
    half = T5_BUCKETS // 2
    max_exact = half // 2
    ret = jnp.where(rel > 0, half, 0)
    n = jnp.abs(rel)
    nf = jnp.maximum(n, 1).astype(jnp.float32)
    large = max_exact + (jnp.log(nf / max_exact) / math.log(T5_MAX_DIST / max_exact)
                         * (half - max_exact)).astype(jnp.int32)
    large = jnp.minimum(large, half - 1)
    return ret + jnp.where(n < max_exact, n, large)


def _toeplitz_kernel(shift_ref, frev_ref, ok_ref, out_ref, *, n_heads, kmins):
    width = frev_ref.shape[-1]
    neg = jnp.full((CHUNK, LANE), NEG_INF, f32)

    def block(rb, carry):
        r0 = pl.multiple_of(rb * CHUNK, CHUNK)
        ok = ok_ref[pl.ds(r0, CHUNK), :] > 0.0
        rolled = pltpu.roll(frev_ref[...], shift_ref[rb], 1)
        ring = jnp.concatenate([rolled[:, :LANE], rolled[:, width - LANE:]], axis=1)
        for h in range(n_heads):
            x = jnp.broadcast_to(ring[h:h + 1, :], (CHUNK, 2 * LANE))
            y = pltpu.roll(x, 0, 1, stride=1, stride_axis=0)[:, :LANE]
            val = jnp.where(ok, y * LOG2E, NEG_INF)
            for v, kmin in enumerate(kmins):
                out_ref[0, v, pl.ds(r0, CHUNK), h * LANE:(h + 1) * LANE] = (
                    val if kmin == 0 else jnp.where(rb >= kmin, val, neg))
        return carry

    n_rb = ok_ref.shape[0] // CHUNK
    unroll = max(u for u in range(1, TOEPLITZ_UNROLL + 1) if n_rb % u == 0)
    lax.fori_loop(0, n_rb, block, 0, unroll=unroll)


def _build_bias(f_ext, rel_lo, rel, ok, *, kmins, const_below=None):
    n_layers, n_rel, n_heads = f_ext.shape
    n_rows = rel.shape[0]
    width = -(-n_rel // LANE) * LANE
    jj, ll = np.arange(CHUNK)[:, None], np.arange(LANE)[None, :]
    shifts = []
    for rb in range(n_rows // CHUNK):
        r, o = rel[rb * CHUNK:(rb + 1) * CHUNK], ok[rb * CHUNK:(rb + 1) * CHUNK]
        base = rel_lo + LANE - 1
        if o.any():
            bases = (r - jj + ll)[o]
            assert (bases == bases[0]).all(), "block is not Toeplitz on its valid entries"
            base = int(bases[0])
            if base - (LANE - 1) < rel_lo:
                assert const_below is not None and r[o].max() <= const_below
                base = rel_lo + LANE - 1
                assert (base + jj - ll)[o].max() <= const_below
            k = (base + jj - ll - rel_lo)[o]
            assert k.min() >= 0 and k.max() < n_rel
        shifts.append((width - ((n_rel - 1) - (base - rel_lo))) % width)
    frev = jnp.pad(jnp.flip(f_ext, axis=1).transpose(0, 2, 1), ((0, 0), (0, 0), (0, width - n_rel)))
    kern = functools.partial(_toeplitz_kernel, n_heads=n_heads, kmins=tuple(kmins))
    return pl.pallas_call(
        kern,
        grid=(n_layers,),
        in_specs=[pl.BlockSpec(memory_space=pltpu.SMEM),
                  pl.BlockSpec((None, n_heads, width), lambda l: (l, 0, 0)),
                  pl.BlockSpec((n_rows, LANE), lambda l: (0, 0))],
        out_specs=pl.BlockSpec((1, len(kmins), n_rows, n_heads * LANE), lambda l: (l, 0, 0, 0)),
        out_shape=jax.ShapeDtypeStruct((n_layers, len(kmins), n_rows, n_heads * LANE), f32),
        compiler_params=pltpu.CompilerParams(dimension_semantics=("arbitrary",),
                                             vmem_limit_bytes=VMEM_LIMIT),
        name="bias_toeplitz",
    )(jnp.asarray(np.asarray(shifts, np.int32)), frev, jnp.asarray(ok.astype(np.float32)))


def _wT_kernel(w_ref, o_ref):
    o_ref[...] = w_ref[...].T.astype(bf16)


def _transpose_weights(w, cols=4 * LANE):
    n_layers, k, n = w.shape
    return pl.pallas_call(
        _wT_kernel,
        grid=(n_layers, n // cols),
        in_specs=[pl.BlockSpec((None, k, cols), lambda l, c: (l, 0, c))],
        out_specs=pl.BlockSpec((None, cols, k), lambda l, c: (l, c, 0)),
        out_shape=jax.ShapeDtypeStruct((n_layers, n, k), bf16),
        compiler_params=pltpu.CompilerParams(dimension_semantics=("arbitrary", "arbitrary"),
                                             vmem_limit_bytes=VMEM_LIMIT),
        name="weights_T",
    )(w)


def _prompt_maps():
    i = np.arange(PAIR)[None, :]
    qi = i // CHUNK
    jb = np.arange(B_WIN)[:, None]
    rel_b = jb - B_KEEP - i
    ok_b = (jb // CHUNK >= qi) & (jb // CHUNK <= qi + B_PREV)
    jc = np.arange(C_WIN)[:, None]
    rel_c = jc - C_KEEP - i
    ok_c = (jc // CHUNK >= qi) & (jc // CHUNK <= qi + C_PREV)
    return rel_b, ok_b, rel_c, ok_c


def _sample_maps(lc_b, lc_c, t_new):
    ns = SAMPLE_STREAMS_PER_BLOCK
    c = np.arange(LANE)[None, :]
    sq, i = c // t_new, c % t_new

    def one(lc):
        r = np.arange(ns * lc + LANE)[:, None]
        is_new = r >= ns * lc
        sk = np.where(is_new, (r - ns * lc) // t_new, r // lc)
        j = np.where(is_new, lc + (r - ns * lc) % t_new, r % lc)
        return (j - lc - i).astype(np.int32), (sk == sq)

    return one(lc_b) + one(lc_c)


def _rms_rows(x, g):
    ms = jnp.mean(x * x, axis=-1, keepdims=True)
    return x * lax.rsqrt(ms + RMS_EPS) * g


def _rms_cols(xT, g_col):
    ms = jnp.mean(xT * xT, axis=0, keepdims=True)
    return xT * lax.rsqrt(ms + RMS_EPS) * g_col


def _project_T(pt_sc, wT_ref, h):
    for r0, r1 in _T_SECTIONS:
        pt_sc[r0:r1, :] = lax.dot_general(wT_ref[r0:r1, :], h, _NT, preferred_element_type=f32)


def _qblockdiag(qa, qb, lower):
    z = jnp.zeros((HEAD_DIM, LANE), bf16)
    if lower is None:
        return jnp.concatenate([jnp.concatenate([qa, z], axis=1),
                                jnp.concatenate([z, qb], axis=1)], axis=0)
    top = jnp.concatenate([qa, qb], axis=1)
    zz = jnp.zeros((HEAD_DIM, 2 * LANE), bf16)
    return jnp.concatenate([top, zz] if lower == 0 else [zz, top], axis=0)


def _with_ones(vT, group):
    ones = jnp.ones((ONES_ROWS, vT.shape[1]), vT.dtype)
    parts = []
    for r in range(0, vT.shape[0], group):
        parts += [vT[r:r + group, :], ones]
    return jnp.concatenate(parts, axis=0)


def _softmax_pv(s_m, sink, vwin):
    s, m = s_m
    if sink is not None:
        m = jnp.maximum(m, sink)
    e = jnp.exp2(s - m)
    o = jnp.dot(vwin, e.astype(bf16), preferred_element_type=f32)
    nv = vwin.shape[0] - ONES_ROWS
    den = o[nv:nv + 1, :]
    if sink is not None:
        den = den + jnp.exp2(sink - m)
    return o[:nv, :] * (1.0 / den)


def _scores(pt_sc, cols, kwin_b, bias_b, kwin_c, bias_c):
    def head_q(t0, hh):
        return (pt_sc[t0 + hh * HEAD_DIM:t0 + (hh + 1) * HEAD_DIM, cols] * (QK_SCALE * LOG2E)).astype(bf16)

    def with_max(s):
        return s, jnp.max(s, axis=0, keepdims=True)

    s_b, s_c = [], []
    for cb in range(B_HEADS // 2):
        ha, hb = 2 * cb, 2 * cb + 1
        qbd = _qblockdiag(head_q(T_BQ, ha), head_q(T_BQ, hb), ha // B_GROUP)
        s = jnp.dot(kwin_b, qbd, preferred_element_type=f32)
        s_b.append(with_max(s + bias_b[:, cb * 2 * LANE:(cb + 1) * 2 * LANE]))
    for cb in range(C_HEADS // 2):
        ha, hb = 2 * cb, 2 * cb + 1
        qbd = _qblockdiag(head_q(T_CQ, ha), head_q(T_CQ, hb), None)
        s = jnp.dot(kwin_c[:, cb * LANE:(cb + 1) * LANE], qbd, preferred_element_type=f32)
        s_c.append(with_max(s + bias_c[:, cb * 2 * LANE:(cb + 1) * 2 * LANE]))
    return s_b, s_c


def _attend(pt_sc, yT_sc, cols, s_b, s_c, vwinT_b, vwinT_c, sink_ref):
    for cb in range(B_HEADS // 2):
        ha, hb = 2 * cb, 2 * cb + 1
        kv = ha // B_GROUP
        sink = sink_ref[:, cb * 2 * LANE:(cb + 1) * 2 * LANE] * LOG2E
        o = _softmax_pv(s_b[cb], sink, vwinT_b[kv * VB_STRIDE:(kv + 1) * VB_STRIDE, :])
        for j, hh in enumerate((ha, hb)):
            gate = pt_sc[T_BZ + hh * HEAD_DIM:T_BZ + (hh + 1) * HEAD_DIM, cols]
            yT_sc[Y_B + hh * HEAD_DIM:Y_B + (hh + 1) * HEAD_DIM, cols] = (
                o[:, j * LANE:(j + 1) * LANE] * gate).astype(bf16)
    for cb in range(C_HEADS // 2):
        ha, hb = 2 * cb, 2 * cb + 1
        o = _softmax_pv(s_c[cb], None, vwinT_c[cb * VC_STRIDE:(cb + 1) * VC_STRIDE, :])
        for j, hh in enumerate((ha, hb)):
            gate = pt_sc[T_CZ + hh * HEAD_DIM:T_CZ + (hh + 1) * HEAD_DIM, cols]
            yT_sc[Y_C + hh * HEAD_DIM:Y_C + (hh + 1) * HEAD_DIM, cols] = (
                o[j * HEAD_DIM:(j + 1) * HEAD_DIM, j * LANE:(j + 1) * LANE] * gate).astype(bf16)


def _gate_rows(pt_sc):
    for r0, r1 in ((T_BZ, T_CQ), (T_CZ, T_END)):
        pt_sc[r0:r1, :] = jax.nn.silu(pt_sc[r0:r1, :])


def _finish(x, yT_sc, wout_ref, gpost_ref):
    out = lax.dot_general(yT_sc[...], wout_ref[...], _TN, preferred_element_type=f32)
    return x + _rms_rows(out, gpost_ref[...])


def _layer_spec(arr, layer):
    shape = arr.shape[1:]
    return pl.BlockSpec((None,) + shape, lambda *_: (layer,) + (0,) * len(shape),
                        pipeline_mode=pl.Buffered(1))


def _const_spec(arr):
    return pl.BlockSpec(arr.shape, lambda *_: (0,) * arr.ndim, pipeline_mode=pl.Buffered(1))


def _prompt_kernel(x_ref, gpre_ref, gpost_ref, wT_ref, wout_ref,
                   ang_ref, awsT_ref, abs_ref, sink_ref, biasb_ref, biasc_ref,
                   xo_ref, bk_ref, bv_ref, ck_ref, cv_ref,
                   pt_sc, kb_ring, kc_ring, vbT_ring, vcT_ring, yT_sc,
                   sb_even, mb_even, sc_even, mc_even, sb_odd, mb_odd, sc_odd, mc_odd):
    t = pl.program_id(1)
    ts = SEQ_TILE

    @pl.when(t == 0)
    def _():
        kb_ring[0:B_KEEP, :] = jnp.zeros((B_KEEP, B_KVW), bf16)
        kc_ring[0:C_KEEP, :] = jnp.zeros((C_KEEP, C_WIDTH), bf16)
        vbT_ring[:, 0:B_KEEP] = jnp.zeros((B_KV * VB_STRIDE, B_KEEP), bf16)
        vcT_ring[:, 0:C_KEEP] = jnp.zeros((C_HEADS // 2 * VC_STRIDE, C_KEEP), bf16)
        for g in range(B_KV):
            vbT_ring[g * VB_STRIDE + HEAD_DIM:(g + 1) * VB_STRIDE, :] = jnp.ones((ONES_ROWS, B_KEEP + ts), bf16)
        for g in range(C_HEADS // 2):
            vcT_ring[g * VC_STRIDE + 2 * HEAD_DIM:(g + 1) * VC_STRIDE, :] = jnp.ones((ONES_ROWS, C_KEEP + ts), bf16)

    x = x_ref[0]
    h = _rms_rows(x, gpre_ref[...]).astype(bf16)
    _project_T(pt_sc, wT_ref, h)
    kb_ring[B_KEEP:, :] = lax.dot_general(h, wT_ref[T_BK:T_BV, :], _NT,
                                          preferred_element_type=f32).astype(bf16)
    kc_ring[C_KEEP:, :] = lax.dot_general(h, wT_ref[T_CK:T_CV, :], _NT,
                                          preferred_element_type=f32).astype(bf16)
    for g in range(B_KV):
        vbT_ring[g * VB_STRIDE:g * VB_STRIDE + HEAD_DIM, B_KEEP:] = (
            pt_sc[T_BV + g * HEAD_DIM:T_BV + (g + 1) * HEAD_DIM, :].astype(bf16))
    for g in range(C_HEADS // 2):
        vcT_ring[g * VC_STRIDE:g * VC_STRIDE + 2 * HEAD_DIM, C_KEEP:] = (
            pt_sc[T_CV + g * 2 * HEAD_DIM:T_CV + (g + 1) * 2 * HEAD_DIM, :].astype(bf16))

    _gate_rows(pt_sc)

    @pl.when(t == pl.num_programs(1) - 1)
    def _():
        bk_ref[0] = lax.dot_general(wT_ref[T_BK:T_BV, :], h[ts - B_KEEP:, :], _NT, preferred_element_type=f32)
        ck_ref[0] = lax.dot_general(wT_ref[T_CK:T_CV, :], h[ts - C_KEEP:, :], _NT, preferred_element_type=f32)
        bv_ref[0] = pt_sc[T_BV:T_BZ, ts - B_KEEP:]
        cv_ref[0] = pt_sc[T_CV:T_CZ, ts - C_KEEP:]

    vn = _rms_cols(pt_sc[T_AV:T_AV + A_WIDTH, :], ang_ref[...]).astype(bf16)
    n_ac = ts // A_CHUNK
    row = lax.broadcasted_iota(jnp.int32, (A_CHUNK, A_CHUNK), 0)
    col = lax.broadcasted_iota(jnp.int32, (A_CHUNK, A_CHUNK), 1)
    for g in range(A_GROUPS):
        rows = slice(g * A_GD, (g + 1) * A_GD)
        wgt = jnp.where(row <= col, awsT_ref[g], 0.0).astype(bf16)
        lhs = jnp.concatenate([vn[rows, c * A_CHUNK:(c + 1) * A_CHUNK] for c in range(n_ac)], axis=0)
        mix = jnp.dot(lhs, wgt, preferred_element_type=f32) + abs_ref[g]
        for c in range(n_ac):
            cols = slice(c * A_CHUNK, (c + 1) * A_CHUNK)
            u = pt_sc[T_AU + g * A_GD:T_AU + (g + 1) * A_GD, cols]
            z = pt_sc[T_AZ + g * A_GD:T_AZ + (g + 1) * A_GD, cols]
            yT_sc[Y_A + g * A_GD:Y_A + (g + 1) * A_GD, cols] = (
                u * mix[c * A_GD:(c + 1) * A_GD, :] * jax.nn.silu(z)).astype(bf16)

    n_pairs = ts // PAIR

    even, odd = (sb_even, mb_even, sc_even, mc_even), (sb_odd, mb_odd, sc_odd, mc_odd)

    def scores_into(p, bufs):
        sb_ref, mb_ref, sc_ref, mc_ref = bufs
        pg = t * n_pairs + p
        start = p * PAIR if isinstance(p, int) else pl.multiple_of(p * PAIR, PAIR)
        s_b, s_c = _scores(pt_sc, pl.ds(start, PAIR),
                           kb_ring[pl.ds(start, B_WIN), :], biasb_ref.at[jnp.minimum(pg, 1)],
                           kc_ring[pl.ds(start, C_WIN), :], biasc_ref.at[jnp.minimum(pg, C_PREV // 2)])
        for cb, (v, m) in enumerate(s_b):
            sb_ref[cb] = v
            mb_ref[cb] = m
        for cb, (v, m) in enumerate(s_c):
            sc_ref[cb] = v
            mc_ref[cb] = m

    def attend_from(p, bufs):
        sb_ref, mb_ref, sc_ref, mc_ref = bufs
        start = p * PAIR if isinstance(p, int) else pl.multiple_of(p * PAIR, PAIR)
        _attend(pt_sc, yT_sc, pl.ds(start, PAIR),
                [(sb_ref[cb], mb_ref[cb]) for cb in range(B_HEADS // 2)],
                [(sc_ref[cb], mc_ref[cb]) for cb in range(C_HEADS // 2)],
                vbT_ring[:, pl.ds(start, B_WIN)], vcT_ring[:, pl.ds(start, C_WIN)], sink_ref)

    scores_into(0, even)

    def two_pairs(q, carry):
        p = 2 * q
        scores_into(p + 1, odd)
        attend_from(p, even)
        scores_into(jnp.minimum(p + 2, n_pairs - 1), even)
        attend_from(p + 1, odd)
        return carry

    lax.fori_loop(0, n_pairs // 2, two_pairs, 0)

    xo_ref[0] = _finish(x, yT_sc, wout_ref, gpost_ref)

    kb_ring[0:B_KEEP, :] = kb_ring[ts:ts + B_KEEP, :]
    kc_ring[0:C_KEEP, :] = kc_ring[ts:ts + C_KEEP, :]
    vbT_ring[:, 0:B_KEEP] = vbT_ring[:, ts:ts + B_KEEP]
    vcT_ring[:, 0:C_KEEP] = vcT_ring[:, ts:ts + C_KEEP]


def _prompt_layer(layer, x, p):
    nb, seq, d = x.shape
    ts = SEQ_TILE
    assert seq % ts == 0 and ts == C_KEEP
    kv_spec = lambda keep, width: pl.BlockSpec((1, width, keep), lambda b, t: (b, 0, 0))
    return pl.pallas_call(
        _prompt_kernel,
        grid=(nb, seq // ts),
        in_specs=[
            pl.BlockSpec((1, ts, d), lambda b, t: (b, t, 0)),
            _layer_spec(p["gpre"], layer), _layer_spec(p["gpost"], layer),
            _layer_spec(p["wT"], layer),
            _layer_spec(p["wout"], layer),
            _layer_spec(p["ang"], layer), _layer_spec(p["awsT"], layer), _layer_spec(p["abs"], layer),
            _layer_spec(p["sink"], layer), _const_spec(p["bias_pb"]), _layer_spec(p["bias_pc"], layer),
        ],
        out_specs=[
            pl.BlockSpec((1, ts, d), lambda b, t: (b, t, 0)),
            kv_spec(B_KEEP, B_KVW), kv_spec(B_KEEP, B_KVW), kv_spec(C_KEEP, C_WIDTH), kv_spec(C_KEEP, C_WIDTH),
        ],
        out_shape=[
            jax.ShapeDtypeStruct((nb, seq, d), f32),
            jax.ShapeDtypeStruct((nb, B_KVW, B_KEEP), f32),
            jax.ShapeDtypeStruct((nb, B_KVW, B_KEEP), f32),
            jax.ShapeDtypeStruct((nb, C_WIDTH, C_KEEP), f32),
            jax.ShapeDtypeStruct((nb, C_WIDTH, C_KEEP), f32),
        ],
        scratch_shapes=[
            pltpu.VMEM((T_END, ts), f32),
            pltpu.VMEM((B_KEEP + ts, B_KVW), bf16),
            pltpu.VMEM((C_KEEP + ts, C_WIDTH), bf16),
            pltpu.VMEM((B_KV * VB_STRIDE, B_KEEP + ts), bf16),
            pltpu.VMEM((C_HEADS // 2 * VC_STRIDE, C_KEEP + ts), bf16),
            pltpu.VMEM((A_WIDTH + B_WIDTH + C_WIDTH, ts), bf16),
        ] + 2 * [
            pltpu.VMEM((B_HEADS // 2, B_WIN, 2 * LANE), f32), pltpu.VMEM((B_HEADS // 2, 1, 2 * LANE), f32),
            pltpu.VMEM((C_HEADS // 2, C_WIN, 2 * LANE), f32), pltpu.VMEM((C_HEADS // 2, 1, 2 * LANE), f32),
        ],
        compiler_params=pltpu.CompilerParams(dimension_semantics=("arbitrary", "arbitrary"),
                                             vmem_limit_bytes=VMEM_LIMIT),
        name="prompt_layer",
    )(x, p["gpre"], p["gpost"], p["wT"], p["wout"], p["ang"], p["awsT"],
      p["abs"], p["sink"], p["bias_pb"], p["bias_pc"])


def _sample_kernel(x_ref, cbk_ref, cbv_ref, cck_ref, ccv_ref, gpre_ref, gpost_ref, wT_ref,
                   wout_ref, ang_ref, aws_ref, arep_ref, akeep_ref, abias_ref,
                   sink_ref, biasb_ref, biasc_ref,
                   xo_ref, bk_ref, bv_ref, ck_ref, cv_ref, av_ref,
                   pt_sc, yT_sc):
    ns = SAMPLE_STREAMS_PER_BLOCK
    x = x_ref[...]
    ntok = x.shape[0]
    h = _rms_rows(x, gpre_ref[...]).astype(bf16)
    _project_T(pt_sc, wT_ref, h)
    kvn_b = lax.dot_general(h, wT_ref[T_BK:T_BZ, :], _NT, preferred_element_type=f32)
    kvn_c = lax.dot_general(h, wT_ref[T_CK:T_CZ, :], _NT, preferred_element_type=f32)
    kn_c = kvn_c[:, 0:C_WIDTH]
    bk_ref[...] = kvn_b[:, 0:B_KVW]
    bv_ref[...] = kvn_b[:, B_KVW:]
    ck_ref[...] = kn_c
    cv_ref[...] = kvn_c[:, C_WIDTH:]

    vn = _rms_cols(pt_sc[T_AV:T_AV + A_WIDTH, :], ang_ref[...])
    av_ref[...] = vn.T
    vnb = vn.astype(bf16)
    keep = akeep_ref[...] > 0.0
    rep = arep_ref[...]
    for g in range(A_GROUPS):
        rows = slice(g * A_GD, (g + 1) * A_GD)
        tiled = lax.dot_general(jnp.dot(rep, aws_ref[g].astype(bf16), preferred_element_type=f32).astype(bf16),
                                rep, _NT, preferred_element_type=f32)
        wgt = jnp.where(keep, tiled, 0.0).astype(bf16)
        mix = jnp.dot(vnb[rows, :], wgt, preferred_element_type=f32) + abias_ref[g]
        u = pt_sc[T_AU + g * A_GD:T_AU + (g + 1) * A_GD, :]
        z = pt_sc[T_AZ + g * A_GD:T_AZ + (g + 1) * A_GD, :]
        yT_sc[Y_A + g * A_GD:Y_A + (g + 1) * A_GD, :] = (u * mix * jax.nn.silu(z)).astype(bf16)

    _gate_rows(pt_sc)
    for blk in range(ntok // LANE):
        cols = slice(blk * LANE, (blk + 1) * LANE)
        streams = range(blk * ns, (blk + 1) * ns)
        kwin_b = jnp.concatenate([cbk_ref[s].T.astype(bf16) for s in streams]
                                 + [kvn_b[cols, 0:B_KVW].astype(bf16)], axis=0)
        vwinT_b = jnp.concatenate([cbv_ref[s].astype(bf16) for s in streams]
                                  + [pt_sc[T_BV:T_BV + B_KVW, cols].astype(bf16)], axis=1)
        kwin_c = jnp.concatenate([cck_ref[s].T.astype(bf16) for s in streams]
                                 + [kn_c[cols, :].astype(bf16)], axis=0)
        vwinT_c = jnp.concatenate([ccv_ref[s].astype(bf16) for s in streams]
                                  + [pt_sc[T_CV:T_CV + C_WIDTH, cols].astype(bf16)], axis=1)
        s_b, s_c = _scores(pt_sc, cols, kwin_b, biasb_ref, kwin_c, biasc_ref)
        _attend(pt_sc, yT_sc, cols, s_b, s_c, _with_ones(vwinT_b, HEAD_DIM), _with_ones(vwinT_c, 2 * HEAD_DIM),
                sink_ref)

    xo_ref[...] = _finish(x, yT_sc, wout_ref, gpost_ref)


def _sample_layer(layer, x, caches, p, *, t_new, tok_per_step):
    ntok, d = x.shape
    spb = tok_per_step // t_new
    cbk, cbv, cck, ccv = caches

    def rows(width):
        return pl.BlockSpec((tok_per_step, width), lambda i: (i, 0))

    def cache(c):
        return pl.BlockSpec((None, spb) + c.shape[2:], lambda i: (layer, i, 0, 0))

    return pl.pallas_call(
        _sample_kernel,
        grid=(ntok // tok_per_step,),
        in_specs=[
            rows(d), cache(cbk), cache(cbv), cache(cck), cache(ccv),
            _layer_spec(p["gpre"], layer), _layer_spec(p["gpost"], layer),
            _layer_spec(p["wT"], layer),
            _layer_spec(p["wout"], layer),
            _layer_spec(p["ang"], layer), _layer_spec(p["aws_s"], layer), _const_spec(p["arep"]),
            _const_spec(p["akeep"]), _layer_spec(p["abias_s"], layer),
            _layer_spec(p["sink"], layer), _const_spec(p["bias_sb"]), _layer_spec(p["bias_sc"], layer),
        ],
        out_specs=[rows(d), rows(B_KVW), rows(B_KVW), rows(C_WIDTH), rows(C_WIDTH), rows(A_WIDTH)],
        out_shape=[
            jax.ShapeDtypeStruct((ntok, d), f32),
            jax.ShapeDtypeStruct((ntok, B_KVW), f32), jax.ShapeDtypeStruct((ntok, B_KVW), f32),
            jax.ShapeDtypeStruct((ntok, C_WIDTH), f32), jax.ShapeDtypeStruct((ntok, C_WIDTH), f32),
            jax.ShapeDtypeStruct((ntok, A_WIDTH), f32),
        ],
        scratch_shapes=[
            pltpu.VMEM((T_END, tok_per_step), f32),
            pltpu.VMEM((A_WIDTH + B_WIDTH + C_WIDTH, tok_per_step), bf16),
        ],
        compiler_params=pltpu.CompilerParams(dimension_semantics=("arbitrary",),
                                             vmem_limit_bytes=VMEM_LIMIT),
        name="sample_layer",
    )(x, cbk, cbv, cck, ccv, p["gpre"], p["gpost"], p["wT"], p["wout"],
      p["ang"], p["aws_s"], p["arep"], p["akeep"], p["abias_s"], p["sink"], p["bias_sb"], p["bias_sc"])


def kernel(x_prompt, x_sample, cache_b_k, cache_b_v, cache_c_k, cache_c_v, g_pre, g_post, w_in, w_out,
           a_norm_g, a_ws, a_bs, b_sinks, c_rel_bias, t5_bias):
    depth = w_in.shape[0]
    nb, seq, d = x_prompt.shape
    ns_all, t_new, _ = x_sample.shape
    lc_b, lc_c = cache_b_k.shape[2], cache_c_k.shape[2]
    assert lc_b == B_KEEP and lc_c == C_KEEP and LANE % t_new == 0
    assert SAMPLE_STREAMS_PER_BLOCK * t_new == LANE
    tok_per_step = 2 * LANE
    reps = tok_per_step // t_new

    rel_pb, ok_pb, rel_pc, ok_pc = _prompt_maps()
    rel_sb, ok_sb, rel_sc, ok_sc = _sample_maps(lc_b, lc_c, t_new)
    b_lo = int(min(rel_pb.min(), rel_sb.min()))
    b_hi = int(max(rel_pb.max(), rel_sb.max()))
    t5_rel = t5_bias[_t5_bucket(jnp.arange(b_lo, b_hi + 1))][None]
    c_lo = -C_CLIP - 2 * LANE + 1
    c_hi = int(max(rel_pc.max(), rel_sc.max()))
    assert c_hi < C_CLIP
    c_rel = jnp.concatenate([jnp.broadcast_to(c_rel_bias[:, :1], (depth, -C_CLIP - c_lo, C_HEADS)),
                             c_rel_bias[:, :c_hi + C_CLIP + 1]], axis=1)
    r_ = np.arange(tok_per_step)
    awsT = a_ws.transpose(0, 1, 3, 2)

    p = dict(
        bias_pb=_build_bias(t5_rel, b_lo, rel_pb, ok_pb, kmins=(B_PREV, 0))[0],
        bias_sb=_build_bias(t5_rel, b_lo, rel_sb, ok_sb, kmins=(0,))[0, 0],
        bias_pc=_build_bias(c_rel, c_lo, rel_pc, ok_pc, const_below=-C_CLIP,
                            kmins=tuple(max(C_PREV - 2 * v, 0) for v in range(C_PREV // 2 + 1))),
        bias_sc=_build_bias(c_rel, c_lo, rel_sc, ok_sc, const_below=-C_CLIP, kmins=(0,))[:, 0],
        wT=_transpose_weights(w_in),
        wout=w_out.astype(bf16),
        gpre=g_pre[:, None, :],
        gpost=g_post[:, None, :],
        ang=a_norm_g[:, :, None],
        awsT=awsT,
        abs=a_bs[:, :, None, :],
        sink=jnp.repeat(b_sinks, LANE, axis=1)[:, None, :],
        aws_s=awsT[:, :, :t_new, :t_new],
        arep=jnp.asarray((r_[:, None] % t_new == np.arange(t_new)[None, :]).astype(np.float32), bf16),
        abias_s=jnp.tile(a_bs[:, :, None, :t_new], (1, 1, 1, reps)),
        akeep=jnp.asarray(((r_[:, None] // t_new == r_[None, :] // t_new)
                           & (r_[:, None] % t_new <= r_[None, :] % t_new)).astype(np.float32)),
    )

    xp = x_prompt
    xs = x_sample.reshape(ns_all * t_new, d)
    caches = tuple(c.transpose(0, 1, 3, 4, 2).reshape(depth, ns_all, -1, c.shape[2])
                   for c in (cache_b_k, cache_b_v, cache_c_k, cache_c_v))

    pk, pv, pck, pcv = [], [], [], []
    sk, sv, sck, scv, sav = [], [], [], [], []
    for l in range(depth):
        xp, bk, bv, ck, cv = _prompt_layer(l, xp, p)
        pk.append(bk); pv.append(bv); pck.append(ck); pcv.append(cv)
        xs, bk, bv, ck, cv, av = _sample_layer(l, xs, caches, p, t_new=t_new, tok_per_step=tok_per_step)
        sk.append(bk); sv.append(bv); sck.append(ck); scv.append(cv); sav.append(av)

    def stk(xs_, lead, heads):
        return jnp.stack(xs_).reshape(depth, lead, -1, heads, HEAD_DIM)

    def stk_fm(xs_, heads):
        a = jnp.stack(xs_)
        return a.reshape(depth, nb, heads, HEAD_DIM, a.shape[-1]).transpose(0, 1, 4, 2, 3)

    return (xp, xs.reshape(ns_all, t_new, d),
            stk_fm(pk, B_KV), stk_fm(pv, B_KV), stk_fm(pck, C_HEADS), stk_fm(pcv, C_HEADS),
            stk(sk, ns_all, B_KV), stk(sv, ns_all, B_KV), stk(sck, ns_all, C_HEADS), stk(scv, ns_all, C_HEADS),
            jnp.stack(sav).reshape(depth, ns_all, t_new, A_WIDTH))
```

```python
import functools
import math

import numpy as np
import jax
import jax.numpy as jnp
from jax import lax
from jax.experimental import pallas as pl
from jax.experimental.pallas import tpu as pltpu

bf16 = jnp.bfloat16
f32 = jnp.float32

HEAD_DIM = 64
CHUNK = 64
A_WIDTH, A_GROUPS, A_CHUNK = 256, 4, 128
A_GD = A_WIDTH // A_GROUPS
B_HEADS, B_KV = 8, 2
B_GROUP = B_HEADS // B_KV
B_WIDTH, B_KVW = B_HEADS * HEAD_DIM, B_KV * HEAD_DIM
B_PREV = 2
C_HEADS = 4
C_WIDTH = C_HEADS * HEAD_DIM
C_PREV = 8
C_CLIP = 128
T5_BUCKETS, T5_MAX_DIST = 32, 128
RMS_EPS = 1e-6
NEG_INF = -1e30
QK_SCALE = HEAD_DIM ** -0.5
LOG2E = math.log2(math.e)

LANE = 128
PAIR = 2 * CHUNK
B_WIN = (B_PREV + 2) * CHUNK
C_WIN = (C_PREV + 2) * CHUNK
B_KEEP = B_PREV * CHUNK
C_KEEP = C_PREV * CHUNK
ONES_ROWS = 16
VB_STRIDE = HEAD_DIM + ONES_ROWS
VC_STRIDE = 2 * HEAD_DIM + ONES_ROWS
SEQ_TILE = 512
SAMPLE_STREAMS_PER_BLOCK = 4
VMEM_LIMIT = 56 * 1024 * 1024
TOEPLITZ_UNROLL = 17

_IN_SIZES = [A_WIDTH, A_WIDTH, A_WIDTH, B_WIDTH, B_KVW, B_KVW, B_WIDTH, C_WIDTH, C_WIDTH, C_WIDTH, C_WIDTH]
_IN_OFF = [int(v) for v in np.cumsum([0] + _IN_SIZES)]
(T_AU, T_AV, T_AZ, T_BQ, T_BK, T_BV, T_BZ, T_CQ, T_CK, T_CV, T_CZ, T_END) = _IN_OFF
_T_SECTIONS = ((T_AU, T_BK), (T_BV, T_CK), (T_CV, T_END))
Y_A, Y_B, Y_C = 0, A_WIDTH, A_WIDTH + B_WIDTH

_NT = (((1,), (1,)), ((), ()))
_TN = (((0,), (0,)), ((), ()))


def _t5_bucket(rel):
    half = T5_BUCKETS // 2
    max_exact = half // 2
    ret = jnp.where(rel > 0, half, 0)
    n = jnp.abs(rel)
    nf = jnp.maximum(n, 1).astype(jnp.float32)
    large = max_exact + (jnp.log(nf / max_exact) / math.log(T5_MAX_DIST / max_exact)
                         * (half - max_exact)).astype(jnp.int32)
    large = jnp.minimum(large, half - 1)
    return ret + jnp.where(n < max_exact, n, large)


def _toeplitz_kernel(shift_ref, frev_ref, ok_ref, out_ref, *, n_heads, kmins):
    width = frev_ref.shape[-1]
    neg = jnp.full((CHUNK, LANE), NEG_INF, f32)

    def block(rb, carry):
        r0 = pl.multiple_of(rb * CHUNK, CHUNK)
        ok = ok_ref[pl.ds(r0, CHUNK), :] > 0.0
        rolled = pltpu.roll(frev_ref[...], shift_ref[rb], 1)
        ring = jnp.concatenate([rolled[:, :LANE], rolled[:, width - LANE:]], axis=1)
        for h in range(n_heads):
            x = jnp.broadcast_to(ring[h:h + 1, :], (CHUNK, 2 * LANE))
            y = pltpu.roll(x, 0, 1, stride=1, stride_axis=0)[:, :LANE]
            val = jnp.where(ok, y * LOG2E, NEG_INF)
            for v, kmin in enumerate(kmins):
                out_ref[0, v, pl.ds(r0, CHUNK), h * LANE:(h + 1) * LANE] = (
                    val if kmin == 0 else jnp.where(rb >= kmin, val, neg))
        return carry

    n_rb = ok_ref.shape[0] // CHUNK
    unroll = max(u for u in range(1, TOEPLITZ_UNROLL + 1) if n_rb % u == 0)
    lax.fori_loop(0, n_rb, block, 0, unroll=unroll)


def _build_bias(f_ext, rel_lo, rel, ok, *, kmins, const_below=None):
    n_layers, n_rel, n_heads = f_ext.shape
    n_rows = rel.shape[0]
    width = -(-n_rel // LANE) * LANE
    jj, ll = np.arange(CHUNK)[:, None], np.arange(LANE)[None, :]
    shifts = []
    for rb in range(n_rows // CHUNK):
        r, o = rel[rb * CHUNK:(rb + 1) * CHUNK], ok[rb * CHUNK:(rb + 1) * CHUNK]
        base = rel_lo + LANE - 1
        if o.any():
            bases = (r - jj + ll)[o]
            assert (bases == bases[0]).all(), "block is not Toeplitz on its valid entries"
            base = int(bases[0])
            if base - (LANE - 1) < rel_lo:
                assert const_below is not None and r[o].max() <= const_below
                base = rel_lo + LANE - 1
                assert (base + jj - ll)[o].max() <= const_below
            k = (base + jj - ll - rel_lo)[o]
            assert k.min() >= 0 and k.max() < n_rel
        shifts.append((width - ((n_rel - 1) - (base - rel_lo))) % width)
    frev = jnp.pad(jnp.flip(f_ext, axis=1).transpose(0, 2, 1), ((0, 0), (0, 0), (0, width - n_rel)))
    kern = functools.partial(_toeplitz_kernel, n_heads=n_heads, kmins=tuple(kmins))
    return pl.pallas_call(
        kern,
        grid=(n_layers,),
        in_specs=[pl.BlockSpec(memory_space=pltpu.SMEM),
                  pl.BlockSpec((None, n_heads, width), lambda l: (l, 0, 0)),
                  pl.BlockSpec((n_rows, LANE), lambda l: (0, 0))],
        out_specs=pl.BlockSpec((1, len(kmins), n_rows, n_heads * LANE), lambda l: (l, 0, 0, 0)),
        out_shape=jax.ShapeDtypeStruct((n_layers, len(kmins), n_rows, n_heads * LANE), f32),
        compiler_params=pltpu.CompilerParams(dimension_semantics=("arbitrary",),
                                             vmem_limit_bytes=VMEM_LIMIT),
        name="bias_toeplitz",
    )(jnp.asarray(np.asarray(shifts, np.int32)), frev, jnp.asarray(ok.astype(np.float32)))


def _wT_kernel(w_ref, o_ref):
    o_ref[...] = w_ref[...].T.astype(bf16)


def _transpose_weights(w, cols=4 * LANE):
    n_layers, k, n = w.shape
    return pl.pallas_call(
        _wT_kernel,
        grid=(n_layers, n // cols),
        in_specs=[pl.BlockSpec((None, k, cols), lambda l, c: (l, 0, c))],
        out_specs=pl.BlockSpec((None, cols, k), lambda l, c: (l, c, 0)),
        out_shape=jax.ShapeDtypeStruct((n_layers, n, k), bf16),
        compiler_params=pltpu.CompilerParams(dimension_semantics=("arbitrary", "arbitrary"),
                                             vmem_limit_bytes=VMEM_LIMIT),
        name="weights_T",
    )(w)


def _prompt_maps():
    i = np.arange(PAIR)[None, :]
    qi = i // CHUNK
    jb = np.arange(B_WIN)[:, None]
    rel_b = jb - B_KEEP - i
    ok_b = (jb // CHUNK >= qi) & (jb // CHUNK <= qi + B_PREV)
    jc = np.arange(C_WIN)[:, None]
    rel_c = jc - C_KEEP - i
    ok_c = (jc // CHUNK >= qi) & (jc // CHUNK <= qi + C_PREV)
    return rel_b, ok_b, rel_c, ok_c


def _sample_maps(lc_b, lc_c, t_new):
    ns = SAMPLE_STREAMS_PER_BLOCK
    c = np.arange(LANE)[None, :]
    sq, i = c // t_new, c % t_new

    def one(lc):
        r = np.arange(ns * lc + LANE)[:, None]
        is_new = r >= ns * lc
        sk = np.where(is_new, (r - ns * lc) // t_new, r // lc)
        j = np.where(is_new, lc + (r - ns * lc) % t_new, r % lc)
        return (j - lc - i).astype(np.int32), (sk == sq)

    return one(lc_b) + one(lc_c)


def _rms_rows(x, g):
    ms = jnp.mean(x * x, axis=-1, keepdims=True)
    return x * lax.rsqrt(ms + RMS_EPS) * g


def _rms_cols(xT, g_col):
    ms = jnp.mean(xT * xT, axis=0, keepdims=True)
    return xT * lax.rsqrt(ms + RMS_EPS) * g_col


def _project_T(pt_sc, wT_ref, h):
    for r0, r1 in _T_SECTIONS:
        pt_sc[r0:r1, :] = lax.dot_general(wT_ref[r0:r1, :], h, _NT, preferred_element_type=f32)


def _qblockdiag(qa, qb, lower):
    z = jnp.zeros((HEAD_DIM, LANE), bf16)
    if lower is None:
        return jnp.concatenate([jnp.concatenate([qa, z], axis=1),
                                jnp.concatenate([z, qb], axis=1)], axis=0)
    top = jnp.concatenate([qa, qb], axis=1)
    zz = jnp.zeros((HEAD_DIM, 2 * LANE), bf16)
    return jnp.concatenate([top, zz] if lower == 0 else [zz, top], axis=0)


def _with_ones(vT, group):
    ones = jnp.ones((ONES_ROWS, vT.shape[1]), vT.dtype)
    parts = []
    for r in range(0, vT.shape[0], group):
        parts += [vT[r:r + group, :], ones]
    return jnp.concatenate(parts, axis=0)


def _softmax_pv(s_m, sink, vwin):
    s, m = s_m
    if sink is not None:
        m = jnp.maximum(m, sink)
    e = jnp.exp2(s - m)
    o = jnp.dot(vwin, e.astype(bf16), preferred_element_type=f32)
    nv = vwin.shape[0] - ONES_ROWS
    den = o[nv:nv + 1, :]
    if sink is not None:
        den = den + jnp.exp2(sink - m)
    return o[:nv, :] * (1.0 / den)


def _scores(pt_sc, cols, kwin_b, bias_b, kwin_c, bias_c):
    def head_q(t0, hh):
        return (pt_sc[t0 + hh * HEAD_DIM:t0 + (hh + 1) * HEAD_DIM, cols] * (QK_SCALE * LOG2E)).astype(bf16)

    def with_max(s):
        return s, jnp.max(s, axis=0, keepdims=True)

    s_b, s_c = [], []
    for cb in range(B_HEADS // 2):
        ha, hb = 2 * cb, 2 * cb + 1
        qbd = _qblockdiag(head_q(T_BQ, ha), head_q(T_BQ, hb), ha // B_GROUP)
        s = jnp.dot(kwin_b, qbd, preferred_element_type=f32)
        s_b.append(with_max(s + bias_b[:, cb * 2 * LANE:(cb + 1) * 2 * LANE]))
    for cb in range(C_HEADS // 2):
        ha, hb = 2 * cb, 2 * cb + 1
        qbd = _qblockdiag(head_q(T_CQ, ha), head_q(T_CQ, hb), None)
        s = jnp.dot(kwin_c[:, cb * LANE:(cb + 1) * LANE], qbd, preferred_element_type=f32)
        s_c.append(with_max(s + bias_c[:, cb * 2 * LANE:(cb + 1) * 2 * LANE]))
    return s_b, s_c


def _attend(pt_sc, yT_sc, cols, s_b, s_c, vwinT_b, vwinT_c, sink_ref):
    for cb in range(B_HEADS // 2):
        ha, hb = 2 * cb, 2 * cb + 1
        kv = ha // B_GROUP
        sink = sink_ref[:, cb * 2 * LANE:(cb + 1) * 2 * LANE] * LOG2E
        o = _softmax_pv(s_b[cb], sink, vwinT_b[kv * VB_STRIDE:(kv + 1) * VB_STRIDE, :])
        for j, hh in enumerate((ha, hb)):
            gate = pt_sc[T_BZ + hh * HEAD_DIM:T_BZ + (hh + 1) * HEAD_DIM, cols]
            yT_sc[Y_B + hh * HEAD_DIM:Y_B + (hh + 1) * HEAD_DIM, cols] = (
                o[:, j * LANE:(j + 1) * LANE] * gate).astype(bf16)
    for cb in range(C_HEADS // 2):
        ha, hb = 2 * cb, 2 * cb + 1
        o = _softmax_pv(s_c[cb], None, vwinT_c[cb * VC_STRIDE:(cb + 1) * VC_STRIDE, :])
        for j, hh in enumerate((ha, hb)):
            gate = pt_sc[T_CZ + hh * HEAD_DIM:T_CZ + (hh + 1) * HEAD_DIM, cols]
            yT_sc[Y_C + hh * HEAD_DIM:Y_C + (hh + 1) * HEAD_DIM, cols] = (
                o[j * HEAD_DIM:(j + 1) * HEAD_DIM, j * LANE:(j + 1) * LANE] * gate).astype(bf16)


def _gate_rows(pt_sc):
    for r0, r1 in ((T_BZ, T_CQ), (T_CZ, T_END)):
        pt_sc[r0:r1, :] = jax.nn.silu(pt_sc[r0:r1, :])


def _finish(x, yT_sc, wout_ref, gpost_ref):
    out = lax.dot_general(yT_sc[...], wout_ref[...], _TN, preferred_element_type=f32)
    return x + _rms_rows(out, gpost_ref[...])


def _layer_spec(arr, layer):
    shape = arr.shape[1:]
    return pl.BlockSpec((None,) + shape, lambda *_: (layer,) + (0,) * len(shape),
                        pipeline_mode=pl.Buffered(1))


def _const_spec(arr):
    return pl.BlockSpec(arr.shape, lambda *_: (0,) * arr.ndim, pipeline_mode=pl.Buffered(1))


def _prompt_kernel(x_ref, gpre_ref, gpost_ref, wT_ref, wout_ref,
                   ang_ref, awsT_ref, abs_ref, sink_ref, biasb_ref, biasc_ref,
                   xo_ref, bk_ref, bv_ref, ck_ref, cv_ref,
                   pt_sc, kb_ring, kc_ring, vbT_ring, vcT_ring, yT_sc,
                   sb_even, mb_even, sc_even, mc_even, sb_odd, mb_odd, sc_odd, mc_odd):
    t = pl.program_id(1)
    ts = SEQ_TILE

    @pl.when(t == 0)
    def _():
        kb_ring[0:B_KEEP, :] = jnp.zeros((B_KEEP, B_KVW), bf16)
        kc_ring[0:C_KEEP, :] = jnp.zeros((C_KEEP, C_WIDTH), bf16)
        vbT_ring[:, 0:B_KEEP] = jnp.zeros((B_KV * VB_STRIDE, B_KEEP), bf16)
        vcT_ring[:, 0:C_KEEP] = jnp.zeros((C_HEADS // 2 * VC_STRIDE, C_KEEP), bf16)
        for g in range(B_KV):
            vbT_ring[g * VB_STRIDE + HEAD_DIM:(g + 1) * VB_STRIDE, :] = jnp.ones((ONES_ROWS, B_KEEP + ts), bf16)
        for g in range(C_HEADS // 2):
            vcT_ring[g * VC_STRIDE + 2 * HEAD_DIM:(g + 1) * VC_STRIDE, :] = jnp.ones((ONES_ROWS, C_KEEP + ts), bf16)

    x = x_ref[0]
    h = _rms_rows(x, gpre_ref[...]).astype(bf16)
    _project_T(pt_sc, wT_ref, h)
    kb_ring[B_KEEP:, :] = lax.dot_general(h, wT_ref[T_BK:T_BV, :], _NT,
                                          preferred_element_type=f32).astype(bf16)
    kc_ring[C_KEEP:, :] = lax.dot_general(h, wT_ref[T_CK:T_CV, :], _NT,
                                          preferred_element_type=f32).astype(bf16)
    for g in range(B_KV):
        vbT_ring[g * VB_STRIDE:g * VB_STRIDE + HEAD_DIM, B_KEEP:] = (
            pt_sc[T_BV + g * HEAD_DIM:T_BV + (g + 1) * HEAD_DIM, :].astype(bf16))
    for g in range(C_HEADS // 2):
        vcT_ring[g * VC_STRIDE:g * VC_STRIDE + 2 * HEAD_DIM, C_KEEP:] = (
            pt_sc[T_CV + g * 2 * HEAD_DIM:T_CV + (g + 1) * 2 * HEAD_DIM, :].astype(bf16))

    _gate_rows(pt_sc)

    @pl.when(t == pl.num_programs(1) - 1)
    def _():
        bk_ref[0] = lax.dot_general(wT_ref[T_BK:T_BV, :], h[ts - B_KEEP:, :], _NT, preferred_element_type=f32)
        ck_ref[0] = lax.dot_general(wT_ref[T_CK:T_CV, :], h[ts - C_KEEP:, :], _NT, preferred_element_type=f32)
        bv_ref[0] = pt_sc[T_BV:T_BZ, ts - B_KEEP:]
        cv_ref[0] = pt_sc[T_CV:T_CZ, ts - C_KEEP:]

    vn = _rms_cols(pt_sc[T_AV:T_AV + A_WIDTH, :], ang_ref[...]).astype(bf16)
    n_ac = ts // A_CHUNK
    row = lax.broadcasted_iota(jnp.int32, (A_CHUNK, A_CHUNK), 0)
    col = lax.broadcasted_iota(jnp.int32, (A_CHUNK, A_CHUNK), 1)
    for g in range(A_GROUPS):
        rows = slice(g * A_GD, (g + 1) * A_GD)
        wgt = jnp.where(row <= col, awsT_ref[g], 0.0).astype(bf16)
        lhs = jnp.concatenate([vn[rows, c * A_CHUNK:(c + 1) * A_CHUNK] for c in range(n_ac)], axis=0)
        mix = jnp.dot(lhs, wgt, preferred_element_type=f32) + abs_ref[g]
        for c in range(n_ac):
            cols = slice(c * A_CHUNK, (c + 1) * A_CHUNK)
            u = pt_sc[T_AU + g * A_GD:T_AU + (g + 1) * A_GD, cols]
            z = pt_sc[T_AZ + g * A_GD:T_AZ + (g + 1) * A_GD, cols]
            yT_sc[Y_A + g * A_GD:Y_A + (g + 1) * A_GD, cols] = (
                u * mix[c * A_GD:(c + 1) * A_GD, :] * jax.nn.silu(z)).astype(bf16)

    n_pairs = ts // PAIR

    even, odd = (sb_even, mb_even, sc_even, mc_even), (sb_odd, mb_odd, sc_odd, mc_odd)

    def scores_into(p, bufs):
        sb_ref, mb_ref, sc_ref, mc_ref = bufs
        pg = t * n_pairs + p
        start = p * PAIR if isinstance(p, int) else pl.multiple_of(p * PAIR, PAIR)
        s_b, s_c = _scores(pt_sc, pl.ds(start, PAIR),
                           kb_ring[pl.ds(start, B_WIN), :], biasb_ref.at[jnp.minimum(pg, 1)],
                           kc_ring[pl.ds(start, C_WIN), :], biasc_ref.at[jnp.minimum(pg, C_PREV // 2)])
        for cb, (v, m) in enumerate(s_b):
            sb_ref[cb] = v
            mb_ref[cb] = m
        for cb, (v, m) in enumerate(s_c):
            sc_ref[cb] = v
            mc_ref[cb] = m

    def attend_from(p, bufs):
        sb_ref, mb_ref, sc_ref, mc_ref = bufs
        start = p * PAIR if isinstance(p, int) else pl.multiple_of(p * PAIR, PAIR)
        _attend(pt_sc, yT_sc, pl.ds(start, PAIR),
                [(sb_ref[cb], mb_ref[cb]) for cb in range(B_HEADS // 2)],
                [(sc_ref[cb], mc_ref[cb]) for cb in range(C_HEADS // 2)],
                vbT_ring[:, pl.ds(start, B_WIN)], vcT_ring[:, pl.ds(start, C_WIN)], sink_ref)

    scores_into(0, even)

    def two_pairs(q, carry):
        p = 2 * q
        scores_into(p + 1, odd)
        attend_from(p, even)
        scores_into(p + 2, even)
        attend_from(p + 1, odd)
        return carry

    def last_pairs(q, carry):
        p = 2 * q
        scores_into(p + 1, odd)
        attend_from(p, even)
        attend_from(p + 1, odd)
        return carry

    n_full = jnp.minimum(t + n_pairs, n_pairs // 2 - 1)
    lax.fori_loop(0, n_full, two_pairs, 0)
    lax.fori_loop(n_full, n_full + 1, last_pairs, 0)

    xo_ref[0] = _finish(x, yT_sc, wout_ref, gpost_ref)

    kb_ring[0:B_KEEP, :] = kb_ring[ts:ts + B_KEEP, :]
    kc_ring[0:C_KEEP, :] = kc_ring[ts:ts + C_KEEP, :]
    vbT_ring[:, 0:B_KEEP] = vbT_ring[:, ts:ts + B_KEEP]
    vcT_ring[:, 0:C_KEEP] = vcT_ring[:, ts:ts + C_KEEP]


def _prompt_layer(layer, x, p):
    nb, seq, d = x.shape
    ts = SEQ_TILE
    assert seq % ts == 0 and ts == C_KEEP
    kv_spec = lambda keep, width: pl.BlockSpec((1, width, keep), lambda b, t: (b, 0, 0))
    return pl.pallas_call(
        _prompt_kernel,
        grid=(nb, seq // ts),
        in_specs=[
            pl.BlockSpec((1, ts, d), lambda b, t: (b, t, 0)),
            _layer_spec(p["gpre"], layer), _layer_spec(p["gpost"], layer),
            _layer_spec(p["wT"], layer),
            _layer_spec(p["wout"], layer),
            _layer_spec(p["ang"], layer), _layer_spec(p["awsT"], layer), _layer_spec(p["abs"], layer),
            _layer_spec(p["sink"], layer), _const_spec(p["bias_pb"]), _layer_spec(p["bias_pc"], layer),
        ],
        out_specs=[
            pl.BlockSpec((1, ts, d), lambda b, t: (b, t, 0)),
            kv_spec(B_KEEP, B_KVW), kv_spec(B_KEEP, B_KVW), kv_spec(C_KEEP, C_WIDTH), kv_spec(C_KEEP, C_WIDTH),
        ],
        out_shape=[
            jax.ShapeDtypeStruct((nb, seq, d), f32),
            jax.ShapeDtypeStruct((nb, B_KVW, B_KEEP), f32),
            jax.ShapeDtypeStruct((nb, B_KVW, B_KEEP), f32),
            jax.ShapeDtypeStruct((nb, C_WIDTH, C_KEEP), f32),
            jax.ShapeDtypeStruct((nb, C_WIDTH, C_KEEP), f32),
        ],
        scratch_shapes=[
            pltpu.VMEM((T_END, ts), f32),
            pltpu.VMEM((B_KEEP + ts, B_KVW), bf16),
            pltpu.VMEM((C_KEEP + ts, C_WIDTH), bf16),
            pltpu.VMEM((B_KV * VB_STRIDE, B_KEEP + ts), bf16),
            pltpu.VMEM((C_HEADS // 2 * VC_STRIDE, C_KEEP + ts), bf16),
            pltpu.VMEM((A_WIDTH + B_WIDTH + C_WIDTH, ts), bf16),
        ] + 2 * [
            pltpu.VMEM((B_HEADS // 2, B_WIN, 2 * LANE), f32), pltpu.VMEM((B_HEADS // 2, 1, 2 * LANE), f32),
            pltpu.VMEM((C_HEADS // 2, C_WIN, 2 * LANE), f32), pltpu.VMEM((C_HEADS // 2, 1, 2 * LANE), f32),
        ],
        compiler_params=pltpu.CompilerParams(dimension_semantics=("arbitrary", "arbitrary"),
                                             vmem_limit_bytes=VMEM_LIMIT),
        name="prompt_layer",
    )(x, p["gpre"], p["gpost"], p["wT"], p["wout"], p["ang"], p["awsT"],
      p["abs"], p["sink"], p["bias_pb"], p["bias_pc"])


def _sample_kernel(x_ref, cbk_ref, cbv_ref, cck_ref, ccv_ref, gpre_ref, gpost_ref, wT_ref,
                   wout_ref, ang_ref, aws_ref, arep_ref, akeep_ref, abias_ref,
                   sink_ref, biasb_ref, biasc_ref,
                   xo_ref, bk_ref, bv_ref, ck_ref, cv_ref, av_ref,
                   pt_sc, yT_sc):
    ns = SAMPLE_STREAMS_PER_BLOCK
    x = x_ref[...]
    ntok = x.shape[0]
    h = _rms_rows(x, gpre_ref[...]).astype(bf16)
    _project_T(pt_sc, wT_ref, h)
    kvn_b = lax.dot_general(h, wT_ref[T_BK:T_BZ, :], _NT, preferred_element_type=f32)
    kvn_c = lax.dot_general(h, wT_ref[T_CK:T_CZ, :], _NT, preferred_element_type=f32)
    kn_c = kvn_c[:, 0:C_WIDTH]
    bk_ref[...] = kvn_b[:, 0:B_KVW]
    bv_ref[...] = kvn_b[:, B_KVW:]
    ck_ref[...] = kn_c
    cv_ref[...] = kvn_c[:, C_WIDTH:]

    vn = _rms_cols(pt_sc[T_AV:T_AV + A_WIDTH, :], ang_ref[...])
    av_ref[...] = vn.T
    vnb = vn.astype(bf16)
    keep = akeep_ref[...] > 0.0
    rep = arep_ref[...]
    for g in range(A_GROUPS):
        rows = slice(g * A_GD, (g + 1) * A_GD)
        tiled = lax.dot_general(jnp.dot(rep, aws_ref[g].astype(bf16), preferred_element_type=f32).astype(bf16),
                                rep, _NT, preferred_element_type=f32)
        wgt = jnp.where(keep, tiled, 0.0).astype(bf16)
        mix = jnp.dot(vnb[rows, :], wgt, preferred_element_type=f32) + abias_ref[g]
        u = pt_sc[T_AU + g * A_GD:T_AU + (g + 1) * A_GD, :]
        z = pt_sc[T_AZ + g * A_GD:T_AZ + (g + 1) * A_GD, :]
        yT_sc[Y_A + g * A_GD:Y_A + (g + 1) * A_GD, :] = (u * mix * jax.nn.silu(z)).astype(bf16)

    _gate_rows(pt_sc)
    for blk in range(ntok // LANE):
        cols = slice(blk * LANE, (blk + 1) * LANE)
        streams = range(blk * ns, (blk + 1) * ns)
        kwin_b = jnp.concatenate([cbk_ref[s].T.astype(bf16) for s in streams]
                                 + [kvn_b[cols, 0:B_KVW].astype(bf16)], axis=0)
        vwinT_b = jnp.concatenate([cbv_ref[s].astype(bf16) for s in streams]
                                  + [pt_sc[T_BV:T_BV + B_KVW, cols].astype(bf16)], axis=1)
        kwin_c = jnp.concatenate([cck_ref[s].T.astype(bf16) for s in streams]
                                 + [kn_c[cols, :].astype(bf16)], axis=0)
        vwinT_c = jnp.concatenate([ccv_ref[s].astype(bf16) for s in streams]
                                  + [pt_sc[T_CV:T_CV + C_WIDTH, cols].astype(bf16)], axis=1)
        s_b, s_c = _scores(pt_sc, cols, kwin_b, biasb_ref, kwin_c, biasc_ref)
        _attend(pt_sc, yT_sc, cols, s_b, s_c, _with_ones(vwinT_b, HEAD_DIM), _with_ones(vwinT_c, 2 * HEAD_DIM),
                sink_ref)

    xo_ref[...] = _finish(x, yT_sc, wout_ref, gpost_ref)


def _sample_layer(layer, x, caches, p, *, t_new, tok_per_step):
    ntok, d = x.shape
    spb = tok_per_step // t_new
    cbk, cbv, cck, ccv = caches

    def rows(width):
        return pl.BlockSpec((tok_per_step, width), lambda i: (i, 0))

    def cache(c):
        return pl.BlockSpec((None, spb) + c.shape[2:], lambda i: (layer, i, 0, 0))

    return pl.pallas_call(
        _sample_kernel,
        grid=(ntok // tok_per_step,),
        in_specs=[
            rows(d), cache(cbk), cache(cbv), cache(cck), cache(ccv),
            _layer_spec(p["gpre"], layer), _layer_spec(p["gpost"], layer),
            _layer_spec(p["wT"], layer),
            _layer_spec(p["wout"], layer),
            _layer_spec(p["ang"], layer), _layer_spec(p["aws_s"], layer), _const_spec(p["arep"]),
            _const_spec(p["akeep"]), _layer_spec(p["abias_s"], layer),
            _layer_spec(p["sink"], layer), _const_spec(p["bias_sb"]), _layer_spec(p["bias_sc"], layer),
        ],
        out_specs=[rows(d), rows(B_KVW), rows(B_KVW), rows(C_WIDTH), rows(C_WIDTH), rows(A_WIDTH)],
        out_shape=[
            jax.ShapeDtypeStruct((ntok, d), f32),
            jax.ShapeDtypeStruct((ntok, B_KVW), f32), jax.ShapeDtypeStruct((ntok, B_KVW), f32),
            jax.ShapeDtypeStruct((ntok, C_WIDTH), f32), jax.ShapeDtypeStruct((ntok, C_WIDTH), f32),
            jax.ShapeDtypeStruct((ntok, A_WIDTH), f32),
        ],
        scratch_shapes=[
            pltpu.VMEM((T_END, tok_per_step), f32),
            pltpu.VMEM((A_WIDTH + B_WIDTH + C_WIDTH, tok_per_step), bf16),
        ],
        compiler_params=pltpu.CompilerParams(dimension_semantics=("arbitrary",),
                                             vmem_limit_bytes=VMEM_LIMIT),
        name="sample_layer",
    )(x, cbk, cbv, cck, ccv, p["gpre"], p["gpost"], p["wT"], p["wout"],
      p["ang"], p["aws_s"], p["arep"], p["akeep"], p["abias_s"], p["sink"], p["bias_sb"], p["bias_sc"])


def kernel(x_prompt, x_sample, cache_b_k, cache_b_v, cache_c_k, cache_c_v, g_pre, g_post, w_in, w_out,
           a_norm_g, a_ws, a_bs, b_sinks, c_rel_bias, t5_bias):
    depth = w_in.shape[0]
    nb, seq, d = x_prompt.shape
    ns_all, t_new, _ = x_sample.shape
    lc_b, lc_c = cache_b_k.shape[2], cache_c_k.shape[2]
    assert lc_b == B_KEEP and lc_c == C_KEEP and LANE % t_new == 0
    assert SAMPLE_STREAMS_PER_BLOCK * t_new == LANE
    tok_per_step = 2 * LANE
    reps = tok_per_step // t_new

    rel_pb, ok_pb, rel_pc, ok_pc = _prompt_maps()
    rel_sb, ok_sb, rel_sc, ok_sc = _sample_maps(lc_b, lc_c, t_new)
    b_lo = int(min(rel_pb.min(), rel_sb.min()))
    b_hi = int(max(rel_pb.max(), rel_sb.max()))
    t5_rel = t5_bias[_t5_bucket(jnp.arange(b_lo, b_hi + 1))][None]
    c_lo = -C_CLIP - 2 * LANE + 1
    c_hi = int(max(rel_pc.max(), rel_sc.max()))
    assert c_hi < C_CLIP
    c_rel = jnp.concatenate([jnp.broadcast_to(c_rel_bias[:, :1], (depth, -C_CLIP - c_lo, C_HEADS)),
                             c_rel_bias[:, :c_hi + C_CLIP + 1]], axis=1)
    r_ = np.arange(tok_per_step)
    awsT = a_ws.transpose(0, 1, 3, 2)

    p = dict(
        bias_pb=_build_bias(t5_rel, b_lo, rel_pb, ok_pb, kmins=(B_PREV, 0))[0],
        bias_sb=_build_bias(t5_rel, b_lo, rel_sb, ok_sb, kmins=(0,))[0, 0],
        bias_pc=_build_bias(c_rel, c_lo, rel_pc, ok_pc, const_below=-C_CLIP,
                            kmins=tuple(max(C_PREV - 2 * v, 0) for v in range(C_PREV // 2 + 1))),
        bias_sc=_build_bias(c_rel, c_lo, rel_sc, ok_sc, const_below=-C_CLIP, kmins=(0,))[:, 0],
        wT=_transpose_weights(w_in),
        wout=w_out.astype(bf16),
        gpre=g_pre[:, None, :],
        gpost=g_post[:, None, :],
        ang=a_norm_g[:, :, None],
        awsT=awsT,
        abs=a_bs[:, :, None, :],
        sink=jnp.repeat(b_sinks, LANE, axis=1)[:, None, :],
        aws_s=awsT[:, :, :t_new, :t_new],
        arep=jnp.asarray((r_[:, None] % t_new == np.arange(t_new)[None, :]).astype(np.float32), bf16),
        abias_s=jnp.tile(a_bs[:, :, None, :t_new], (1, 1, 1, reps)),
        akeep=jnp.asarray(((r_[:, None] // t_new == r_[None, :] // t_new)
                           & (r_[:, None] % t_new <= r_[None, :] % t_new)).astype(np.float32)),
    )

    xp = x_prompt
    xs = x_sample.reshape(ns_all * t_new, d)
    caches = tuple(c.transpose(0, 1, 3, 4, 2).reshape(depth, ns_all, -1, c.shape[2])
                   for c in (cache_b_k, cache_b_v, cache_c_k, cache_c_v))

    pk, pv, pck, pcv = [], [], [], []
    sk, sv, sck, scv, sav = [], [], [], [], []
    for l in range(depth):
        xp, bk, bv, ck, cv = _prompt_layer(l, xp, p)
        pk.append(bk); pv.append(bv); pck.append(ck); pcv.append(cv)
        xs, bk, bv, ck, cv, av = _sample_layer(l, xs, caches, p, t_new=t_new, tok_per_step=tok_per_step)
        sk.append(bk); sv.append(bv); sck.append(ck); scv.append(cv); sav.append(av)

    def stk(xs_, lead, heads):
        return jnp.stack(xs_).reshape(depth, lead, -1, heads, HEAD_DIM)

    def stk_fm(xs_, heads):
        a = jnp.stack(xs_)
        return a.reshape(depth, nb, heads, HEAD_DIM, a.shape[-1]).transpose(0, 1, 4, 2, 3)

    return (xp, xs.reshape(ns_all, t_new, d),
            stk_fm(pk, B_KV), stk_fm(pv, B_KV), stk_fm(pck, C_HEADS), stk_fm(pcv, C_HEADS),
            stk(sk, ns_all, B_KV), stk(sv, ns_all, B_KV), stk(sck, ns_all, C_HEADS), stk(scv, ns_all, C_HEADS),
            jnp.stack(sav).reshape(depth, ns_all, t_new, A_WIDTH))
```

```python
import functools
import math

import numpy as np
import jax
import jax.numpy as jnp
from jax import lax
from jax.experimental import pallas as pl
from jax.experimental.pallas import tpu as pltpu

bf16 = jnp.bfloat16
f32 = jnp.float32

HEAD_DIM = 64
CHUNK = 64
A_WIDTH, A_GROUPS, A_CHUNK = 256, 4, 128
A_GD = A_WIDTH // A_GROUPS
B_HEADS, B_KV = 8, 2
B_GROUP = B_HEADS // B_KV
B_WIDTH, B_KVW = B_HEADS * HEAD_DIM, B_KV * HEAD_DIM
B_PREV = 2
C_HEADS = 4
C_WIDTH = C_HEADS * HEAD_DIM
C_PREV = 8
C_CLIP = 128
T5_BUCKETS, T5_MAX_DIST = 32, 128
RMS_EPS = 1e-6
NEG_INF = -1e30
QK_SCALE = HEAD_DIM ** -0.5
LOG2E = math.log2(math.e)

LANE = 128
PAIR = 2 * CHUNK
B_WIN = (B_PREV + 2) * CHUNK
C_WIN = (C_PREV + 2) * CHUNK
B_KEEP = B_PREV * CHUNK
C_KEEP = C_PREV * CHUNK
ONES_ROWS = 16
VB_STRIDE = HEAD_DIM + ONES_ROWS
VC_STRIDE = 2 * HEAD_DIM + ONES_ROWS
SEQ_TILE = 512
SAMPLE_STREAMS_PER_BLOCK = 4
VMEM_LIMIT = 56 * 1024 * 1024
TOEPLITZ_UNROLL = 17

_IN_SIZES = [A_WIDTH, A_WIDTH, A_WIDTH, B_WIDTH, B_KVW, B_KVW, B_WIDTH, C_WIDTH, C_WIDTH, C_WIDTH, C_WIDTH]
_IN_OFF = [int(v) for v in np.cumsum([0] + _IN_SIZES)]
(T_AU, T_AV, T_AZ, T_BQ, T_BK, T_BV, T_BZ, T_CQ, T_CK, T_CV, T_CZ, T_END) = _IN_OFF
_T_SECTIONS = ((T_AU, T_BK), (T_BV, T_CK), (T_CV, T_END))
Y_A, Y_B, Y_C = 0, A_WIDTH, A_WIDTH + B_WIDTH

_NT = (((1,), (1,)), ((), ()))
_TN = (((0,), (0,)), ((), ()))


def _t5_bucket(rel):
    half = T5_BUCKETS // 2
    max_exact = half // 2
    ret = jnp.where(rel > 0, half, 0)
    n = jnp.abs(rel)
    nf = jnp.maximum(n, 1).astype(jnp.float32)
    large = max_exact + (jnp.log(nf / max_exact) / math.log(T5_MAX_DIST / max_exact)
                         * (half - max_exact)).astype(jnp.int32)
    large = jnp.minimum(large, half - 1)
    return ret + jnp.where(n < max_exact, n, large)


def _toeplitz_kernel(shift_ref, frev_ref, ok_ref, out_ref, *, n_heads, kmins):
    width = frev_ref.shape[-1]
    neg = jnp.full((CHUNK, LANE), NEG_INF, f32)

    def block(rb, carry):
        r0 = pl.multiple_of(rb * CHUNK, CHUNK)
        ok = ok_ref[pl.ds(r0, CHUNK), :] > 0.0
        rolled = pltpu.roll(frev_ref[...], shift_ref[rb], 1)
        ring = jnp.concatenate([rolled[:, :LANE], rolled[:, width - LANE:]], axis=1)
        for h in range(n_heads):
            x = jnp.broadcast_to(ring[h:h + 1, :], (CHUNK, 2 * LANE))
            y = pltpu.roll(x, 0, 1, stride=1, stride_axis=0)[:, :LANE]
            val = jnp.where(ok, y * LOG2E, NEG_INF)
            for v, kmin in enumerate(kmins):
                out_ref[0, v, pl.ds(r0, CHUNK), h * LANE:(h + 1) * LANE] = (
                    val if kmin == 0 else jnp.where(rb >= kmin, val, neg))
        return carry

    n_rb = ok_ref.shape[0] // CHUNK
    unroll = max(u for u in range(1, TOEPLITZ_UNROLL + 1) if n_rb % u == 0)
    lax.fori_loop(0, n_rb, block, 0, unroll=unroll)


def _build_bias(f_ext, rel_lo, rel, ok, *, kmins, const_below=None):
    n_layers, n_rel, n_heads = f_ext.shape
    n_rows = rel.shape[0]
    width = -(-n_rel // LANE) * LANE
    jj, ll = np.arange(CHUNK)[:, None], np.arange(LANE)[None, :]
    shifts = []
    for rb in range(n_rows // CHUNK):
        r, o = rel[rb * CHUNK:(rb + 1) * CHUNK], ok[rb * CHUNK:(rb + 1) * CHUNK]
        base = rel_lo + LANE - 1
        if o.any():
            bases = (r - jj + ll)[o]
            assert (bases == bases[0]).all(), "block is not Toeplitz on its valid entries"
            base = int(bases[0])
            if base - (LANE - 1) < rel_lo:
                assert const_below is not None and r[o].max() <= const_below
                base = rel_lo + LANE - 1
                assert (base + jj - ll)[o].max() <= const_below
            k = (base + jj - ll - rel_lo)[o]
            assert k.min() >= 0 and k.max() < n_rel
        shifts.append((width - ((n_rel - 1) - (base - rel_lo))) % width)
    frev = jnp.pad(jnp.flip(f_ext, axis=1).transpose(0, 2, 1), ((0, 0), (0, 0), (0, width - n_rel)))
    kern = functools.partial(_toeplitz_kernel, n_heads=n_heads, kmins=tuple(kmins))
    return pl.pallas_call(
        kern,
        grid=(n_layers,),
        in_specs=[pl.BlockSpec(memory_space=pltpu.SMEM),
                  pl.BlockSpec((None, n_heads, width), lambda l: (l, 0, 0)),
                  pl.BlockSpec((n_rows, LANE), lambda l: (0, 0))],
        out_specs=pl.BlockSpec((1, len(kmins), n_rows, n_heads * LANE), lambda l: (l, 0, 0, 0)),
        out_shape=jax.ShapeDtypeStruct((n_layers, len(kmins), n_rows, n_heads * LANE), f32),
        compiler_params=pltpu.CompilerParams(dimension_semantics=("arbitrary",),
                                             vmem_limit_bytes=VMEM_LIMIT),
        name="bias_toeplitz",
    )(jnp.asarray(np.asarray(shifts, np.int32)), frev, jnp.asarray(ok.astype(np.float32)))


def _wT_kernel(w_ref, o_ref):
    o_ref[...] = w_ref[...].T.astype(bf16)


def _transpose_weights(w, cols=4 * LANE):
    n_layers, k, n = w.shape
    return pl.pallas_call(
        _wT_kernel,
        grid=(n_layers, n // cols),
        in_specs=[pl.BlockSpec((None, k, cols), lambda l, c: (l, 0, c))],
        out_specs=pl.BlockSpec((None, cols, k), lambda l, c: (l, c, 0)),
        out_shape=jax.ShapeDtypeStruct((n_layers, n, k), bf16),
        compiler_params=pltpu.CompilerParams(dimension_semantics=("arbitrary", "arbitrary"),
                                             vmem_limit_bytes=VMEM_LIMIT),
        name="weights_T",
    )(w)


def _prompt_maps():
    i = np.arange(PAIR)[None, :]
    qi = i // CHUNK
    jb = np.arange(B_WIN)[:, None]
    rel_b = jb - B_KEEP - i
    ok_b = (jb // CHUNK >= qi) & (jb // CHUNK <= qi + B_PREV)
    jc = np.arange(C_WIN)[:, None]
    rel_c = jc - C_KEEP - i
    ok_c = (jc // CHUNK >= qi) & (jc // CHUNK <= qi + C_PREV)
    return rel_b, ok_b, rel_c, ok_c


def _sample_maps(lc_b, lc_c, t_new):
    ns = SAMPLE_STREAMS_PER_BLOCK
    c = np.arange(LANE)[None, :]
    sq, i = c // t_new, c % t_new

    def one(lc):
        r = np.arange(ns * lc + LANE)[:, None]
        is_new = r >= ns * lc
        sk = np.where(is_new, (r - ns * lc) // t_new, r // lc)
        j = np.where(is_new, lc + (r - ns * lc) % t_new, r % lc)
        return (j - lc - i).astype(np.int32), (sk == sq)

    return one(lc_b) + one(lc_c)


def _rms_rows(x, g):
    ms = jnp.mean(x * x, axis=-1, keepdims=True)
    return x * lax.rsqrt(ms + RMS_EPS) * g


def _rms_cols(xT, g_col):
    ms = jnp.mean(xT * xT, axis=0, keepdims=True)
    return xT * lax.rsqrt(ms + RMS_EPS) * g_col


def _project_T(pt_sc, wT_ref, h):
    for r0, r1 in _T_SECTIONS:
        pt_sc[r0:r1, :] = lax.dot_general(wT_ref[r0:r1, :], h, _NT, preferred_element_type=f32)


def _qblockdiag(qa, qb, lower):
    z = jnp.zeros((HEAD_DIM, LANE), bf16)
    if lower is None:
        return jnp.concatenate([jnp.concatenate([qa, z], axis=1),
                                jnp.concatenate([z, qb], axis=1)], axis=0)
    top = jnp.concatenate([qa, qb], axis=1)
    zz = jnp.zeros((HEAD_DIM, 2 * LANE), bf16)
    return jnp.concatenate([top, zz] if lower == 0 else [zz, top], axis=0)


def _with_ones(vT, group):
    ones = jnp.ones((ONES_ROWS, vT.shape[1]), vT.dtype)
    parts = []
    for r in range(0, vT.shape[0], group):
        parts += [vT[r:r + group, :], ones]
    return jnp.concatenate(parts, axis=0)


def _softmax_pv(s_m, sink, vwin):
    s, m = s_m
    if sink is not None:
        m = jnp.maximum(m, sink)
    e = jnp.exp2(s - m)
    o = jnp.dot(vwin, e.astype(bf16), preferred_element_type=f32)
    nv = vwin.shape[0] - ONES_ROWS
    den = o[nv:nv + 1, :]
    if sink is not None:
        den = den + jnp.exp2(sink - m)
    return o[:nv, :] * (1.0 / den)


def _scores(pt_sc, cols, kwin_b, bias_b, kwin_c, bias_c):
    def head_q(t0, hh):
        return (pt_sc[t0 + hh * HEAD_DIM:t0 + (hh + 1) * HEAD_DIM, cols] * (QK_SCALE * LOG2E)).astype(bf16)

    def with_max(s):
        return s, jnp.max(s, axis=0, keepdims=True)

    s_b, s_c = [], []
    for cb in range(B_HEADS // 2):
        ha, hb = 2 * cb, 2 * cb + 1
        qbd = _qblockdiag(head_q(T_BQ, ha), head_q(T_BQ, hb), ha // B_GROUP)
        s = jnp.dot(kwin_b, qbd, preferred_element_type=f32)
        s_b.append(with_max(s + bias_b[:, cb * 2 * LANE:(cb + 1) * 2 * LANE]))
    for cb in range(C_HEADS // 2):
        ha, hb = 2 * cb, 2 * cb + 1
        qbd = _qblockdiag(head_q(T_CQ, ha), head_q(T_CQ, hb), None)
        s = jnp.dot(kwin_c[:, cb * LANE:(cb + 1) * LANE], qbd, preferred_element_type=f32)
        s_c.append(with_max(s + bias_c[:, cb * 2 * LANE:(cb + 1) * 2 * LANE]))
    return s_b, s_c


def _attend(pt_sc, yT_sc, cols, s_b, s_c, vwinT_b, vwinT_c, sink_ref):
    for cb in range(B_HEADS // 2):
        ha, hb = 2 * cb, 2 * cb + 1
        kv = ha // B_GROUP
        sink = sink_ref[:, cb * 2 * LANE:(cb + 1) * 2 * LANE] * LOG2E
        o = _softmax_pv(s_b[cb], sink, vwinT_b[kv * VB_STRIDE:(kv + 1) * VB_STRIDE, :])
        for j, hh in enumerate((ha, hb)):
            gate = pt_sc[T_BZ + hh * HEAD_DIM:T_BZ + (hh + 1) * HEAD_DIM, cols]
            yT_sc[Y_B + hh * HEAD_DIM:Y_B + (hh + 1) * HEAD_DIM, cols] = (
                o[:, j * LANE:(j + 1) * LANE] * gate).astype(bf16)
    for cb in range(C_HEADS // 2):
        ha, hb = 2 * cb, 2 * cb + 1
        o = _softmax_pv(s_c[cb], None, vwinT_c[cb * VC_STRIDE:(cb + 1) * VC_STRIDE, :])
        for j, hh in enumerate((ha, hb)):
            gate = pt_sc[T_CZ + hh * HEAD_DIM:T_CZ + (hh + 1) * HEAD_DIM, cols]
            yT_sc[Y_C + hh * HEAD_DIM:Y_C + (hh + 1) * HEAD_DIM, cols] = (
                o[j * HEAD_DIM:(j + 1) * HEAD_DIM, j * LANE:(j + 1) * LANE] * gate).astype(bf16)


def _gate_rows(pt_sc):
    for r0, r1 in ((T_BZ, T_CQ), (T_CZ, T_END)):
        pt_sc[r0:r1, :] = jax.nn.silu(pt_sc[r0:r1, :])


def _finish(x, yT_sc, wout_ref, gpost_ref):
    out = lax.dot_general(yT_sc[...], wout_ref[...], _TN, preferred_element_type=f32)
    return x + _rms_rows(out, gpost_ref[...])


def _layer_spec(arr, layer):
    shape = arr.shape[1:]
    return pl.BlockSpec((None,) + shape, lambda *_: (layer,) + (0,) * len(shape),
                        pipeline_mode=pl.Buffered(1))


def _const_spec(arr):
    return pl.BlockSpec(arr.shape, lambda *_: (0,) * arr.ndim, pipeline_mode=pl.Buffered(1))


def _prompt_kernel(x_ref, gpre_ref, gpost_ref, wT_ref, wout_ref,
                   ang_ref, awsT_ref, abs_ref, sink_ref, biasb_ref, biasc_ref,
                   xo_ref, bk_ref, bv_ref, ck_ref, cv_ref,
                   pt_sc, kb_ring, kc_ring, vbT_ring, vcT_ring, yT_sc,
                   sb_even, mb_even, sc_even, mc_even, sb_odd, mb_odd, sc_odd, mc_odd):
    t = pl.program_id(1)
    ts = SEQ_TILE

    @pl.when(t == 0)
    def _():
        kb_ring[0:B_KEEP, :] = jnp.zeros((B_KEEP, B_KVW), bf16)
        kc_ring[0:C_KEEP, :] = jnp.zeros((C_KEEP, C_WIDTH), bf16)
        vbT_ring[:, 0:B_KEEP] = jnp.zeros((B_KV * VB_STRIDE, B_KEEP), bf16)
        vcT_ring[:, 0:C_KEEP] = jnp.zeros((C_HEADS // 2 * VC_STRIDE, C_KEEP), bf16)
        for g in range(B_KV):
            vbT_ring[g * VB_STRIDE + HEAD_DIM:(g + 1) * VB_STRIDE, :] = jnp.ones((ONES_ROWS, B_KEEP + ts), bf16)
        for g in range(C_HEADS // 2):
            vcT_ring[g * VC_STRIDE + 2 * HEAD_DIM:(g + 1) * VC_STRIDE, :] = jnp.ones((ONES_ROWS, C_KEEP + ts), bf16)

    x = x_ref[0]
    h = _rms_rows(x, gpre_ref[...]).astype(bf16)
    _project_T(pt_sc, wT_ref, h)
    kb_ring[B_KEEP:, :] = lax.dot_general(h, wT_ref[T_BK:T_BV, :], _NT,
                                          preferred_element_type=f32).astype(bf16)
    kc_ring[C_KEEP:, :] = lax.dot_general(h, wT_ref[T_CK:T_CV, :], _NT,
                                          preferred_element_type=f32).astype(bf16)
    for g in range(B_KV):
        vbT_ring[g * VB_STRIDE:g * VB_STRIDE + HEAD_DIM, B_KEEP:] = (
            pt_sc[T_BV + g * HEAD_DIM:T_BV + (g + 1) * HEAD_DIM, :].astype(bf16))
    for g in range(C_HEADS // 2):
        vcT_ring[g * VC_STRIDE:g * VC_STRIDE + 2 * HEAD_DIM, C_KEEP:] = (
            pt_sc[T_CV + g * 2 * HEAD_DIM:T_CV + (g + 1) * 2 * HEAD_DIM, :].astype(bf16))

    _gate_rows(pt_sc)

    @pl.when(t == pl.num_programs(1) - 1)
    def _():
        bk_ref[0] = lax.dot_general(wT_ref[T_BK:T_BV, :], h[ts - B_KEEP:, :], _NT, preferred_element_type=f32)
        ck_ref[0] = lax.dot_general(wT_ref[T_CK:T_CV, :], h[ts - C_KEEP:, :], _NT, preferred_element_type=f32)
        bv_ref[0] = pt_sc[T_BV:T_BZ, ts - B_KEEP:]
        cv_ref[0] = pt_sc[T_CV:T_CZ, ts - C_KEEP:]

    vn = _rms_cols(pt_sc[T_AV:T_AV + A_WIDTH, :], ang_ref[...]).astype(bf16)
    n_ac = ts // A_CHUNK
    row = lax.broadcasted_iota(jnp.int32, (A_CHUNK, A_CHUNK), 0)
    col = lax.broadcasted_iota(jnp.int32, (A_CHUNK, A_CHUNK), 1)
    for g in range(A_GROUPS):
        rows = slice(g * A_GD, (g + 1) * A_GD)
        wgt = jnp.where(row <= col, awsT_ref[g], 0.0).astype(bf16)
        lhs = jnp.concatenate([vn[rows, c * A_CHUNK:(c + 1) * A_CHUNK] for c in range(n_ac)], axis=0)
        mix = jnp.dot(lhs, wgt, preferred_element_type=f32) + abs_ref[g]
        for c in range(n_ac):
            cols = slice(c * A_CHUNK, (c + 1) * A_CHUNK)
            u = pt_sc[T_AU + g * A_GD:T_AU + (g + 1) * A_GD, cols]
            z = pt_sc[T_AZ + g * A_GD:T_AZ + (g + 1) * A_GD, cols]
            yT_sc[Y_A + g * A_GD:Y_A + (g + 1) * A_GD, cols] = (
                u * mix[c * A_GD:(c + 1) * A_GD, :] * jax.nn.silu(z)).astype(bf16)

    n_pairs = ts // PAIR

    even, odd = (sb_even, mb_even, sc_even, mc_even), (sb_odd, mb_odd, sc_odd, mc_odd)

    def scores_into(p, bufs):
        sb_ref, mb_ref, sc_ref, mc_ref = bufs
        pg = t * n_pairs + p
        start = p * PAIR if isinstance(p, int) else pl.multiple_of(p * PAIR, PAIR)
        s_b, s_c = _scores(pt_sc, pl.ds(start, PAIR),
                           kb_ring[pl.ds(start, B_WIN), :], biasb_ref.at[jnp.minimum(pg, 1)],
                           kc_ring[pl.ds(start, C_WIN), :], biasc_ref.at[jnp.minimum(pg, C_PREV // 2)])
        for cb, (v, m) in enumerate(s_b):
            sb_ref[cb] = v
            mb_ref[cb] = m
        for cb, (v, m) in enumerate(s_c):
            sc_ref[cb] = v
            mc_ref[cb] = m

    def attend_from(p, bufs):
        sb_ref, mb_ref, sc_ref, mc_ref = bufs
        start = p * PAIR if isinstance(p, int) else pl.multiple_of(p * PAIR, PAIR)
        _attend(pt_sc, yT_sc, pl.ds(start, PAIR),
                [(sb_ref[cb], mb_ref[cb]) for cb in range(B_HEADS // 2)],
                [(sc_ref[cb], mc_ref[cb]) for cb in range(C_HEADS // 2)],
                vbT_ring[:, pl.ds(start, B_WIN)], vcT_ring[:, pl.ds(start, C_WIN)], sink_ref)

    scores_into(0, even)

    def finish_rows(p):
        start = p * PAIR if isinstance(p, int) else pl.multiple_of(p * PAIR, PAIR)
        rows = pl.ds(start, PAIR)
        xo_ref[0, rows, :] = _finish(x_ref[0, rows, :], yT_sc.at[:, rows], wout_ref, gpost_ref)

    assert n_pairs == 4

    def first_pairs(q, carry):
        p = 2 * q
        scores_into(p + 1, odd)
        attend_from(p, even)
        scores_into(p + 2, even)
        finish_rows(p)
        attend_from(p + 1, odd)
        return carry

    def last_pairs(q, carry):
        p = 2 * q
        scores_into(p + 1, odd)
        finish_rows(p - 1)
        attend_from(p, even)
        finish_rows(p)
        attend_from(p + 1, odd)
        return carry

    n_full = jnp.minimum(t + n_pairs, n_pairs // 2 - 1)
    lax.fori_loop(0, n_full, first_pairs, 0)
    lax.fori_loop(n_full, n_full + 1, last_pairs, 0)
    finish_rows(n_pairs - 1)

    kb_ring[0:B_KEEP, :] = kb_ring[ts:ts + B_KEEP, :]
    kc_ring[0:C_KEEP, :] = kc_ring[ts:ts + C_KEEP, :]
    vbT_ring[:, 0:B_KEEP] = vbT_ring[:, ts:ts + B_KEEP]
    vcT_ring[:, 0:C_KEEP] = vcT_ring[:, ts:ts + C_KEEP]


def _prompt_layer(layer, x, p):
    nb, seq, d = x.shape
    ts = SEQ_TILE
    assert seq % ts == 0 and ts == C_KEEP
    kv_spec = lambda keep, width: pl.BlockSpec((1, width, keep), lambda b, t: (b, 0, 0))
    return pl.pallas_call(
        _prompt_kernel,
        grid=(nb, seq // ts),
        in_specs=[
            pl.BlockSpec((1, ts, d), lambda b, t: (b, t, 0)),
            _layer_spec(p["gpre"], layer), _layer_spec(p["gpost"], layer),
            _layer_spec(p["wT"], layer),
            _layer_spec(p["wout"], layer),
            _layer_spec(p["ang"], layer), _layer_spec(p["awsT"], layer), _layer_spec(p["abs"], layer),
            _layer_spec(p["sink"], layer), _const_spec(p["bias_pb"]), _layer_spec(p["bias_pc"], layer),
        ],
        out_specs=[
            pl.BlockSpec((1, ts, d), lambda b, t: (b, t, 0)),
            kv_spec(B_KEEP, B_KVW), kv_spec(B_KEEP, B_KVW), kv_spec(C_KEEP, C_WIDTH), kv_spec(C_KEEP, C_WIDTH),
        ],
        out_shape=[
            jax.ShapeDtypeStruct((nb, seq, d), f32),
            jax.ShapeDtypeStruct((nb, B_KVW, B_KEEP), f32),
            jax.ShapeDtypeStruct((nb, B_KVW, B_KEEP), f32),
            jax.ShapeDtypeStruct((nb, C_WIDTH, C_KEEP), f32),
            jax.ShapeDtypeStruct((nb, C_WIDTH, C_KEEP), f32),
        ],
        scratch_shapes=[
            pltpu.VMEM((T_END, ts), f32),
            pltpu.VMEM((B_KEEP + ts, B_KVW), bf16),
            pltpu.VMEM((C_KEEP + ts, C_WIDTH), bf16),
            pltpu.VMEM((B_KV * VB_STRIDE, B_KEEP + ts), bf16),
            pltpu.VMEM((C_HEADS // 2 * VC_STRIDE, C_KEEP + ts), bf16),
            pltpu.VMEM((A_WIDTH + B_WIDTH + C_WIDTH, ts), bf16),
        ] + 2 * [
            pltpu.VMEM((B_HEADS // 2, B_WIN, 2 * LANE), f32), pltpu.VMEM((B_HEADS // 2, 1, 2 * LANE), f32),
            pltpu.VMEM((C_HEADS // 2, C_WIN, 2 * LANE), f32), pltpu.VMEM((C_HEADS // 2, 1, 2 * LANE), f32),
        ],
        compiler_params=pltpu.CompilerParams(dimension_semantics=("arbitrary", "arbitrary"),
                                             vmem_limit_bytes=VMEM_LIMIT),
        name="prompt_layer",
    )(x, p["gpre"], p["gpost"], p["wT"], p["wout"], p["ang"], p["awsT"],
      p["abs"], p["sink"], p["bias_pb"], p["bias_pc"])


def _sample_kernel(x_ref, cbk_ref, cbv_ref, cck_ref, ccv_ref, gpre_ref, gpost_ref, wT_ref,
                   wout_ref, ang_ref, aws_ref, arep_ref, akeep_ref, abias_ref,
                   sink_ref, biasb_ref, biasc_ref,
                   xo_ref, bk_ref, bv_ref, ck_ref, cv_ref, av_ref,
                   pt_sc, yT_sc):
    ns = SAMPLE_STREAMS_PER_BLOCK
    x = x_ref[...]
    ntok = x.shape[0]
    h = _rms_rows(x, gpre_ref[...]).astype(bf16)
    _project_T(pt_sc, wT_ref, h)
    kvn_b = lax.dot_general(h, wT_ref[T_BK:T_BZ, :], _NT, preferred_element_type=f32)
    kvn_c = lax.dot_general(h, wT_ref[T_CK:T_CZ, :], _NT, preferred_element_type=f32)
    kn_c = kvn_c[:, 0:C_WIDTH]
    bk_ref[...] = kvn_b[:, 0:B_KVW]
    bv_ref[...] = kvn_b[:, B_KVW:]
    ck_ref[...] = kn_c
    cv_ref[...] = kvn_c[:, C_WIDTH:]

    vn = _rms_cols(pt_sc[T_AV:T_AV + A_WIDTH, :], ang_ref[...])
    av_ref[...] = vn.T
    vnb = vn.astype(bf16)
    keep = akeep_ref[...] > 0.0
    rep = arep_ref[...]
    for g in range(A_GROUPS):
        rows = slice(g * A_GD, (g + 1) * A_GD)
        tiled = lax.dot_general(jnp.dot(rep, aws_ref[g].astype(bf16), preferred_element_type=f32).astype(bf16),
                                rep, _NT, preferred_element_type=f32)
        wgt = jnp.where(keep, tiled, 0.0).astype(bf16)
        mix = jnp.dot(vnb[rows, :], wgt, preferred_element_type=f32) + abias_ref[g]
        u = pt_sc[T_AU + g * A_GD:T_AU + (g + 1) * A_GD, :]
        z = pt_sc[T_AZ + g * A_GD:T_AZ + (g + 1) * A_GD, :]
        yT_sc[Y_A + g * A_GD:Y_A + (g + 1) * A_GD, :] = (u * mix * jax.nn.silu(z)).astype(bf16)

    _gate_rows(pt_sc)
    for blk in range(ntok // LANE):
        cols = slice(blk * LANE, (blk + 1) * LANE)
        streams = range(blk * ns, (blk + 1) * ns)
        kwin_b = jnp.concatenate([cbk_ref[s].T.astype(bf16) for s in streams]
                                 + [kvn_b[cols, 0:B_KVW].astype(bf16)], axis=0)
        vwinT_b = jnp.concatenate([cbv_ref[s].astype(bf16) for s in streams]
                                  + [pt_sc[T_BV:T_BV + B_KVW, cols].astype(bf16)], axis=1)
        kwin_c = jnp.concatenate([cck_ref[s].T.astype(bf16) for s in streams]
                                 + [kn_c[cols, :].astype(bf16)], axis=0)
        vwinT_c = jnp.concatenate([ccv_ref[s].astype(bf16) for s in streams]
                                  + [pt_sc[T_CV:T_CV + C_WIDTH, cols].astype(bf16)], axis=1)
        s_b, s_c = _scores(pt_sc, cols, kwin_b, biasb_ref, kwin_c, biasc_ref)
        _attend(pt_sc, yT_sc, cols, s_b, s_c, _with_ones(vwinT_b, HEAD_DIM), _with_ones(vwinT_c, 2 * HEAD_DIM),
                sink_ref)

    xo_ref[...] = _finish(x, yT_sc, wout_ref, gpost_ref)


def _sample_layer(layer, x, caches, p, *, t_new, tok_per_step):
    ntok, d = x.shape
    spb = tok_per_step // t_new
    cbk, cbv, cck, ccv = caches

    def rows(width):
        return pl.BlockSpec((tok_per_step, width), lambda i: (i, 0))

    def cache(c):
        return pl.BlockSpec((None, spb) + c.shape[2:], lambda i: (layer, i, 0, 0))

    return pl.pallas_call(
        _sample_kernel,
        grid=(ntok // tok_per_step,),
        in_specs=[
            rows(d), cache(cbk), cache(cbv), cache(cck), cache(ccv),
            _layer_spec(p["gpre"], layer), _layer_spec(p["gpost"], layer),
            _layer_spec(p["wT"], layer),
            _layer_spec(p["wout"], layer),
            _layer_spec(p["ang"], layer), _layer_spec(p["aws_s"], layer), _const_spec(p["arep"]),
            _const_spec(p["akeep"]), _layer_spec(p["abias_s"], layer),
            _layer_spec(p["sink"], layer), _const_spec(p["bias_sb"]), _layer_spec(p["bias_sc"], layer),
        ],
        out_specs=[rows(d), rows(B_KVW), rows(B_KVW), rows(C_WIDTH), rows(C_WIDTH), rows(A_WIDTH)],
        out_shape=[
            jax.ShapeDtypeStruct((ntok, d), f32),
            jax.ShapeDtypeStruct((ntok, B_KVW), f32), jax.ShapeDtypeStruct((ntok, B_KVW), f32),
            jax.ShapeDtypeStruct((ntok, C_WIDTH), f32), jax.ShapeDtypeStruct((ntok, C_WIDTH), f32),
            jax.ShapeDtypeStruct((ntok, A_WIDTH), f32),
        ],
        scratch_shapes=[
            pltpu.VMEM((T_END, tok_per_step), f32),
            pltpu.VMEM((A_WIDTH + B_WIDTH + C_WIDTH, tok_per_step), bf16),
        ],
        compiler_params=pltpu.CompilerParams(dimension_semantics=("arbitrary",),
                                             vmem_limit_bytes=VMEM_LIMIT),
        name="sample_layer",
    )(x, cbk, cbv, cck, ccv, p["gpre"], p["gpost"], p["wT"], p["wout"],
      p["ang"], p["aws_s"], p["arep"], p["akeep"], p["abias_s"], p["sink"], p["bias_sb"], p["bias_sc"])


def kernel(x_prompt, x_sample, cache_b_k, cache_b_v, cache_c_k, cache_c_v, g_pre, g_post, w_in, w_out,
           a_norm_g, a_ws, a_bs, b_sinks, c_rel_bias, t5_bias):
    depth = w_in.shape[0]
    nb, seq, d = x_prompt.shape
    ns_all, t_new, _ = x_sample.shape
    lc_b, lc_c = cache_b_k.shape[2], cache_c_k.shape[2]
    assert lc_b == B_KEEP and lc_c == C_KEEP and LANE % t_new == 0
    assert SAMPLE_STREAMS_PER_BLOCK * t_new == LANE
    tok_per_step = 2 * LANE
    reps = tok_per_step // t_new

    rel_pb, ok_pb, rel_pc, ok_pc = _prompt_maps()
    rel_sb, ok_sb, rel_sc, ok_sc = _sample_maps(lc_b, lc_c, t_new)
    b_lo = int(min(rel_pb.min(), rel_sb.min()))
    b_hi = int(max(rel_pb.max(), rel_sb.max()))
    t5_rel = t5_bias[_t5_bucket(jnp.arange(b_lo, b_hi + 1))][None]
    c_lo = -C_CLIP - 2 * LANE + 1
    c_hi = int(max(rel_pc.max(), rel_sc.max()))
    assert c_hi < C_CLIP
    c_rel = jnp.concatenate([jnp.broadcast_to(c_rel_bias[:, :1], (depth, -C_CLIP - c_lo, C_HEADS)),
                             c_rel_bias[:, :c_hi + C_CLIP + 1]], axis=1)
    r_ = np.arange(tok_per_step)
    awsT = a_ws.transpose(0, 1, 3, 2)

    p = dict(
        bias_pb=_build_bias(t5_rel, b_lo, rel_pb, ok_pb, kmins=(B_PREV, 0))[0],
        bias_sb=_build_bias(t5_rel, b_lo, rel_sb, ok_sb, kmins=(0,))[0, 0],
        bias_pc=_build_bias(c_rel, c_lo, rel_pc, ok_pc, const_below=-C_CLIP,
                            kmins=tuple(max(C_PREV - 2 * v, 0) for v in range(C_PREV // 2 + 1))),
        bias_sc=_build_bias(c_rel, c_lo, rel_sc, ok_sc, const_below=-C_CLIP, kmins=(0,))[:, 0],
        wT=_transpose_weights(w_in),
        wout=w_out.astype(bf16),
        gpre=g_pre[:, None, :],
        gpost=g_post[:, None, :],
        ang=a_norm_g[:, :, None],
        awsT=awsT,
        abs=a_bs[:, :, None, :],
        sink=jnp.repeat(b_sinks, LANE, axis=1)[:, None, :],
        aws_s=awsT[:, :, :t_new, :t_new],
        arep=jnp.asarray((r_[:, None] % t_new == np.arange(t_new)[None, :]).astype(np.float32), bf16),
        abias_s=jnp.tile(a_bs[:, :, None, :t_new], (1, 1, 1, reps)),
        akeep=jnp.asarray(((r_[:, None] // t_new == r_[None, :] // t_new)
                           & (r_[:, None] % t_new <= r_[None, :] % t_new)).astype(np.float32)),
    )

    xp = x_prompt
    xs = x_sample.reshape(ns_all * t_new, d)
    caches = tuple(c.transpose(0, 1, 3, 4, 2).reshape(depth, ns_all, -1, c.shape[2])
                   for c in (cache_b_k, cache_b_v, cache_c_k, cache_c_v))

    pk, pv, pck, pcv = [], [], [], []
    sk, sv, sck, scv, sav = [], [], [], [], []
    for l in range(depth):
        xp, bk, bv, ck, cv = _prompt_layer(l, xp, p)
        pk.append(bk); pv.append(bv); pck.append(ck); pcv.append(cv)
        xs, bk, bv, ck, cv, av = _sample_layer(l, xs, caches, p, t_new=t_new, tok_per_step=tok_per_step)
        sk.append(bk); sv.append(bv); sck.append(ck); scv.append(cv); sav.append(av)

    def stk(xs_, lead, heads):
        return jnp.stack(xs_).reshape(depth, lead, -1, heads, HEAD_DIM)

    def stk_fm(xs_, heads):
        a = jnp.stack(xs_)
        return a.reshape(depth, nb, heads, HEAD_DIM, a.shape[-1]).transpose(0, 1, 4, 2, 3)

    return (xp, xs.reshape(ns_all, t_new, d),
            stk_fm(pk, B_KV), stk_fm(pv, B_KV), stk_fm(pck, C_HEADS), stk_fm(pcv, C_HEADS),
            stk(sk, ns_all, B_KV), stk(sv, ns_all, B_KV), stk(sck, ns_all, C_HEADS), stk(scv, ns_all, C_HEADS),
            jnp.stack(sav).reshape(depth, ns_all, t_new, A_WIDTH))
```

```python
import functools
import math

import numpy as np
import jax
import jax.numpy as jnp
from jax import lax
from jax.experimental import pallas as pl
from jax.experimental.pallas import tpu as pltpu

bf16 = jnp.bfloat16
f32 = jnp.float32

HEAD_DIM = 64
CHUNK = 64
A_WIDTH, A_GROUPS, A_CHUNK = 256, 4, 128
A_GD = A_WIDTH // A_GROUPS
B_HEADS, B_KV = 8, 2
B_GROUP = B_HEADS // B_KV
B_WIDTH, B_KVW = B_HEADS * HEAD_DIM, B_KV * HEAD_DIM
B_PREV = 2
C_HEADS = 4
C_WIDTH = C_HEADS * HEAD_DIM
C_PREV = 8
C_CLIP = 128
T5_BUCKETS, T5_MAX_DIST = 32, 128
RMS_EPS = 1e-6
NEG_INF = -1e30
QK_SCALE = HEAD_DIM ** -0.5
LOG2E = math.log2(math.e)

LANE = 128
PAIR = 2 * CHUNK
B_WIN = (B_PREV + 2) * CHUNK
C_WIN = (C_PREV + 2) * CHUNK
B_KEEP = B_PREV * CHUNK
C_KEEP = C_PREV * CHUNK
ONES_ROWS = 16
VB_STRIDE = HEAD_DIM + ONES_ROWS
VC_STRIDE = 2 * HEAD_DIM + ONES_ROWS
SEQ_TILE = 512
SAMPLE_STREAMS_PER_BLOCK = 4
VMEM_LIMIT = 56 * 1024 * 1024
TOEPLITZ_UNROLL = 17

_IN_SIZES = [A_WIDTH, A_WIDTH, A_WIDTH, B_WIDTH, B_KVW, B_KVW, B_WIDTH, C_WIDTH, C_WIDTH, C_WIDTH, C_WIDTH]
_IN_OFF = [int(v) for v in np.cumsum([0] + _IN_SIZES)]
(T_AU, T_AV, T_AZ, T_BQ, T_BK, T_BV, T_BZ, T_CQ, T_CK, T_CV, T_CZ, T_END) = _IN_OFF
_T_SECTIONS = ((T_AU, T_BK), (T_BV, T_CK), (T_CV, T_END))
Y_A, Y_B, Y_C = 0, A_WIDTH, A_WIDTH + B_WIDTH

_NT = (((1,), (1,)), ((), ()))
_TN = (((0,), (0,)), ((), ()))


def _t5_bucket(rel):
    half = T5_BUCKETS // 2
    max_exact = half // 2
    ret = jnp.where(rel > 0, half, 0)
    n = jnp.abs(rel)
    nf = jnp.maximum(n, 1).astype(jnp.float32)
    large = max_exact + (jnp.log(nf / max_exact) / math.log(T5_MAX_DIST / max_exact)
                         * (half - max_exact)).astype(jnp.int32)
    large = jnp.minimum(large, half - 1)
    return ret + jnp.where(n < max_exact, n, large)


def _toeplitz_kernel(shift_ref, frev_ref, ok_ref, out_ref, *, n_heads, kmins):
    width = frev_ref.shape[-1]
    neg = jnp.full((CHUNK, LANE), NEG_INF, f32)

    def block(rb, carry):
        r0 = pl.multiple_of(rb * CHUNK, CHUNK)
        ok = ok_ref[pl.ds(r0, CHUNK), :] > 0.0
        rolled = pltpu.roll(frev_ref[...], shift_ref[rb], 1)
        ring = jnp.concatenate([rolled[:, :LANE], rolled[:, width - LANE:]], axis=1)
        for h in range(n_heads):
            x = jnp.broadcast_to(ring[h:h + 1, :], (CHUNK, 2 * LANE))
            y = pltpu.roll(x, 0, 1, stride=1, stride_axis=0)[:, :LANE]
            val = jnp.where(ok, y * LOG2E, NEG_INF)
            for v, kmin in enumerate(kmins):
                out_ref[0, v, pl.ds(r0, CHUNK), h * LANE:(h + 1) * LANE] = (
                    val if kmin == 0 else jnp.where(rb >= kmin, val, neg))
        return carry

    n_rb = ok_ref.shape[0] // CHUNK
    unroll = max(u for u in range(1, TOEPLITZ_UNROLL + 1) if n_rb % u == 0)
    lax.fori_loop(0, n_rb, block, 0, unroll=unroll)


def _build_bias(f_ext, rel_lo, rel, ok, *, kmins, const_below=None):
    n_layers, n_rel, n_heads = f_ext.shape
    n_rows = rel.shape[0]
    width = -(-n_rel // LANE) * LANE
    jj, ll = np.arange(CHUNK)[:, None], np.arange(LANE)[None, :]
    shifts = []
    for rb in range(n_rows // CHUNK):
        r, o = rel[rb * CHUNK:(rb + 1) * CHUNK], ok[rb * CHUNK:(rb + 1) * CHUNK]
        base = rel_lo + LANE - 1
        if o.any():
            bases = (r - jj + ll)[o]
            assert (bases == bases[0]).all(), "block is not Toeplitz on its valid entries"
            base = int(bases[0])
            if base - (LANE - 1) < rel_lo:
                assert const_below is not None and r[o].max() <= const_below
                base = rel_lo + LANE - 1
                assert (base + jj - ll)[o].max() <= const_below
            k = (base + jj - ll - rel_lo)[o]
            assert k.min() >= 0 and k.max() < n_rel
        shifts.append((width - ((n_rel - 1) - (base - rel_lo))) % width)
    frev = jnp.pad(jnp.flip(f_ext, axis=1).transpose(0, 2, 1), ((0, 0), (0, 0), (0, width - n_rel)))
    kern = functools.partial(_toeplitz_kernel, n_heads=n_heads, kmins=tuple(kmins))
    return pl.pallas_call(
        kern,
        grid=(n_layers,),
        in_specs=[pl.BlockSpec(memory_space=pltpu.SMEM),
                  pl.BlockSpec((None, n_heads, width), lambda l: (l, 0, 0)),
                  pl.BlockSpec((n_rows, LANE), lambda l: (0, 0))],
        out_specs=pl.BlockSpec((1, len(kmins), n_rows, n_heads * LANE), lambda l: (l, 0, 0, 0)),
        out_shape=jax.ShapeDtypeStruct((n_layers, len(kmins), n_rows, n_heads * LANE), f32),
        compiler_params=pltpu.CompilerParams(dimension_semantics=("arbitrary",),
                                             vmem_limit_bytes=VMEM_LIMIT),
        name="bias_toeplitz",
    )(jnp.asarray(np.asarray(shifts, np.int32)), frev, jnp.asarray(ok.astype(np.float32)))


def _wT_kernel(w_ref, o_ref):
    o_ref[...] = w_ref[...].T.astype(bf16)


def _transpose_weights(w, cols=4 * LANE):
    n_layers, k, n = w.shape
    return pl.pallas_call(
        _wT_kernel,
        grid=(n_layers, n // cols),
        in_specs=[pl.BlockSpec((None, k, cols), lambda l, c: (l, 0, c))],
        out_specs=pl.BlockSpec((None, cols, k), lambda l, c: (l, c, 0)),
        out_shape=jax.ShapeDtypeStruct((n_layers, n, k), bf16),
        compiler_params=pltpu.CompilerParams(dimension_semantics=("arbitrary", "arbitrary"),
                                             vmem_limit_bytes=VMEM_LIMIT),
        name="weights_T",
    )(w)


def _prompt_maps():
    i = np.arange(PAIR)[None, :]
    qi = i // CHUNK
    jb = np.arange(B_WIN)[:, None]
    rel_b = jb - B_KEEP - i
    ok_b = (jb // CHUNK >= qi) & (jb // CHUNK <= qi + B_PREV)
    jc = np.arange(C_WIN)[:, None]
    rel_c = jc - C_KEEP - i
    ok_c = (jc // CHUNK >= qi) & (jc // CHUNK <= qi + C_PREV)
    return rel_b, ok_b, rel_c, ok_c


def _sample_maps(lc_b, lc_c, t_new):
    ns = SAMPLE_STREAMS_PER_BLOCK
    c = np.arange(LANE)[None, :]
    sq, i = c // t_new, c % t_new

    def one(lc):
        r = np.arange(ns * lc + LANE)[:, None]
        is_new = r >= ns * lc
        sk = np.where(is_new, (r - ns * lc) // t_new, r // lc)
        j = np.where(is_new, lc + (r - ns * lc) % t_new, r % lc)
        return (j - lc - i).astype(np.int32), (sk == sq)

    return one(lc_b) + one(lc_c)


def _rms_rows(x, g):
    ms = jnp.mean(x * x, axis=-1, keepdims=True)
    return x * lax.rsqrt(ms + RMS_EPS) * g


def _rms_cols(xT, g_col):
    ms = jnp.mean(xT * xT, axis=0, keepdims=True)
    return xT * lax.rsqrt(ms + RMS_EPS) * g_col


def _project_T(pt_sc, wT_ref, h):
    for r0, r1 in _T_SECTIONS:
        pt_sc[r0:r1, :] = lax.dot_general(wT_ref[r0:r1, :], h, _NT, preferred_element_type=f32)


def _qblockdiag(qa, qb, lower):
    z = jnp.zeros((HEAD_DIM, LANE), bf16)
    if lower is None:
        return jnp.concatenate([jnp.concatenate([qa, z], axis=1),
                                jnp.concatenate([z, qb], axis=1)], axis=0)
    top = jnp.concatenate([qa, qb], axis=1)
    zz = jnp.zeros((HEAD_DIM, 2 * LANE), bf16)
    return jnp.concatenate([top, zz] if lower == 0 else [zz, top], axis=0)


def _with_ones(vT, group):
    ones = jnp.ones((ONES_ROWS, vT.shape[1]), vT.dtype)
    parts = []
    for r in range(0, vT.shape[0], group):
        parts += [vT[r:r + group, :], ones]
    return jnp.concatenate(parts, axis=0)


def _softmax_pv(s_m, sink, vwin):
    s, m = s_m
    if sink is not None:
        m = jnp.maximum(m, sink)
    e = jnp.exp2(s - m)
    o = jnp.dot(vwin, e.astype(bf16), preferred_element_type=f32)
    nv = vwin.shape[0] - ONES_ROWS
    den = o[nv:nv + 1, :]
    if sink is not None:
        den = den + jnp.exp2(sink - m)
    return o[:nv, :] * (1.0 / den)


def _scores(q_sc, cols, kwin_b, bias_b, kwin_c, bias_c):
    def head_q(r0, hh):
        return q_sc[r0 + hh * HEAD_DIM:r0 + (hh + 1) * HEAD_DIM, cols]

    def with_max(s):
        return s, jnp.max(s, axis=0, keepdims=True)

    s_b, s_c = [], []
    for cb in range(B_HEADS // 2):
        ha, hb = 2 * cb, 2 * cb + 1
        qbd = _qblockdiag(head_q(0, ha), head_q(0, hb), ha // B_GROUP)
        s = jnp.dot(kwin_b, qbd, preferred_element_type=f32)
        s_b.append(with_max(s + bias_b[:, cb * 2 * LANE:(cb + 1) * 2 * LANE]))
    for cb in range(C_HEADS // 2):
        ha, hb = 2 * cb, 2 * cb + 1
        qbd = _qblockdiag(head_q(B_WIDTH, ha), head_q(B_WIDTH, hb), None)
        s = jnp.dot(kwin_c[:, cb * LANE:(cb + 1) * LANE], qbd, preferred_element_type=f32)
        s_c.append(with_max(s + bias_c[:, cb * 2 * LANE:(cb + 1) * 2 * LANE]))
    return s_b, s_c


def _attend(pt_sc, yT_sc, cols, s_b, s_c, vwinT_b, vwinT_c, sink_ref):
    for cb in range(B_HEADS // 2):
        ha, hb = 2 * cb, 2 * cb + 1
        kv = ha // B_GROUP
        sink = sink_ref[:, cb * 2 * LANE:(cb + 1) * 2 * LANE] * LOG2E
        o = _softmax_pv(s_b[cb], sink, vwinT_b[kv * VB_STRIDE:(kv + 1) * VB_STRIDE, :])
        for j, hh in enumerate((ha, hb)):
            gate = pt_sc[T_BZ + hh * HEAD_DIM:T_BZ + (hh + 1) * HEAD_DIM, cols]
            yT_sc[Y_B + hh * HEAD_DIM:Y_B + (hh + 1) * HEAD_DIM, cols] = (
                o[:, j * LANE:(j + 1) * LANE] * gate).astype(bf16)
    for cb in range(C_HEADS // 2):
        ha, hb = 2 * cb, 2 * cb + 1
        o = _softmax_pv(s_c[cb], None, vwinT_c[cb * VC_STRIDE:(cb + 1) * VC_STRIDE, :])
        for j, hh in enumerate((ha, hb)):
            gate = pt_sc[T_CZ + hh * HEAD_DIM:T_CZ + (hh + 1) * HEAD_DIM, cols]
            yT_sc[Y_C + hh * HEAD_DIM:Y_C + (hh + 1) * HEAD_DIM, cols] = (
                o[j * HEAD_DIM:(j + 1) * HEAD_DIM, j * LANE:(j + 1) * LANE] * gate).astype(bf16)


def _prepare_rows(pt_sc, q_sc):
    for r0, r1 in ((T_BZ, T_CQ), (T_CZ, T_END)):
        pt_sc[r0:r1, :] = jax.nn.silu(pt_sc[r0:r1, :])
    q_sc[0:B_WIDTH, :] = (pt_sc[T_BQ:T_BK, :] * (QK_SCALE * LOG2E)).astype(bf16)
    q_sc[B_WIDTH:, :] = (pt_sc[T_CQ:T_CK, :] * (QK_SCALE * LOG2E)).astype(bf16)


def _finish(x, yT_sc, wout_ref, gpost_ref):
    out = lax.dot_general(yT_sc[...], wout_ref[...], _TN, preferred_element_type=f32)
    return x + _rms_rows(out, gpost_ref[...])


def _layer_spec(arr, layer):
    shape = arr.shape[1:]
    return pl.BlockSpec((None,) + shape, lambda *_: (layer,) + (0,) * len(shape),
                        pipeline_mode=pl.Buffered(1))


def _const_spec(arr):
    return pl.BlockSpec(arr.shape, lambda *_: (0,) * arr.ndim, pipeline_mode=pl.Buffered(1))


def _prompt_kernel(x_ref, gpre_ref, gpost_ref, wT_ref, wout_ref,
                   ang_ref, awsT_ref, abs_ref, sink_ref, biasb_ref, biasc_ref,
                   xo_ref, bk_ref, bv_ref, ck_ref, cv_ref,
                   pt_sc, kb_ring, kc_ring, vbT_ring, vcT_ring, yT_sc, q_sc,
                   sb_even, mb_even, sc_even, mc_even, sb_odd, mb_odd, sc_odd, mc_odd):
    t = pl.program_id(1)
    ts = SEQ_TILE

    @pl.when(t == 0)
    def _():
        kb_ring[0:B_KEEP, :] = jnp.zeros((B_KEEP, B_KVW), bf16)
        kc_ring[0:C_KEEP, :] = jnp.zeros((C_KEEP, C_WIDTH), bf16)
        vbT_ring[:, 0:B_KEEP] = jnp.zeros((B_KV * VB_STRIDE, B_KEEP), bf16)
        vcT_ring[:, 0:C_KEEP] = jnp.zeros((C_HEADS // 2 * VC_STRIDE, C_KEEP), bf16)
        for g in range(B_KV):
            vbT_ring[g * VB_STRIDE + HEAD_DIM:(g + 1) * VB_STRIDE, :] = jnp.ones((ONES_ROWS, B_KEEP + ts), bf16)
        for g in range(C_HEADS // 2):
            vcT_ring[g * VC_STRIDE + 2 * HEAD_DIM:(g + 1) * VC_STRIDE, :] = jnp.ones((ONES_ROWS, C_KEEP + ts), bf16)

    x = x_ref[0]
    h = _rms_rows(x, gpre_ref[...]).astype(bf16)
    _project_T(pt_sc, wT_ref, h)
    kb_ring[B_KEEP:, :] = lax.dot_general(h, wT_ref[T_BK:T_BV, :], _NT,
                                          preferred_element_type=f32).astype(bf16)
    kc_ring[C_KEEP:, :] = lax.dot_general(h, wT_ref[T_CK:T_CV, :], _NT,
                                          preferred_element_type=f32).astype(bf16)
    for g in range(B_KV):
        vbT_ring[g * VB_STRIDE:g * VB_STRIDE + HEAD_DIM, B_KEEP:] = (
            pt_sc[T_BV + g * HEAD_DIM:T_BV + (g + 1) * HEAD_DIM, :].astype(bf16))
    for g in range(C_HEADS // 2):
        vcT_ring[g * VC_STRIDE:g * VC_STRIDE + 2 * HEAD_DIM, C_KEEP:] = (
            pt_sc[T_CV + g * 2 * HEAD_DIM:T_CV + (g + 1) * 2 * HEAD_DIM, :].astype(bf16))

    _prepare_rows(pt_sc, q_sc)

    @pl.when(t == pl.num_programs(1) - 1)
    def _():
        bk_ref[0] = lax.dot_general(wT_ref[T_BK:T_BV, :], h[ts - B_KEEP:, :], _NT, preferred_element_type=f32)
        ck_ref[0] = lax.dot_general(wT_ref[T_CK:T_CV, :], h[ts - C_KEEP:, :], _NT, preferred_element_type=f32)
        bv_ref[0] = pt_sc[T_BV:T_BZ, ts - B_KEEP:]
        cv_ref[0] = pt_sc[T_CV:T_CZ, ts - C_KEEP:]

    vn = _rms_cols(pt_sc[T_AV:T_AV + A_WIDTH, :], ang_ref[...]).astype(bf16)
    n_ac = ts // A_CHUNK
    row = lax.broadcasted_iota(jnp.int32, (A_CHUNK, A_CHUNK), 0)
    col = lax.broadcasted_iota(jnp.int32, (A_CHUNK, A_CHUNK), 1)
    for g in range(A_GROUPS):
        rows = slice(g * A_GD, (g + 1) * A_GD)
        wgt = jnp.where(row <= col, awsT_ref[g], 0.0).astype(bf16)
        lhs = jnp.concatenate([vn[rows, c * A_CHUNK:(c + 1) * A_CHUNK] for c in range(n_ac)], axis=0)
        mix = jnp.dot(lhs, wgt, preferred_element_type=f32) + abs_ref[g]
        for c in range(n_ac):
            cols = slice(c * A_CHUNK, (c + 1) * A_CHUNK)
            u = pt_sc[T_AU + g * A_GD:T_AU + (g + 1) * A_GD, cols]
            z = pt_sc[T_AZ + g * A_GD:T_AZ + (g + 1) * A_GD, cols]
            yT_sc[Y_A + g * A_GD:Y_A + (g + 1) * A_GD, cols] = (
                u * mix[c * A_GD:(c + 1) * A_GD, :] * jax.nn.silu(z)).astype(bf16)

    n_pairs = ts // PAIR

    even, odd = (sb_even, mb_even, sc_even, mc_even), (sb_odd, mb_odd, sc_odd, mc_odd)

    def scores_into(p, bufs):
        sb_ref, mb_ref, sc_ref, mc_ref = bufs
        pg = t * n_pairs + p
        start = p * PAIR if isinstance(p, int) else pl.multiple_of(p * PAIR, PAIR)
        s_b, s_c = _scores(q_sc, pl.ds(start, PAIR),
                           kb_ring[pl.ds(start, B_WIN), :], biasb_ref.at[jnp.minimum(pg, 1)],
                           kc_ring[pl.ds(start, C_WIN), :], biasc_ref.at[jnp.minimum(pg, C_PREV // 2)])
        for cb, (v, m) in enumerate(s_b):
            sb_ref[cb] = v
            mb_ref[cb] = m
        for cb, (v, m) in enumerate(s_c):
            sc_ref[cb] = v
            mc_ref[cb] = m

    def attend_from(p, bufs):
        sb_ref, mb_ref, sc_ref, mc_ref = bufs
        start = p * PAIR if isinstance(p, int) else pl.multiple_of(p * PAIR, PAIR)
        _attend(pt_sc, yT_sc, pl.ds(start, PAIR),
                [(sb_ref[cb], mb_ref[cb]) for cb in range(B_HEADS // 2)],
                [(sc_ref[cb], mc_ref[cb]) for cb in range(C_HEADS // 2)],
                vbT_ring[:, pl.ds(start, B_WIN)], vcT_ring[:, pl.ds(start, C_WIN)], sink_ref)

    scores_into(0, even)

    def finish_rows(p):
        start = p * PAIR if isinstance(p, int) else pl.multiple_of(p * PAIR, PAIR)
        rows = pl.ds(start, PAIR)
        xo_ref[0, rows, :] = _finish(x_ref[0, rows, :], yT_sc.at[:, rows], wout_ref, gpost_ref)

    assert n_pairs == 4

    def first_pairs(q, carry):
        p = 2 * q
        scores_into(p + 1, odd)
        attend_from(p, even)
        scores_into(p + 2, even)
        finish_rows(p)
        attend_from(p + 1, odd)
        return carry

    def last_pairs(q, carry):
        p = 2 * q
        scores_into(p + 1, odd)
        finish_rows(p - 1)
        attend_from(p, even)
        finish_rows(p)
        attend_from(p + 1, odd)
        return carry

    n_full = jnp.minimum(t + n_pairs, n_pairs // 2 - 1)
    lax.fori_loop(0, n_full, first_pairs, 0)
    lax.fori_loop(n_full, n_full + 1, last_pairs, 0)
    finish_rows(n_pairs - 1)

    kb_ring[0:B_KEEP, :] = kb_ring[ts:ts + B_KEEP, :]
    kc_ring[0:C_KEEP, :] = kc_ring[ts:ts + C_KEEP, :]
    vbT_ring[:, 0:B_KEEP] = vbT_ring[:, ts:ts + B_KEEP]
    vcT_ring[:, 0:C_KEEP] = vcT_ring[:, ts:ts + C_KEEP]


def _prompt_layer(layer, x, p):
    nb, seq, d = x.shape
    ts = SEQ_TILE
    assert seq % ts == 0 and ts == C_KEEP
    kv_spec = lambda keep, width: pl.BlockSpec((1, width, keep), lambda b, t: (b, 0, 0))
    return pl.pallas_call(
        _prompt_kernel,
        grid=(nb, seq // ts),
        in_specs=[
            pl.BlockSpec((1, ts, d), lambda b, t: (b, t, 0)),
            _layer_spec(p["gpre"], layer), _layer_spec(p["gpost"], layer),
            _layer_spec(p["wT"], layer),
            _layer_spec(p["wout"], layer),
            _layer_spec(p["ang"], layer), _layer_spec(p["awsT"], layer), _layer_spec(p["abs"], layer),
            _layer_spec(p["sink"], layer), _const_spec(p["bias_pb"]), _layer_spec(p["bias_pc"], layer),
        ],
        out_specs=[
            pl.BlockSpec((1, ts, d), lambda b, t: (b, t, 0)),
            kv_spec(B_KEEP, B_KVW), kv_spec(B_KEEP, B_KVW), kv_spec(C_KEEP, C_WIDTH), kv_spec(C_KEEP, C_WIDTH),
        ],
        out_shape=[
            jax.ShapeDtypeStruct((nb, seq, d), f32),
            jax.ShapeDtypeStruct((nb, B_KVW, B_KEEP), f32),
            jax.ShapeDtypeStruct((nb, B_KVW, B_KEEP), f32),
            jax.ShapeDtypeStruct((nb, C_WIDTH, C_KEEP), f32),
            jax.ShapeDtypeStruct((nb, C_WIDTH, C_KEEP), f32),
        ],
        scratch_shapes=[
            pltpu.VMEM((T_END, ts), f32),
            pltpu.VMEM((B_KEEP + ts, B_KVW), bf16),
            pltpu.VMEM((C_KEEP + ts, C_WIDTH), bf16),
            pltpu.VMEM((B_KV * VB_STRIDE, B_KEEP + ts), bf16),
            pltpu.VMEM((C_HEADS // 2 * VC_STRIDE, C_KEEP + ts), bf16),
            pltpu.VMEM((A_WIDTH + B_WIDTH + C_WIDTH, ts), bf16),
            pltpu.VMEM((B_WIDTH + C_WIDTH, ts), bf16),
        ] + 2 * [
            pltpu.VMEM((B_HEADS // 2, B_WIN, 2 * LANE), f32), pltpu.VMEM((B_HEADS // 2, 1, 2 * LANE), f32),
            pltpu.VMEM((C_HEADS // 2, C_WIN, 2 * LANE), f32), pltpu.VMEM((C_HEADS // 2, 1, 2 * LANE), f32),
        ],
        compiler_params=pltpu.CompilerParams(dimension_semantics=("arbitrary", "arbitrary"),
                                             vmem_limit_bytes=VMEM_LIMIT),
        name="prompt_layer",
    )(x, p["gpre"], p["gpost"], p["wT"], p["wout"], p["ang"], p["awsT"],
      p["abs"], p["sink"], p["bias_pb"], p["bias_pc"])


def _sample_kernel(x_ref, cbk_ref, cbv_ref, cck_ref, ccv_ref, gpre_ref, gpost_ref, wT_ref,
                   wout_ref, ang_ref, aws_ref, arep_ref, akeep_ref, abias_ref,
                   sink_ref, biasb_ref, biasc_ref,
                   xo_ref, bk_ref, bv_ref, ck_ref, cv_ref, av_ref,
                   pt_sc, yT_sc, q_sc):
    ns = SAMPLE_STREAMS_PER_BLOCK
    x = x_ref[...]
    ntok = x.shape[0]
    h = _rms_rows(x, gpre_ref[...]).astype(bf16)
    _project_T(pt_sc, wT_ref, h)
    kvn_b = lax.dot_general(h, wT_ref[T_BK:T_BZ, :], _NT, preferred_element_type=f32)
    kvn_c = lax.dot_general(h, wT_ref[T_CK:T_CZ, :], _NT, preferred_element_type=f32)
    kn_c = kvn_c[:, 0:C_WIDTH]
    bk_ref[...] = kvn_b[:, 0:B_KVW]
    bv_ref[...] = kvn_b[:, B_KVW:]
    ck_ref[...] = kn_c
    cv_ref[...] = kvn_c[:, C_WIDTH:]

    vn = _rms_cols(pt_sc[T_AV:T_AV + A_WIDTH, :], ang_ref[...])
    av_ref[...] = vn.T
    vnb = vn.astype(bf16)
    keep = akeep_ref[...] > 0.0
    rep = arep_ref[...]
    for g in range(A_GROUPS):
        rows = slice(g * A_GD, (g + 1) * A_GD)
        tiled = lax.dot_general(jnp.dot(rep, aws_ref[g].astype(bf16), preferred_element_type=f32).astype(bf16),
                                rep, _NT, preferred_element_type=f32)
        wgt = jnp.where(keep, tiled, 0.0).astype(bf16)
        mix = jnp.dot(vnb[rows, :], wgt, preferred_element_type=f32) + abias_ref[g]
        u = pt_sc[T_AU + g * A_GD:T_AU + (g + 1) * A_GD, :]
        z = pt_sc[T_AZ + g * A_GD:T_AZ + (g + 1) * A_GD, :]
        yT_sc[Y_A + g * A_GD:Y_A + (g + 1) * A_GD, :] = (u * mix * jax.nn.silu(z)).astype(bf16)

    _prepare_rows(pt_sc, q_sc)
    for blk in range(ntok // LANE):
        cols = slice(blk * LANE, (blk + 1) * LANE)
        streams = range(blk * ns, (blk + 1) * ns)
        kwin_b = jnp.concatenate([cbk_ref[s].T.astype(bf16) for s in streams]
                                 + [kvn_b[cols, 0:B_KVW].astype(bf16)], axis=0)
        vwinT_b = jnp.concatenate([cbv_ref[s].astype(bf16) for s in streams]
                                  + [pt_sc[T_BV:T_BV + B_KVW, cols].astype(bf16)], axis=1)
        kwin_c = jnp.concatenate([cck_ref[s].T.astype(bf16) for s in streams]
                                 + [kn_c[cols, :].astype(bf16)], axis=0)
        vwinT_c = jnp.concatenate([ccv_ref[s].astype(bf16) for s in streams]
                                  + [pt_sc[T_CV:T_CV + C_WIDTH, cols].astype(bf16)], axis=1)
        s_b, s_c = _scores(q_sc, cols, kwin_b, biasb_ref, kwin_c, biasc_ref)
        _attend(pt_sc, yT_sc, cols, s_b, s_c, _with_ones(vwinT_b, HEAD_DIM), _with_ones(vwinT_c, 2 * HEAD_DIM),
                sink_ref)

    xo_ref[...] = _finish(x, yT_sc, wout_ref, gpost_ref)


def _sample_layer(layer, x, caches, p, *, t_new, tok_per_step):
    ntok, d = x.shape
    spb = tok_per_step // t_new
    cbk, cbv, cck, ccv = caches

    def rows(width):
        return pl.BlockSpec((tok_per_step, width), lambda i: (i, 0))

    def cache(c):
        return pl.BlockSpec((None, spb) + c.shape[2:], lambda i: (layer, i, 0, 0))

    return pl.pallas_call(
        _sample_kernel,
        grid=(ntok // tok_per_step,),
        in_specs=[
            rows(d), cache(cbk), cache(cbv), cache(cck), cache(ccv),
            _layer_spec(p["gpre"], layer), _layer_spec(p["gpost"], layer),
            _layer_spec(p["wT"], layer),
            _layer_spec(p["wout"], layer),
            _layer_spec(p["ang"], layer), _layer_spec(p["aws_s"], layer), _const_spec(p["arep"]),
            _const_spec(p["akeep"]), _layer_spec(p["abias_s"], layer),
            _layer_spec(p["sink"], layer), _const_spec(p["bias_sb"]), _layer_spec(p["bias_sc"], layer),
        ],
        out_specs=[rows(d), rows(B_KVW), rows(B_KVW), rows(C_WIDTH), rows(C_WIDTH), rows(A_WIDTH)],
        out_shape=[
            jax.ShapeDtypeStruct((ntok, d), f32),
            jax.ShapeDtypeStruct((ntok, B_KVW), f32), jax.ShapeDtypeStruct((ntok, B_KVW), f32),
            jax.ShapeDtypeStruct((ntok, C_WIDTH), f32), jax.ShapeDtypeStruct((ntok, C_WIDTH), f32),
            jax.ShapeDtypeStruct((ntok, A_WIDTH), f32),
        ],
        scratch_shapes=[
            pltpu.VMEM((T_END, tok_per_step), f32),
            pltpu.VMEM((A_WIDTH + B_WIDTH + C_WIDTH, tok_per_step), bf16),
            pltpu.VMEM((B_WIDTH + C_WIDTH, tok_per_step), bf16),
        ],
        compiler_params=pltpu.CompilerParams(dimension_semantics=("arbitrary",),
                                             vmem_limit_bytes=VMEM_LIMIT),
        name="sample_layer",
    )(x, cbk, cbv, cck, ccv, p["gpre"], p["gpost"], p["wT"], p["wout"],
      p["ang"], p["aws_s"], p["arep"], p["akeep"], p["abias_s"], p["sink"], p["bias_sb"], p["bias_sc"])


def kernel(x_prompt, x_sample, cache_b_k, cache_b_v, cache_c_k, cache_c_v, g_pre, g_post, w_in, w_out,
           a_norm_g, a_ws, a_bs, b_sinks, c_rel_bias, t5_bias):
    depth = w_in.shape[0]
    nb, seq, d = x_prompt.shape
    ns_all, t_new, _ = x_sample.shape
    lc_b, lc_c = cache_b_k.shape[2], cache_c_k.shape[2]
    assert lc_b == B_KEEP and lc_c == C_KEEP and LANE % t_new == 0
    assert SAMPLE_STREAMS_PER_BLOCK * t_new == LANE
    tok_per_step = 2 * LANE
    reps = tok_per_step // t_new

    rel_pb, ok_pb, rel_pc, ok_pc = _prompt_maps()
    rel_sb, ok_sb, rel_sc, ok_sc = _sample_maps(lc_b, lc_c, t_new)
    b_lo = int(min(rel_pb.min(), rel_sb.min()))
    b_hi = int(max(rel_pb.max(), rel_sb.max()))
    t5_rel = t5_bias[_t5_bucket(jnp.arange(b_lo, b_hi + 1))][None]
    c_lo = -C_CLIP - 2 * LANE + 1
    c_hi = int(max(rel_pc.max(), rel_sc.max()))
    assert c_hi < C_CLIP
    c_rel = jnp.concatenate([jnp.broadcast_to(c_rel_bias[:, :1], (depth, -C_CLIP - c_lo, C_HEADS)),
                             c_rel_bias[:, :c_hi + C_CLIP + 1]], axis=1)
    r_ = np.arange(tok_per_step)
    awsT = a_ws.transpose(0, 1, 3, 2)

    p = dict(
        bias_pb=_build_bias(t5_rel, b_lo, rel_pb, ok_pb, kmins=(B_PREV, 0))[0],
        bias_sb=_build_bias(t5_rel, b_lo, rel_sb, ok_sb, kmins=(0,))[0, 0],
        bias_pc=_build_bias(c_rel, c_lo, rel_pc, ok_pc, const_below=-C_CLIP,
                            kmins=tuple(max(C_PREV - 2 * v, 0) for v in range(C_PREV // 2 + 1))),
        bias_sc=_build_bias(c_rel, c_lo, rel_sc, ok_sc, const_below=-C_CLIP, kmins=(0,))[:, 0],
        wT=_transpose_weights(w_in),
        wout=w_out.astype(bf16),
        gpre=g_pre[:, None, :],
        gpost=g_post[:, None, :],
        ang=a_norm_g[:, :, None],
        awsT=awsT,
        abs=a_bs[:, :, None, :],
        sink=jnp.repeat(b_sinks, LANE, axis=1)[:, None, :],
        aws_s=awsT[:, :, :t_new, :t_new],
        arep=jnp.asarray((r_[:, None] % t_new == np.arange(t_new)[None, :]).astype(np.float32), bf16),
        abias_s=jnp.tile(a_bs[:, :, None, :t_new], (1, 1, 1, reps)),
        akeep=jnp.asarray(((r_[:, None] // t_new == r_[None, :] // t_new)
                           & (r_[:, None] % t_new <= r_[None, :] % t_new)).astype(np.float32)),
    )

    xp = x_prompt
    xs = x_sample.reshape(ns_all * t_new, d)
    caches = tuple(c.transpose(0, 1, 3, 4, 2).reshape(depth, ns_all, -1, c.shape[2])
                   for c in (cache_b_k, cache_b_v, cache_c_k, cache_c_v))

    pk, pv, pck, pcv = [], [], [], []
    sk, sv, sck, scv, sav = [], [], [], [], []
    for l in range(depth):
        xp, bk, bv, ck, cv = _prompt_layer(l, xp, p)
        pk.append(bk); pv.append(bv); pck.append(ck); pcv.append(cv)
        xs, bk, bv, ck, cv, av = _sample_layer(l, xs, caches, p, t_new=t_new, tok_per_step=tok_per_step)
        sk.append(bk); sv.append(bv); sck.append(ck); scv.append(cv); sav.append(av)

    def stk(xs_, lead, heads):
        return jnp.stack(xs_).reshape(depth, lead, -1, heads, HEAD_DIM)

    def stk_fm(xs_, heads):
        a = jnp.stack(xs_)
        return a.reshape(depth, nb, heads, HEAD_DIM, a.shape[-1]).transpose(0, 1, 4, 2, 3)

    return (xp, xs.reshape(ns_all, t_new, d),
            stk_fm(pk, B_KV), stk_fm(pv, B_KV), stk_fm(pck, C_HEADS), stk_fm(pcv, C_HEADS),
            stk(sk, ns_all, B_KV), stk(sv, ns_all, B_KV), stk(sck, ns_all, C_HEADS), stk(scv, ns_all, C_HEADS),
            jnp.stack(sav).reshape(depth, ns_all, t_new, A_WIDTH))
```

```python
import functools
import math

import numpy as np
import jax
import jax.numpy as jnp
from jax import lax
from jax.experimental import pallas as pl
from jax.experimental.pallas import tpu as pltpu

bf16 = jnp.bfloat16
f32 = jnp.float32

HEAD_DIM = 64
CHUNK = 64
A_WIDTH, A_GROUPS, A_CHUNK = 256, 4, 128
A_GD = A_WIDTH // A_GROUPS
B_HEADS, B_KV = 8, 2
B_GROUP = B_HEADS // B_KV
B_WIDTH, B_KVW = B_HEADS * HEAD_DIM, B_KV * HEAD_DIM
B_PREV = 2
C_HEADS = 4
C_WIDTH = C_HEADS * HEAD_DIM
C_PREV = 8
C_CLIP = 128
T5_BUCKETS, T5_MAX_DIST = 32, 128
RMS_EPS = 1e-6
NEG_INF = -1e30
QK_SCALE = HEAD_DIM ** -0.5
LOG2E = math.log2(math.e)

LANE = 128
PAIR = 2 * CHUNK
B_WIN = (B_PREV + 2) * CHUNK
C_WIN = (C_PREV + 2) * CHUNK
B_KEEP = B_PREV * CHUNK
C_KEEP = C_PREV * CHUNK
ONES_ROWS = 16
VB_STRIDE = HEAD_DIM + ONES_ROWS
VC_STRIDE = 2 * HEAD_DIM + ONES_ROWS
SEQ_TILE = 512
SAMPLE_STREAMS_PER_BLOCK = 4
VMEM_LIMIT = 56 * 1024 * 1024
TOEPLITZ_UNROLL = 17

_IN_SIZES = [A_WIDTH, A_WIDTH, A_WIDTH, B_WIDTH, B_KVW, B_KVW, B_WIDTH, C_WIDTH, C_WIDTH, C_WIDTH, C_WIDTH]
_IN_OFF = [int(v) for v in np.cumsum([0] + _IN_SIZES)]
(T_AU, T_AV, T_AZ, T_BQ, T_BK, T_BV, T_BZ, T_CQ, T_CK, T_CV, T_CZ, T_END) = _IN_OFF
_T_SECTIONS = ((T_AU, T_BK), (T_BV, T_CK), (T_CV, T_END))
Y_A, Y_B, Y_C = 0, A_WIDTH, A_WIDTH + B_WIDTH

_NT = (((1,), (1,)), ((), ()))
_TN = (((0,), (0,)), ((), ()))


def _t5_bucket(rel):
    half = T5_BUCKETS // 2
    max_exact = half // 2
    ret = jnp.where(rel > 0, half, 0)
    n = jnp.abs(rel)
    nf = jnp.maximum(n, 1).astype(jnp.float32)
    large = max_exact + (jnp.log(nf / max_exact) / math.log(T5_MAX_DIST / max_exact)
                         * (half - max_exact)).astype(jnp.int32)
    large = jnp.minimum(large, half - 1)
    return ret + jnp.where(n < max_exact, n, large)


def _toeplitz_kernel(rows_ref, shift_ref, frev_ref, ok_ref, out_ref, *, n_heads, kmins, n_toeplitz,
                     n_const, const_lane):
    width = frev_ref.shape[-1]
    neg = jnp.full((CHUNK, LANE), NEG_INF, f32)

    def store(rb, h, val):
        r0 = pl.multiple_of(rb * CHUNK, CHUNK)
        val = jnp.where(ok_ref[pl.ds(r0, CHUNK), :] > 0.0, val, NEG_INF)
        for v, kmin in enumerate(kmins):
            out_ref[0, v, pl.ds(r0, CHUNK), h * LANE:(h + 1) * LANE] = (
                val if kmin == 0 else jnp.where(rb >= kmin, val, neg))

    def toeplitz_block(i, carry):
        rolled = pltpu.roll(frev_ref[...], shift_ref[i], 1)
        ring = jnp.concatenate([rolled[:, :LANE], rolled[:, width - LANE:]], axis=1)
        for h in range(n_heads):
            x = jnp.broadcast_to(ring[h:h + 1, :], (CHUNK, 2 * LANE))
            store(rows_ref[i], h, pltpu.roll(x, 0, 1, stride=1, stride_axis=0)[:, :LANE] * LOG2E)
        return carry

    def const_block(i, carry):
        for h in range(n_heads):
            store(rows_ref[n_toeplitz + i], h, consts[h])
        return carry

    def trips(n):
        return max(u for u in range(1, TOEPLITZ_UNROLL + 1) if n % u == 0)

    if n_toeplitz:
        lax.fori_loop(0, n_toeplitz, toeplitz_block, 0, unroll=trips(n_toeplitz))
    if n_const:
        consts = [jnp.broadcast_to(frev_ref[h:h + 1, const_lane:const_lane + 1], (CHUNK, LANE)) * LOG2E
                  for h in range(n_heads)]
        lax.fori_loop(0, n_const, const_block, 0, unroll=trips(n_const))


def _build_bias(f_ext, rel_lo, rel, ok, *, kmins, const_below=None):
    n_layers, n_rel, n_heads = f_ext.shape
    n_rows = rel.shape[0]
    width = -(-n_rel // LANE) * LANE
    jj, ll = np.arange(CHUNK)[:, None], np.arange(LANE)[None, :]
    toeplitz_rows, shifts, const_rows = [], [], []
    for rb in range(n_rows // CHUNK):
        r, o = rel[rb * CHUNK:(rb + 1) * CHUNK], ok[rb * CHUNK:(rb + 1) * CHUNK]
        if not o.any() or (const_below is not None and r[o].max() <= const_below):
            const_rows.append(rb)
            continue
        bases = (r - jj + ll)[o]
        assert (bases == bases[0]).all(), "block is not Toeplitz on its valid entries"
        base = int(bases[0])
        k = (base + jj - ll - rel_lo)[o]
        assert k.min() >= 0 and k.max() < n_rel
        toeplitz_rows.append(rb)
        shifts.append((width - ((n_rel - 1) - (base - rel_lo))) % width)
    const_lane = 0 if const_below is None else (n_rel - 1) - (const_below - rel_lo)
    frev = jnp.pad(jnp.flip(f_ext, axis=1).transpose(0, 2, 1), ((0, 0), (0, 0), (0, width - n_rel)))
    kern = functools.partial(_toeplitz_kernel, n_heads=n_heads, kmins=tuple(kmins),
                             n_toeplitz=len(toeplitz_rows), n_const=len(const_rows), const_lane=const_lane)
    rows_arg = jnp.asarray(np.asarray(toeplitz_rows + const_rows, np.int32))
    return pl.pallas_call(
        kern,
        grid=(n_layers,),
        in_specs=[pl.BlockSpec(memory_space=pltpu.SMEM),
                  pl.BlockSpec(memory_space=pltpu.SMEM),
                  pl.BlockSpec((None, n_heads, width), lambda l: (l, 0, 0)),
                  pl.BlockSpec((n_rows, LANE), lambda l: (0, 0))],
        out_specs=pl.BlockSpec((1, len(kmins), n_rows, n_heads * LANE), lambda l: (l, 0, 0, 0)),
        out_shape=jax.ShapeDtypeStruct((n_layers, len(kmins), n_rows, n_heads * LANE), f32),
        compiler_params=pltpu.CompilerParams(dimension_semantics=("arbitrary",),
                                             vmem_limit_bytes=VMEM_LIMIT),
        name="bias_toeplitz",
    )(rows_arg, jnp.asarray(np.asarray(shifts or [0], np.int32)), frev, jnp.asarray(ok.astype(np.float32)))


def _wT_kernel(w_ref, o_ref):
    o_ref[...] = w_ref[...].T.astype(bf16)


def _transpose_weights(w, cols=4 * LANE):
    n_layers, k, n = w.shape
    return pl.pallas_call(
        _wT_kernel,
        grid=(n_layers, n // cols),
        in_specs=[pl.BlockSpec((None, k, cols), lambda l, c: (l, 0, c))],
        out_specs=pl.BlockSpec((None, cols, k), lambda l, c: (l, c, 0)),
        out_shape=jax.ShapeDtypeStruct((n_layers, n, k), bf16),
        compiler_params=pltpu.CompilerParams(dimension_semantics=("arbitrary", "arbitrary"),
                                             vmem_limit_bytes=VMEM_LIMIT),
        name="weights_T",
    )(w)


def _prompt_maps():
    i = np.arange(PAIR)[None, :]
    qi = i // CHUNK
    jb = np.arange(B_WIN)[:, None]
    rel_b = jb - B_KEEP - i
    ok_b = (jb // CHUNK >= qi) & (jb // CHUNK <= qi + B_PREV)
    jc = np.arange(C_WIN)[:, None]
    rel_c = jc - C_KEEP - i
    ok_c = (jc // CHUNK >= qi) & (jc // CHUNK <= qi + C_PREV)
    return rel_b, ok_b, rel_c, ok_c


def _sample_maps(lc_b, lc_c, t_new):
    ns = SAMPLE_STREAMS_PER_BLOCK
    c = np.arange(LANE)[None, :]
    sq, i = c // t_new, c % t_new

    def one(lc):
        r = np.arange(ns * lc + LANE)[:, None]
        is_new = r >= ns * lc
        sk = np.where(is_new, (r - ns * lc) // t_new, r // lc)
        j = np.where(is_new, lc + (r - ns * lc) % t_new, r % lc)
        return (j - lc - i).astype(np.int32), (sk == sq)

    return one(lc_b) + one(lc_c)


def _rms_rows(x, g):
    ms = jnp.mean(x * x, axis=-1, keepdims=True)
    return x * lax.rsqrt(ms + RMS_EPS) * g


def _rms_cols(xT, g_col):
    ms = jnp.mean(xT * xT, axis=0, keepdims=True)
    return xT * lax.rsqrt(ms + RMS_EPS) * g_col


def _project_T(pt_sc, wT_ref, h):
    for r0, r1 in _T_SECTIONS:
        pt_sc[r0:r1, :] = lax.dot_general(wT_ref[r0:r1, :], h, _NT, preferred_element_type=f32)


def _qblockdiag(qa, qb, lower):
    z = jnp.zeros((HEAD_DIM, LANE), bf16)
    if lower is None:
        return jnp.concatenate([jnp.concatenate([qa, z], axis=1),
                                jnp.concatenate([z, qb], axis=1)], axis=0)
    top = jnp.concatenate([qa, qb], axis=1)
    zz = jnp.zeros((HEAD_DIM, 2 * LANE), bf16)
    return jnp.concatenate([top, zz] if lower == 0 else [zz, top], axis=0)


def _with_ones(vT, group):
    ones = jnp.ones((ONES_ROWS, vT.shape[1]), vT.dtype)
    parts = []
    for r in range(0, vT.shape[0], group):
        parts += [vT[r:r + group, :], ones]
    return jnp.concatenate(parts, axis=0)


def _softmax_pv(s_m, sink, vwin):
    s, m = s_m
    if sink is not None:
        m = jnp.maximum(m, sink)
    e = jnp.exp2(s - m)
    o = jnp.dot(vwin, e.astype(bf16), preferred_element_type=f32)
    nv = vwin.shape[0] - ONES_ROWS
    den = o[nv:nv + 1, :]
    if sink is not None:
        den = den + jnp.exp2(sink - m)
    return o[:nv, :] * (1.0 / den)


def _scores(q_sc, cols, kwin_b, bias_b, kwin_c, bias_c):
    def head_q(r0, hh):
        return q_sc[r0 + hh * HEAD_DIM:r0 + (hh + 1) * HEAD_DIM, cols]

    def with_max(s):
        return s, jnp.max(s, axis=0, keepdims=True)

    s_b, s_c = [], []
    for cb in range(B_HEADS // 2):
        ha, hb = 2 * cb, 2 * cb + 1
        qbd = _qblockdiag(head_q(0, ha), head_q(0, hb), ha // B_GROUP)
        s = jnp.dot(kwin_b, qbd, preferred_element_type=f32)
        s_b.append(with_max(s + bias_b[:, cb * 2 * LANE:(cb + 1) * 2 * LANE]))
    for cb in range(C_HEADS // 2):
        ha, hb = 2 * cb, 2 * cb + 1
        qbd = _qblockdiag(head_q(B_WIDTH, ha), head_q(B_WIDTH, hb), None)
        s = jnp.dot(kwin_c[:, cb * LANE:(cb + 1) * LANE], qbd, preferred_element_type=f32)
        s_c.append(with_max(s + bias_c[:, cb * 2 * LANE:(cb + 1) * 2 * LANE]))
    return s_b, s_c


def _attend(pt_sc, yT_sc, cols, s_b, s_c, vwinT_b, vwinT_c, sink_ref):
    for cb in range(B_HEADS // 2):
        ha, hb = 2 * cb, 2 * cb + 1
        kv = ha // B_GROUP
        sink = sink_ref[:, cb * 2 * LANE:(cb + 1) * 2 * LANE] * LOG2E
        o = _softmax_pv(s_b[cb], sink, vwinT_b[kv * VB_STRIDE:(kv + 1) * VB_STRIDE, :])
        for j, hh in enumerate((ha, hb)):
            gate = pt_sc[T_BZ + hh * HEAD_DIM:T_BZ + (hh + 1) * HEAD_DIM, cols]
            yT_sc[Y_B + hh * HEAD_DIM:Y_B + (hh + 1) * HEAD_DIM, cols] = (
                o[:, j * LANE:(j + 1) * LANE] * gate).astype(bf16)
    for cb in range(C_HEADS // 2):
        ha, hb = 2 * cb, 2 * cb + 1
        o = _softmax_pv(s_c[cb], None, vwinT_c[cb * VC_STRIDE:(cb + 1) * VC_STRIDE, :])
        for j, hh in enumerate((ha, hb)):
            gate = pt_sc[T_CZ + hh * HEAD_DIM:T_CZ + (hh + 1) * HEAD_DIM, cols]
            yT_sc[Y_C + hh * HEAD_DIM:Y_C + (hh + 1) * HEAD_DIM, cols] = (
                o[j * HEAD_DIM:(j + 1) * HEAD_DIM, j * LANE:(j + 1) * LANE] * gate).astype(bf16)


def _prepare_rows(pt_sc, q_sc):
    for r0, r1 in ((T_BZ, T_CQ), (T_CZ, T_END)):
        pt_sc[r0:r1, :] = jax.nn.silu(pt_sc[r0:r1, :])
    q_sc[0:B_WIDTH, :] = (pt_sc[T_BQ:T_BK, :] * (QK_SCALE * LOG2E)).astype(bf16)
    q_sc[B_WIDTH:, :] = (pt_sc[T_CQ:T_CK, :] * (QK_SCALE * LOG2E)).astype(bf16)


def _finish(x, yT_sc, wout_ref, gpost_ref):
    out = lax.dot_general(yT_sc[...], wout_ref[...], _TN, preferred_element_type=f32)
    return x + _rms_rows(out, gpost_ref[...])


def _layer_spec(arr, layer):
    shape = arr.shape[1:]
    return pl.BlockSpec((None,) + shape, lambda *_: (layer,) + (0,) * len(shape),
                        pipeline_mode=pl.Buffered(1))


def _const_spec(arr):
    return pl.BlockSpec(arr.shape, lambda *_: (0,) * arr.ndim, pipeline_mode=pl.Buffered(1))


def _prompt_kernel(x_ref, gpre_ref, gpost_ref, wT_ref, wout_ref,
                   ang_ref, awsT_ref, abs_ref, sink_ref, biasb_ref, biasc_ref,
                   xo_ref, bk_ref, bv_ref, ck_ref, cv_ref,
                   pt_sc, kb_ring, kc_ring, vbT_ring, vcT_ring, yT_sc, q_sc,
                   sb_even, mb_even, sc_even, mc_even, sb_odd, mb_odd, sc_odd, mc_odd):
    t = pl.program_id(1)
    ts = SEQ_TILE

    @pl.when(t == 0)
    def _():
        kb_ring[0:B_KEEP, :] = jnp.zeros((B_KEEP, B_KVW), bf16)
        kc_ring[0:C_KEEP, :] = jnp.zeros((C_KEEP, C_WIDTH), bf16)
        vbT_ring[:, 0:B_KEEP] = jnp.zeros((B_KV * VB_STRIDE, B_KEEP), bf16)
        vcT_ring[:, 0:C_KEEP] = jnp.zeros((C_HEADS // 2 * VC_STRIDE, C_KEEP), bf16)
        for g in range(B_KV):
            vbT_ring[g * VB_STRIDE + HEAD_DIM:(g + 1) * VB_STRIDE, :] = jnp.ones((ONES_ROWS, B_KEEP + ts), bf16)
        for g in range(C_HEADS // 2):
            vcT_ring[g * VC_STRIDE + 2 * HEAD_DIM:(g + 1) * VC_STRIDE, :] = jnp.ones((ONES_ROWS, C_KEEP + ts), bf16)

    x = x_ref[0]
    h = _rms_rows(x, gpre_ref[...]).astype(bf16)
    _project_T(pt_sc, wT_ref, h)
    kb_ring[B_KEEP:, :] = lax.dot_general(h, wT_ref[T_BK:T_BV, :], _NT,
                                          preferred_element_type=f32).astype(bf16)
    kc_ring[C_KEEP:, :] = lax.dot_general(h, wT_ref[T_CK:T_CV, :], _NT,
                                          preferred_element_type=f32).astype(bf16)
    for g in range(B_KV):
        vbT_ring[g * VB_STRIDE:g * VB_STRIDE + HEAD_DIM, B_KEEP:] = (
            pt_sc[T_BV + g * HEAD_DIM:T_BV + (g + 1) * HEAD_DIM, :].astype(bf16))
    for g in range(C_HEADS // 2):
        vcT_ring[g * VC_STRIDE:g * VC_STRIDE + 2 * HEAD_DIM, C_KEEP:] = (
            pt_sc[T_CV + g * 2 * HEAD_DIM:T_CV + (g + 1) * 2 * HEAD_DIM, :].astype(bf16))

    _prepare_rows(pt_sc, q_sc)

    @pl.when(t == pl.num_programs(1) - 1)
    def _():
        bk_ref[0] = lax.dot_general(wT_ref[T_BK:T_BV, :], h[ts - B_KEEP:, :], _NT, preferred_element_type=f32)
        ck_ref[0] = lax.dot_general(wT_ref[T_CK:T_CV, :], h[ts - C_KEEP:, :], _NT, preferred_element_type=f32)
        bv_ref[0] = pt_sc[T_BV:T_BZ, ts - B_KEEP:]
        cv_ref[0] = pt_sc[T_CV:T_CZ, ts - C_KEEP:]

    vn = _rms_cols(pt_sc[T_AV:T_AV + A_WIDTH, :], ang_ref[...]).astype(bf16)
    n_ac = ts // A_CHUNK
    row = lax.broadcasted_iota(jnp.int32, (A_CHUNK, A_CHUNK), 0)
    col = lax.broadcasted_iota(jnp.int32, (A_CHUNK, A_CHUNK), 1)
    for g in range(A_GROUPS):
        rows = slice(g * A_GD, (g + 1) * A_GD)
        wgt = jnp.where(row <= col, awsT_ref[g], 0.0).astype(bf16)
        lhs = jnp.concatenate([vn[rows, c * A_CHUNK:(c + 1) * A_CHUNK] for c in range(n_ac)], axis=0)
        mix = jnp.dot(lhs, wgt, preferred_element_type=f32) + abs_ref[g]
        for c in range(n_ac):
            cols = slice(c * A_CHUNK, (c + 1) * A_CHUNK)
            u = pt_sc[T_AU + g * A_GD:T_AU + (g + 1) * A_GD, cols]
            z = pt_sc[T_AZ + g * A_GD:T_AZ + (g + 1) * A_GD, cols]
            yT_sc[Y_A + g * A_GD:Y_A + (g + 1) * A_GD, cols] = (
                u * mix[c * A_GD:(c + 1) * A_GD, :] * jax.nn.silu(z)).astype(bf16)

    n_pairs = ts // PAIR

    even, odd = (sb_even, mb_even, sc_even, mc_even), (sb_odd, mb_odd, sc_odd, mc_odd)

    def scores_into(p, bufs):
        sb_ref, mb_ref, sc_ref, mc_ref = bufs
        pg = t * n_pairs + p
        start = p * PAIR if isinstance(p, int) else pl.multiple_of(p * PAIR, PAIR)
        s_b, s_c = _scores(q_sc, pl.ds(start, PAIR),
                           kb_ring[pl.ds(start, B_WIN), :], biasb_ref.at[jnp.minimum(pg, 1)],
                           kc_ring[pl.ds(start, C_WIN), :], biasc_ref.at[jnp.minimum(pg, C_PREV // 2)])
        for cb, (v, m) in enumerate(s_b):
            sb_ref[cb] = v
            mb_ref[cb] = m
        for cb, (v, m) in enumerate(s_c):
            sc_ref[cb] = v
            mc_ref[cb] = m

    def attend_from(p, bufs):
        sb_ref, mb_ref, sc_ref, mc_ref = bufs
        start = p * PAIR if isinstance(p, int) else pl.multiple_of(p * PAIR, PAIR)
        _attend(pt_sc, yT_sc, pl.ds(start, PAIR),
                [(sb_ref[cb], mb_ref[cb]) for cb in range(B_HEADS // 2)],
                [(sc_ref[cb], mc_ref[cb]) for cb in range(C_HEADS // 2)],
                vbT_ring[:, pl.ds(start, B_WIN)], vcT_ring[:, pl.ds(start, C_WIN)], sink_ref)

    scores_into(0, even)

    def finish_rows(p):
        start = p * PAIR if isinstance(p, int) else pl.multiple_of(p * PAIR, PAIR)
        rows = pl.ds(start, PAIR)
        xo_ref[0, rows, :] = _finish(x_ref[0, rows, :], yT_sc.at[:, rows], wout_ref, gpost_ref)

    assert n_pairs == 4

    def first_pairs(q, carry):
        p = 2 * q
        scores_into(p + 1, odd)
        attend_from(p, even)
        scores_into(p + 2, even)
        finish_rows(p)
        attend_from(p + 1, odd)
        return carry

    def last_pairs(q, carry):
        p = 2 * q
        scores_into(p + 1, odd)
        finish_rows(p - 1)
        attend_from(p, even)
        finish_rows(p)
        attend_from(p + 1, odd)
        return carry

    n_full = jnp.minimum(t + n_pairs, n_pairs // 2 - 1)
    lax.fori_loop(0, n_full, first_pairs, 0)
    lax.fori_loop(n_full, n_full + 1, last_pairs, 0)
    finish_rows(n_pairs - 1)

    kb_ring[0:B_KEEP, :] = kb_ring[ts:ts + B_KEEP, :]
    kc_ring[0:C_KEEP, :] = kc_ring[ts:ts + C_KEEP, :]
    vbT_ring[:, 0:B_KEEP] = vbT_ring[:, ts:ts + B_KEEP]
    vcT_ring[:, 0:C_KEEP] = vcT_ring[:, ts:ts + C_KEEP]


def _prompt_layer(layer, x, p):
    nb, seq, d = x.shape
    ts = SEQ_TILE
    assert seq % ts == 0 and ts == C_KEEP
    kv_spec = lambda keep, width: pl.BlockSpec((1, width, keep), lambda b, t: (b, 0, 0))
    return pl.pallas_call(
        _prompt_kernel,
        grid=(nb, seq // ts),
        in_specs=[
            pl.BlockSpec((1, ts, d), lambda b, t: (b, t, 0)),
            _layer_spec(p["gpre"], layer), _layer_spec(p["gpost"], layer),
            _layer_spec(p["wT"], layer),
            _layer_spec(p["wout"], layer),
            _layer_spec(p["ang"], layer), _layer_spec(p["awsT"], layer), _layer_spec(p["abs"], layer),
            _layer_spec(p["sink"], layer), _const_spec(p["bias_pb"]), _layer_spec(p["bias_pc"], layer),
        ],
        out_specs=[
            pl.BlockSpec((1, ts, d), lambda b, t: (b, t, 0)),
            kv_spec(B_KEEP, B_KVW), kv_spec(B_KEEP, B_KVW), kv_spec(C_KEEP, C_WIDTH), kv_spec(C_KEEP, C_WIDTH),
        ],
        out_shape=[
            jax.ShapeDtypeStruct((nb, seq, d), f32),
            jax.ShapeDtypeStruct((nb, B_KVW, B_KEEP), f32),
            jax.ShapeDtypeStruct((nb, B_KVW, B_KEEP), f32),
            jax.ShapeDtypeStruct((nb, C_WIDTH, C_KEEP), f32),
            jax.ShapeDtypeStruct((nb, C_WIDTH, C_KEEP), f32),
        ],
        scratch_shapes=[
            pltpu.VMEM((T_END, ts), f32),
            pltpu.VMEM((B_KEEP + ts, B_KVW), bf16),
            pltpu.VMEM((C_KEEP + ts, C_WIDTH), bf16),
            pltpu.VMEM((B_KV * VB_STRIDE, B_KEEP + ts), bf16),
            pltpu.VMEM((C_HEADS // 2 * VC_STRIDE, C_KEEP + ts), bf16),
            pltpu.VMEM((A_WIDTH + B_WIDTH + C_WIDTH, ts), bf16),
            pltpu.VMEM((B_WIDTH + C_WIDTH, ts), bf16),
        ] + 2 * [
            pltpu.VMEM((B_HEADS // 2, B_WIN, 2 * LANE), f32), pltpu.VMEM((B_HEADS // 2, 1, 2 * LANE), f32),
            pltpu.VMEM((C_HEADS // 2, C_WIN, 2 * LANE), f32), pltpu.VMEM((C_HEADS // 2, 1, 2 * LANE), f32),
        ],
        compiler_params=pltpu.CompilerParams(dimension_semantics=("arbitrary", "arbitrary"),
                                             vmem_limit_bytes=VMEM_LIMIT),
        name="prompt_layer",
    )(x, p["gpre"], p["gpost"], p["wT"], p["wout"], p["ang"], p["awsT"],
      p["abs"], p["sink"], p["bias_pb"], p["bias_pc"])


def _sample_kernel(x_ref, cbk_ref, cbv_ref, cck_ref, ccv_ref, gpre_ref, gpost_ref, wT_ref,
                   wout_ref, ang_ref, aws_ref, arep_ref, akeep_ref, abias_ref,
                   sink_ref, biasb_ref, biasc_ref,
                   xo_ref, bk_ref, bv_ref, ck_ref, cv_ref, av_ref,
                   pt_sc, yT_sc, q_sc):
    ns = SAMPLE_STREAMS_PER_BLOCK
    x = x_ref[...]
    ntok = x.shape[0]
    h = _rms_rows(x, gpre_ref[...]).astype(bf16)
    _project_T(pt_sc, wT_ref, h)
    kvn_b = lax.dot_general(h, wT_ref[T_BK:T_BZ, :], _NT, preferred_element_type=f32)
    kvn_c = lax.dot_general(h, wT_ref[T_CK:T_CZ, :], _NT, preferred_element_type=f32)
    kn_c = kvn_c[:, 0:C_WIDTH]
    bk_ref[...] = kvn_b[:, 0:B_KVW]
    bv_ref[...] = kvn_b[:, B_KVW:]
    ck_ref[...] = kn_c
    cv_ref[...] = kvn_c[:, C_WIDTH:]

    vn = _rms_cols(pt_sc[T_AV:T_AV + A_WIDTH, :], ang_ref[...])
    av_ref[...] = vn.T
    vnb = vn.astype(bf16)
    keep = akeep_ref[...] > 0.0
    rep = arep_ref[...]
    for g in range(A_GROUPS):
        rows = slice(g * A_GD, (g + 1) * A_GD)
        tiled = lax.dot_general(jnp.dot(rep, aws_ref[g].astype(bf16), preferred_element_type=f32).astype(bf16),
                                rep, _NT, preferred_element_type=f32)
        wgt = jnp.where(keep, tiled, 0.0).astype(bf16)
        mix = jnp.dot(vnb[rows, :], wgt, preferred_element_type=f32) + abias_ref[g]
        u = pt_sc[T_AU + g * A_GD:T_AU + (g + 1) * A_GD, :]
        z = pt_sc[T_AZ + g * A_GD:T_AZ + (g + 1) * A_GD, :]
        yT_sc[Y_A + g * A_GD:Y_A + (g + 1) * A_GD, :] = (u * mix * jax.nn.silu(z)).astype(bf16)

    _prepare_rows(pt_sc, q_sc)
    for blk in range(ntok // LANE):
        cols = slice(blk * LANE, (blk + 1) * LANE)
        streams = range(blk * ns, (blk + 1) * ns)
        kwin_b = jnp.concatenate([cbk_ref[s].T.astype(bf16) for s in streams]
                                 + [kvn_b[cols, 0:B_KVW].astype(bf16)], axis=0)
        vwinT_b = jnp.concatenate([cbv_ref[s].astype(bf16) for s in streams]
                                  + [pt_sc[T_BV:T_BV + B_KVW, cols].astype(bf16)], axis=1)
        kwin_c = jnp.concatenate([cck_ref[s].T.astype(bf16) for s in streams]
                                 + [kn_c[cols, :].astype(bf16)], axis=0)
        vwinT_c = jnp.concatenate([ccv_ref[s].astype(bf16) for s in streams]
                                  + [pt_sc[T_CV:T_CV + C_WIDTH, cols].astype(bf16)], axis=1)
        s_b, s_c = _scores(q_sc, cols, kwin_b, biasb_ref, kwin_c, biasc_ref)
        _attend(pt_sc, yT_sc, cols, s_b, s_c, _with_ones(vwinT_b, HEAD_DIM), _with_ones(vwinT_c, 2 * HEAD_DIM),
                sink_ref)

    xo_ref[...] = _finish(x, yT_sc, wout_ref, gpost_ref)


def _sample_layer(layer, x, caches, p, *, t_new, tok_per_step):
    ntok, d = x.shape
    spb = tok_per_step // t_new
    cbk, cbv, cck, ccv = caches

    def rows(width):
        return pl.BlockSpec((tok_per_step, width), lambda i: (i, 0))

    def cache(c):
        return pl.BlockSpec((None, spb) + c.shape[2:], lambda i: (layer, i, 0, 0))

    return pl.pallas_call(
        _sample_kernel,
        grid=(ntok // tok_per_step,),
        in_specs=[
            rows(d), cache(cbk), cache(cbv), cache(cck), cache(ccv),
            _layer_spec(p["gpre"], layer), _layer_spec(p["gpost"], layer),
            _layer_spec(p["wT"], layer),
            _layer_spec(p["wout"], layer),
            _layer_spec(p["ang"], layer), _layer_spec(p["aws_s"], layer), _const_spec(p["arep"]),
            _const_spec(p["akeep"]), _layer_spec(p["abias_s"], layer),
            _layer_spec(p["sink"], layer), _const_spec(p["bias_sb"]), _layer_spec(p["bias_sc"], layer),
        ],
        out_specs=[rows(d), rows(B_KVW), rows(B_KVW), rows(C_WIDTH), rows(C_WIDTH), rows(A_WIDTH)],
        out_shape=[
            jax.ShapeDtypeStruct((ntok, d), f32),
            jax.ShapeDtypeStruct((ntok, B_KVW), f32), jax.ShapeDtypeStruct((ntok, B_KVW), f32),
            jax.ShapeDtypeStruct((ntok, C_WIDTH), f32), jax.ShapeDtypeStruct((ntok, C_WIDTH), f32),
            jax.ShapeDtypeStruct((ntok, A_WIDTH), f32),
        ],
        scratch_shapes=[
            pltpu.VMEM((T_END, tok_per_step), f32),
            pltpu.VMEM((A_WIDTH + B_WIDTH + C_WIDTH, tok_per_step), bf16),
            pltpu.VMEM((B_WIDTH + C_WIDTH, tok_per_step), bf16),
        ],
        compiler_params=pltpu.CompilerParams(dimension_semantics=("arbitrary",),
                                             vmem_limit_bytes=VMEM_LIMIT),
        name="sample_layer",
    )(x, cbk, cbv, cck, ccv, p["gpre"], p["gpost"], p["wT"], p["wout"],
      p["ang"], p["aws_s"], p["arep"], p["akeep"], p["abias_s"], p["sink"], p["bias_sb"], p["bias_sc"])


def kernel(x_prompt, x_sample, cache_b_k, cache_b_v, cache_c_k, cache_c_v, g_pre, g_post, w_in, w_out,
           a_norm_g, a_ws, a_bs, b_sinks, c_rel_bias, t5_bias):
    depth = w_in.shape[0]
    nb, seq, d = x_prompt.shape
    ns_all, t_new, _ = x_sample.shape
    lc_b, lc_c = cache_b_k.shape[2], cache_c_k.shape[2]
    assert lc_b == B_KEEP and lc_c == C_KEEP and LANE % t_new == 0
    assert SAMPLE_STREAMS_PER_BLOCK * t_new == LANE
    tok_per_step = 2 * LANE
    reps = tok_per_step // t_new

    rel_pb, ok_pb, rel_pc, ok_pc = _prompt_maps()
    rel_sb, ok_sb, rel_sc, ok_sc = _sample_maps(lc_b, lc_c, t_new)
    b_lo = int(min(rel_pb.min(), rel_sb.min()))
    b_hi = int(max(rel_pb.max(), rel_sb.max()))
    t5_rel = t5_bias[_t5_bucket(jnp.arange(b_lo, b_hi + 1))][None]
    c_lo = -C_CLIP - LANE + 1
    c_hi = int(max(rel_pc.max(), rel_sc.max()))
    assert c_hi < C_CLIP
    c_rel = jnp.concatenate([jnp.broadcast_to(c_rel_bias[:, :1], (depth, -C_CLIP - c_lo, C_HEADS)),
                             c_rel_bias[:, :c_hi + C_CLIP + 1]], axis=1)
    r_ = np.arange(tok_per_step)
    awsT = a_ws.transpose(0, 1, 3, 2)

    p = dict(
        bias_pb=_build_bias(t5_rel, b_lo, rel_pb, ok_pb, kmins=(B_PREV, 0))[0],
        bias_sb=_build_bias(t5_rel, b_lo, rel_sb, ok_sb, kmins=(0,))[0, 0],
        bias_pc=_build_bias(c_rel, c_lo, rel_pc, ok_pc, const_below=-C_CLIP,
                            kmins=tuple(max(C_PREV - 2 * v, 0) for v in range(C_PREV // 2 + 1))),
        bias_sc=_build_bias(c_rel, c_lo, rel_sc, ok_sc, const_below=-C_CLIP, kmins=(0,))[:, 0],
        wT=_transpose_weights(w_in),
        wout=w_out.astype(bf16),
        gpre=g_pre[:, None, :],
        gpost=g_post[:, None, :],
        ang=a_norm_g[:, :, None],
        awsT=awsT,
        abs=a_bs[:, :, None, :],
        sink=jnp.repeat(b_sinks, LANE, axis=1)[:, None, :],
        aws_s=awsT[:, :, :t_new, :t_new],
        arep=jnp.asarray((r_[:, None] % t_new == np.arange(t_new)[None, :]).astype(np.float32), bf16),
        abias_s=jnp.tile(a_bs[:, :, None, :t_new], (1, 1, 1, reps)),
        akeep=jnp.asarray(((r_[:, None] // t_new == r_[None, :] // t_new)
                           & (r_[:, None] % t_new <= r_[None, :] % t_new)).astype(np.float32)),
    )

    xp = x_prompt
    xs = x_sample.reshape(ns_all * t_new, d)
    caches = tuple(c.transpose(0, 1, 3, 4, 2).reshape(depth, ns_all, -1, c.shape[2])
                   for c in (cache_b_k, cache_b_v, cache_c_k, cache_c_v))

    pk, pv, pck, pcv = [], [], [], []
    sk, sv, sck, scv, sav = [], [], [], [], []
    for l in range(depth):
        xp, bk, bv, ck, cv = _prompt_layer(l, xp, p)
        pk.append(bk); pv.append(bv); pck.append(ck); pcv.append(cv)
        xs, bk, bv, ck, cv, av = _sample_layer(l, xs, caches, p, t_new=t_new, tok_per_step=tok_per_step)
        sk.append(bk); sv.append(bv); sck.append(ck); scv.append(cv); sav.append(av)

    def stk(xs_, lead, heads):
        return jnp.stack(xs_).reshape(depth, lead, -1, heads, HEAD_DIM)

    def stk_fm(xs_, heads):
        a = jnp.stack(xs_)
        return a.reshape(depth, nb, heads, HEAD_DIM, a.shape[-1]).transpose(0, 1, 4, 2, 3)

    return (xp, xs.reshape(ns_all, t_new, d),
            stk_fm(pk, B_KV), stk_fm(pv, B_KV), stk_fm(pck, C_HEADS), stk_fm(pcv, C_HEADS),
            stk(sk, ns_all, B_KV), stk(sv, ns_all, B_KV), stk(sck, ns_all, C_HEADS), stk(scv, ns_all, C_HEADS),
            jnp.stack(sav).reshape(depth, ns_all, t_new, A_WIDTH))
```

```python
import functools
import math

import numpy as np
import jax
import jax.numpy as jnp
from jax import lax
from jax.experimental import pallas as pl
from jax.experimental.pallas import tpu as pltpu

bf16 = jnp.bfloat16
f32 = jnp.float32

HEAD_DIM = 64
CHUNK = 64
A_WIDTH, A_GROUPS, A_CHUNK = 256, 4, 128
A_GD = A_WIDTH // A_GROUPS
B_HEADS, B_KV = 8, 2
B_GROUP = B_HEADS // B_KV
B_WIDTH, B_KVW = B_HEADS * HEAD_DIM, B_KV * HEAD_DIM
B_PREV = 2
C_HEADS = 4
C_WIDTH = C_HEADS * HEAD_DIM
C_PREV = 8
C_CLIP = 128
T5_BUCKETS, T5_MAX_DIST = 32, 128
RMS_EPS = 1e-6
NEG_INF = -1e30
QK_SCALE = HEAD_DIM ** -0.5
LOG2E = math.log2(math.e)

LANE = 128
PAIR = 2 * CHUNK
B_WIN = (B_PREV + 2) * CHUNK
C_WIN = (C_PREV + 2) * CHUNK
B_KEEP = B_PREV * CHUNK
C_KEEP = C_PREV * CHUNK
ONES_ROWS = 16
VB_STRIDE = HEAD_DIM + ONES_ROWS
VC_STRIDE = 2 * HEAD_DIM + ONES_ROWS
SEQ_TILE = 512
SAMPLE_STREAMS_PER_BLOCK = 4
VMEM_LIMIT = 56 * 1024 * 1024
TOEPLITZ_UNROLL = 17

_IN_SIZES = [A_WIDTH, A_WIDTH, A_WIDTH, B_WIDTH, B_KVW, B_KVW, B_WIDTH, C_WIDTH, C_WIDTH, C_WIDTH, C_WIDTH]
_IN_OFF = [int(v) for v in np.cumsum([0] + _IN_SIZES)]
(T_AU, T_AV, T_AZ, T_BQ, T_BK, T_BV, T_BZ, T_CQ, T_CK, T_CV, T_CZ, T_END) = _IN_OFF
_T_SECTIONS = ((T_AU, T_BK), (T_BV, T_CK), (T_CV, T_END))
Y_A, Y_B, Y_C = 0, A_WIDTH, A_WIDTH + B_WIDTH

_NT = (((1,), (1,)), ((), ()))
_TN = (((0,), (0,)), ((), ()))


def _t5_bucket(rel):
    half = T5_BUCKETS // 2
    max_exact = half // 2
    ret = jnp.where(rel > 0, half, 0)
    n = jnp.abs(rel)
    nf = jnp.maximum(n, 1).astype(jnp.float32)
    large = max_exact + (jnp.log(nf / max_exact) / math.log(T5_MAX_DIST / max_exact)
                         * (half - max_exact)).astype(jnp.int32)
    large = jnp.minimum(large, half - 1)
    return ret + jnp.where(n < max_exact, n, large)


def _toeplitz_kernel(rows_ref, shift_ref, frev_ref, ok_ref, out_ref, *, n_heads, kmins, n_toeplitz,
                     n_const, const_lane):
    width = frev_ref.shape[-1]
    neg = jnp.full((CHUNK, LANE), NEG_INF, f32)

    def store(rb, h, val):
        r0 = pl.multiple_of(rb * CHUNK, CHUNK)
        val = jnp.where(ok_ref[pl.ds(r0, CHUNK), :] > 0.0, val, NEG_INF)
        for v, kmin in enumerate(kmins):
            out_ref[0, v, pl.ds(r0, CHUNK), h * LANE:(h + 1) * LANE] = (
                val if kmin == 0 else jnp.where(rb >= kmin, val, neg))

    def toeplitz_block(i, carry):
        rolled = pltpu.roll(frev_ref[...], shift_ref[i], 1)
        ring = jnp.concatenate([rolled[:, :LANE], rolled[:, width - LANE:]], axis=1)
        for h in range(n_heads):
            x = jnp.broadcast_to(ring[h:h + 1, :], (CHUNK, 2 * LANE))
            store(rows_ref[i], h, pltpu.roll(x, 0, 1, stride=1, stride_axis=0)[:, :LANE] * LOG2E)
        return carry

    def const_block(i, carry):
        for h in range(n_heads):
            store(rows_ref[n_toeplitz + i], h, consts[h])
        return carry

    def trips(n):
        return max(u for u in range(1, TOEPLITZ_UNROLL + 1) if n % u == 0)

    if n_toeplitz:
        lax.fori_loop(0, n_toeplitz, toeplitz_block, 0, unroll=trips(n_toeplitz))
    if n_const:
        consts = [jnp.broadcast_to(frev_ref[h:h + 1, const_lane:const_lane + 1], (CHUNK, LANE)) * LOG2E
                  for h in range(n_heads)]
        lax.fori_loop(0, n_const, const_block, 0, unroll=trips(n_const))


def _build_bias(f_ext, rel_lo, rel, ok, *, kmins, const_below=None):
    n_layers, n_rel, n_heads = f_ext.shape
    n_rows = rel.shape[0]
    width = -(-n_rel // LANE) * LANE
    jj, ll = np.arange(CHUNK)[:, None], np.arange(LANE)[None, :]
    toeplitz_rows, shifts, const_rows = [], [], []
    for rb in range(n_rows // CHUNK):
        r, o = rel[rb * CHUNK:(rb + 1) * CHUNK], ok[rb * CHUNK:(rb + 1) * CHUNK]
        if not o.any() or (const_below is not None and r[o].max() <= const_below):
            const_rows.append(rb)
            continue
        bases = (r - jj + ll)[o]
        assert (bases == bases[0]).all(), "block is not Toeplitz on its valid entries"
        base = int(bases[0])
        k = (base + jj - ll - rel_lo)[o]
        assert k.min() >= 0 and k.max() < n_rel
        toeplitz_rows.append(rb)
        shifts.append((width - ((n_rel - 1) - (base - rel_lo))) % width)
    const_lane = 0 if const_below is None else (n_rel - 1) - (const_below - rel_lo)
    frev = jnp.pad(jnp.flip(f_ext, axis=1).transpose(0, 2, 1), ((0, 0), (0, 0), (0, width - n_rel)))
    kern = functools.partial(_toeplitz_kernel, n_heads=n_heads, kmins=tuple(kmins),
                             n_toeplitz=len(toeplitz_rows), n_const=len(const_rows), const_lane=const_lane)
    rows_arg = jnp.asarray(np.asarray(toeplitz_rows + const_rows, np.int32))
    return pl.pallas_call(
        kern,
        grid=(n_layers,),
        in_specs=[pl.BlockSpec(memory_space=pltpu.SMEM),
                  pl.BlockSpec(memory_space=pltpu.SMEM),
                  pl.BlockSpec((None, n_heads, width), lambda l: (l, 0, 0)),
                  pl.BlockSpec((n_rows, LANE), lambda l: (0, 0))],
        out_specs=pl.BlockSpec((1, len(kmins), n_rows, n_heads * LANE), lambda l: (l, 0, 0, 0)),
        out_shape=jax.ShapeDtypeStruct((n_layers, len(kmins), n_rows, n_heads * LANE), f32),
        compiler_params=pltpu.CompilerParams(dimension_semantics=("arbitrary",),
                                             vmem_limit_bytes=VMEM_LIMIT),
        name="bias_toeplitz",
    )(rows_arg, jnp.asarray(np.asarray(shifts or [0], np.int32)), frev, jnp.asarray(ok.astype(np.float32)))


def _wT_kernel(w_ref, o_ref):
    o_ref[...] = w_ref[...].T.astype(bf16)


def _transpose_weights(w, cols=4 * LANE):
    n_layers, k, n = w.shape
    return pl.pallas_call(
        _wT_kernel,
        grid=(n_layers, n // cols),
        in_specs=[pl.BlockSpec((None, k, cols), lambda l, c: (l, 0, c))],
        out_specs=pl.BlockSpec((None, cols, k), lambda l, c: (l, c, 0)),
        out_shape=jax.ShapeDtypeStruct((n_layers, n, k), bf16),
        compiler_params=pltpu.CompilerParams(dimension_semantics=("arbitrary", "arbitrary"),
                                             vmem_limit_bytes=VMEM_LIMIT),
        name="weights_T",
    )(w)


def _prompt_maps():
    i = np.arange(PAIR)[None, :]
    qi = i // CHUNK
    jb = np.arange(B_WIN)[:, None]
    rel_b = jb - B_KEEP - i
    ok_b = (jb // CHUNK >= qi) & (jb // CHUNK <= qi + B_PREV)
    jc = np.arange(C_WIN)[:, None]
    rel_c = jc - C_KEEP - i
    ok_c = (jc // CHUNK >= qi) & (jc // CHUNK <= qi + C_PREV)
    return rel_b, ok_b, rel_c, ok_c


def _sample_maps(lc_b, lc_c, t_new):
    ns = SAMPLE_STREAMS_PER_BLOCK
    c = np.arange(LANE)[None, :]
    sq, i = c // t_new, c % t_new

    def one(lc):
        r = np.arange(ns * lc + LANE)[:, None]
        is_new = r >= ns * lc
        sk = np.where(is_new, (r - ns * lc) // t_new, r // lc)
        j = np.where(is_new, lc + (r - ns * lc) % t_new, r % lc)
        return (j - lc - i).astype(np.int32), (sk == sq)

    return one(lc_b) + one(lc_c)


def _rms_rows(x, g):
    ms = jnp.mean(x * x, axis=-1, keepdims=True)
    return x * lax.rsqrt(ms + RMS_EPS) * g


def _rms_cols(xT, g_col):
    ms = jnp.mean(xT * xT, axis=0, keepdims=True)
    return xT * lax.rsqrt(ms + RMS_EPS) * g_col


def _project_T(pt_sc, wT_ref, h):
    for r0, r1 in _T_SECTIONS:
        pt_sc[r0:r1, :] = lax.dot_general(wT_ref[r0:r1, :], h, _NT, preferred_element_type=f32)


def _qblockdiag(qa, qb, lower):
    z = jnp.zeros((HEAD_DIM, LANE), bf16)
    if lower is None:
        return jnp.concatenate([jnp.concatenate([qa, z], axis=1),
                                jnp.concatenate([z, qb], axis=1)], axis=0)
    top = jnp.concatenate([qa, qb], axis=1)
    zz = jnp.zeros((HEAD_DIM, 2 * LANE), bf16)
    return jnp.concatenate([top, zz] if lower == 0 else [zz, top], axis=0)


def _with_ones(vT, group):
    ones = jnp.ones((ONES_ROWS, vT.shape[1]), vT.dtype)
    parts = []
    for r in range(0, vT.shape[0], group):
        parts += [vT[r:r + group, :], ones]
    return jnp.concatenate(parts, axis=0)


def _softmax_pv(s_m, sink, vwin):
    s, m = s_m
    if sink is not None:
        m = jnp.maximum(m, sink)
    e = jnp.exp2(s - m)
    o = jnp.dot(vwin, e.astype(bf16), preferred_element_type=f32)
    nv = vwin.shape[0] - ONES_ROWS
    den = o[nv:nv + 1, :]
    if sink is not None:
        den = den + jnp.exp2(sink - m)
    return o[:nv, :] * (1.0 / den)


def _scores(q_sc, cols, kwin_b, bias_b, kwin_c, bias_c):
    def head_q(r0, hh):
        return q_sc[r0 + hh * HEAD_DIM:r0 + (hh + 1) * HEAD_DIM, cols]

    def with_max(s):
        return s, jnp.max(s, axis=0, keepdims=True)

    s_b, s_c = [], []
    for cb in range(B_HEADS // 2):
        ha, hb = 2 * cb, 2 * cb + 1
        qbd = _qblockdiag(head_q(0, ha), head_q(0, hb), ha // B_GROUP)
        s = jnp.dot(kwin_b, qbd, preferred_element_type=f32)
        s_b.append(with_max(s + bias_b[:, cb * 2 * LANE:(cb + 1) * 2 * LANE]))
    for cb in range(C_HEADS // 2):
        ha, hb = 2 * cb, 2 * cb + 1
        qbd = _qblockdiag(head_q(B_WIDTH, ha), head_q(B_WIDTH, hb), None)
        s = jnp.dot(kwin_c[:, cb * LANE:(cb + 1) * LANE], qbd, preferred_element_type=f32)
        s_c.append(with_max(s + bias_c[:, cb * 2 * LANE:(cb + 1) * 2 * LANE]))
    return s_b, s_c


def _attend(pt_sc, yT_sc, cols, s_b, s_c, vwinT_b, vwinT_c, sink_ref):
    for cb in range(B_HEADS // 2):
        ha, hb = 2 * cb, 2 * cb + 1
        kv = ha // B_GROUP
        sink = sink_ref[:, cb * 2 * LANE:(cb + 1) * 2 * LANE] * LOG2E
        o = _softmax_pv(s_b[cb], sink, vwinT_b[kv * VB_STRIDE:(kv + 1) * VB_STRIDE, :])
        for j, hh in enumerate((ha, hb)):
            gate = pt_sc[T_BZ + hh * HEAD_DIM:T_BZ + (hh + 1) * HEAD_DIM, cols]
            yT_sc[Y_B + hh * HEAD_DIM:Y_B + (hh + 1) * HEAD_DIM, cols] = (
                o[:, j * LANE:(j + 1) * LANE] * gate).astype(bf16)
    for cb in range(C_HEADS // 2):
        ha, hb = 2 * cb, 2 * cb + 1
        o = _softmax_pv(s_c[cb], None, vwinT_c[cb * VC_STRIDE:(cb + 1) * VC_STRIDE, :])
        for j, hh in enumerate((ha, hb)):
            gate = pt_sc[T_CZ + hh * HEAD_DIM:T_CZ + (hh + 1) * HEAD_DIM, cols]
            yT_sc[Y_C + hh * HEAD_DIM:Y_C + (hh + 1) * HEAD_DIM, cols] = (
                o[j * HEAD_DIM:(j + 1) * HEAD_DIM, j * LANE:(j + 1) * LANE] * gate).astype(bf16)


def _prepare_rows(pt_sc, q_sc):
    for r0, r1 in ((T_BZ, T_CQ), (T_CZ, T_END)):
        pt_sc[r0:r1, :] = jax.nn.silu(pt_sc[r0:r1, :])
    q_sc[0:B_WIDTH, :] = (pt_sc[T_BQ:T_BK, :] * (QK_SCALE * LOG2E)).astype(bf16)
    q_sc[B_WIDTH:, :] = (pt_sc[T_CQ:T_CK, :] * (QK_SCALE * LOG2E)).astype(bf16)


def _finish(x, yT_sc, wout_ref, gpost_ref):
    out = lax.dot_general(yT_sc[...], wout_ref[...], _TN, preferred_element_type=f32)
    return x + _rms_rows(out, gpost_ref[...])


def _layer_spec(arr, layer):
    shape = arr.shape[1:]
    return pl.BlockSpec((None,) + shape, lambda *_: (layer,) + (0,) * len(shape),
                        pipeline_mode=pl.Buffered(1))


def _const_spec(arr):
    return pl.BlockSpec(arr.shape, lambda *_: (0,) * arr.ndim, pipeline_mode=pl.Buffered(1))


def _prompt_kernel(x_ref, gpre_ref, gpost_ref, wT_ref, wout_ref,
                   ang_ref, awsT_ref, abs_ref, sink_ref, biasb_ref, biasc_ref,
                   bk_all, bv_all, ck_all, cv_all,
                   xo_ref, bk_ref, bv_ref, ck_ref, cv_ref,
                   pt_sc, kb_ring, kc_ring, vbT_ring, vcT_ring, yT_sc, q_sc,
                   sb_even, mb_even, sc_even, mc_even, sb_odd, mb_odd, sc_odd, mc_odd):
    t = pl.program_id(1)
    ts = SEQ_TILE

    @pl.when(t == 0)
    def _():
        kb_ring[0:B_KEEP, :] = jnp.zeros((B_KEEP, B_KVW), bf16)
        kc_ring[0:C_KEEP, :] = jnp.zeros((C_KEEP, C_WIDTH), bf16)
        vbT_ring[:, 0:B_KEEP] = jnp.zeros((B_KV * VB_STRIDE, B_KEEP), bf16)
        vcT_ring[:, 0:C_KEEP] = jnp.zeros((C_HEADS // 2 * VC_STRIDE, C_KEEP), bf16)
        for g in range(B_KV):
            vbT_ring[g * VB_STRIDE + HEAD_DIM:(g + 1) * VB_STRIDE, :] = jnp.ones((ONES_ROWS, B_KEEP + ts), bf16)
        for g in range(C_HEADS // 2):
            vcT_ring[g * VC_STRIDE + 2 * HEAD_DIM:(g + 1) * VC_STRIDE, :] = jnp.ones((ONES_ROWS, C_KEEP + ts), bf16)

    x = x_ref[0]
    h = _rms_rows(x, gpre_ref[...]).astype(bf16)
    _project_T(pt_sc, wT_ref, h)
    kb_ring[B_KEEP:, :] = lax.dot_general(h, wT_ref[T_BK:T_BV, :], _NT,
                                          preferred_element_type=f32).astype(bf16)
    kc_ring[C_KEEP:, :] = lax.dot_general(h, wT_ref[T_CK:T_CV, :], _NT,
                                          preferred_element_type=f32).astype(bf16)
    for g in range(B_KV):
        vbT_ring[g * VB_STRIDE:g * VB_STRIDE + HEAD_DIM, B_KEEP:] = (
            pt_sc[T_BV + g * HEAD_DIM:T_BV + (g + 1) * HEAD_DIM, :].astype(bf16))
    for g in range(C_HEADS // 2):
        vcT_ring[g * VC_STRIDE:g * VC_STRIDE + 2 * HEAD_DIM, C_KEEP:] = (
            pt_sc[T_CV + g * 2 * HEAD_DIM:T_CV + (g + 1) * 2 * HEAD_DIM, :].astype(bf16))

    _prepare_rows(pt_sc, q_sc)

    @pl.when(t == pl.num_programs(1) - 1)
    def _():
        bk_ref[0] = lax.dot_general(wT_ref[T_BK:T_BV, :], h[ts - B_KEEP:, :], _NT, preferred_element_type=f32)
        ck_ref[0] = lax.dot_general(wT_ref[T_CK:T_CV, :], h[ts - C_KEEP:, :], _NT, preferred_element_type=f32)
        bv_ref[0] = pt_sc[T_BV:T_BZ, ts - B_KEEP:]
        cv_ref[0] = pt_sc[T_CV:T_CZ, ts - C_KEEP:]

    vn = _rms_cols(pt_sc[T_AV:T_AV + A_WIDTH, :], ang_ref[...]).astype(bf16)
    n_ac = ts // A_CHUNK
    row = lax.broadcasted_iota(jnp.int32, (A_CHUNK, A_CHUNK), 0)
    col = lax.broadcasted_iota(jnp.int32, (A_CHUNK, A_CHUNK), 1)
    for g in range(A_GROUPS):
        rows = slice(g * A_GD, (g + 1) * A_GD)
        wgt = jnp.where(row <= col, awsT_ref[g], 0.0).astype(bf16)
        lhs = jnp.concatenate([vn[rows, c * A_CHUNK:(c + 1) * A_CHUNK] for c in range(n_ac)], axis=0)
        mix = jnp.dot(lhs, wgt, preferred_element_type=f32) + abs_ref[g]
        for c in range(n_ac):
            cols = slice(c * A_CHUNK, (c + 1) * A_CHUNK)
            u = pt_sc[T_AU + g * A_GD:T_AU + (g + 1) * A_GD, cols]
            z = pt_sc[T_AZ + g * A_GD:T_AZ + (g + 1) * A_GD, cols]
            yT_sc[Y_A + g * A_GD:Y_A + (g + 1) * A_GD, cols] = (
                u * mix[c * A_GD:(c + 1) * A_GD, :] * jax.nn.silu(z)).astype(bf16)

    n_pairs = ts // PAIR

    even, odd = (sb_even, mb_even, sc_even, mc_even), (sb_odd, mb_odd, sc_odd, mc_odd)

    def scores_into(p, bufs):
        sb_ref, mb_ref, sc_ref, mc_ref = bufs
        pg = t * n_pairs + p
        start = p * PAIR if isinstance(p, int) else pl.multiple_of(p * PAIR, PAIR)
        s_b, s_c = _scores(q_sc, pl.ds(start, PAIR),
                           kb_ring[pl.ds(start, B_WIN), :], biasb_ref.at[jnp.minimum(pg, 1)],
                           kc_ring[pl.ds(start, C_WIN), :], biasc_ref.at[jnp.minimum(pg, C_PREV // 2)])
        for cb, (v, m) in enumerate(s_b):
            sb_ref[cb] = v
            mb_ref[cb] = m
        for cb, (v, m) in enumerate(s_c):
            sc_ref[cb] = v
            mc_ref[cb] = m

    def attend_from(p, bufs):
        sb_ref, mb_ref, sc_ref, mc_ref = bufs
        start = p * PAIR if isinstance(p, int) else pl.multiple_of(p * PAIR, PAIR)
        _attend(pt_sc, yT_sc, pl.ds(start, PAIR),
                [(sb_ref[cb], mb_ref[cb]) for cb in range(B_HEADS // 2)],
                [(sc_ref[cb], mc_ref[cb]) for cb in range(C_HEADS // 2)],
                vbT_ring[:, pl.ds(start, B_WIN)], vcT_ring[:, pl.ds(start, C_WIN)], sink_ref)

    scores_into(0, even)

    def finish_rows(p):
        start = p * PAIR if isinstance(p, int) else pl.multiple_of(p * PAIR, PAIR)
        rows = pl.ds(start, PAIR)
        xo_ref[0, rows, :] = _finish(x_ref[0, rows, :], yT_sc.at[:, rows], wout_ref, gpost_ref)

    assert n_pairs == 4

    def first_pairs(q, carry):
        p = 2 * q
        scores_into(p + 1, odd)
        attend_from(p, even)
        scores_into(p + 2, even)
        finish_rows(p)
        attend_from(p + 1, odd)
        return carry

    def last_pairs(q, carry):
        p = 2 * q
        scores_into(p + 1, odd)
        finish_rows(p - 1)
        attend_from(p, even)
        finish_rows(p)
        attend_from(p + 1, odd)
        return carry

    n_full = jnp.minimum(t + n_pairs, n_pairs // 2 - 1)
    lax.fori_loop(0, n_full, first_pairs, 0)
    lax.fori_loop(n_full, n_full + 1, last_pairs, 0)
    finish_rows(n_pairs - 1)

    kb_ring[0:B_KEEP, :] = kb_ring[ts:ts + B_KEEP, :]
    kc_ring[0:C_KEEP, :] = kc_ring[ts:ts + C_KEEP, :]
    vbT_ring[:, 0:B_KEEP] = vbT_ring[:, ts:ts + B_KEEP]
    vcT_ring[:, 0:C_KEEP] = vcT_ring[:, ts:ts + C_KEEP]


def _prompt_layer(layer, x, p, kv_all):
    nb, seq, d = x.shape
    ts = SEQ_TILE
    assert seq % ts == 0 and ts == C_KEEP
    kv_spec = lambda keep, width: pl.BlockSpec((None, 1, width, keep), lambda b, t: (layer, b, 0, 0))
    n_in = 11
    return pl.pallas_call(
        _prompt_kernel,
        grid=(nb, seq // ts),
        in_specs=[
            pl.BlockSpec((1, ts, d), lambda b, t: (b, t, 0)),
            _layer_spec(p["gpre"], layer), _layer_spec(p["gpost"], layer),
            _layer_spec(p["wT"], layer),
            _layer_spec(p["wout"], layer),
            _layer_spec(p["ang"], layer), _layer_spec(p["awsT"], layer), _layer_spec(p["abs"], layer),
            _layer_spec(p["sink"], layer), _const_spec(p["bias_pb"]), _layer_spec(p["bias_pc"], layer),
        ] + [pl.BlockSpec(memory_space=pl.ANY)] * len(kv_all),
        out_specs=[
            pl.BlockSpec((1, ts, d), lambda b, t: (b, t, 0)),
            kv_spec(B_KEEP, B_KVW), kv_spec(B_KEEP, B_KVW), kv_spec(C_KEEP, C_WIDTH), kv_spec(C_KEEP, C_WIDTH),
        ],
        out_shape=[jax.ShapeDtypeStruct((nb, seq, d), f32)]
        + [jax.ShapeDtypeStruct(a.shape, a.dtype) for a in kv_all],
        input_output_aliases={n_in + i: 1 + i for i in range(len(kv_all))},
        scratch_shapes=[
            pltpu.VMEM((T_END, ts), f32),
            pltpu.VMEM((B_KEEP + ts, B_KVW), bf16),
            pltpu.VMEM((C_KEEP + ts, C_WIDTH), bf16),
            pltpu.VMEM((B_KV * VB_STRIDE, B_KEEP + ts), bf16),
            pltpu.VMEM((C_HEADS // 2 * VC_STRIDE, C_KEEP + ts), bf16),
            pltpu.VMEM((A_WIDTH + B_WIDTH + C_WIDTH, ts), bf16),
            pltpu.VMEM((B_WIDTH + C_WIDTH, ts), bf16),
        ] + 2 * [
            pltpu.VMEM((B_HEADS // 2, B_WIN, 2 * LANE), f32), pltpu.VMEM((B_HEADS // 2, 1, 2 * LANE), f32),
            pltpu.VMEM((C_HEADS // 2, C_WIN, 2 * LANE), f32), pltpu.VMEM((C_HEADS // 2, 1, 2 * LANE), f32),
        ],
        compiler_params=pltpu.CompilerParams(dimension_semantics=("arbitrary", "arbitrary"),
                                             vmem_limit_bytes=VMEM_LIMIT),
        name="prompt_layer",
    )(x, p["gpre"], p["gpost"], p["wT"], p["wout"], p["ang"], p["awsT"],
      p["abs"], p["sink"], p["bias_pb"], p["bias_pc"], *kv_all)


def _sample_kernel(x_ref, cbk_ref, cbv_ref, cck_ref, ccv_ref, gpre_ref, gpost_ref, wT_ref,
                   wout_ref, ang_ref, aws_ref, arep_ref, akeep_ref, abias_ref,
                   sink_ref, biasb_ref, biasc_ref,
                   xo_ref, bk_ref, bv_ref, ck_ref, cv_ref, av_ref,
                   pt_sc, yT_sc, q_sc):
    ns = SAMPLE_STREAMS_PER_BLOCK
    x = x_ref[...]
    ntok = x.shape[0]
    h = _rms_rows(x, gpre_ref[...]).astype(bf16)
    _project_T(pt_sc, wT_ref, h)
    kvn_b = lax.dot_general(h, wT_ref[T_BK:T_BZ, :], _NT, preferred_element_type=f32)
    kvn_c = lax.dot_general(h, wT_ref[T_CK:T_CZ, :], _NT, preferred_element_type=f32)
    kn_c = kvn_c[:, 0:C_WIDTH]
    bk_ref[...] = kvn_b[:, 0:B_KVW]
    bv_ref[...] = kvn_b[:, B_KVW:]
    ck_ref[...] = kn_c
    cv_ref[...] = kvn_c[:, C_WIDTH:]

    vn = _rms_cols(pt_sc[T_AV:T_AV + A_WIDTH, :], ang_ref[...])
    av_ref[...] = vn.T
    vnb = vn.astype(bf16)
    keep = akeep_ref[...] > 0.0
    rep = arep_ref[...]
    for g in range(A_GROUPS):
        rows = slice(g * A_GD, (g + 1) * A_GD)
        tiled = lax.dot_general(jnp.dot(rep, aws_ref[g].astype(bf16), preferred_element_type=f32).astype(bf16),
                                rep, _NT, preferred_element_type=f32)
        wgt = jnp.where(keep, tiled, 0.0).astype(bf16)
        mix = jnp.dot(vnb[rows, :], wgt, preferred_element_type=f32) + abias_ref[g]
        u = pt_sc[T_AU + g * A_GD:T_AU + (g + 1) * A_GD, :]
        z = pt_sc[T_AZ + g * A_GD:T_AZ + (g + 1) * A_GD, :]
        yT_sc[Y_A + g * A_GD:Y_A + (g + 1) * A_GD, :] = (u * mix * jax.nn.silu(z)).astype(bf16)

    _prepare_rows(pt_sc, q_sc)
    for blk in range(ntok // LANE):
        cols = slice(blk * LANE, (blk + 1) * LANE)
        streams = range(blk * ns, (blk + 1) * ns)
        kwin_b = jnp.concatenate([cbk_ref[s].T.astype(bf16) for s in streams]
                                 + [kvn_b[cols, 0:B_KVW].astype(bf16)], axis=0)
        vwinT_b = jnp.concatenate([cbv_ref[s].astype(bf16) for s in streams]
                                  + [pt_sc[T_BV:T_BV + B_KVW, cols].astype(bf16)], axis=1)
        kwin_c = jnp.concatenate([cck_ref[s].T.astype(bf16) for s in streams]
                                 + [kn_c[cols, :].astype(bf16)], axis=0)
        vwinT_c = jnp.concatenate([ccv_ref[s].astype(bf16) for s in streams]
                                  + [pt_sc[T_CV:T_CV + C_WIDTH, cols].astype(bf16)], axis=1)
        s_b, s_c = _scores(q_sc, cols, kwin_b, biasb_ref, kwin_c, biasc_ref)
        _attend(pt_sc, yT_sc, cols, s_b, s_c, _with_ones(vwinT_b, HEAD_DIM), _with_ones(vwinT_c, 2 * HEAD_DIM),
                sink_ref)

    xo_ref[...] = _finish(x, yT_sc, wout_ref, gpost_ref)


def _sample_layer(layer, x, caches, p, *, t_new, tok_per_step):
    ntok, d = x.shape
    spb = tok_per_step // t_new
    cbk, cbv, cck, ccv = caches

    def rows(width):
        return pl.BlockSpec((tok_per_step, width), lambda i: (i, 0))

    def cache(c):
        return pl.BlockSpec((None, spb) + c.shape[2:], lambda i: (layer, i, 0, 0))

    return pl.pallas_call(
        _sample_kernel,
        grid=(ntok // tok_per_step,),
        in_specs=[
            rows(d), cache(cbk), cache(cbv), cache(cck), cache(ccv),
            _layer_spec(p["gpre"], layer), _layer_spec(p["gpost"], layer),
            _layer_spec(p["wT"], layer),
            _layer_spec(p["wout"], layer),
            _layer_spec(p["ang"], layer), _layer_spec(p["aws_s"], layer), _const_spec(p["arep"]),
            _const_spec(p["akeep"]), _layer_spec(p["abias_s"], layer),
            _layer_spec(p["sink"], layer), _const_spec(p["bias_sb"]), _layer_spec(p["bias_sc"], layer),
        ],
        out_specs=[rows(d), rows(B_KVW), rows(B_KVW), rows(C_WIDTH), rows(C_WIDTH), rows(A_WIDTH)],
        out_shape=[
            jax.ShapeDtypeStruct((ntok, d), f32),
            jax.ShapeDtypeStruct((ntok, B_KVW), f32), jax.ShapeDtypeStruct((ntok, B_KVW), f32),
            jax.ShapeDtypeStruct((ntok, C_WIDTH), f32), jax.ShapeDtypeStruct((ntok, C_WIDTH), f32),
            jax.ShapeDtypeStruct((ntok, A_WIDTH), f32),
        ],
        scratch_shapes=[
            pltpu.VMEM((T_END, tok_per_step), f32),
            pltpu.VMEM((A_WIDTH + B_WIDTH + C_WIDTH, tok_per_step), bf16),
            pltpu.VMEM((B_WIDTH + C_WIDTH, tok_per_step), bf16),
        ],
        compiler_params=pltpu.CompilerParams(dimension_semantics=("arbitrary",),
                                             vmem_limit_bytes=VMEM_LIMIT),
        name="sample_layer",
    )(x, cbk, cbv, cck, ccv, p["gpre"], p["gpost"], p["wT"], p["wout"],
      p["ang"], p["aws_s"], p["arep"], p["akeep"], p["abias_s"], p["sink"], p["bias_sb"], p["bias_sc"])


def kernel(x_prompt, x_sample, cache_b_k, cache_b_v, cache_c_k, cache_c_v, g_pre, g_post, w_in, w_out,
           a_norm_g, a_ws, a_bs, b_sinks, c_rel_bias, t5_bias):
    depth = w_in.shape[0]
    nb, seq, d = x_prompt.shape
    ns_all, t_new, _ = x_sample.shape
    lc_b, lc_c = cache_b_k.shape[2], cache_c_k.shape[2]
    assert lc_b == B_KEEP and lc_c == C_KEEP and LANE % t_new == 0
    assert SAMPLE_STREAMS_PER_BLOCK * t_new == LANE
    tok_per_step = 2 * LANE
    reps = tok_per_step // t_new

    rel_pb, ok_pb, rel_pc, ok_pc = _prompt_maps()
    rel_sb, ok_sb, rel_sc, ok_sc = _sample_maps(lc_b, lc_c, t_new)
    b_lo = int(min(rel_pb.min(), rel_sb.min()))
    b_hi = int(max(rel_pb.max(), rel_sb.max()))
    t5_rel = t5_bias[_t5_bucket(jnp.arange(b_lo, b_hi + 1))][None]
    c_lo = -C_CLIP - LANE + 1
    c_hi = int(max(rel_pc.max(), rel_sc.max()))
    assert c_hi < C_CLIP
    c_rel = jnp.concatenate([jnp.broadcast_to(c_rel_bias[:, :1], (depth, -C_CLIP - c_lo, C_HEADS)),
                             c_rel_bias[:, :c_hi + C_CLIP + 1]], axis=1)
    r_ = np.arange(tok_per_step)
    awsT = a_ws.transpose(0, 1, 3, 2)

    p = dict(
        bias_pb=_build_bias(t5_rel, b_lo, rel_pb, ok_pb, kmins=(B_PREV, 0))[0],
        bias_sb=_build_bias(t5_rel, b_lo, rel_sb, ok_sb, kmins=(0,))[0, 0],
        bias_pc=_build_bias(c_rel, c_lo, rel_pc, ok_pc, const_below=-C_CLIP,
                            kmins=tuple(max(C_PREV - 2 * v, 0) for v in range(C_PREV // 2 + 1))),
        bias_sc=_build_bias(c_rel, c_lo, rel_sc, ok_sc, const_below=-C_CLIP, kmins=(0,))[:, 0],
        wT=_transpose_weights(w_in),
        wout=w_out.astype(bf16),
        gpre=g_pre[:, None, :],
        gpost=g_post[:, None, :],
        ang=a_norm_g[:, :, None],
        awsT=awsT,
        abs=a_bs[:, :, None, :],
        sink=jnp.repeat(b_sinks, LANE, axis=1)[:, None, :],
        aws_s=awsT[:, :, :t_new, :t_new],
        arep=jnp.asarray((r_[:, None] % t_new == np.arange(t_new)[None, :]).astype(np.float32), bf16),
        abias_s=jnp.tile(a_bs[:, :, None, :t_new], (1, 1, 1, reps)),
        akeep=jnp.asarray(((r_[:, None] // t_new == r_[None, :] // t_new)
                           & (r_[:, None] % t_new <= r_[None, :] % t_new)).astype(np.float32)),
    )

    xp = x_prompt
    xs = x_sample.reshape(ns_all * t_new, d)
    caches = tuple(c.transpose(0, 1, 3, 4, 2).reshape(depth, ns_all, -1, c.shape[2])
                   for c in (cache_b_k, cache_b_v, cache_c_k, cache_c_v))

    kv_all = tuple(jnp.zeros((depth, nb, w, keep), f32)
                   for w, keep in ((B_KVW, B_KEEP), (B_KVW, B_KEEP), (C_WIDTH, C_KEEP), (C_WIDTH, C_KEEP)))
    sk, sv, sck, scv, sav = [], [], [], [], []
    for l in range(depth):
        xp, *kv_all = _prompt_layer(l, xp, p, kv_all)
        xs, bk, bv, ck, cv, av = _sample_layer(l, xs, caches, p, t_new=t_new, tok_per_step=tok_per_step)
        sk.append(bk); sv.append(bv); sck.append(ck); scv.append(cv); sav.append(av)

    def stk(xs_, lead, heads):
        return jnp.stack(xs_).reshape(depth, lead, -1, heads, HEAD_DIM)

    def from_fm(a, heads):
        return a.reshape(depth, nb, heads, HEAD_DIM, a.shape[-1]).transpose(0, 1, 4, 2, 3)

    pk, pv, pck, pcv = kv_all
    return (xp, xs.reshape(ns_all, t_new, d),
            from_fm(pk, B_KV), from_fm(pv, B_KV), from_fm(pck, C_HEADS), from_fm(pcv, C_HEADS),
            stk(sk, ns_all, B_KV), stk(sv, ns_all, B_KV), stk(sck, ns_all, C_HEADS), stk(scv, ns_all, C_HEADS),
            jnp.stack(sav).reshape(depth, ns_all, t_new, A_WIDTH))
```

```python
import functools
import math

import numpy as np
import jax
import jax.numpy as jnp
from jax import lax
from jax.experimental import pallas as pl
from jax.experimental.pallas import tpu as pltpu

bf16 = jnp.bfloat16
f32 = jnp.float32

HEAD_DIM = 64
CHUNK = 64
A_WIDTH, A_GROUPS, A_CHUNK = 256, 4, 128
A_GD = A_WIDTH // A_GROUPS
B_HEADS, B_KV = 8, 2
B_GROUP = B_HEADS // B_KV
B_WIDTH, B_KVW = B_HEADS * HEAD_DIM, B_KV * HEAD_DIM
B_PREV = 2
C_HEADS = 4
C_WIDTH = C_HEADS * HEAD_DIM
C_PREV = 8
C_CLIP = 128
T5_BUCKETS, T5_MAX_DIST = 32, 128
RMS_EPS = 1e-6
NEG_INF = -1e30
QK_SCALE = HEAD_DIM ** -0.5
LOG2E = math.log2(math.e)

LANE = 128
PAIR = 2 * CHUNK
B_WIN = (B_PREV + 2) * CHUNK
C_WIN = (C_PREV + 2) * CHUNK
B_KEEP = B_PREV * CHUNK
C_KEEP = C_PREV * CHUNK
ONES_ROWS = 16
VB_STRIDE = HEAD_DIM + ONES_ROWS
VC_STRIDE = 2 * HEAD_DIM + ONES_ROWS
SEQ_TILE = 512
SAMPLE_STREAMS_PER_BLOCK = 4
VMEM_LIMIT = 56 * 1024 * 1024
TOEPLITZ_UNROLL = 17

_IN_SIZES = [A_WIDTH, A_WIDTH, A_WIDTH, B_WIDTH, B_KVW, B_KVW, B_WIDTH, C_WIDTH, C_WIDTH, C_WIDTH, C_WIDTH]
_IN_OFF = [int(v) for v in np.cumsum([0] + _IN_SIZES)]
(T_AU, T_AV, T_AZ, T_BQ, T_BK, T_BV, T_BZ, T_CQ, T_CK, T_CV, T_CZ, T_END) = _IN_OFF
_T_SECTIONS = ((T_AU, T_BK), (T_BV, T_CK), (T_CV, T_END))
Y_A, Y_B, Y_C = 0, A_WIDTH, A_WIDTH + B_WIDTH

_NT = (((1,), (1,)), ((), ()))
_TN = (((0,), (0,)), ((), ()))


def _t5_bucket(rel):
    half = T5_BUCKETS // 2
    max_exact = half // 2
    ret = jnp.where(rel > 0, half, 0)
    n = jnp.abs(rel)
    nf = jnp.maximum(n, 1).astype(jnp.float32)
    large = max_exact + (jnp.log(nf / max_exact) / math.log(T5_MAX_DIST / max_exact)
                         * (half - max_exact)).astype(jnp.int32)
    large = jnp.minimum(large, half - 1)
    return ret + jnp.where(n < max_exact, n, large)


def _toeplitz_kernel(rows_ref, shift_ref, frev_ref, ok_ref, out_ref, *, n_heads, kmins, n_toeplitz,
                     n_const, const_lane):
    width = frev_ref.shape[-1]
    neg = jnp.full((CHUNK, LANE), NEG_INF, f32)

    def store(rb, h, val):
        r0 = pl.multiple_of(rb * CHUNK, CHUNK)
        val = jnp.where(ok_ref[pl.ds(r0, CHUNK), :] > 0.0, val, NEG_INF)
        for v, kmin in enumerate(kmins):
            out_ref[0, v, pl.ds(r0, CHUNK), h * LANE:(h + 1) * LANE] = (
                val if kmin == 0 else jnp.where(rb >= kmin, val, neg))

    def toeplitz_block(i, carry):
        rolled = pltpu.roll(frev_ref[...], shift_ref[i], 1)
        ring = jnp.concatenate([rolled[:, :LANE], rolled[:, width - LANE:]], axis=1)
        for h in range(n_heads):
            x = jnp.broadcast_to(ring[h:h + 1, :], (CHUNK, 2 * LANE))
            store(rows_ref[i], h, pltpu.roll(x, 0, 1, stride=1, stride_axis=0)[:, :LANE] * LOG2E)
        return carry

    def const_block(i, carry):
        for h in range(n_heads):
            store(rows_ref[n_toeplitz + i], h, consts[h])
        return carry

    def trips(n):
        return max(u for u in range(1, TOEPLITZ_UNROLL + 1) if n % u == 0)

    if n_toeplitz:
        lax.fori_loop(0, n_toeplitz, toeplitz_block, 0, unroll=trips(n_toeplitz))
    if n_const:
        consts = [jnp.broadcast_to(frev_ref[h:h + 1, const_lane:const_lane + 1], (CHUNK, LANE)) * LOG2E
                  for h in range(n_heads)]
        lax.fori_loop(0, n_const, const_block, 0, unroll=trips(n_const))


def _build_bias(f_ext, rel_lo, rel, ok, *, kmins, const_below=None):
    n_layers, n_rel, n_heads = f_ext.shape
    n_rows = rel.shape[0]
    width = -(-n_rel // LANE) * LANE
    jj, ll = np.arange(CHUNK)[:, None], np.arange(LANE)[None, :]
    toeplitz_rows, shifts, const_rows = [], [], []
    for rb in range(n_rows // CHUNK):
        r, o = rel[rb * CHUNK:(rb + 1) * CHUNK], ok[rb * CHUNK:(rb + 1) * CHUNK]
        if not o.any() or (const_below is not None and r[o].max() <= const_below):
            const_rows.append(rb)
            continue
        bases = (r - jj + ll)[o]
        assert (bases == bases[0]).all(), "block is not Toeplitz on its valid entries"
        base = int(bases[0])
        k = (base + jj - ll - rel_lo)[o]
        assert k.min() >= 0 and k.max() < n_rel
        toeplitz_rows.append(rb)
        shifts.append((width - ((n_rel - 1) - (base - rel_lo))) % width)
    const_lane = 0 if const_below is None else (n_rel - 1) - (const_below - rel_lo)
    frev = jnp.pad(jnp.flip(f_ext, axis=1).transpose(0, 2, 1), ((0, 0), (0, 0), (0, width - n_rel)))
    kern = functools.partial(_toeplitz_kernel, n_heads=n_heads, kmins=tuple(kmins),
                             n_toeplitz=len(toeplitz_rows), n_const=len(const_rows), const_lane=const_lane)
    rows_arg = jnp.asarray(np.asarray(toeplitz_rows + const_rows, np.int32))
    return pl.pallas_call(
        kern,
        grid=(n_layers,),
        in_specs=[pl.BlockSpec(memory_space=pltpu.SMEM),
                  pl.BlockSpec(memory_space=pltpu.SMEM),
                  pl.BlockSpec((None, n_heads, width), lambda l: (l, 0, 0)),
                  pl.BlockSpec((n_rows, LANE), lambda l: (0, 0))],
        out_specs=pl.BlockSpec((1, len(kmins), n_rows, n_heads * LANE), lambda l: (l, 0, 0, 0)),
        out_shape=jax.ShapeDtypeStruct((n_layers, len(kmins), n_rows, n_heads * LANE), f32),
        compiler_params=pltpu.CompilerParams(dimension_semantics=("arbitrary",),
                                             vmem_limit_bytes=VMEM_LIMIT),
        name="bias_toeplitz",
    )(rows_arg, jnp.asarray(np.asarray(shifts or [0], np.int32)), frev, jnp.asarray(ok.astype(np.float32)))


def _wT_kernel(w_ref, o_ref):
    o_ref[...] = w_ref[...].T.astype(bf16)


def _transpose_weights(w, cols=8 * LANE):
    n_layers, k, n = w.shape
    return pl.pallas_call(
        _wT_kernel,
        grid=(n_layers, n // cols),
        in_specs=[pl.BlockSpec((None, k, cols), lambda l, c: (l, 0, c))],
        out_specs=pl.BlockSpec((None, cols, k), lambda l, c: (l, c, 0)),
        out_shape=jax.ShapeDtypeStruct((n_layers, n, k), bf16),
        compiler_params=pltpu.CompilerParams(dimension_semantics=("arbitrary", "arbitrary"),
                                             vmem_limit_bytes=VMEM_LIMIT),
        name="weights_T",
    )(w)


def _prompt_maps():
    i = np.arange(PAIR)[None, :]
    qi = i // CHUNK
    jb = np.arange(B_WIN)[:, None]
    rel_b = jb - B_KEEP - i
    ok_b = (jb // CHUNK >= qi) & (jb // CHUNK <= qi + B_PREV)
    jc = np.arange(C_WIN)[:, None]
    rel_c = jc - C_KEEP - i
    ok_c = (jc // CHUNK >= qi) & (jc // CHUNK <= qi + C_PREV)
    return rel_b, ok_b, rel_c, ok_c


def _sample_maps(lc_b, lc_c, t_new):
    ns = SAMPLE_STREAMS_PER_BLOCK
    c = np.arange(LANE)[None, :]
    sq, i = c // t_new, c % t_new

    def one(lc):
        r = np.arange(ns * lc + LANE)[:, None]
        is_new = r >= ns * lc
        sk = np.where(is_new, (r - ns * lc) // t_new, r // lc)
        j = np.where(is_new, lc + (r - ns * lc) % t_new, r % lc)
        return (j - lc - i).astype(np.int32), (sk == sq)

    return one(lc_b) + one(lc_c)


def _rms_rows(x, g):
    ms = jnp.mean(x * x, axis=-1, keepdims=True)
    return x * lax.rsqrt(ms + RMS_EPS) * g


def _rms_cols(xT, g_col):
    ms = jnp.mean(xT * xT, axis=0, keepdims=True)
    return xT * lax.rsqrt(ms + RMS_EPS) * g_col


def _project_T(pt_sc, wT_ref, h):
    for r0, r1 in _T_SECTIONS:
        pt_sc[r0:r1, :] = lax.dot_general(wT_ref[r0:r1, :], h, _NT, preferred_element_type=f32)


def _qblockdiag(qa, qb, lower):
    z = jnp.zeros((HEAD_DIM, LANE), bf16)
    if lower is None:
        return jnp.concatenate([jnp.concatenate([qa, z], axis=1),
                                jnp.concatenate([z, qb], axis=1)], axis=0)
    top = jnp.concatenate([qa, qb], axis=1)
    zz = jnp.zeros((HEAD_DIM, 2 * LANE), bf16)
    return jnp.concatenate([top, zz] if lower == 0 else [zz, top], axis=0)


def _with_ones(vT, group):
    ones = jnp.ones((ONES_ROWS, vT.shape[1]), vT.dtype)
    parts = []
    for r in range(0, vT.shape[0], group):
        parts += [vT[r:r + group, :], ones]
    return jnp.concatenate(parts, axis=0)


def _softmax_pv(s_m, sink, vwin):
    s, m = s_m
    if sink is not None:
        m = jnp.maximum(m, sink)
    e = jnp.exp2(s - m)
    o = jnp.dot(vwin, e.astype(bf16), preferred_element_type=f32)
    nv = vwin.shape[0] - ONES_ROWS
    den = o[nv:nv + 1, :]
    if sink is not None:
        den = den + jnp.exp2(sink - m)
    return o[:nv, :] * (1.0 / den)


def _scores(q_sc, cols, kwin_b, bias_b, kwin_c, bias_c):
    def head_q(r0, hh):
        return q_sc[r0 + hh * HEAD_DIM:r0 + (hh + 1) * HEAD_DIM, cols]

    def with_max(s):
        return s, jnp.max(s, axis=0, keepdims=True)

    s_b, s_c = [], []
    for cb in range(B_HEADS // 2):
        ha, hb = 2 * cb, 2 * cb + 1
        qbd = _qblockdiag(head_q(0, ha), head_q(0, hb), ha // B_GROUP)
        s = jnp.dot(kwin_b, qbd, preferred_element_type=f32)
        s_b.append(with_max(s + bias_b[:, cb * 2 * LANE:(cb + 1) * 2 * LANE]))
    for cb in range(C_HEADS // 2):
        ha, hb = 2 * cb, 2 * cb + 1
        qbd = _qblockdiag(head_q(B_WIDTH, ha), head_q(B_WIDTH, hb), None)
        s = jnp.dot(kwin_c[:, cb * LANE:(cb + 1) * LANE], qbd, preferred_element_type=f32)
        s_c.append(with_max(s + bias_c[:, cb * 2 * LANE:(cb + 1) * 2 * LANE]))
    return s_b, s_c


def _attend(pt_sc, yT_sc, cols, s_b, s_c, vwinT_b, vwinT_c, sink_ref):
    for cb in range(B_HEADS // 2):
        ha, hb = 2 * cb, 2 * cb + 1
        kv = ha // B_GROUP
        sink = sink_ref[:, cb * 2 * LANE:(cb + 1) * 2 * LANE] * LOG2E
        o = _softmax_pv(s_b[cb], sink, vwinT_b[kv * VB_STRIDE:(kv + 1) * VB_STRIDE, :])
        for j, hh in enumerate((ha, hb)):
            gate = pt_sc[T_BZ + hh * HEAD_DIM:T_BZ + (hh + 1) * HEAD_DIM, cols]
            yT_sc[Y_B + hh * HEAD_DIM:Y_B + (hh + 1) * HEAD_DIM, cols] = (
                o[:, j * LANE:(j + 1) * LANE] * gate).astype(bf16)
    for cb in range(C_HEADS // 2):
        ha, hb = 2 * cb, 2 * cb + 1
        o = _softmax_pv(s_c[cb], None, vwinT_c[cb * VC_STRIDE:(cb + 1) * VC_STRIDE, :])
        for j, hh in enumerate((ha, hb)):
            gate = pt_sc[T_CZ + hh * HEAD_DIM:T_CZ + (hh + 1) * HEAD_DIM, cols]
            yT_sc[Y_C + hh * HEAD_DIM:Y_C + (hh + 1) * HEAD_DIM, cols] = (
                o[j * HEAD_DIM:(j + 1) * HEAD_DIM, j * LANE:(j + 1) * LANE] * gate).astype(bf16)


def _prepare_rows(pt_sc, q_sc):
    for r0, r1 in ((T_BZ, T_CQ), (T_CZ, T_END)):
        pt_sc[r0:r1, :] = jax.nn.silu(pt_sc[r0:r1, :])
    q_sc[0:B_WIDTH, :] = (pt_sc[T_BQ:T_BK, :] * (QK_SCALE * LOG2E)).astype(bf16)
    q_sc[B_WIDTH:, :] = (pt_sc[T_CQ:T_CK, :] * (QK_SCALE * LOG2E)).astype(bf16)


def _finish(x, yT_sc, wout_ref, gpost_ref):
    out = lax.dot_general(yT_sc[...], wout_ref[...], _TN, preferred_element_type=f32)
    return x + _rms_rows(out, gpost_ref[...])


def _layer_spec(arr, layer):
    shape = arr.shape[1:]
    return pl.BlockSpec((None,) + shape, lambda *_: (layer,) + (0,) * len(shape),
                        pipeline_mode=pl.Buffered(1))


def _const_spec(arr):
    return pl.BlockSpec(arr.shape, lambda *_: (0,) * arr.ndim, pipeline_mode=pl.Buffered(1))


def _prompt_kernel(x_ref, gpre_ref, gpost_ref, wT_ref, wout_ref,
                   ang_ref, awsT_ref, abs_ref, sink_ref, biasb_ref, biasc_ref,
                   bk_all, bv_all, ck_all, cv_all,
                   xo_ref, bk_ref, bv_ref, ck_ref, cv_ref,
                   pt_sc, kb_ring, kc_ring, vbT_ring, vcT_ring, yT_sc, q_sc,
                   sb_even, mb_even, sc_even, mc_even, sb_odd, mb_odd, sc_odd, mc_odd):
    t = pl.program_id(1)
    ts = SEQ_TILE

    @pl.when(t == 0)
    def _():
        kb_ring[0:B_KEEP, :] = jnp.zeros((B_KEEP, B_KVW), bf16)
        kc_ring[0:C_KEEP, :] = jnp.zeros((C_KEEP, C_WIDTH), bf16)
        vbT_ring[:, 0:B_KEEP] = jnp.zeros((B_KV * VB_STRIDE, B_KEEP), bf16)
        vcT_ring[:, 0:C_KEEP] = jnp.zeros((C_HEADS // 2 * VC_STRIDE, C_KEEP), bf16)
        for g in range(B_KV):
            vbT_ring[g * VB_STRIDE + HEAD_DIM:(g + 1) * VB_STRIDE, :] = jnp.ones((ONES_ROWS, B_KEEP + ts), bf16)
        for g in range(C_HEADS // 2):
            vcT_ring[g * VC_STRIDE + 2 * HEAD_DIM:(g + 1) * VC_STRIDE, :] = jnp.ones((ONES_ROWS, C_KEEP + ts), bf16)

    x = x_ref[0]
    h = _rms_rows(x, gpre_ref[...]).astype(bf16)
    _project_T(pt_sc, wT_ref, h)
    kb_ring[B_KEEP:, :] = lax.dot_general(h, wT_ref[T_BK:T_BV, :], _NT,
                                          preferred_element_type=f32).astype(bf16)
    kc_ring[C_KEEP:, :] = lax.dot_general(h, wT_ref[T_CK:T_CV, :], _NT,
                                          preferred_element_type=f32).astype(bf16)
    for g in range(B_KV):
        vbT_ring[g * VB_STRIDE:g * VB_STRIDE + HEAD_DIM, B_KEEP:] = (
            pt_sc[T_BV + g * HEAD_DIM:T_BV + (g + 1) * HEAD_DIM, :].astype(bf16))
    for g in range(C_HEADS // 2):
        vcT_ring[g * VC_STRIDE:g * VC_STRIDE + 2 * HEAD_DIM, C_KEEP:] = (
            pt_sc[T_CV + g * 2 * HEAD_DIM:T_CV + (g + 1) * 2 * HEAD_DIM, :].astype(bf16))

    _prepare_rows(pt_sc, q_sc)

    @pl.when(t == pl.num_programs(1) - 1)
    def _():
        bk_ref[0] = lax.dot_general(wT_ref[T_BK:T_BV, :], h[ts - B_KEEP:, :], _NT, preferred_element_type=f32)
        ck_ref[0] = lax.dot_general(wT_ref[T_CK:T_CV, :], h[ts - C_KEEP:, :], _NT, preferred_element_type=f32)
        bv_ref[0] = pt_sc[T_BV:T_BZ, ts - B_KEEP:]
        cv_ref[0] = pt_sc[T_CV:T_CZ, ts - C_KEEP:]

    vn = _rms_cols(pt_sc[T_AV:T_AV + A_WIDTH, :], ang_ref[...]).astype(bf16)
    n_ac = ts // A_CHUNK
    row = lax.broadcasted_iota(jnp.int32, (A_CHUNK, A_CHUNK), 0)
    col = lax.broadcasted_iota(jnp.int32, (A_CHUNK, A_CHUNK), 1)
    for g in range(A_GROUPS):
        rows = slice(g * A_GD, (g + 1) * A_GD)
        wgt = jnp.where(row <= col, awsT_ref[g], 0.0).astype(bf16)
        lhs = jnp.concatenate([vn[rows, c * A_CHUNK:(c + 1) * A_CHUNK] for c in range(n_ac)], axis=0)
        mix = jnp.dot(lhs, wgt, preferred_element_type=f32) + abs_ref[g]
        for c in range(n_ac):
            cols = slice(c * A_CHUNK, (c + 1) * A_CHUNK)
            u = pt_sc[T_AU + g * A_GD:T_AU + (g + 1) * A_GD, cols]
            z = pt_sc[T_AZ + g * A_GD:T_AZ + (g + 1) * A_GD, cols]
            yT_sc[Y_A + g * A_GD:Y_A + (g + 1) * A_GD, cols] = (
                u * mix[c * A_GD:(c + 1) * A_GD, :] * jax.nn.silu(z)).astype(bf16)

    n_pairs = ts // PAIR

    even, odd = (sb_even, mb_even, sc_even, mc_even), (sb_odd, mb_odd, sc_odd, mc_odd)

    def scores_into(p, bufs):
        sb_ref, mb_ref, sc_ref, mc_ref = bufs
        pg = t * n_pairs + p
        start = p * PAIR if isinstance(p, int) else pl.multiple_of(p * PAIR, PAIR)
        s_b, s_c = _scores(q_sc, pl.ds(start, PAIR),
                           kb_ring[pl.ds(start, B_WIN), :], biasb_ref.at[jnp.minimum(pg, 1)],
                           kc_ring[pl.ds(start, C_WIN), :], biasc_ref.at[jnp.minimum(pg, C_PREV // 2)])
        for cb, (v, m) in enumerate(s_b):
            sb_ref[cb] = v
            mb_ref[cb] = m
        for cb, (v, m) in enumerate(s_c):
            sc_ref[cb] = v
            mc_ref[cb] = m

    def attend_from(p, bufs):
        sb_ref, mb_ref, sc_ref, mc_ref = bufs
        start = p * PAIR if isinstance(p, int) else pl.multiple_of(p * PAIR, PAIR)
        _attend(pt_sc, yT_sc, pl.ds(start, PAIR),
                [(sb_ref[cb], mb_ref[cb]) for cb in range(B_HEADS // 2)],
                [(sc_ref[cb], mc_ref[cb]) for cb in range(C_HEADS // 2)],
                vbT_ring[:, pl.ds(start, B_WIN)], vcT_ring[:, pl.ds(start, C_WIN)], sink_ref)

    scores_into(0, even)

    def finish_rows(p):
        start = p * PAIR if isinstance(p, int) else pl.multiple_of(p * PAIR, PAIR)
        rows = pl.ds(start, PAIR)
        xo_ref[0, rows, :] = _finish(x_ref[0, rows, :], yT_sc.at[:, rows], wout_ref, gpost_ref)

    assert n_pairs == 4

    def first_pairs(q, carry):
        p = 2 * q
        scores_into(p + 1, odd)
        attend_from(p, even)
        scores_into(p + 2, even)
        finish_rows(p)
        attend_from(p + 1, odd)
        return carry

    def last_pairs(q, carry):
        p = 2 * q
        scores_into(p + 1, odd)
        finish_rows(p - 1)
        attend_from(p, even)
        finish_rows(p)
        attend_from(p + 1, odd)
        return carry

    n_full = jnp.minimum(t + n_pairs, n_pairs // 2 - 1)
    lax.fori_loop(0, n_full, first_pairs, 0)
    lax.fori_loop(n_full, n_full + 1, last_pairs, 0)
    finish_rows(n_pairs - 1)

    kb_ring[0:B_KEEP, :] = kb_ring[ts:ts + B_KEEP, :]
    kc_ring[0:C_KEEP, :] = kc_ring[ts:ts + C_KEEP, :]
    vbT_ring[:, 0:B_KEEP] = vbT_ring[:, ts:ts + B_KEEP]
    vcT_ring[:, 0:C_KEEP] = vcT_ring[:, ts:ts + C_KEEP]


def _prompt_layer(layer, x, p, kv_all):
    nb, seq, d = x.shape
    ts = SEQ_TILE
    assert seq % ts == 0 and ts == C_KEEP
    kv_spec = lambda keep, width: pl.BlockSpec((None, 1, width, keep), lambda b, t: (layer, b, 0, 0))
    n_in = 11
    return pl.pallas_call(
        _prompt_kernel,
        grid=(nb, seq // ts),
        in_specs=[
            pl.BlockSpec((1, ts, d), lambda b, t: (b, t, 0)),
            _layer_spec(p["gpre"], layer), _layer_spec(p["gpost"], layer),
            _layer_spec(p["wT"], layer),
            _layer_spec(p["wout"], layer),
            _layer_spec(p["ang"], layer), _layer_spec(p["awsT"], layer), _layer_spec(p["abs"], layer),
            _layer_spec(p["sink"], layer), _const_spec(p["bias_pb"]), _layer_spec(p["bias_pc"], layer),
        ] + [pl.BlockSpec(memory_space=pl.ANY)] * len(kv_all),
        out_specs=[
            pl.BlockSpec((1, ts, d), lambda b, t: (b, t, 0)),
            kv_spec(B_KEEP, B_KVW), kv_spec(B_KEEP, B_KVW), kv_spec(C_KEEP, C_WIDTH), kv_spec(C_KEEP, C_WIDTH),
        ],
        out_shape=[jax.ShapeDtypeStruct((nb, seq, d), f32)]
        + [jax.ShapeDtypeStruct(a.shape, a.dtype) for a in kv_all],
        input_output_aliases={n_in + i: 1 + i for i in range(len(kv_all))},
        scratch_shapes=[
            pltpu.VMEM((T_END, ts), f32),
            pltpu.VMEM((B_KEEP + ts, B_KVW), bf16),
            pltpu.VMEM((C_KEEP + ts, C_WIDTH), bf16),
            pltpu.VMEM((B_KV * VB_STRIDE, B_KEEP + ts), bf16),
            pltpu.VMEM((C_HEADS // 2 * VC_STRIDE, C_KEEP + ts), bf16),
            pltpu.VMEM((A_WIDTH + B_WIDTH + C_WIDTH, ts), bf16),
            pltpu.VMEM((B_WIDTH + C_WIDTH, ts), bf16),
        ] + 2 * [
            pltpu.VMEM((B_HEADS // 2, B_WIN, 2 * LANE), f32), pltpu.VMEM((B_HEADS // 2, 1, 2 * LANE), f32),
            pltpu.VMEM((C_HEADS // 2, C_WIN, 2 * LANE), f32), pltpu.VMEM((C_HEADS // 2, 1, 2 * LANE), f32),
        ],
        compiler_params=pltpu.CompilerParams(dimension_semantics=("arbitrary", "arbitrary"),
                                             vmem_limit_bytes=VMEM_LIMIT),
        name="prompt_layer",
    )(x, p["gpre"], p["gpost"], p["wT"], p["wout"], p["ang"], p["awsT"],
      p["abs"], p["sink"], p["bias_pb"], p["bias_pc"], *kv_all)


def _sample_kernel(x_ref, cbk_ref, cbv_ref, cck_ref, ccv_ref, gpre_ref, gpost_ref, wT_ref,
                   wout_ref, ang_ref, aws_ref, arep_ref, akeep_ref, abias_ref,
                   sink_ref, biasb_ref, biasc_ref,
                   bk_all, bv_all, ck_all, cv_all, av_all,
                   xo_ref, bk_ref, bv_ref, ck_ref, cv_ref, av_ref,
                   pt_sc, yT_sc, q_sc):
    ns = SAMPLE_STREAMS_PER_BLOCK
    x = x_ref[...]
    ntok = x.shape[0]
    h = _rms_rows(x, gpre_ref[...]).astype(bf16)
    _project_T(pt_sc, wT_ref, h)
    kvn_b = lax.dot_general(h, wT_ref[T_BK:T_BZ, :], _NT, preferred_element_type=f32)
    kvn_c = lax.dot_general(h, wT_ref[T_CK:T_CZ, :], _NT, preferred_element_type=f32)
    kn_c = kvn_c[:, 0:C_WIDTH]
    bk_ref[...] = kvn_b[:, 0:B_KVW]
    bv_ref[...] = kvn_b[:, B_KVW:]
    ck_ref[...] = kn_c
    cv_ref[...] = kvn_c[:, C_WIDTH:]

    vn = _rms_cols(pt_sc[T_AV:T_AV + A_WIDTH, :], ang_ref[...])
    av_ref[...] = vn.T
    vnb = vn.astype(bf16)
    keep = akeep_ref[...] > 0.0
    rep = arep_ref[...]
    for g in range(A_GROUPS):
        rows = slice(g * A_GD, (g + 1) * A_GD)
        tiled = lax.dot_general(jnp.dot(rep, aws_ref[g].astype(bf16), preferred_element_type=f32).astype(bf16),
                                rep, _NT, preferred_element_type=f32)
        wgt = jnp.where(keep, tiled, 0.0).astype(bf16)
        mix = jnp.dot(vnb[rows, :], wgt, preferred_element_type=f32) + abias_ref[g]
        u = pt_sc[T_AU + g * A_GD:T_AU + (g + 1) * A_GD, :]
        z = pt_sc[T_AZ + g * A_GD:T_AZ + (g + 1) * A_GD, :]
        yT_sc[Y_A + g * A_GD:Y_A + (g + 1) * A_GD, :] = (u * mix * jax.nn.silu(z)).astype(bf16)

    _prepare_rows(pt_sc, q_sc)
    for blk in range(ntok // LANE):
        cols = slice(blk * LANE, (blk + 1) * LANE)
        streams = range(blk * ns, (blk + 1) * ns)
        kwin_b = jnp.concatenate([cbk_ref[s].T.astype(bf16) for s in streams]
                                 + [kvn_b[cols, 0:B_KVW].astype(bf16)], axis=0)
        vwinT_b = jnp.concatenate([cbv_ref[s].astype(bf16) for s in streams]
                                  + [pt_sc[T_BV:T_BV + B_KVW, cols].astype(bf16)], axis=1)
        kwin_c = jnp.concatenate([cck_ref[s].T.astype(bf16) for s in streams]
                                 + [kn_c[cols, :].astype(bf16)], axis=0)
        vwinT_c = jnp.concatenate([ccv_ref[s].astype(bf16) for s in streams]
                                  + [pt_sc[T_CV:T_CV + C_WIDTH, cols].astype(bf16)], axis=1)
        s_b, s_c = _scores(q_sc, cols, kwin_b, biasb_ref, kwin_c, biasc_ref)
        _attend(pt_sc, yT_sc, cols, s_b, s_c, _with_ones(vwinT_b, HEAD_DIM), _with_ones(vwinT_c, 2 * HEAD_DIM),
                sink_ref)

    xo_ref[...] = _finish(x, yT_sc, wout_ref, gpost_ref)


def _sample_layer(layer, x, caches, p, outs_all, *, t_new, tok_per_step):
    ntok, d = x.shape
    spb = tok_per_step // t_new
    cbk, cbv, cck, ccv = caches
    n_in = 17

    def rows(width):
        return pl.BlockSpec((tok_per_step, width), lambda i: (i, 0))

    def layer_rows(a):
        return pl.BlockSpec((None, tok_per_step, a.shape[2]), lambda i: (layer, i, 0))

    def cache(c):
        return pl.BlockSpec((None, spb) + c.shape[2:], lambda i: (layer, i, 0, 0))

    return pl.pallas_call(
        _sample_kernel,
        grid=(ntok // tok_per_step,),
        in_specs=[
            rows(d), cache(cbk), cache(cbv), cache(cck), cache(ccv),
            _layer_spec(p["gpre"], layer), _layer_spec(p["gpost"], layer),
            _layer_spec(p["wT"], layer),
            _layer_spec(p["wout"], layer),
            _layer_spec(p["ang"], layer), _layer_spec(p["aws_s"], layer), _const_spec(p["arep"]),
            _const_spec(p["akeep"]), _layer_spec(p["abias_s"], layer),
            _layer_spec(p["sink"], layer), _const_spec(p["bias_sb"]), _layer_spec(p["bias_sc"], layer),
        ] + [pl.BlockSpec(memory_space=pl.ANY)] * len(outs_all),
        out_specs=[rows(d)] + [layer_rows(a) for a in outs_all],
        out_shape=[jax.ShapeDtypeStruct((ntok, d), f32)]
        + [jax.ShapeDtypeStruct(a.shape, a.dtype) for a in outs_all],
        input_output_aliases={n_in + i: 1 + i for i in range(len(outs_all))},
        scratch_shapes=[
            pltpu.VMEM((T_END, tok_per_step), f32),
            pltpu.VMEM((A_WIDTH + B_WIDTH + C_WIDTH, tok_per_step), bf16),
            pltpu.VMEM((B_WIDTH + C_WIDTH, tok_per_step), bf16),
        ],
        compiler_params=pltpu.CompilerParams(dimension_semantics=("arbitrary",),
                                             vmem_limit_bytes=VMEM_LIMIT),
        name="sample_layer",
    )(x, cbk, cbv, cck, ccv, p["gpre"], p["gpost"], p["wT"], p["wout"],
      p["ang"], p["aws_s"], p["arep"], p["akeep"], p["abias_s"], p["sink"], p["bias_sb"], p["bias_sc"],
      *outs_all)


def kernel(x_prompt, x_sample, cache_b_k, cache_b_v, cache_c_k, cache_c_v, g_pre, g_post, w_in, w_out,
           a_norm_g, a_ws, a_bs, b_sinks, c_rel_bias, t5_bias):
    depth = w_in.shape[0]
    nb, seq, d = x_prompt.shape
    ns_all, t_new, _ = x_sample.shape
    lc_b, lc_c = cache_b_k.shape[2], cache_c_k.shape[2]
    assert lc_b == B_KEEP and lc_c == C_KEEP and LANE % t_new == 0
    assert SAMPLE_STREAMS_PER_BLOCK * t_new == LANE
    tok_per_step = 2 * LANE
    reps = tok_per_step // t_new

    rel_pb, ok_pb, rel_pc, ok_pc = _prompt_maps()
    rel_sb, ok_sb, rel_sc, ok_sc = _sample_maps(lc_b, lc_c, t_new)
    b_lo = int(min(rel_pb.min(), rel_sb.min()))
    b_hi = int(max(rel_pb.max(), rel_sb.max()))
    t5_rel = t5_bias[_t5_bucket(jnp.arange(b_lo, b_hi + 1))][None]
    c_lo = -C_CLIP - LANE + 1
    c_hi = int(max(rel_pc.max(), rel_sc.max()))
    assert c_hi < C_CLIP
    c_rel = jnp.concatenate([jnp.broadcast_to(c_rel_bias[:, :1], (depth, -C_CLIP - c_lo, C_HEADS)),
                             c_rel_bias[:, :c_hi + C_CLIP + 1]], axis=1)
    r_ = np.arange(tok_per_step)
    awsT = a_ws.transpose(0, 1, 3, 2)

    p = dict(
        bias_pb=_build_bias(t5_rel, b_lo, rel_pb, ok_pb, kmins=(B_PREV, 0))[0],
        bias_sb=_build_bias(t5_rel, b_lo, rel_sb, ok_sb, kmins=(0,))[0, 0],
        bias_pc=_build_bias(c_rel, c_lo, rel_pc, ok_pc, const_below=-C_CLIP,
                            kmins=tuple(max(C_PREV - 2 * v, 0) for v in range(C_PREV // 2 + 1))),
        bias_sc=_build_bias(c_rel, c_lo, rel_sc, ok_sc, const_below=-C_CLIP, kmins=(0,))[:, 0],
        wT=_transpose_weights(w_in),
        wout=w_out.astype(bf16),
        gpre=g_pre[:, None, :],
        gpost=g_post[:, None, :],
        ang=a_norm_g[:, :, None],
        awsT=awsT,
        abs=a_bs[:, :, None, :],
        sink=jnp.repeat(b_sinks, LANE, axis=1)[:, None, :],
        aws_s=awsT[:, :, :t_new, :t_new],
        arep=jnp.asarray((r_[:, None] % t_new == np.arange(t_new)[None, :]).astype(np.float32), bf16),
        abias_s=jnp.tile(a_bs[:, :, None, :t_new], (1, 1, 1, reps)),
        akeep=jnp.asarray(((r_[:, None] // t_new == r_[None, :] // t_new)
                           & (r_[:, None] % t_new <= r_[None, :] % t_new)).astype(np.float32)),
    )

    xp = x_prompt
    xs = x_sample.reshape(ns_all * t_new, d)
    caches = tuple(c.transpose(0, 1, 3, 4, 2).reshape(depth, ns_all, -1, c.shape[2])
                   for c in (cache_b_k, cache_b_v, cache_c_k, cache_c_v))

    kv_all = tuple(jnp.zeros((depth, nb, w, keep), f32)
                   for w, keep in ((B_KVW, B_KEEP), (B_KVW, B_KEEP), (C_WIDTH, C_KEEP), (C_WIDTH, C_KEEP)))
    s_all = tuple(jnp.zeros((depth, ns_all * t_new, w), f32)
                  for w in (B_KVW, B_KVW, C_WIDTH, C_WIDTH, A_WIDTH))
    for l in range(depth):
        xp, *kv_all = _prompt_layer(l, xp, p, kv_all)
        xs, *s_all = _sample_layer(l, xs, caches, p, s_all, t_new=t_new, tok_per_step=tok_per_step)

    def heads_of(a, heads):
        return a.reshape(depth, ns_all, t_new, heads, HEAD_DIM)

    def from_fm(a, heads):
        return a.reshape(depth, nb, heads, HEAD_DIM, a.shape[-1]).transpose(0, 1, 4, 2, 3)

    pk, pv, pck, pcv = kv_all
    sk, sv, sck, scv, sav = s_all
    return (xp, xs.reshape(ns_all, t_new, d),
            from_fm(pk, B_KV), from_fm(pv, B_KV), from_fm(pck, C_HEADS), from_fm(pcv, C_HEADS),
            heads_of(sk, B_KV), heads_of(sv, B_KV), heads_of(sck, C_HEADS), heads_of(scv, C_HEADS),
            sav.reshape(depth, ns_all, t_new, A_WIDTH))
```

```python
import functools
import math

import numpy as np
import jax
import jax.numpy as jnp
from jax import lax
from jax.experimental import pallas as pl
from jax.experimental.pallas import tpu as pltpu

bf16 = jnp.bfloat16
f32 = jnp.float32

HEAD_DIM = 64
CHUNK = 64
A_WIDTH, A_GROUPS, A_CHUNK = 256, 4, 128
A_GD = A_WIDTH // A_GROUPS
B_HEADS, B_KV = 8, 2
B_GROUP = B_HEADS // B_KV
B_WIDTH, B_KVW = B_HEADS * HEAD_DIM, B_KV * HEAD_DIM
B_PREV = 2
C_HEADS = 4
C_WIDTH = C_HEADS * HEAD_DIM
C_PREV = 8
C_CLIP = 128
T5_BUCKETS, T5_MAX_DIST = 32, 128
RMS_EPS = 1e-6
NEG_INF = -1e30
QK_SCALE = HEAD_DIM ** -0.5
LOG2E = math.log2(math.e)

LANE = 128
PAIR = 2 * CHUNK
B_WIN = (B_PREV + 2) * CHUNK
C_WIN = (C_PREV + 2) * CHUNK
B_KEEP = B_PREV * CHUNK
C_KEEP = C_PREV * CHUNK
ONES_ROWS = 16
VB_STRIDE = HEAD_DIM + ONES_ROWS
VC_STRIDE = 2 * HEAD_DIM + ONES_ROWS
SEQ_TILE = 512
SAMPLE_STREAMS_PER_BLOCK = 4
VMEM_LIMIT = 56 * 1024 * 1024
TOEPLITZ_UNROLL = 17

_IN_SIZES = [A_WIDTH, A_WIDTH, A_WIDTH, B_WIDTH, B_KVW, B_KVW, B_WIDTH, C_WIDTH, C_WIDTH, C_WIDTH, C_WIDTH]
_IN_OFF = [int(v) for v in np.cumsum([0] + _IN_SIZES)]
(T_AU, T_AV, T_AZ, T_BQ, T_BK, T_BV, T_BZ, T_CQ, T_CK, T_CV, T_CZ, T_END) = _IN_OFF
_T_SECTIONS = ((T_AU, T_BK), (T_BV, T_CK), (T_CV, T_END))
Y_A, Y_B, Y_C = 0, A_WIDTH, A_WIDTH + B_WIDTH

_NT = (((1,), (1,)), ((), ()))
_TN = (((0,), (0,)), ((), ()))


def _t5_bucket(rel):
    half = T5_BUCKETS // 2
    max_exact = half // 2
    ret = jnp.where(rel > 0, half, 0)
    n = jnp.abs(rel)
    nf = jnp.maximum(n, 1).astype(jnp.float32)
    large = max_exact + (jnp.log(nf / max_exact) / math.log(T5_MAX_DIST / max_exact)
                         * (half - max_exact)).astype(jnp.int32)
    large = jnp.minimum(large, half - 1)
    return ret + jnp.where(n < max_exact, n, large)


def _toeplitz_kernel(rows_ref, shift_ref, frev_ref, ok_ref, out_ref, *, n_heads, kmins, n_toeplitz,
                     n_const, const_lane):
    width = frev_ref.shape[-1]
    neg = jnp.full((CHUNK, LANE), NEG_INF, f32)

    def store(rb, h, val):
        r0 = pl.multiple_of(rb * CHUNK, CHUNK)
        val = jnp.where(ok_ref[pl.ds(r0, CHUNK), :] > 0.0, val, NEG_INF)
        for v, kmin in enumerate(kmins):
            out_ref[0, v, pl.ds(r0, CHUNK), h * LANE:(h + 1) * LANE] = (
                val if kmin == 0 else jnp.where(rb >= kmin, val, neg))

    def toeplitz_block(i, carry):
        rolled = pltpu.roll(frev_ref[...], shift_ref[i], 1)
        ring = jnp.concatenate([rolled[:, :LANE], rolled[:, width - LANE:]], axis=1)
        for h in range(n_heads):
            x = jnp.broadcast_to(ring[h:h + 1, :], (CHUNK, 2 * LANE))
            store(rows_ref[i], h, pltpu.roll(x, 0, 1, stride=1, stride_axis=0)[:, :LANE] * LOG2E)
        return carry

    def const_block(i, carry):
        for h in range(n_heads):
            store(rows_ref[n_toeplitz + i], h, consts[h])
        return carry

    def trips(n):
        return max(u for u in range(1, TOEPLITZ_UNROLL + 1) if n % u == 0)

    if n_toeplitz:
        lax.fori_loop(0, n_toeplitz, toeplitz_block, 0, unroll=trips(n_toeplitz))
    if n_const:
        consts = [jnp.broadcast_to(frev_ref[h:h + 1, const_lane:const_lane + 1], (CHUNK, LANE)) * LOG2E
                  for h in range(n_heads)]
        lax.fori_loop(0, n_const, const_block, 0, unroll=trips(n_const))


def _build_bias(f_ext, rel_lo, rel, ok, *, kmins, const_below=None):
    n_layers, n_rel, n_heads = f_ext.shape
    n_rows = rel.shape[0]
    width = -(-n_rel // LANE) * LANE
    jj, ll = np.arange(CHUNK)[:, None], np.arange(LANE)[None, :]
    toeplitz_rows, shifts, const_rows = [], [], []
    for rb in range(n_rows // CHUNK):
        r, o = rel[rb * CHUNK:(rb + 1) * CHUNK], ok[rb * CHUNK:(rb + 1) * CHUNK]
        if not o.any() or (const_below is not None and r[o].max() <= const_below):
            const_rows.append(rb)
            continue
        bases = (r - jj + ll)[o]
        assert (bases == bases[0]).all(), "block is not Toeplitz on its valid entries"
        base = int(bases[0])
        k = (base + jj - ll - rel_lo)[o]
        assert k.min() >= 0 and k.max() < n_rel
        toeplitz_rows.append(rb)
        shifts.append((width - ((n_rel - 1) - (base - rel_lo))) % width)
    const_lane = 0 if const_below is None else (n_rel - 1) - (const_below - rel_lo)
    frev = jnp.pad(jnp.flip(f_ext, axis=1).transpose(0, 2, 1), ((0, 0), (0, 0), (0, width - n_rel)))
    kern = functools.partial(_toeplitz_kernel, n_heads=n_heads, kmins=tuple(kmins),
                             n_toeplitz=len(toeplitz_rows), n_const=len(const_rows), const_lane=const_lane)
    rows_arg = jnp.asarray(np.asarray(toeplitz_rows + const_rows, np.int32))
    return pl.pallas_call(
        kern,
        grid=(n_layers,),
        in_specs=[pl.BlockSpec(memory_space=pltpu.SMEM),
                  pl.BlockSpec(memory_space=pltpu.SMEM),
                  pl.BlockSpec((None, n_heads, width), lambda l: (l, 0, 0)),
                  pl.BlockSpec((n_rows, LANE), lambda l: (0, 0))],
        out_specs=pl.BlockSpec((1, len(kmins), n_rows, n_heads * LANE), lambda l: (l, 0, 0, 0)),
        out_shape=jax.ShapeDtypeStruct((n_layers, len(kmins), n_rows, n_heads * LANE), f32),
        compiler_params=pltpu.CompilerParams(dimension_semantics=("arbitrary",),
                                             vmem_limit_bytes=VMEM_LIMIT),
        name="bias_toeplitz",
    )(rows_arg, jnp.asarray(np.asarray(shifts or [0], np.int32)), frev, jnp.asarray(ok.astype(np.float32)))


def _wT_kernel(w_ref, o_ref):
    o_ref[...] = w_ref[...].T.astype(bf16)


def _transpose_weights(w, cols=8 * LANE):
    n_layers, k, n = w.shape
    return pl.pallas_call(
        _wT_kernel,
        grid=(n_layers, n // cols),
        in_specs=[pl.BlockSpec((None, k, cols), lambda l, c: (l, 0, c))],
        out_specs=pl.BlockSpec((None, cols, k), lambda l, c: (l, c, 0)),
        out_shape=jax.ShapeDtypeStruct((n_layers, n, k), bf16),
        compiler_params=pltpu.CompilerParams(dimension_semantics=("arbitrary", "arbitrary"),
                                             vmem_limit_bytes=VMEM_LIMIT),
        name="weights_T",
    )(w)


def _prompt_maps():
    i = np.arange(PAIR)[None, :]
    qi = i // CHUNK
    jb = np.arange(B_WIN)[:, None]
    rel_b = jb - B_KEEP - i
    ok_b = (jb // CHUNK >= qi) & (jb // CHUNK <= qi + B_PREV)
    jc = np.arange(C_WIN)[:, None]
    rel_c = jc - C_KEEP - i
    ok_c = (jc // CHUNK >= qi) & (jc // CHUNK <= qi + C_PREV)
    return rel_b, ok_b, rel_c, ok_c


def _sample_maps(lc_b, lc_c, t_new):
    ns = SAMPLE_STREAMS_PER_BLOCK
    c = np.arange(LANE)[None, :]
    sq, i = c // t_new, c % t_new

    def one(lc):
        r = np.arange(ns * lc + LANE)[:, None]
        is_new = r >= ns * lc
        sk = np.where(is_new, (r - ns * lc) // t_new, r // lc)
        j = np.where(is_new, lc + (r - ns * lc) % t_new, r % lc)
        return (j - lc - i).astype(np.int32), (sk == sq)

    return one(lc_b) + one(lc_c)


def _rms_rows(x, g):
    ms = jnp.mean(x * x, axis=-1, keepdims=True)
    return x * lax.rsqrt(ms + RMS_EPS) * g


def _rms_cols(xT, g_col):
    ms = jnp.mean(xT * xT, axis=0, keepdims=True)
    return xT * lax.rsqrt(ms + RMS_EPS) * g_col


def _project_T(pt_sc, wT_ref, h):
    for r0, r1 in _T_SECTIONS:
        pt_sc[r0:r1, :] = lax.dot_general(wT_ref[r0:r1, :], h, _NT, preferred_element_type=f32)


def _qblockdiag(qa, qb, lower):
    z = jnp.zeros((HEAD_DIM, LANE), bf16)
    if lower is None:
        return jnp.concatenate([jnp.concatenate([qa, z], axis=1),
                                jnp.concatenate([z, qb], axis=1)], axis=0)
    top = jnp.concatenate([qa, qb], axis=1)
    zz = jnp.zeros((HEAD_DIM, 2 * LANE), bf16)
    return jnp.concatenate([top, zz] if lower == 0 else [zz, top], axis=0)


def _with_ones(vT, group):
    ones = jnp.ones((ONES_ROWS, vT.shape[1]), vT.dtype)
    parts = []
    for r in range(0, vT.shape[0], group):
        parts += [vT[r:r + group, :], ones]
    return jnp.concatenate(parts, axis=0)


def _softmax_pv(s_m, sink, vwin):
    s, m = s_m
    if sink is not None:
        m = jnp.maximum(m, sink)
    e = jnp.exp2(s - m)
    o = jnp.dot(vwin, e.astype(bf16), preferred_element_type=f32)
    nv = vwin.shape[0] - ONES_ROWS
    den = o[nv:nv + 1, :]
    if sink is not None:
        den = den + jnp.exp2(sink - m)
    return o[:nv, :] * (1.0 / den)


def _scores(q_sc, cols, kwin_b, bias_b, kwin_c, bias_c):
    def head_q(r0, hh):
        return q_sc[r0 + hh * HEAD_DIM:r0 + (hh + 1) * HEAD_DIM, cols]

    def with_max(s):
        return s, jnp.max(s, axis=0, keepdims=True)

    s_b, s_c = [], []
    for cb in range(B_HEADS // 2):
        ha, hb = 2 * cb, 2 * cb + 1
        qbd = _qblockdiag(head_q(0, ha), head_q(0, hb), ha // B_GROUP)
        s = jnp.dot(kwin_b, qbd, preferred_element_type=f32)
        s_b.append(with_max(s + bias_b[:, cb * 2 * LANE:(cb + 1) * 2 * LANE]))
    for cb in range(C_HEADS // 2):
        ha, hb = 2 * cb, 2 * cb + 1
        qbd = _qblockdiag(head_q(B_WIDTH, ha), head_q(B_WIDTH, hb), None)
        s = jnp.dot(kwin_c[:, cb * LANE:(cb + 1) * LANE], qbd, preferred_element_type=f32)
        s_c.append(with_max(s + bias_c[:, cb * 2 * LANE:(cb + 1) * 2 * LANE]))
    return s_b, s_c


def _attend(pt_sc, yT_sc, cols, s_b, s_c, vwinT_b, vwinT_c, sink_ref):
    for cb in range(B_HEADS // 2):
        ha, hb = 2 * cb, 2 * cb + 1
        kv = ha // B_GROUP
        sink = sink_ref[:, cb * 2 * LANE:(cb + 1) * 2 * LANE] * LOG2E
        o = _softmax_pv(s_b[cb], sink, vwinT_b[kv * VB_STRIDE:(kv + 1) * VB_STRIDE, :])
        for j, hh in enumerate((ha, hb)):
            gate = pt_sc[T_BZ + hh * HEAD_DIM:T_BZ + (hh + 1) * HEAD_DIM, cols]
            yT_sc[Y_B + hh * HEAD_DIM:Y_B + (hh + 1) * HEAD_DIM, cols] = (
                o[:, j * LANE:(j + 1) * LANE] * gate).astype(bf16)
    for cb in range(C_HEADS // 2):
        ha, hb = 2 * cb, 2 * cb + 1
        o = _softmax_pv(s_c[cb], None, vwinT_c[cb * VC_STRIDE:(cb + 1) * VC_STRIDE, :])
        for j, hh in enumerate((ha, hb)):
            gate = pt_sc[T_CZ + hh * HEAD_DIM:T_CZ + (hh + 1) * HEAD_DIM, cols]
            yT_sc[Y_C + hh * HEAD_DIM:Y_C + (hh + 1) * HEAD_DIM, cols] = (
                o[j * HEAD_DIM:(j + 1) * HEAD_DIM, j * LANE:(j + 1) * LANE] * gate).astype(bf16)


def _prepare_rows(pt_sc, q_sc):
    for r0, r1 in ((T_BZ, T_CQ), (T_CZ, T_END)):
        pt_sc[r0:r1, :] = jax.nn.silu(pt_sc[r0:r1, :])
    q_sc[0:B_WIDTH, :] = (pt_sc[T_BQ:T_BK, :] * (QK_SCALE * LOG2E)).astype(bf16)
    q_sc[B_WIDTH:, :] = (pt_sc[T_CQ:T_CK, :] * (QK_SCALE * LOG2E)).astype(bf16)


def _finish(x, yT_sc, wout_ref, gpost_ref):
    out = lax.dot_general(yT_sc[...], wout_ref[...], _TN, preferred_element_type=f32)
    return x + _rms_rows(out, gpost_ref[...])


def _layer_spec(arr, layer):
    shape = arr.shape[1:]
    return pl.BlockSpec((None,) + shape, lambda *_: (layer,) + (0,) * len(shape),
                        pipeline_mode=pl.Buffered(1))


def _const_spec(arr):
    return pl.BlockSpec(arr.shape, lambda *_: (0,) * arr.ndim, pipeline_mode=pl.Buffered(1))


def _prompt_kernel(x_ref, gpre_ref, gpost_ref, wT_ref, wout_ref,
                   ang_ref, awsT_ref, abs_ref, sink_ref, biasb_ref, biasc_ref,
                   bk_all, bv_all, ck_all, cv_all,
                   xo_ref, bk_ref, bv_ref, ck_ref, cv_ref,
                   pt_sc, kb_ring, kc_ring, vbT_ring, vcT_ring, yT_sc, q_sc,
                   sb_even, mb_even, sc_even, mc_even, sb_odd, mb_odd, sc_odd, mc_odd):
    t = pl.program_id(1)
    ts = SEQ_TILE

    @pl.when(t == 0)
    def _():
        kb_ring[0:B_KEEP, :] = jnp.zeros((B_KEEP, B_KVW), bf16)
        kc_ring[0:C_KEEP, :] = jnp.zeros((C_KEEP, C_WIDTH), bf16)
        vbT_ring[:, 0:B_KEEP] = jnp.zeros((B_KV * VB_STRIDE, B_KEEP), bf16)
        vcT_ring[:, 0:C_KEEP] = jnp.zeros((C_HEADS // 2 * VC_STRIDE, C_KEEP), bf16)
        for g in range(B_KV):
            vbT_ring[g * VB_STRIDE + HEAD_DIM:(g + 1) * VB_STRIDE, :] = jnp.ones((ONES_ROWS, B_KEEP + ts), bf16)
        for g in range(C_HEADS // 2):
            vcT_ring[g * VC_STRIDE + 2 * HEAD_DIM:(g + 1) * VC_STRIDE, :] = jnp.ones((ONES_ROWS, C_KEEP + ts), bf16)

    x = x_ref[0]
    h = _rms_rows(x, gpre_ref[...]).astype(bf16)
    _project_T(pt_sc, wT_ref, h)
    kb_ring[B_KEEP:, :] = lax.dot_general(h, wT_ref[T_BK:T_BV, :], _NT,
                                          preferred_element_type=f32).astype(bf16)
    kc_ring[C_KEEP:, :] = lax.dot_general(h, wT_ref[T_CK:T_CV, :], _NT,
                                          preferred_element_type=f32).astype(bf16)
    for g in range(B_KV):
        vbT_ring[g * VB_STRIDE:g * VB_STRIDE + HEAD_DIM, B_KEEP:] = (
            pt_sc[T_BV + g * HEAD_DIM:T_BV + (g + 1) * HEAD_DIM, :].astype(bf16))
    for g in range(C_HEADS // 2):
        vcT_ring[g * VC_STRIDE:g * VC_STRIDE + 2 * HEAD_DIM, C_KEEP:] = (
            pt_sc[T_CV + g * 2 * HEAD_DIM:T_CV + (g + 1) * 2 * HEAD_DIM, :].astype(bf16))

    _prepare_rows(pt_sc, q_sc)

    @pl.when(t == pl.num_programs(1) - 1)
    def _():
        bk_ref[0] = lax.dot_general(wT_ref[T_BK:T_BV, :], h[ts - B_KEEP:, :], _NT, preferred_element_type=f32)
        ck_ref[0] = lax.dot_general(wT_ref[T_CK:T_CV, :], h[ts - C_KEEP:, :], _NT, preferred_element_type=f32)
        bv_ref[0] = pt_sc[T_BV:T_BZ, ts - B_KEEP:]
        cv_ref[0] = pt_sc[T_CV:T_CZ, ts - C_KEEP:]

    vn = _rms_cols(pt_sc[T_AV:T_AV + A_WIDTH, :], ang_ref[...]).astype(bf16)
    n_ac = ts // A_CHUNK
    row = lax.broadcasted_iota(jnp.int32, (A_CHUNK, A_CHUNK), 0)
    col = lax.broadcasted_iota(jnp.int32, (A_CHUNK, A_CHUNK), 1)
    for g in range(A_GROUPS):
        rows = slice(g * A_GD, (g + 1) * A_GD)
        wgt = jnp.where(row <= col, awsT_ref[g], 0.0).astype(bf16)
        lhs = jnp.concatenate([vn[rows, c * A_CHUNK:(c + 1) * A_CHUNK] for c in range(n_ac)], axis=0)
        mix = jnp.dot(lhs, wgt, preferred_element_type=f32) + abs_ref[g]
        for c in range(n_ac):
            cols = slice(c * A_CHUNK, (c + 1) * A_CHUNK)
            u = pt_sc[T_AU + g * A_GD:T_AU + (g + 1) * A_GD, cols]
            z = pt_sc[T_AZ + g * A_GD:T_AZ + (g + 1) * A_GD, cols]
            yT_sc[Y_A + g * A_GD:Y_A + (g + 1) * A_GD, cols] = (
                u * mix[c * A_GD:(c + 1) * A_GD, :] * jax.nn.silu(z)).astype(bf16)

    n_pairs = ts // PAIR

    even, odd = (sb_even, mb_even, sc_even, mc_even), (sb_odd, mb_odd, sc_odd, mc_odd)

    def scores_into(p, bufs):
        sb_ref, mb_ref, sc_ref, mc_ref = bufs
        pg = t * n_pairs + p
        start = p * PAIR if isinstance(p, int) else pl.multiple_of(p * PAIR, PAIR)
        s_b, s_c = _scores(q_sc, pl.ds(start, PAIR),
                           kb_ring[pl.ds(start, B_WIN), :], biasb_ref.at[jnp.minimum(pg, 1)],
                           kc_ring[pl.ds(start, C_WIN), :], biasc_ref.at[jnp.minimum(pg, C_PREV // 2)])
        for cb, (v, m) in enumerate(s_b):
            sb_ref[cb] = v
            mb_ref[cb] = m
        for cb, (v, m) in enumerate(s_c):
            sc_ref[cb] = v
            mc_ref[cb] = m

    def attend_from(p, bufs):
        sb_ref, mb_ref, sc_ref, mc_ref = bufs
        start = p * PAIR if isinstance(p, int) else pl.multiple_of(p * PAIR, PAIR)
        _attend(pt_sc, yT_sc, pl.ds(start, PAIR),
                [(sb_ref[cb], mb_ref[cb]) for cb in range(B_HEADS // 2)],
                [(sc_ref[cb], mc_ref[cb]) for cb in range(C_HEADS // 2)],
                vbT_ring[:, pl.ds(start, B_WIN)], vcT_ring[:, pl.ds(start, C_WIN)], sink_ref)

    scores_into(0, even)

    def finish_rows(p):
        start = p * PAIR if isinstance(p, int) else pl.multiple_of(p * PAIR, PAIR)
        rows = pl.ds(start, PAIR)
        xo_ref[0, rows, :] = _finish(x_ref[0, rows, :], yT_sc.at[:, rows], wout_ref, gpost_ref)

    assert n_pairs == 4

    def first_pairs(q, carry):
        p = 2 * q
        scores_into(p + 1, odd)
        attend_from(p, even)
        scores_into(p + 2, even)
        finish_rows(p)
        attend_from(p + 1, odd)
        return carry

    def last_pairs(q, carry):
        p = 2 * q
        scores_into(p + 1, odd)
        finish_rows(p - 1)
        attend_from(p, even)
        finish_rows(p)
        attend_from(p + 1, odd)
        return carry

    n_full = jnp.minimum(t + n_pairs, n_pairs // 2 - 1)
    lax.fori_loop(0, n_full, first_pairs, 0)
    lax.fori_loop(n_full, n_full + 1, last_pairs, 0)
    finish_rows(n_pairs - 1)

    kb_ring[0:B_KEEP, :] = kb_ring[ts:ts + B_KEEP, :]
    kc_ring[0:C_KEEP, :] = kc_ring[ts:ts + C_KEEP, :]
    vbT_ring[:, 0:B_KEEP] = vbT_ring[:, ts:ts + B_KEEP]
    vcT_ring[:, 0:C_KEEP] = vcT_ring[:, ts:ts + C_KEEP]


def _with_absent(kernel_fn, n_before, n_absent):
    return lambda *refs: kernel_fn(*refs[:n_before], *([None] * n_absent), *refs[n_before:])


def _prompt_layer(layer, x, p, kv_all):
    nb, seq, d = x.shape
    ts = SEQ_TILE
    assert seq % ts == 0 and ts == C_KEEP
    kv_spec = lambda keep, width: pl.BlockSpec((None, 1, width, keep), lambda b, t: (layer, b, 0, 0))
    n_in = 11
    carried = [a for a in kv_all if not isinstance(a, jax.ShapeDtypeStruct)]
    assert len(carried) in (0, len(kv_all))
    kernel_fn = _prompt_kernel if carried else _with_absent(_prompt_kernel, n_in, len(kv_all))
    return pl.pallas_call(
        kernel_fn,
        grid=(nb, seq // ts),
        in_specs=[
            pl.BlockSpec((1, ts, d), lambda b, t: (b, t, 0)),
            _layer_spec(p["gpre"], layer), _layer_spec(p["gpost"], layer),
            _layer_spec(p["wT"], layer),
            _layer_spec(p["wout"], layer),
            _layer_spec(p["ang"], layer), _layer_spec(p["awsT"], layer), _layer_spec(p["abs"], layer),
            _layer_spec(p["sink"], layer), _const_spec(p["bias_pb"]), _layer_spec(p["bias_pc"], layer),
        ] + [pl.BlockSpec(memory_space=pl.ANY)] * len(carried),
        out_specs=[
            pl.BlockSpec((1, ts, d), lambda b, t: (b, t, 0)),
            kv_spec(B_KEEP, B_KVW), kv_spec(B_KEEP, B_KVW), kv_spec(C_KEEP, C_WIDTH), kv_spec(C_KEEP, C_WIDTH),
        ],
        out_shape=[jax.ShapeDtypeStruct((nb, seq, d), f32)]
        + [jax.ShapeDtypeStruct(a.shape, a.dtype) for a in kv_all],
        input_output_aliases={n_in + i: 1 + i for i in range(len(carried))},
        scratch_shapes=[
            pltpu.VMEM((T_END, ts), f32),
            pltpu.VMEM((B_KEEP + ts, B_KVW), bf16),
            pltpu.VMEM((C_KEEP + ts, C_WIDTH), bf16),
            pltpu.VMEM((B_KV * VB_STRIDE, B_KEEP + ts), bf16),
            pltpu.VMEM((C_HEADS // 2 * VC_STRIDE, C_KEEP + ts), bf16),
            pltpu.VMEM((A_WIDTH + B_WIDTH + C_WIDTH, ts), bf16),
            pltpu.VMEM((B_WIDTH + C_WIDTH, ts), bf16),
        ] + 2 * [
            pltpu.VMEM((B_HEADS // 2, B_WIN, 2 * LANE), f32), pltpu.VMEM((B_HEADS // 2, 1, 2 * LANE), f32),
            pltpu.VMEM((C_HEADS // 2, C_WIN, 2 * LANE), f32), pltpu.VMEM((C_HEADS // 2, 1, 2 * LANE), f32),
        ],
        compiler_params=pltpu.CompilerParams(dimension_semantics=("arbitrary", "arbitrary"),
                                             vmem_limit_bytes=VMEM_LIMIT),
        name="prompt_layer",
    )(x, p["gpre"], p["gpost"], p["wT"], p["wout"], p["ang"], p["awsT"],
      p["abs"], p["sink"], p["bias_pb"], p["bias_pc"], *carried)


def _sample_kernel(x_ref, cbk_ref, cbv_ref, cck_ref, ccv_ref, gpre_ref, gpost_ref, wT_ref,
                   wout_ref, ang_ref, aws_ref, arep_ref, akeep_ref, abias_ref,
                   sink_ref, biasb_ref, biasc_ref,
                   bk_all, bv_all, ck_all, cv_all, av_all,
                   xo_ref, bk_ref, bv_ref, ck_ref, cv_ref, av_ref,
                   pt_sc, yT_sc, q_sc):
    ns = SAMPLE_STREAMS_PER_BLOCK
    x = x_ref[...]
    ntok = x.shape[0]
    h = _rms_rows(x, gpre_ref[...]).astype(bf16)
    _project_T(pt_sc, wT_ref, h)
    kvn_b = lax.dot_general(h, wT_ref[T_BK:T_BZ, :], _NT, preferred_element_type=f32)
    kvn_c = lax.dot_general(h, wT_ref[T_CK:T_CZ, :], _NT, preferred_element_type=f32)
    kn_c = kvn_c[:, 0:C_WIDTH]
    bk_ref[...] = kvn_b[:, 0:B_KVW]
    bv_ref[...] = kvn_b[:, B_KVW:]
    ck_ref[...] = kn_c
    cv_ref[...] = kvn_c[:, C_WIDTH:]

    vn = _rms_cols(pt_sc[T_AV:T_AV + A_WIDTH, :], ang_ref[...])
    av_ref[...] = vn.T
    vnb = vn.astype(bf16)
    keep = akeep_ref[...] > 0.0
    rep = arep_ref[...]
    for g in range(A_GROUPS):
        rows = slice(g * A_GD, (g + 1) * A_GD)
        tiled = lax.dot_general(jnp.dot(rep, aws_ref[g].astype(bf16), preferred_element_type=f32).astype(bf16),
                                rep, _NT, preferred_element_type=f32)
        wgt = jnp.where(keep, tiled, 0.0).astype(bf16)
        mix = jnp.dot(vnb[rows, :], wgt, preferred_element_type=f32) + abias_ref[g]
        u = pt_sc[T_AU + g * A_GD:T_AU + (g + 1) * A_GD, :]
        z = pt_sc[T_AZ + g * A_GD:T_AZ + (g + 1) * A_GD, :]
        yT_sc[Y_A + g * A_GD:Y_A + (g + 1) * A_GD, :] = (u * mix * jax.nn.silu(z)).astype(bf16)

    _prepare_rows(pt_sc, q_sc)
    for blk in range(ntok // LANE):
        cols = slice(blk * LANE, (blk + 1) * LANE)
        streams = range(blk * ns, (blk + 1) * ns)
        kwin_b = jnp.concatenate([cbk_ref[s].T.astype(bf16) for s in streams]
                                 + [kvn_b[cols, 0:B_KVW].astype(bf16)], axis=0)
        vwinT_b = jnp.concatenate([cbv_ref[s].astype(bf16) for s in streams]
                                  + [pt_sc[T_BV:T_BV + B_KVW, cols].astype(bf16)], axis=1)
        kwin_c = jnp.concatenate([cck_ref[s].T.astype(bf16) for s in streams]
                                 + [kn_c[cols, :].astype(bf16)], axis=0)
        vwinT_c = jnp.concatenate([ccv_ref[s].astype(bf16) for s in streams]
                                  + [pt_sc[T_CV:T_CV + C_WIDTH, cols].astype(bf16)], axis=1)
        s_b, s_c = _scores(q_sc, cols, kwin_b, biasb_ref, kwin_c, biasc_ref)
        _attend(pt_sc, yT_sc, cols, s_b, s_c, _with_ones(vwinT_b, HEAD_DIM), _with_ones(vwinT_c, 2 * HEAD_DIM),
                sink_ref)

    xo_ref[...] = _finish(x, yT_sc, wout_ref, gpost_ref)


def _sample_layer(layer, x, caches, p, outs_all, *, t_new, tok_per_step):
    ntok, d = x.shape
    spb = tok_per_step // t_new
    cbk, cbv, cck, ccv = caches
    n_in = 17
    carried = [a for a in outs_all if not isinstance(a, jax.ShapeDtypeStruct)]
    assert len(carried) in (0, len(outs_all))
    kernel_fn = _sample_kernel if carried else _with_absent(_sample_kernel, n_in, len(outs_all))

    def rows(width):
        return pl.BlockSpec((tok_per_step, width), lambda i: (i, 0))

    def layer_rows(a):
        return pl.BlockSpec((None, tok_per_step, a.shape[2]), lambda i: (layer, i, 0))

    def cache(c):
        return pl.BlockSpec((None, spb) + c.shape[2:], lambda i: (layer, i, 0, 0))

    return pl.pallas_call(
        kernel_fn,
        grid=(ntok // tok_per_step,),
        in_specs=[
            rows(d), cache(cbk), cache(cbv), cache(cck), cache(ccv),
            _layer_spec(p["gpre"], layer), _layer_spec(p["gpost"], layer),
            _layer_spec(p["wT"], layer),
            _layer_spec(p["wout"], layer),
            _layer_spec(p["ang"], layer), _layer_spec(p["aws_s"], layer), _const_spec(p["arep"]),
            _const_spec(p["akeep"]), _layer_spec(p["abias_s"], layer),
            _layer_spec(p["sink"], layer), _const_spec(p["bias_sb"]), _layer_spec(p["bias_sc"], layer),
        ] + [pl.BlockSpec(memory_space=pl.ANY)] * len(carried),
        out_specs=[rows(d)] + [layer_rows(a) for a in outs_all],
        out_shape=[jax.ShapeDtypeStruct((ntok, d), f32)]
        + [jax.ShapeDtypeStruct(a.shape, a.dtype) for a in outs_all],
        input_output_aliases={n_in + i: 1 + i for i in range(len(carried))},
        scratch_shapes=[
            pltpu.VMEM((T_END, tok_per_step), f32),
            pltpu.VMEM((A_WIDTH + B_WIDTH + C_WIDTH, tok_per_step), bf16),
            pltpu.VMEM((B_WIDTH + C_WIDTH, tok_per_step), bf16),
        ],
        compiler_params=pltpu.CompilerParams(dimension_semantics=("arbitrary",),
                                             vmem_limit_bytes=VMEM_LIMIT),
        name="sample_layer",
    )(x, cbk, cbv, cck, ccv, p["gpre"], p["gpost"], p["wT"], p["wout"],
      p["ang"], p["aws_s"], p["arep"], p["akeep"], p["abias_s"], p["sink"], p["bias_sb"], p["bias_sc"],
      *carried)


def kernel(x_prompt, x_sample, cache_b_k, cache_b_v, cache_c_k, cache_c_v, g_pre, g_post, w_in, w_out,
           a_norm_g, a_ws, a_bs, b_sinks, c_rel_bias, t5_bias):
    depth = w_in.shape[0]
    nb, seq, d = x_prompt.shape
    ns_all, t_new, _ = x_sample.shape
    lc_b, lc_c = cache_b_k.shape[2], cache_c_k.shape[2]
    assert lc_b == B_KEEP and lc_c == C_KEEP and LANE % t_new == 0
    assert SAMPLE_STREAMS_PER_BLOCK * t_new == LANE
    tok_per_step = 2 * LANE
    reps = tok_per_step // t_new

    rel_pb, ok_pb, rel_pc, ok_pc = _prompt_maps()
    rel_sb, ok_sb, rel_sc, ok_sc = _sample_maps(lc_b, lc_c, t_new)
    b_lo = int(min(rel_pb.min(), rel_sb.min()))
    b_hi = int(max(rel_pb.max(), rel_sb.max()))
    t5_rel = t5_bias[_t5_bucket(jnp.arange(b_lo, b_hi + 1))][None]
    c_lo = -C_CLIP - LANE + 1
    c_hi = int(max(rel_pc.max(), rel_sc.max()))
    assert c_hi < C_CLIP
    c_rel = jnp.concatenate([jnp.broadcast_to(c_rel_bias[:, :1], (depth, -C_CLIP - c_lo, C_HEADS)),
                             c_rel_bias[:, :c_hi + C_CLIP + 1]], axis=1)
    r_ = np.arange(tok_per_step)
    awsT = a_ws.transpose(0, 1, 3, 2)

    p = dict(
        bias_pb=_build_bias(t5_rel, b_lo, rel_pb, ok_pb, kmins=(B_PREV, 0))[0],
        bias_sb=_build_bias(t5_rel, b_lo, rel_sb, ok_sb, kmins=(0,))[0, 0],
        bias_pc=_build_bias(c_rel, c_lo, rel_pc, ok_pc, const_below=-C_CLIP,
                            kmins=tuple(max(C_PREV - 2 * v, 0) for v in range(C_PREV // 2 + 1))),
        bias_sc=_build_bias(c_rel, c_lo, rel_sc, ok_sc, const_below=-C_CLIP, kmins=(0,))[:, 0],
        wT=_transpose_weights(w_in),
        wout=w_out.astype(bf16),
        gpre=g_pre[:, None, :],
        gpost=g_post[:, None, :],
        ang=a_norm_g[:, :, None],
        awsT=awsT,
        abs=a_bs[:, :, None, :],
        sink=jnp.repeat(b_sinks, LANE, axis=1)[:, None, :],
        aws_s=awsT[:, :, :t_new, :t_new],
        arep=jnp.asarray((r_[:, None] % t_new == np.arange(t_new)[None, :]).astype(np.float32), bf16),
        abias_s=jnp.tile(a_bs[:, :, None, :t_new], (1, 1, 1, reps)),
        akeep=jnp.asarray(((r_[:, None] // t_new == r_[None, :] // t_new)
                           & (r_[:, None] % t_new <= r_[None, :] % t_new)).astype(np.float32)),
    )

    xp = x_prompt
    xs = x_sample.reshape(ns_all * t_new, d)
    caches = tuple(c.transpose(0, 1, 3, 4, 2).reshape(depth, ns_all, -1, c.shape[2])
                   for c in (cache_b_k, cache_b_v, cache_c_k, cache_c_v))

    kv_all = tuple(jax.ShapeDtypeStruct((depth, nb, w, keep), f32)
                   for w, keep in ((B_KVW, B_KEEP), (B_KVW, B_KEEP), (C_WIDTH, C_KEEP), (C_WIDTH, C_KEEP)))
    s_all = tuple(jax.ShapeDtypeStruct((depth, ns_all * t_new, w), f32)
                  for w in (B_KVW, B_KVW, C_WIDTH, C_WIDTH, A_WIDTH))
    for l in range(depth):
        xp, *kv_all = _prompt_layer(l, xp, p, kv_all)
        xs, *s_all = _sample_layer(l, xs, caches, p, s_all, t_new=t_new, tok_per_step=tok_per_step)

    def heads_of(a, heads):
        return a.reshape(depth, ns_all, t_new, heads, HEAD_DIM)

    def from_fm(a, heads):
        return a.reshape(depth, nb, heads, HEAD_DIM, a.shape[-1]).transpose(0, 1, 4, 2, 3)

    pk, pv, pck, pcv = kv_all
    sk, sv, sck, scv, sav = s_all
    return (xp, xs.reshape(ns_all, t_new, d),
            from_fm(pk, B_KV), from_fm(pv, B_KV), from_fm(pck, C_HEADS), from_fm(pcv, C_HEADS),
            heads_of(sk, B_KV), heads_of(sv, B_KV), heads_of(sck, C_HEADS), heads_of(scv, C_HEADS),
            sav.reshape(depth, ns_all, t_new, A_WIDTH))
```

```python
import functools
import math

import numpy as np
import jax
import jax.numpy as jnp
from jax import lax
from jax.experimental import pallas as pl
from jax.experimental.pallas import tpu as pltpu

bf16 = jnp.bfloat16
f32 = jnp.float32

HEAD_DIM = 64
CHUNK = 64
A_WIDTH, A_GROUPS, A_CHUNK = 256, 4, 128
A_GD = A_WIDTH // A_GROUPS
B_HEADS, B_KV = 8, 2
B_GROUP = B_HEADS // B_KV
B_WIDTH, B_KVW = B_HEADS * HEAD_DIM, B_KV * HEAD_DIM
B_PREV = 2
C_HEADS = 4
C_WIDTH = C_HEADS * HEAD_DIM
C_PREV = 8
C_CLIP = 128
T5_BUCKETS, T5_MAX_DIST = 32, 128
RMS_EPS = 1e-6
NEG_INF = -1e30
QK_SCALE = HEAD_DIM ** -0.5
LOG2E = math.log2(math.e)

LANE = 128
PAIR = 2 * CHUNK
B_WIN = (B_PREV + 2) * CHUNK
C_WIN = (C_PREV + 2) * CHUNK
B_KEEP = B_PREV * CHUNK
C_KEEP = C_PREV * CHUNK
ONES_ROWS = 16
VB_STRIDE = HEAD_DIM + ONES_ROWS
VC_STRIDE = 2 * HEAD_DIM + ONES_ROWS
SEQ_TILE = 512
SAMPLE_STREAMS_PER_BLOCK = 4
VMEM_LIMIT = 56 * 1024 * 1024
TOEPLITZ_UNROLL = 17

_IN_SIZES = [A_WIDTH, A_WIDTH, A_WIDTH, B_WIDTH, B_KVW, B_KVW, B_WIDTH, C_WIDTH, C_WIDTH, C_WIDTH, C_WIDTH]
_IN_OFF = [int(v) for v in np.cumsum([0] + _IN_SIZES)]
(T_AU, T_AV, T_AZ, T_BQ, T_BK, T_BV, T_BZ, T_CQ, T_CK, T_CV, T_CZ, T_END) = _IN_OFF
_T_SECTIONS = ((T_AU, T_BK), (T_BV, T_CK), (T_CV, T_END))
Y_A, Y_B, Y_C = 0, A_WIDTH, A_WIDTH + B_WIDTH

_NT = (((1,), (1,)), ((), ()))
_TN = (((0,), (0,)), ((), ()))


def _t5_bucket(rel):
    half = T5_BUCKETS // 2
    max_exact = half // 2
    ret = jnp.where(rel > 0, half, 0)
    n = jnp.abs(rel)
    nf = jnp.maximum(n, 1).astype(jnp.float32)
    large = max_exact + (jnp.log(nf / max_exact) / math.log(T5_MAX_DIST / max_exact)
                         * (half - max_exact)).astype(jnp.int32)
    large = jnp.minimum(large, half - 1)
    return ret + jnp.where(n < max_exact, n, large)


def _toeplitz_kernel(rows_ref, shift_ref, frev_ref, ok_ref, out_ref, *, n_heads, kmins, n_toeplitz,
                     n_const, const_lane):
    width = frev_ref.shape[-1]
    neg = jnp.full((CHUNK, LANE), NEG_INF, f32)

    def store(rb, h, val):
        r0 = pl.multiple_of(rb * CHUNK, CHUNK)
        val = jnp.where(ok_ref[pl.ds(r0, CHUNK), :] > 0.0, val, NEG_INF)
        for v, kmin in enumerate(kmins):
            out_ref[0, v, pl.ds(r0, CHUNK), h * LANE:(h + 1) * LANE] = (
                val if kmin == 0 else jnp.where(rb >= kmin, val, neg))

    def toeplitz_block(i, carry):
        rolled = pltpu.roll(frev_ref[...], shift_ref[i], 1)
        ring = jnp.concatenate([rolled[:, :LANE], rolled[:, width - LANE:]], axis=1)
        for h in range(n_heads):
            x = jnp.broadcast_to(ring[h:h + 1, :], (CHUNK, 2 * LANE))
            store(rows_ref[i], h, pltpu.roll(x, 0, 1, stride=1, stride_axis=0)[:, :LANE] * LOG2E)
        return carry

    def const_block(i, carry):
        for h in range(n_heads):
            store(rows_ref[n_toeplitz + i], h, consts[h])
        return carry

    def trips(n):
        return max(u for u in range(1, TOEPLITZ_UNROLL + 1) if n % u == 0)

    if n_toeplitz:
        lax.fori_loop(0, n_toeplitz, toeplitz_block, 0, unroll=trips(n_toeplitz))
    if n_const:
        consts = [jnp.broadcast_to(frev_ref[h:h + 1, const_lane:const_lane + 1], (CHUNK, LANE)) * LOG2E
                  for h in range(n_heads)]
        lax.fori_loop(0, n_const, const_block, 0, unroll=trips(n_const))


def _build_bias(f_ext, rel_lo, rel, ok, *, kmins, const_below=None):
    n_layers, n_rel, n_heads = f_ext.shape
    n_rows = rel.shape[0]
    width = -(-n_rel // LANE) * LANE
    jj, ll = np.arange(CHUNK)[:, None], np.arange(LANE)[None, :]
    toeplitz_rows, shifts, const_rows = [], [], []
    for rb in range(n_rows // CHUNK):
        r, o = rel[rb * CHUNK:(rb + 1) * CHUNK], ok[rb * CHUNK:(rb + 1) * CHUNK]
        if not o.any() or (const_below is not None and r[o].max() <= const_below):
            const_rows.append(rb)
            continue
        bases = (r - jj + ll)[o]
        assert (bases == bases[0]).all(), "block is not Toeplitz on its valid entries"
        base = int(bases[0])
        k = (base + jj - ll - rel_lo)[o]
        assert k.min() >= 0 and k.max() < n_rel
        toeplitz_rows.append(rb)
        shifts.append((width - ((n_rel - 1) - (base - rel_lo))) % width)
    const_lane = 0 if const_below is None else (n_rel - 1) - (const_below - rel_lo)
    frev = jnp.pad(jnp.flip(f_ext, axis=1).transpose(0, 2, 1), ((0, 0), (0, 0), (0, width - n_rel)))
    kern = functools.partial(_toeplitz_kernel, n_heads=n_heads, kmins=tuple(kmins),
                             n_toeplitz=len(toeplitz_rows), n_const=len(const_rows), const_lane=const_lane)
    rows_arg = jnp.asarray(np.asarray(toeplitz_rows + const_rows, np.int32))
    return pl.pallas_call(
        kern,
        grid=(n_layers,),
        in_specs=[pl.BlockSpec(memory_space=pltpu.SMEM),
                  pl.BlockSpec(memory_space=pltpu.SMEM),
                  pl.BlockSpec((None, n_heads, width), lambda l: (l, 0, 0)),
                  pl.BlockSpec((n_rows, LANE), lambda l: (0, 0))],
        out_specs=pl.BlockSpec((1, len(kmins), n_rows, n_heads * LANE), lambda l: (l, 0, 0, 0)),
        out_shape=jax.ShapeDtypeStruct((n_layers, len(kmins), n_rows, n_heads * LANE), f32),
        compiler_params=pltpu.CompilerParams(dimension_semantics=("arbitrary",),
                                             vmem_limit_bytes=VMEM_LIMIT),
        name="bias_toeplitz",
    )(rows_arg, jnp.asarray(np.asarray(shifts or [0], np.int32)), frev, jnp.asarray(ok.astype(np.float32)))


def _wT_kernel(w_ref, o_ref):
    o_ref[...] = w_ref[...].T.astype(bf16)


def _transpose_weights(w, cols=8 * LANE):
    n_layers, k, n = w.shape
    return pl.pallas_call(
        _wT_kernel,
        grid=(n_layers, n // cols),
        in_specs=[pl.BlockSpec((None, k, cols), lambda l, c: (l, 0, c))],
        out_specs=pl.BlockSpec((None, cols, k), lambda l, c: (l, c, 0)),
        out_shape=jax.ShapeDtypeStruct((n_layers, n, k), bf16),
        compiler_params=pltpu.CompilerParams(dimension_semantics=("arbitrary", "arbitrary"),
                                             vmem_limit_bytes=VMEM_LIMIT),
        name="weights_T",
    )(w)


def _prompt_maps():
    i = np.arange(PAIR)[None, :]
    qi = i // CHUNK
    jb = np.arange(B_WIN)[:, None]
    rel_b = jb - B_KEEP - i
    ok_b = (jb // CHUNK >= qi) & (jb // CHUNK <= qi + B_PREV)
    jc = np.arange(C_WIN)[:, None]
    rel_c = jc - C_KEEP - i
    ok_c = (jc // CHUNK >= qi) & (jc // CHUNK <= qi + C_PREV)
    return rel_b, ok_b, rel_c, ok_c


def _sample_maps(lc_b, lc_c, t_new):
    ns = SAMPLE_STREAMS_PER_BLOCK
    c = np.arange(LANE)[None, :]
    sq, i = c // t_new, c % t_new

    def one(lc):
        r = np.arange(ns * lc + LANE)[:, None]
        is_new = r >= ns * lc
        sk = np.where(is_new, (r - ns * lc) // t_new, r // lc)
        j = np.where(is_new, lc + (r - ns * lc) % t_new, r % lc)
        return (j - lc - i).astype(np.int32), (sk == sq)

    return one(lc_b) + one(lc_c)


def _rms_rows(x, g):
    ms = jnp.mean(x * x, axis=-1, keepdims=True)
    return x * lax.rsqrt(ms + RMS_EPS) * g


def _rms_cols(xT, g_col):
    ms = jnp.mean(xT * xT, axis=0, keepdims=True)
    return xT * lax.rsqrt(ms + RMS_EPS) * g_col


def _project_T(pt_sc, wT_ref, h):
    for r0, r1 in _T_SECTIONS:
        pt_sc[r0:r1, :] = lax.dot_general(wT_ref[r0:r1, :], h, _NT, preferred_element_type=f32)


def _qblockdiag(qa, qb, lower):
    z = jnp.zeros((HEAD_DIM, LANE), bf16)
    if lower is None:
        return jnp.concatenate([jnp.concatenate([qa, z], axis=1),
                                jnp.concatenate([z, qb], axis=1)], axis=0)
    top = jnp.concatenate([qa, qb], axis=1)
    zz = jnp.zeros((HEAD_DIM, 2 * LANE), bf16)
    return jnp.concatenate([top, zz] if lower == 0 else [zz, top], axis=0)


def _with_ones(vT, group):
    ones = jnp.ones((ONES_ROWS, vT.shape[1]), vT.dtype)
    parts = []
    for r in range(0, vT.shape[0], group):
        parts += [vT[r:r + group, :], ones]
    return jnp.concatenate(parts, axis=0)


def _softmax_pv(s_m, sink, vwin):
    s, m = s_m
    if sink is not None:
        m = jnp.maximum(m, sink)
    e = jnp.exp2(s - m)
    o = jnp.dot(vwin, e.astype(bf16), preferred_element_type=f32)
    nv = vwin.shape[0] - ONES_ROWS
    den = o[nv:nv + 1, :]
    if sink is not None:
        den = den + jnp.exp2(sink - m)
    return o[:nv, :] * (1.0 / den)


def _scores(q_sc, cols, kwin_b, bias_b, kwin_c, bias_c):
    def head_q(r0, hh):
        return q_sc[r0 + hh * HEAD_DIM:r0 + (hh + 1) * HEAD_DIM, cols]

    def with_max(s):
        return s, jnp.max(s, axis=0, keepdims=True)

    s_b, s_c = [], []
    for cb in range(B_HEADS // 2):
        ha, hb = 2 * cb, 2 * cb + 1
        qbd = _qblockdiag(head_q(0, ha), head_q(0, hb), ha // B_GROUP)
        s = jnp.dot(kwin_b, qbd, preferred_element_type=f32)
        s_b.append(with_max(s + bias_b[:, cb * 2 * LANE:(cb + 1) * 2 * LANE]))
    for cb in range(C_HEADS // 2):
        ha, hb = 2 * cb, 2 * cb + 1
        qbd = _qblockdiag(head_q(B_WIDTH, ha), head_q(B_WIDTH, hb), None)
        s = jnp.dot(kwin_c[:, cb * LANE:(cb + 1) * LANE], qbd, preferred_element_type=f32)
        s_c.append(with_max(s + bias_c[:, cb * 2 * LANE:(cb + 1) * 2 * LANE]))
    return s_b, s_c


def _attend(pt_sc, yT_sc, cols, s_b, s_c, vwinT_b, vwinT_c, sink_ref):
    for cb in range(B_HEADS // 2):
        ha, hb = 2 * cb, 2 * cb + 1
        kv = ha // B_GROUP
        sink = sink_ref[:, cb * 2 * LANE:(cb + 1) * 2 * LANE] * LOG2E
        o = _softmax_pv(s_b[cb], sink, vwinT_b[kv * VB_STRIDE:(kv + 1) * VB_STRIDE, :])
        for j, hh in enumerate((ha, hb)):
            gate = pt_sc[T_BZ + hh * HEAD_DIM:T_BZ + (hh + 1) * HEAD_DIM, cols]
            yT_sc[Y_B + hh * HEAD_DIM:Y_B + (hh + 1) * HEAD_DIM, cols] = (
                o[:, j * LANE:(j + 1) * LANE] * gate).astype(bf16)
    for cb in range(C_HEADS // 2):
        ha, hb = 2 * cb, 2 * cb + 1
        o = _softmax_pv(s_c[cb], None, vwinT_c[cb * VC_STRIDE:(cb + 1) * VC_STRIDE, :])
        for j, hh in enumerate((ha, hb)):
            gate = pt_sc[T_CZ + hh * HEAD_DIM:T_CZ + (hh + 1) * HEAD_DIM, cols]
            yT_sc[Y_C + hh * HEAD_DIM:Y_C + (hh + 1) * HEAD_DIM, cols] = (
                o[j * HEAD_DIM:(j + 1) * HEAD_DIM, j * LANE:(j + 1) * LANE] * gate).astype(bf16)


def _prepare_rows(pt_sc, q_sc):
    for r0, r1 in ((T_BZ, T_CQ), (T_CZ, T_END)):
        pt_sc[r0:r1, :] = jax.nn.silu(pt_sc[r0:r1, :])
    q_sc[0:B_WIDTH, :] = (pt_sc[T_BQ:T_BK, :] * (QK_SCALE * LOG2E)).astype(bf16)
    q_sc[B_WIDTH:, :] = (pt_sc[T_CQ:T_CK, :] * (QK_SCALE * LOG2E)).astype(bf16)


def _finish(x, yT_sc, wout_ref, gpost_ref):
    out = lax.dot_general(yT_sc[...], wout_ref[...], _TN, preferred_element_type=f32)
    return x + _rms_rows(out, gpost_ref[...])


def _layer_spec(arr, layer):
    shape = arr.shape[1:]
    return pl.BlockSpec((None,) + shape, lambda *_: (layer,) + (0,) * len(shape),
                        pipeline_mode=pl.Buffered(1))


def _const_spec(arr):
    return pl.BlockSpec(arr.shape, lambda *_: (0,) * arr.ndim, pipeline_mode=pl.Buffered(1))


def _prompt_kernel(x_ref, gpre_ref, gpost_ref, wT_ref, wout_ref,
                   ang_ref, awsT_ref, abs_ref, sink_ref, biasb_ref, biasc_ref,
                   bk_all, bv_all, ck_all, cv_all,
                   xo_ref, bk_ref, bv_ref, ck_ref, cv_ref,
                   pt_sc, kb_ring, kc_ring, vbT_ring, vcT_ring, yT_sc, q_sc,
                   sb_even, mb_even, sc_even, mc_even, sb_odd, mb_odd, sc_odd, mc_odd):
    t = pl.program_id(1)
    ts = SEQ_TILE

    @pl.when(t == 0)
    def _():
        kb_ring[0:B_KEEP, :] = jnp.zeros((B_KEEP, B_KVW), bf16)
        kc_ring[0:C_KEEP, :] = jnp.zeros((C_KEEP, C_WIDTH), bf16)
        vbT_ring[:, 0:B_KEEP] = jnp.zeros((B_KV * VB_STRIDE, B_KEEP), bf16)
        vcT_ring[:, 0:C_KEEP] = jnp.zeros((C_HEADS // 2 * VC_STRIDE, C_KEEP), bf16)
        for g in range(B_KV):
            vbT_ring[g * VB_STRIDE + HEAD_DIM:(g + 1) * VB_STRIDE, :] = jnp.ones((ONES_ROWS, B_KEEP + ts), bf16)
        for g in range(C_HEADS // 2):
            vcT_ring[g * VC_STRIDE + 2 * HEAD_DIM:(g + 1) * VC_STRIDE, :] = jnp.ones((ONES_ROWS, C_KEEP + ts), bf16)

    x = x_ref[0]
    h = _rms_rows(x, gpre_ref[...]).astype(bf16)
    _project_T(pt_sc, wT_ref, h)
    kb_ring[B_KEEP:, :] = lax.dot_general(h, wT_ref[T_BK:T_BV, :], _NT,
                                          preferred_element_type=f32).astype(bf16)
    kc_ring[C_KEEP:, :] = lax.dot_general(h, wT_ref[T_CK:T_CV, :], _NT,
                                          preferred_element_type=f32).astype(bf16)
    for g in range(B_KV):
        vbT_ring[g * VB_STRIDE:g * VB_STRIDE + HEAD_DIM, B_KEEP:] = (
            pt_sc[T_BV + g * HEAD_DIM:T_BV + (g + 1) * HEAD_DIM, :].astype(bf16))
    for g in range(C_HEADS // 2):
        vcT_ring[g * VC_STRIDE:g * VC_STRIDE + 2 * HEAD_DIM, C_KEEP:] = (
            pt_sc[T_CV + g * 2 * HEAD_DIM:T_CV + (g + 1) * 2 * HEAD_DIM, :].astype(bf16))

    _prepare_rows(pt_sc, q_sc)

    vn = _rms_cols(pt_sc[T_AV:T_AV + A_WIDTH, :], ang_ref[...]).astype(bf16)
    n_ac = ts // A_CHUNK
    row = lax.broadcasted_iota(jnp.int32, (A_CHUNK, A_CHUNK), 0)
    col = lax.broadcasted_iota(jnp.int32, (A_CHUNK, A_CHUNK), 1)
    for g in range(A_GROUPS):
        rows = slice(g * A_GD, (g + 1) * A_GD)
        wgt = jnp.where(row <= col, awsT_ref[g], 0.0).astype(bf16)
        lhs = jnp.concatenate([vn[rows, c * A_CHUNK:(c + 1) * A_CHUNK] for c in range(n_ac)], axis=0)
        mix = jnp.dot(lhs, wgt, preferred_element_type=f32) + abs_ref[g]
        for c in range(n_ac):
            cols = slice(c * A_CHUNK, (c + 1) * A_CHUNK)
            u = pt_sc[T_AU + g * A_GD:T_AU + (g + 1) * A_GD, cols]
            z = pt_sc[T_AZ + g * A_GD:T_AZ + (g + 1) * A_GD, cols]
            yT_sc[Y_A + g * A_GD:Y_A + (g + 1) * A_GD, cols] = (
                u * mix[c * A_GD:(c + 1) * A_GD, :] * jax.nn.silu(z)).astype(bf16)

    n_pairs = ts // PAIR

    even, odd = (sb_even, mb_even, sc_even, mc_even), (sb_odd, mb_odd, sc_odd, mc_odd)

    def scores_into(p, bufs):
        sb_ref, mb_ref, sc_ref, mc_ref = bufs
        pg = t * n_pairs + p
        start = p * PAIR if isinstance(p, int) else pl.multiple_of(p * PAIR, PAIR)
        s_b, s_c = _scores(q_sc, pl.ds(start, PAIR),
                           kb_ring[pl.ds(start, B_WIN), :], biasb_ref.at[jnp.minimum(pg, 1)],
                           kc_ring[pl.ds(start, C_WIN), :], biasc_ref.at[jnp.minimum(pg, C_PREV // 2)])
        for cb, (v, m) in enumerate(s_b):
            sb_ref[cb] = v
            mb_ref[cb] = m
        for cb, (v, m) in enumerate(s_c):
            sc_ref[cb] = v
            mc_ref[cb] = m

    def attend_from(p, bufs):
        sb_ref, mb_ref, sc_ref, mc_ref = bufs
        start = p * PAIR if isinstance(p, int) else pl.multiple_of(p * PAIR, PAIR)
        _attend(pt_sc, yT_sc, pl.ds(start, PAIR),
                [(sb_ref[cb], mb_ref[cb]) for cb in range(B_HEADS // 2)],
                [(sc_ref[cb], mc_ref[cb]) for cb in range(C_HEADS // 2)],
                vbT_ring[:, pl.ds(start, B_WIN)], vcT_ring[:, pl.ds(start, C_WIN)], sink_ref)

    scores_into(0, even)

    def finish_rows(p):
        start = p * PAIR if isinstance(p, int) else pl.multiple_of(p * PAIR, PAIR)
        rows = pl.ds(start, PAIR)
        xo_ref[0, rows, :] = _finish(x_ref[0, rows, :], yT_sc.at[:, rows], wout_ref, gpost_ref)

    assert n_pairs == 4

    def first_pairs(q, carry):
        p = 2 * q
        scores_into(p + 1, odd)
        attend_from(p, even)
        scores_into(p + 2, even)
        finish_rows(p)
        attend_from(p + 1, odd)
        return carry

    def last_pairs(q, carry):
        p = 2 * q
        scores_into(p + 1, odd)
        finish_rows(p - 1)
        attend_from(p, even)
        finish_rows(p)
        attend_from(p + 1, odd)
        return carry

    n_full = jnp.minimum(t + n_pairs, n_pairs // 2 - 1)
    lax.fori_loop(0, n_full, first_pairs, 0)
    lax.fori_loop(n_full, n_full + 1, last_pairs, 0)
    finish_rows(n_pairs - 1)

    kb_ring[0:B_KEEP, :] = kb_ring[ts:ts + B_KEEP, :]
    kc_ring[0:C_KEEP, :] = kc_ring[ts:ts + C_KEEP, :]
    vbT_ring[:, 0:B_KEEP] = vbT_ring[:, ts:ts + B_KEEP]
    vcT_ring[:, 0:C_KEEP] = vcT_ring[:, ts:ts + C_KEEP]

    @pl.when(t == pl.num_programs(1) - 1)
    def _():
        h_last = _rms_rows(x_ref[0], gpre_ref[...]).astype(bf16)
        bk_ref[0] = lax.dot_general(wT_ref[T_BK:T_BV, :], h_last[ts - B_KEEP:, :], _NT,
                                    preferred_element_type=f32)
        ck_ref[0] = lax.dot_general(wT_ref[T_CK:T_CV, :], h_last[ts - C_KEEP:, :], _NT,
                                    preferred_element_type=f32)
        bv_ref[0] = pt_sc[T_BV:T_BZ, ts - B_KEEP:]
        cv_ref[0] = pt_sc[T_CV:T_CZ, ts - C_KEEP:]


def _with_absent(kernel_fn, n_before, n_absent):
    return lambda *refs: kernel_fn(*refs[:n_before], *([None] * n_absent), *refs[n_before:])


def _prompt_layer(layer, x, p, kv_all):
    nb, seq, d = x.shape
    ts = SEQ_TILE
    assert seq % ts == 0 and ts == C_KEEP
    kv_spec = lambda keep, width: pl.BlockSpec((None, 1, width, keep), lambda b, t: (layer, b, 0, 0))
    n_in = 11
    carried = [a for a in kv_all if not isinstance(a, jax.ShapeDtypeStruct)]
    assert len(carried) in (0, len(kv_all))
    kernel_fn = _prompt_kernel if carried else _with_absent(_prompt_kernel, n_in, len(kv_all))
    return pl.pallas_call(
        kernel_fn,
        grid=(nb, seq // ts),
        in_specs=[
            pl.BlockSpec((1, ts, d), lambda b, t: (b, t, 0)),
            _layer_spec(p["gpre"], layer), _layer_spec(p["gpost"], layer),
            _layer_spec(p["wT"], layer),
            _layer_spec(p["wout"], layer),
            _layer_spec(p["ang"], layer), _layer_spec(p["awsT"], layer), _layer_spec(p["abs"], layer),
            _layer_spec(p["sink"], layer), _const_spec(p["bias_pb"]), _layer_spec(p["bias_pc"], layer),
        ] + [pl.BlockSpec(memory_space=pl.ANY)] * len(carried),
        out_specs=[
            pl.BlockSpec((1, ts, d), lambda b, t: (b, t, 0)),
            kv_spec(B_KEEP, B_KVW), kv_spec(B_KEEP, B_KVW), kv_spec(C_KEEP, C_WIDTH), kv_spec(C_KEEP, C_WIDTH),
        ],
        out_shape=[jax.ShapeDtypeStruct((nb, seq, d), f32)]
        + [jax.ShapeDtypeStruct(a.shape, a.dtype) for a in kv_all],
        input_output_aliases={n_in + i: 1 + i for i in range(len(carried))},
        scratch_shapes=[
            pltpu.VMEM((T_END, ts), f32),
            pltpu.VMEM((B_KEEP + ts, B_KVW), bf16),
            pltpu.VMEM((C_KEEP + ts, C_WIDTH), bf16),
            pltpu.VMEM((B_KV * VB_STRIDE, B_KEEP + ts), bf16),
            pltpu.VMEM((C_HEADS // 2 * VC_STRIDE, C_KEEP + ts), bf16),
            pltpu.VMEM((A_WIDTH + B_WIDTH + C_WIDTH, ts), bf16),
            pltpu.VMEM((B_WIDTH + C_WIDTH, ts), bf16),
        ] + 2 * [
            pltpu.VMEM((B_HEADS // 2, B_WIN, 2 * LANE), f32), pltpu.VMEM((B_HEADS // 2, 1, 2 * LANE), f32),
            pltpu.VMEM((C_HEADS // 2, C_WIN, 2 * LANE), f32), pltpu.VMEM((C_HEADS // 2, 1, 2 * LANE), f32),
        ],
        compiler_params=pltpu.CompilerParams(dimension_semantics=("arbitrary", "arbitrary"),
                                             vmem_limit_bytes=VMEM_LIMIT),
        name="prompt_layer",
    )(x, p["gpre"], p["gpost"], p["wT"], p["wout"], p["ang"], p["awsT"],
      p["abs"], p["sink"], p["bias_pb"], p["bias_pc"], *carried)


def _sample_kernel(x_ref, cbk_ref, cbv_ref, cck_ref, ccv_ref, gpre_ref, gpost_ref, wT_ref,
                   wout_ref, ang_ref, aws_ref, arep_ref, akeep_ref, abias_ref,
                   sink_ref, biasb_ref, biasc_ref,
                   bk_all, bv_all, ck_all, cv_all, av_all,
                   xo_ref, bk_ref, bv_ref, ck_ref, cv_ref, av_ref,
                   pt_sc, yT_sc, q_sc):
    ns = SAMPLE_STREAMS_PER_BLOCK
    x = x_ref[...]
    ntok = x.shape[0]
    h = _rms_rows(x, gpre_ref[...]).astype(bf16)
    _project_T(pt_sc, wT_ref, h)
    kvn_b = lax.dot_general(h, wT_ref[T_BK:T_BZ, :], _NT, preferred_element_type=f32)
    kvn_c = lax.dot_general(h, wT_ref[T_CK:T_CZ, :], _NT, preferred_element_type=f32)
    kn_c = kvn_c[:, 0:C_WIDTH]
    bk_ref[...] = kvn_b[:, 0:B_KVW]
    bv_ref[...] = kvn_b[:, B_KVW:]
    ck_ref[...] = kn_c
    cv_ref[...] = kvn_c[:, C_WIDTH:]

    vn = _rms_cols(pt_sc[T_AV:T_AV + A_WIDTH, :], ang_ref[...])
    av_ref[...] = vn.T
    vnb = vn.astype(bf16)
    keep = akeep_ref[...] > 0.0
    rep = arep_ref[...]
    for g in range(A_GROUPS):
        rows = slice(g * A_GD, (g + 1) * A_GD)
        tiled = lax.dot_general(jnp.dot(rep, aws_ref[g].astype(bf16), preferred_element_type=f32).astype(bf16),
                                rep, _NT, preferred_element_type=f32)
        wgt = jnp.where(keep, tiled, 0.0).astype(bf16)
        mix = jnp.dot(vnb[rows, :], wgt, preferred_element_type=f32) + abias_ref[g]
        u = pt_sc[T_AU + g * A_GD:T_AU + (g + 1) * A_GD, :]
        z = pt_sc[T_AZ + g * A_GD:T_AZ + (g + 1) * A_GD, :]
        yT_sc[Y_A + g * A_GD:Y_A + (g + 1) * A_GD, :] = (u * mix * jax.nn.silu(z)).astype(bf16)

    _prepare_rows(pt_sc, q_sc)
    for blk in range(ntok // LANE):
        cols = slice(blk * LANE, (blk + 1) * LANE)
        streams = range(blk * ns, (blk + 1) * ns)
        kwin_b = jnp.concatenate([cbk_ref[s].T.astype(bf16) for s in streams]
                                 + [kvn_b[cols, 0:B_KVW].astype(bf16)], axis=0)
        vwinT_b = jnp.concatenate([cbv_ref[s].astype(bf16) for s in streams]
                                  + [pt_sc[T_BV:T_BV + B_KVW, cols].astype(bf16)], axis=1)
        kwin_c = jnp.concatenate([cck_ref[s].T.astype(bf16) for s in streams]
                                 + [kn_c[cols, :].astype(bf16)], axis=0)
        vwinT_c = jnp.concatenate([ccv_ref[s].astype(bf16) for s in streams]
                                  + [pt_sc[T_CV:T_CV + C_WIDTH, cols].astype(bf16)], axis=1)
        s_b, s_c = _scores(q_sc, cols, kwin_b, biasb_ref, kwin_c, biasc_ref)
        _attend(pt_sc, yT_sc, cols, s_b, s_c, _with_ones(vwinT_b, HEAD_DIM), _with_ones(vwinT_c, 2 * HEAD_DIM),
                sink_ref)

    xo_ref[...] = _finish(x, yT_sc, wout_ref, gpost_ref)


def _sample_layer(layer, x, caches, p, outs_all, *, t_new, tok_per_step):
    ntok, d = x.shape
    spb = tok_per_step // t_new
    cbk, cbv, cck, ccv = caches
    n_in = 17
    carried = [a for a in outs_all if not isinstance(a, jax.ShapeDtypeStruct)]
    assert len(carried) in (0, len(outs_all))
    kernel_fn = _sample_kernel if carried else _with_absent(_sample_kernel, n_in, len(outs_all))

    def rows(width):
        return pl.BlockSpec((tok_per_step, width), lambda i: (i, 0))

    def layer_rows(a):
        return pl.BlockSpec((None, tok_per_step, a.shape[2]), lambda i: (layer, i, 0))

    def cache(c):
        return pl.BlockSpec((None, spb) + c.shape[2:], lambda i: (layer, i, 0, 0))

    return pl.pallas_call(
        kernel_fn,
        grid=(ntok // tok_per_step,),
        in_specs=[
            rows(d), cache(cbk), cache(cbv), cache(cck), cache(ccv),
            _layer_spec(p["gpre"], layer), _layer_spec(p["gpost"], layer),
            _layer_spec(p["wT"], layer),
            _layer_spec(p["wout"], layer),
            _layer_spec(p["ang"], layer), _layer_spec(p["aws_s"], layer), _const_spec(p["arep"]),
            _const_spec(p["akeep"]), _layer_spec(p["abias_s"], layer),
            _layer_spec(p["sink"], layer), _const_spec(p["bias_sb"]), _layer_spec(p["bias_sc"], layer),
        ] + [pl.BlockSpec(memory_space=pl.ANY)] * len(carried),
        out_specs=[rows(d)] + [layer_rows(a) for a in outs_all],
        out_shape=[jax.ShapeDtypeStruct((ntok, d), f32)]
        + [jax.ShapeDtypeStruct(a.shape, a.dtype) for a in outs_all],
        input_output_aliases={n_in + i: 1 + i for i in range(len(carried))},
        scratch_shapes=[
            pltpu.VMEM((T_END, tok_per_step), f32),
            pltpu.VMEM((A_WIDTH + B_WIDTH + C_WIDTH, tok_per_step), bf16),
            pltpu.VMEM((B_WIDTH + C_WIDTH, tok_per_step), bf16),
        ],
        compiler_params=pltpu.CompilerParams(dimension_semantics=("arbitrary",),
                                             vmem_limit_bytes=VMEM_LIMIT),
        name="sample_layer",
    )(x, cbk, cbv, cck, ccv, p["gpre"], p["gpost"], p["wT"], p["wout"],
      p["ang"], p["aws_s"], p["arep"], p["akeep"], p["abias_s"], p["sink"], p["bias_sb"], p["bias_sc"],
      *carried)


def kernel(x_prompt, x_sample, cache_b_k, cache_b_v, cache_c_k, cache_c_v, g_pre, g_post, w_in, w_out,
           a_norm_g, a_ws, a_bs, b_sinks, c_rel_bias, t5_bias):
    depth = w_in.shape[0]
    nb, seq, d = x_prompt.shape
    ns_all, t_new, _ = x_sample.shape
    lc_b, lc_c = cache_b_k.shape[2], cache_c_k.shape[2]
    assert lc_b == B_KEEP and lc_c == C_KEEP and LANE % t_new == 0
    assert SAMPLE_STREAMS_PER_BLOCK * t_new == LANE
    tok_per_step = 2 * LANE
    reps = tok_per_step // t_new

    rel_pb, ok_pb, rel_pc, ok_pc = _prompt_maps()
    rel_sb, ok_sb, rel_sc, ok_sc = _sample_maps(lc_b, lc_c, t_new)
    b_lo = int(min(rel_pb.min(), rel_sb.min()))
    b_hi = int(max(rel_pb.max(), rel_sb.max()))
    t5_rel = t5_bias[_t5_bucket(jnp.arange(b_lo, b_hi + 1))][None]
    c_lo = -C_CLIP - LANE + 1
    c_hi = int(max(rel_pc.max(), rel_sc.max()))
    assert c_hi < C_CLIP
    c_rel = jnp.concatenate([jnp.broadcast_to(c_rel_bias[:, :1], (depth, -C_CLIP - c_lo, C_HEADS)),
                             c_rel_bias[:, :c_hi + C_CLIP + 1]], axis=1)
    r_ = np.arange(tok_per_step)
    awsT = a_ws.transpose(0, 1, 3, 2)

    p = dict(
        bias_pb=_build_bias(t5_rel, b_lo, rel_pb, ok_pb, kmins=(B_PREV, 0))[0],
        bias_sb=_build_bias(t5_rel, b_lo, rel_sb, ok_sb, kmins=(0,))[0, 0],
        bias_pc=_build_bias(c_rel, c_lo, rel_pc, ok_pc, const_below=-C_CLIP,
                            kmins=tuple(max(C_PREV - 2 * v, 0) for v in range(C_PREV // 2 + 1))),
        bias_sc=_build_bias(c_rel, c_lo, rel_sc, ok_sc, const_below=-C_CLIP, kmins=(0,))[:, 0],
        wT=_transpose_weights(w_in),
        wout=w_out.astype(bf16),
        gpre=g_pre[:, None, :],
        gpost=g_post[:, None, :],
        ang=a_norm_g[:, :, None],
        awsT=awsT,
        abs=a_bs[:, :, None, :],
        sink=jnp.repeat(b_sinks, LANE, axis=1)[:, None, :],
        aws_s=awsT[:, :, :t_new, :t_new],
        arep=jnp.asarray((r_[:, None] % t_new == np.arange(t_new)[None, :]).astype(np.float32), bf16),
        abias_s=jnp.tile(a_bs[:, :, None, :t_new], (1, 1, 1, reps)),
        akeep=jnp.asarray(((r_[:, None] // t_new == r_[None, :] // t_new)
                           & (r_[:, None] % t_new <= r_[None, :] % t_new)).astype(np.float32)),
    )

    xp = x_prompt
    xs = x_sample.reshape(ns_all * t_new, d)
    caches = tuple(c.transpose(0, 1, 3, 4, 2).reshape(depth, ns_all, -1, c.shape[2])
                   for c in (cache_b_k, cache_b_v, cache_c_k, cache_c_v))

    kv_all = tuple(jax.ShapeDtypeStruct((depth, nb, w, keep), f32)
                   for w, keep in ((B_KVW, B_KEEP), (B_KVW, B_KEEP), (C_WIDTH, C_KEEP), (C_WIDTH, C_KEEP)))
    s_all = tuple(jax.ShapeDtypeStruct((depth, ns_all * t_new, w), f32)
                  for w in (B_KVW, B_KVW, C_WIDTH, C_WIDTH, A_WIDTH))
    for l in range(depth):
        xp, *kv_all = _prompt_layer(l, xp, p, kv_all)
        xs, *s_all = _sample_layer(l, xs, caches, p, s_all, t_new=t_new, tok_per_step=tok_per_step)

    def heads_of(a, heads):
        return a.reshape(depth, ns_all, t_new, heads, HEAD_DIM)

    def from_fm(a, heads):
        return a.reshape(depth, nb, heads, HEAD_DIM, a.shape[-1]).transpose(0, 1, 4, 2, 3)

    pk, pv, pck, pcv = kv_all
    sk, sv, sck, scv, sav = s_all
    return (xp, xs.reshape(ns_all, t_new, d),
            from_fm(pk, B_KV), from_fm(pv, B_KV), from_fm(pck, C_HEADS), from_fm(pcv, C_HEADS),
            heads_of(sk, B_KV), heads_of(sv, B_KV), heads_of(sck, C_HEADS), heads_of(scv, C_HEADS),
            sav.reshape(depth, ns_all, t_new, A_WIDTH))
```

```python
import functools
import math

import numpy as np
import jax
import jax.numpy as jnp
from jax import lax
from jax.experimental import pallas as pl
from jax.experimental.pallas import tpu as pltpu

bf16 = jnp.bfloat16
f32 = jnp.float32

HEAD_DIM = 64
CHUNK = 64
A_WIDTH, A_GROUPS, A_CHUNK = 256, 4, 128
A_GD = A_WIDTH // A_GROUPS
B_HEADS, B_KV = 8, 2
B_GROUP = B_HEADS // B_KV
B_WIDTH, B_KVW = B_HEADS * HEAD_DIM, B_KV * HEAD_DIM
B_PREV = 2
C_HEADS = 4
C_WIDTH = C_HEADS * HEAD_DIM
C_PREV = 8
C_CLIP = 128
T5_BUCKETS, T5_MAX_DIST = 32, 128
RMS_EPS = 1e-6
NEG_INF = -1e30
QK_SCALE = HEAD_DIM ** -0.5
LOG2E = math.log2(math.e)

LANE = 128
PAIR = 2 * CHUNK
B_WIN = (B_PREV + 2) * CHUNK
C_WIN = (C_PREV + 2) * CHUNK
B_KEEP = B_PREV * CHUNK
C_KEEP = C_PREV * CHUNK
ONES_ROWS = 16
VB_STRIDE = HEAD_DIM + ONES_ROWS
VC_STRIDE = 2 * HEAD_DIM + ONES_ROWS
SEQ_TILE = 512
SAMPLE_STREAMS_PER_BLOCK = 4
VMEM_LIMIT = 56 * 1024 * 1024
TOEPLITZ_UNROLL = 17

_IN_SIZES = [A_WIDTH, A_WIDTH, A_WIDTH, B_WIDTH, B_KVW, B_KVW, B_WIDTH, C_WIDTH, C_WIDTH, C_WIDTH, C_WIDTH]
_IN_OFF = [int(v) for v in np.cumsum([0] + _IN_SIZES)]
(T_AU, T_AV, T_AZ, T_BQ, T_BK, T_BV, T_BZ, T_CQ, T_CK, T_CV, T_CZ, T_END) = _IN_OFF
_T_SECTIONS = ((T_AU, T_BK), (T_BV, T_CK), (T_CV, T_END))
Y_A, Y_B, Y_C = 0, A_WIDTH, A_WIDTH + B_WIDTH

_NT = (((1,), (1,)), ((), ()))
_TN = (((0,), (0,)), ((), ()))


def _t5_bucket(rel):
    half = T5_BUCKETS // 2
    max_exact = half // 2
    ret = jnp.where(rel > 0, half, 0)
    n = jnp.abs(rel)
    nf = jnp.maximum(n, 1).astype(jnp.float32)
    large = max_exact + (jnp.log(nf / max_exact) / math.log(T5_MAX_DIST / max_exact)
                         * (half - max_exact)).astype(jnp.int32)
    large = jnp.minimum(large, half - 1)
    return ret + jnp.where(n < max_exact, n, large)


def _toeplitz_kernel(rows_ref, shift_ref, frev_ref, ok_ref, out_ref, *, n_heads, kmins, n_toeplitz,
                     n_const, const_lane):
    width = frev_ref.shape[-1]
    neg = jnp.full((CHUNK, LANE), NEG_INF, f32)

    def store(rb, h, val):
        r0 = pl.multiple_of(rb * CHUNK, CHUNK)
        val = jnp.where(ok_ref[pl.ds(r0, CHUNK), :] > 0.0, val, NEG_INF)
        for v, kmin in enumerate(kmins):
            out_ref[0, v, pl.ds(r0, CHUNK), h * LANE:(h + 1) * LANE] = (
                val if kmin == 0 else jnp.where(rb >= kmin, val, neg))

    def toeplitz_block(i, carry):
        rolled = pltpu.roll(frev_ref[...], shift_ref[i], 1)
        ring = jnp.concatenate([rolled[:, :LANE], rolled[:, width - LANE:]], axis=1)
        for h in range(n_heads):
            x = jnp.broadcast_to(ring[h:h + 1, :], (CHUNK, 2 * LANE))
            store(rows_ref[i], h, pltpu.roll(x, 0, 1, stride=1, stride_axis=0)[:, :LANE] * LOG2E)
        return carry

    def const_block(i, carry):
        for h in range(n_heads):
            store(rows_ref[n_toeplitz + i], h, consts[h])
        return carry

    def trips(n):
        return max(u for u in range(1, TOEPLITZ_UNROLL + 1) if n % u == 0)

    if n_toeplitz:
        lax.fori_loop(0, n_toeplitz, toeplitz_block, 0, unroll=trips(n_toeplitz))
    if n_const:
        consts = [jnp.broadcast_to(frev_ref[h:h + 1, const_lane:const_lane + 1], (CHUNK, LANE)) * LOG2E
                  for h in range(n_heads)]
        lax.fori_loop(0, n_const, const_block, 0, unroll=trips(n_const))


def _build_bias(f_ext, rel_lo, rel, ok, *, kmins, const_below=None):
    n_layers, n_rel, n_heads = f_ext.shape
    n_rows = rel.shape[0]
    width = -(-n_rel // LANE) * LANE
    jj, ll = np.arange(CHUNK)[:, None], np.arange(LANE)[None, :]
    toeplitz_rows, shifts, const_rows = [], [], []
    for rb in range(n_rows // CHUNK):
        r, o = rel[rb * CHUNK:(rb + 1) * CHUNK], ok[rb * CHUNK:(rb + 1) * CHUNK]
        if not o.any() or (const_below is not None and r[o].max() <= const_below):
            const_rows.append(rb)
            continue
        bases = (r - jj + ll)[o]
        assert (bases == bases[0]).all(), "block is not Toeplitz on its valid entries"
        base = int(bases[0])
        k = (base + jj - ll - rel_lo)[o]
        assert k.min() >= 0 and k.max() < n_rel
        toeplitz_rows.append(rb)
        shifts.append((width - ((n_rel - 1) - (base - rel_lo))) % width)
    const_lane = 0 if const_below is None else (n_rel - 1) - (const_below - rel_lo)
    frev = jnp.pad(jnp.flip(f_ext, axis=1).transpose(0, 2, 1), ((0, 0), (0, 0), (0, width - n_rel)))
    kern = functools.partial(_toeplitz_kernel, n_heads=n_heads, kmins=tuple(kmins),
                             n_toeplitz=len(toeplitz_rows), n_const=len(const_rows), const_lane=const_lane)
    rows_arg = jnp.asarray(np.asarray(toeplitz_rows + const_rows, np.int32))
    return pl.pallas_call(
        kern,
        grid=(n_layers,),
        in_specs=[pl.BlockSpec(memory_space=pltpu.SMEM),
                  pl.BlockSpec(memory_space=pltpu.SMEM),
                  pl.BlockSpec((None, n_heads, width), lambda l: (l, 0, 0)),
                  pl.BlockSpec((n_rows, LANE), lambda l: (0, 0))],
        out_specs=pl.BlockSpec((1, len(kmins), n_rows, n_heads * LANE), lambda l: (l, 0, 0, 0)),
        out_shape=jax.ShapeDtypeStruct((n_layers, len(kmins), n_rows, n_heads * LANE), f32),
        compiler_params=pltpu.CompilerParams(dimension_semantics=("arbitrary",),
                                             vmem_limit_bytes=VMEM_LIMIT),
        name="bias_toeplitz",
    )(rows_arg, jnp.asarray(np.asarray(shifts or [0], np.int32)), frev, jnp.asarray(ok.astype(np.float32)))


def _wT_kernel(w_ref, o_ref):
    o_ref[...] = w_ref[...].T.astype(bf16)


def _transpose_weights(w, cols=8 * LANE):
    n_layers, k, n = w.shape
    return pl.pallas_call(
        _wT_kernel,
        grid=(n_layers, n // cols),
        in_specs=[pl.BlockSpec((None, k, cols), lambda l, c: (l, 0, c))],
        out_specs=pl.BlockSpec((None, cols, k), lambda l, c: (l, c, 0)),
        out_shape=jax.ShapeDtypeStruct((n_layers, n, k), bf16),
        compiler_params=pltpu.CompilerParams(dimension_semantics=("arbitrary", "arbitrary"),
                                             vmem_limit_bytes=VMEM_LIMIT),
        name="weights_T",
    )(w)


def _prompt_maps():
    i = np.arange(PAIR)[None, :]
    qi = i // CHUNK
    jb = np.arange(B_WIN)[:, None]
    rel_b = jb - B_KEEP - i
    ok_b = (jb // CHUNK >= qi) & (jb // CHUNK <= qi + B_PREV)
    jc = np.arange(C_WIN)[:, None]
    rel_c = jc - C_KEEP - i
    ok_c = (jc // CHUNK >= qi) & (jc // CHUNK <= qi + C_PREV)
    return rel_b, ok_b, rel_c, ok_c


def _sample_maps(lc_b, lc_c, t_new):
    ns = SAMPLE_STREAMS_PER_BLOCK
    c = np.arange(LANE)[None, :]
    sq, i = c // t_new, c % t_new

    def one(lc):
        r = np.arange(ns * lc + LANE)[:, None]
        is_new = r >= ns * lc
        sk = np.where(is_new, (r - ns * lc) // t_new, r // lc)
        j = np.where(is_new, lc + (r - ns * lc) % t_new, r % lc)
        return (j - lc - i).astype(np.int32), (sk == sq)

    return one(lc_b) + one(lc_c)


def _rms_rows(x, g):
    ms = jnp.mean(x * x, axis=-1, keepdims=True)
    return x * lax.rsqrt(ms + RMS_EPS) * g


def _rms_cols(xT, g_col):
    ms = jnp.mean(xT * xT, axis=0, keepdims=True)
    return xT * lax.rsqrt(ms + RMS_EPS) * g_col


def _project_T(pt_sc, wT_ref, h):
    for r0, r1 in _T_SECTIONS:
        pt_sc[r0:r1, :] = lax.dot_general(wT_ref[r0:r1, :], h, _NT, preferred_element_type=f32)


def _qblockdiag(qa, qb, lower):
    z = jnp.zeros((HEAD_DIM, LANE), bf16)
    if lower is None:
        return jnp.concatenate([jnp.concatenate([qa, z], axis=1),
                                jnp.concatenate([z, qb], axis=1)], axis=0)
    top = jnp.concatenate([qa, qb], axis=1)
    zz = jnp.zeros((HEAD_DIM, 2 * LANE), bf16)
    return jnp.concatenate([top, zz] if lower == 0 else [zz, top], axis=0)


def _with_ones(vT, group):
    ones = jnp.ones((ONES_ROWS, vT.shape[1]), vT.dtype)
    parts = []
    for r in range(0, vT.shape[0], group):
        parts += [vT[r:r + group, :], ones]
    return jnp.concatenate(parts, axis=0)


def _softmax_pv(s_m, sink, vwin):
    s, m = s_m
    if sink is not None:
        m = jnp.maximum(m, sink)
    e = jnp.exp2(s - m)
    o = jnp.dot(vwin, e.astype(bf16), preferred_element_type=f32)
    nv = vwin.shape[0] - ONES_ROWS
    den = o[nv:nv + 1, :]
    if sink is not None:
        den = den + jnp.exp2(sink - m)
    return o[:nv, :] * (1.0 / den)


def _scores(q_sc, cols, kwin_b, bias_b, kwin_c, bias_c):
    def head_q(r0, hh):
        return q_sc[r0 + hh * HEAD_DIM:r0 + (hh + 1) * HEAD_DIM, cols]

    def with_max(s):
        return s, jnp.max(s, axis=0, keepdims=True)

    s_b, s_c = [], []
    for cb in range(B_HEADS // 2):
        ha, hb = 2 * cb, 2 * cb + 1
        qbd = _qblockdiag(head_q(0, ha), head_q(0, hb), ha // B_GROUP)
        s = jnp.dot(kwin_b, qbd, preferred_element_type=f32)
        s_b.append(with_max(s + bias_b[:, cb * 2 * LANE:(cb + 1) * 2 * LANE]))
    for cb in range(C_HEADS // 2):
        ha, hb = 2 * cb, 2 * cb + 1
        qbd = _qblockdiag(head_q(B_WIDTH, ha), head_q(B_WIDTH, hb), None)
        s = jnp.dot(kwin_c[:, cb * LANE:(cb + 1) * LANE], qbd, preferred_element_type=f32)
        s_c.append(with_max(s + bias_c[:, cb * 2 * LANE:(cb + 1) * 2 * LANE]))
    return s_b, s_c


def _attend(pt_sc, yT_sc, cols, s_b, s_c, vwinT_b, vwinT_c, sink_ref):
    for cb in range(B_HEADS // 2):
        ha, hb = 2 * cb, 2 * cb + 1
        kv = ha // B_GROUP
        sink = sink_ref[:, cb * 2 * LANE:(cb + 1) * 2 * LANE] * LOG2E
        o = _softmax_pv(s_b[cb], sink, vwinT_b[kv * VB_STRIDE:(kv + 1) * VB_STRIDE, :])
        for j, hh in enumerate((ha, hb)):
            gate = pt_sc[T_BZ + hh * HEAD_DIM:T_BZ + (hh + 1) * HEAD_DIM, cols]
            yT_sc[Y_B + hh * HEAD_DIM:Y_B + (hh + 1) * HEAD_DIM, cols] = (
                o[:, j * LANE:(j + 1) * LANE] * gate).astype(bf16)
    for cb in range(C_HEADS // 2):
        ha, hb = 2 * cb, 2 * cb + 1
        o = _softmax_pv(s_c[cb], None, vwinT_c[cb * VC_STRIDE:(cb + 1) * VC_STRIDE, :])
        for j, hh in enumerate((ha, hb)):
            gate = pt_sc[T_CZ + hh * HEAD_DIM:T_CZ + (hh + 1) * HEAD_DIM, cols]
            yT_sc[Y_C + hh * HEAD_DIM:Y_C + (hh + 1) * HEAD_DIM, cols] = (
                o[j * HEAD_DIM:(j + 1) * HEAD_DIM, j * LANE:(j + 1) * LANE] * gate).astype(bf16)


def _prepare_rows(pt_sc, q_sc):
    for r0, r1 in ((T_BZ, T_CQ), (T_CZ, T_END)):
        pt_sc[r0:r1, :] = jax.nn.silu(pt_sc[r0:r1, :])
    q_sc[0:B_WIDTH, :] = (pt_sc[T_BQ:T_BK, :] * (QK_SCALE * LOG2E)).astype(bf16)
    q_sc[B_WIDTH:, :] = (pt_sc[T_CQ:T_CK, :] * (QK_SCALE * LOG2E)).astype(bf16)


def _finish(x, yT_sc, wout_ref, gpost_ref):
    out = lax.dot_general(yT_sc[...], wout_ref[...], _TN, preferred_element_type=f32)
    return x + _rms_rows(out, gpost_ref[...])


def _layer_spec(arr, layer):
    shape = arr.shape[1:]
    return pl.BlockSpec((None,) + shape, lambda *_: (layer,) + (0,) * len(shape),
                        pipeline_mode=pl.Buffered(1))


def _const_spec(arr):
    return pl.BlockSpec(arr.shape, lambda *_: (0,) * arr.ndim, pipeline_mode=pl.Buffered(1))


def _prompt_kernel(x_ref, gpre_ref, gpost_ref, wT_ref, wout_ref,
                   ang_ref, awsT_ref, abs_ref, sink_ref, biasb_ref, biasc_ref,
                   bk_all, bv_all, ck_all, cv_all,
                   xo_ref, bk_ref, bv_ref, ck_ref, cv_ref,
                   pt_sc, kb_ring, kc_ring, vbT_ring, vcT_ring, yT_sc, q_sc,
                   sb_even, mb_even, sc_even, mc_even, sb_odd, mb_odd, sc_odd, mc_odd):
    t = pl.program_id(1)
    ts = SEQ_TILE

    @pl.when(t == 0)
    def _():
        kb_ring[0:B_KEEP, :] = jnp.zeros((B_KEEP, B_KVW), bf16)
        kc_ring[0:C_KEEP, :] = jnp.zeros((C_KEEP, C_WIDTH), bf16)
        vbT_ring[:, 0:B_KEEP] = jnp.zeros((B_KV * VB_STRIDE, B_KEEP), bf16)
        vcT_ring[:, 0:C_KEEP] = jnp.zeros((C_HEADS // 2 * VC_STRIDE, C_KEEP), bf16)
        for g in range(B_KV):
            vbT_ring[g * VB_STRIDE + HEAD_DIM:(g + 1) * VB_STRIDE, :] = jnp.ones((ONES_ROWS, B_KEEP + ts), bf16)
        for g in range(C_HEADS // 2):
            vcT_ring[g * VC_STRIDE + 2 * HEAD_DIM:(g + 1) * VC_STRIDE, :] = jnp.ones((ONES_ROWS, C_KEEP + ts), bf16)

    x = x_ref[0]
    h = _rms_rows(x, gpre_ref[...]).astype(bf16)
    _project_T(pt_sc, wT_ref, h)
    kb_ring[B_KEEP:, :] = lax.dot_general(h, wT_ref[T_BK:T_BV, :], _NT,
                                          preferred_element_type=f32).astype(bf16)
    kc_ring[C_KEEP:, :] = lax.dot_general(h, wT_ref[T_CK:T_CV, :], _NT,
                                          preferred_element_type=f32).astype(bf16)
    for g in range(B_KV):
        vbT_ring[g * VB_STRIDE:g * VB_STRIDE + HEAD_DIM, B_KEEP:] = (
            pt_sc[T_BV + g * HEAD_DIM:T_BV + (g + 1) * HEAD_DIM, :].astype(bf16))
    for g in range(C_HEADS // 2):
        vcT_ring[g * VC_STRIDE:g * VC_STRIDE + 2 * HEAD_DIM, C_KEEP:] = (
            pt_sc[T_CV + g * 2 * HEAD_DIM:T_CV + (g + 1) * 2 * HEAD_DIM, :].astype(bf16))

    _prepare_rows(pt_sc, q_sc)

    vn = _rms_cols(pt_sc[T_AV:T_AV + A_WIDTH, :], ang_ref[...]).astype(bf16)
    n_ac = ts // A_CHUNK
    row = lax.broadcasted_iota(jnp.int32, (A_CHUNK, A_CHUNK), 0)
    col = lax.broadcasted_iota(jnp.int32, (A_CHUNK, A_CHUNK), 1)
    for g in range(A_GROUPS):
        rows = slice(g * A_GD, (g + 1) * A_GD)
        wgt = jnp.where(row <= col, awsT_ref[g], 0.0).astype(bf16)
        lhs = jnp.concatenate([vn[rows, c * A_CHUNK:(c + 1) * A_CHUNK] for c in range(n_ac)], axis=0)
        mix = jnp.dot(lhs, wgt, preferred_element_type=f32) + abs_ref[g]
        for c in range(n_ac):
            cols = slice(c * A_CHUNK, (c + 1) * A_CHUNK)
            u = pt_sc[T_AU + g * A_GD:T_AU + (g + 1) * A_GD, cols]
            z = pt_sc[T_AZ + g * A_GD:T_AZ + (g + 1) * A_GD, cols]
            yT_sc[Y_A + g * A_GD:Y_A + (g + 1) * A_GD, cols] = (
                u * mix[c * A_GD:(c + 1) * A_GD, :] * jax.nn.silu(z)).astype(bf16)

    n_pairs = ts // PAIR

    even, odd = (sb_even, mb_even, sc_even, mc_even), (sb_odd, mb_odd, sc_odd, mc_odd)

    def scores_into(p, bufs):
        sb_ref, mb_ref, sc_ref, mc_ref = bufs
        pg = t * n_pairs + p
        start = p * PAIR if isinstance(p, int) else pl.multiple_of(p * PAIR, PAIR)
        s_b, s_c = _scores(q_sc, pl.ds(start, PAIR),
                           kb_ring[pl.ds(start, B_WIN), :], biasb_ref.at[jnp.minimum(pg, 1)],
                           kc_ring[pl.ds(start, C_WIN), :], biasc_ref.at[jnp.minimum(pg, C_PREV // 2)])
        for cb, (v, m) in enumerate(s_b):
            sb_ref[cb] = v
            mb_ref[cb] = m
        for cb, (v, m) in enumerate(s_c):
            sc_ref[cb] = v
            mc_ref[cb] = m

    def attend_from(p, bufs):
        sb_ref, mb_ref, sc_ref, mc_ref = bufs
        start = p * PAIR if isinstance(p, int) else pl.multiple_of(p * PAIR, PAIR)
        _attend(pt_sc, yT_sc, pl.ds(start, PAIR),
                [(sb_ref[cb], mb_ref[cb]) for cb in range(B_HEADS // 2)],
                [(sc_ref[cb], mc_ref[cb]) for cb in range(C_HEADS // 2)],
                vbT_ring[:, pl.ds(start, B_WIN)], vcT_ring[:, pl.ds(start, C_WIN)], sink_ref)

    scores_into(0, even)

    def finish_rows(p):
        start = p * PAIR if isinstance(p, int) else pl.multiple_of(p * PAIR, PAIR)
        rows = pl.ds(start, PAIR)
        xo_ref[0, rows, :] = _finish(x_ref[0, rows, :], yT_sc.at[:, rows], wout_ref, gpost_ref)

    assert n_pairs == 4

    def first_pairs(q, carry):
        p = 2 * q
        scores_into(p + 1, odd)
        attend_from(p, even)
        scores_into(p + 2, even)
        finish_rows(p)
        attend_from(p + 1, odd)
        return carry

    def last_pairs(q, carry):
        p = 2 * q
        scores_into(p + 1, odd)
        finish_rows(p - 1)
        attend_from(p, even)
        finish_rows(p)
        attend_from(p + 1, odd)
        return carry

    n_full = jnp.minimum(t + n_pairs, n_pairs // 2 - 1)
    lax.fori_loop(0, n_full, first_pairs, 0)
    lax.fori_loop(n_full, n_full + 1, last_pairs, 0)
    finish_rows(n_pairs - 1)

    kb_ring[0:B_KEEP, :] = kb_ring[ts:ts + B_KEEP, :]
    kc_ring[0:C_KEEP, :] = kc_ring[ts:ts + C_KEEP, :]
    vbT_ring[:, 0:B_KEEP] = vbT_ring[:, ts:ts + B_KEEP]
    vcT_ring[:, 0:C_KEEP] = vcT_ring[:, ts:ts + C_KEEP]

    @pl.when(t == pl.num_programs(1) - 1)
    def _():
        h_last = _rms_rows(x_ref[0], gpre_ref[...]).astype(bf16)
        bk_ref[0] = lax.dot_general(wT_ref[T_BK:T_BV, :], h_last[ts - B_KEEP:, :], _NT,
                                    preferred_element_type=f32)
        ck_ref[0] = lax.dot_general(wT_ref[T_CK:T_CV, :], h_last[ts - C_KEEP:, :], _NT,
                                    preferred_element_type=f32)
        bv_ref[0] = pt_sc[T_BV:T_BZ, ts - B_KEEP:]
        cv_ref[0] = pt_sc[T_CV:T_CZ, ts - C_KEEP:]


def _prompt_layer(layer, x, p, kv_all):
    nb, seq, d = x.shape
    ts = SEQ_TILE
    assert seq % ts == 0 and ts == C_KEEP
    kv_spec = lambda keep, width: pl.BlockSpec((None, 1, width, keep), lambda b, t: (layer, b, 0, 0))
    n_in = 11
    carried, kernel_fn = list(kv_all), _prompt_kernel
    return pl.pallas_call(
        kernel_fn,
        grid=(nb, seq // ts),
        in_specs=[
            pl.BlockSpec((1, ts, d), lambda b, t: (b, t, 0)),
            _layer_spec(p["gpre"], layer), _layer_spec(p["gpost"], layer),
            _layer_spec(p["wT"], layer),
            _layer_spec(p["wout"], layer),
            _layer_spec(p["ang"], layer), _layer_spec(p["awsT"], layer), _layer_spec(p["abs"], layer),
            _layer_spec(p["sink"], layer), _const_spec(p["bias_pb"]), _layer_spec(p["bias_pc"], layer),
        ] + [pl.BlockSpec(memory_space=pl.ANY)] * len(carried),
        out_specs=[
            pl.BlockSpec((1, ts, d), lambda b, t: (b, t, 0)),
            kv_spec(B_KEEP, B_KVW), kv_spec(B_KEEP, B_KVW), kv_spec(C_KEEP, C_WIDTH), kv_spec(C_KEEP, C_WIDTH),
        ],
        out_shape=[jax.ShapeDtypeStruct((nb, seq, d), f32)]
        + [jax.ShapeDtypeStruct(a.shape, a.dtype) for a in kv_all],
        input_output_aliases={n_in + i: 1 + i for i in range(len(carried))},
        scratch_shapes=[
            pltpu.VMEM((T_END, ts), f32),
            pltpu.VMEM((B_KEEP + ts, B_KVW), bf16),
            pltpu.VMEM((C_KEEP + ts, C_WIDTH), bf16),
            pltpu.VMEM((B_KV * VB_STRIDE, B_KEEP + ts), bf16),
            pltpu.VMEM((C_HEADS // 2 * VC_STRIDE, C_KEEP + ts), bf16),
            pltpu.VMEM((A_WIDTH + B_WIDTH + C_WIDTH, ts), bf16),
            pltpu.VMEM((B_WIDTH + C_WIDTH, ts), bf16),
        ] + 2 * [
            pltpu.VMEM((B_HEADS // 2, B_WIN, 2 * LANE), f32), pltpu.VMEM((B_HEADS // 2, 1, 2 * LANE), f32),
            pltpu.VMEM((C_HEADS // 2, C_WIN, 2 * LANE), f32), pltpu.VMEM((C_HEADS // 2, 1, 2 * LANE), f32),
        ],
        compiler_params=pltpu.CompilerParams(dimension_semantics=("arbitrary", "arbitrary"),
                                             vmem_limit_bytes=VMEM_LIMIT),
        name="prompt_layer",
    )(x, p["gpre"], p["gpost"], p["wT"], p["wout"], p["ang"], p["awsT"],
      p["abs"], p["sink"], p["bias_pb"], p["bias_pc"], *carried)


def _sample_kernel(x_ref, cbk_ref, cbv_ref, cck_ref, ccv_ref, gpre_ref, gpost_ref, wT_ref,
                   wout_ref, ang_ref, aws_ref, arep_ref, akeep_ref, abias_ref,
                   sink_ref, biasb_ref, biasc_ref,
                   bk_all, bv_all, ck_all, cv_all, av_all,
                   xo_ref, bk_ref, bv_ref, ck_ref, cv_ref, av_ref,
                   pt_sc, yT_sc, q_sc):
    ns = SAMPLE_STREAMS_PER_BLOCK
    x = x_ref[...]
    ntok = x.shape[0]
    h = _rms_rows(x, gpre_ref[...]).astype(bf16)
    _project_T(pt_sc, wT_ref, h)
    kvn_b = lax.dot_general(h, wT_ref[T_BK:T_BZ, :], _NT, preferred_element_type=f32)
    kvn_c = lax.dot_general(h, wT_ref[T_CK:T_CZ, :], _NT, preferred_element_type=f32)
    kn_c = kvn_c[:, 0:C_WIDTH]
    bk_ref[...] = kvn_b[:, 0:B_KVW]
    bv_ref[...] = kvn_b[:, B_KVW:]
    ck_ref[...] = kn_c
    cv_ref[...] = kvn_c[:, C_WIDTH:]

    vn = _rms_cols(pt_sc[T_AV:T_AV + A_WIDTH, :], ang_ref[...])
    av_ref[...] = vn.T
    vnb = vn.astype(bf16)
    keep = akeep_ref[...] > 0.0
    rep = arep_ref[...]
    for g in range(A_GROUPS):
        rows = slice(g * A_GD, (g + 1) * A_GD)
        tiled = lax.dot_general(jnp.dot(rep, aws_ref[g].astype(bf16), preferred_element_type=f32).astype(bf16),
                                rep, _NT, preferred_element_type=f32)
        wgt = jnp.where(keep, tiled, 0.0).astype(bf16)
        mix = jnp.dot(vnb[rows, :], wgt, preferred_element_type=f32) + abias_ref[g]
        u = pt_sc[T_AU + g * A_GD:T_AU + (g + 1) * A_GD, :]
        z = pt_sc[T_AZ + g * A_GD:T_AZ + (g + 1) * A_GD, :]
        yT_sc[Y_A + g * A_GD:Y_A + (g + 1) * A_GD, :] = (u * mix * jax.nn.silu(z)).astype(bf16)

    _prepare_rows(pt_sc, q_sc)
    for blk in range(ntok // LANE):
        cols = slice(blk * LANE, (blk + 1) * LANE)
        streams = range(blk * ns, (blk + 1) * ns)
        kwin_b = jnp.concatenate([cbk_ref[s].T.astype(bf16) for s in streams]
                                 + [kvn_b[cols, 0:B_KVW].astype(bf16)], axis=0)
        vwinT_b = jnp.concatenate([cbv_ref[s].astype(bf16) for s in streams]
                                  + [pt_sc[T_BV:T_BV + B_KVW, cols].astype(bf16)], axis=1)
        kwin_c = jnp.concatenate([cck_ref[s].T.astype(bf16) for s in streams]
                                 + [kn_c[cols, :].astype(bf16)], axis=0)
        vwinT_c = jnp.concatenate([ccv_ref[s].astype(bf16) for s in streams]
                                  + [pt_sc[T_CV:T_CV + C_WIDTH, cols].astype(bf16)], axis=1)
        s_b, s_c = _scores(q_sc, cols, kwin_b, biasb_ref, kwin_c, biasc_ref)
        _attend(pt_sc, yT_sc, cols, s_b, s_c, _with_ones(vwinT_b, HEAD_DIM), _with_ones(vwinT_c, 2 * HEAD_DIM),
                sink_ref)

    xo_ref[...] = _finish(x, yT_sc, wout_ref, gpost_ref)


def _sample_layer(layer, x, caches, p, outs_all, *, t_new, tok_per_step):
    ntok, d = x.shape
    spb = tok_per_step // t_new
    cbk, cbv, cck, ccv = caches
    n_in = 17
    carried, kernel_fn = list(outs_all), _sample_kernel

    def rows(width):
        return pl.BlockSpec((tok_per_step, width), lambda i: (i, 0))

    def layer_rows(a):
        return pl.BlockSpec((None, tok_per_step, a.shape[2]), lambda i: (layer, i, 0))

    def cache(c):
        return pl.BlockSpec((None, spb) + c.shape[2:], lambda i: (layer, i, 0, 0))

    return pl.pallas_call(
        kernel_fn,
        grid=(ntok // tok_per_step,),
        in_specs=[
            rows(d), cache(cbk), cache(cbv), cache(cck), cache(ccv),
            _layer_spec(p["gpre"], layer), _layer_spec(p["gpost"], layer),
            _layer_spec(p["wT"], layer),
            _layer_spec(p["wout"], layer),
            _layer_spec(p["ang"], layer), _layer_spec(p["aws_s"], layer), _const_spec(p["arep"]),
            _const_spec(p["akeep"]), _layer_spec(p["abias_s"], layer),
            _layer_spec(p["sink"], layer), _const_spec(p["bias_sb"]), _layer_spec(p["bias_sc"], layer),
        ] + [pl.BlockSpec(memory_space=pl.ANY)] * len(carried),
        out_specs=[rows(d)] + [layer_rows(a) for a in outs_all],
        out_shape=[jax.ShapeDtypeStruct((ntok, d), f32)]
        + [jax.ShapeDtypeStruct(a.shape, a.dtype) for a in outs_all],
        input_output_aliases={n_in + i: 1 + i for i in range(len(carried))},
        scratch_shapes=[
            pltpu.VMEM((T_END, tok_per_step), f32),
            pltpu.VMEM((A_WIDTH + B_WIDTH + C_WIDTH, tok_per_step), bf16),
            pltpu.VMEM((B_WIDTH + C_WIDTH, tok_per_step), bf16),
        ],
        compiler_params=pltpu.CompilerParams(dimension_semantics=("arbitrary",),
                                             vmem_limit_bytes=VMEM_LIMIT),
        name="sample_layer",
    )(x, cbk, cbv, cck, ccv, p["gpre"], p["gpost"], p["wT"], p["wout"],
      p["ang"], p["aws_s"], p["arep"], p["akeep"], p["abias_s"], p["sink"], p["bias_sb"], p["bias_sc"],
      *carried)


def kernel(x_prompt, x_sample, cache_b_k, cache_b_v, cache_c_k, cache_c_v, g_pre, g_post, w_in, w_out,
           a_norm_g, a_ws, a_bs, b_sinks, c_rel_bias, t5_bias):
    depth = w_in.shape[0]
    nb, seq, d = x_prompt.shape
    ns_all, t_new, _ = x_sample.shape
    lc_b, lc_c = cache_b_k.shape[2], cache_c_k.shape[2]
    assert lc_b == B_KEEP and lc_c == C_KEEP and LANE % t_new == 0
    assert SAMPLE_STREAMS_PER_BLOCK * t_new == LANE
    tok_per_step = 2 * LANE
    reps = tok_per_step // t_new

    rel_pb, ok_pb, rel_pc, ok_pc = _prompt_maps()
    rel_sb, ok_sb, rel_sc, ok_sc = _sample_maps(lc_b, lc_c, t_new)
    b_lo = int(min(rel_pb.min(), rel_sb.min()))
    b_hi = int(max(rel_pb.max(), rel_sb.max()))
    t5_rel = t5_bias[_t5_bucket(jnp.arange(b_lo, b_hi + 1))][None]
    c_lo = -C_CLIP - LANE + 1
    c_hi = int(max(rel_pc.max(), rel_sc.max()))
    assert c_hi < C_CLIP
    c_rel = jnp.concatenate([jnp.broadcast_to(c_rel_bias[:, :1], (depth, -C_CLIP - c_lo, C_HEADS)),
                             c_rel_bias[:, :c_hi + C_CLIP + 1]], axis=1)
    r_ = np.arange(tok_per_step)
    awsT = a_ws.transpose(0, 1, 3, 2)

    p = dict(
        bias_pb=_build_bias(t5_rel, b_lo, rel_pb, ok_pb, kmins=(B_PREV, 0))[0],
        bias_sb=_build_bias(t5_rel, b_lo, rel_sb, ok_sb, kmins=(0,))[0, 0],
        bias_pc=_build_bias(c_rel, c_lo, rel_pc, ok_pc, const_below=-C_CLIP,
                            kmins=tuple(max(C_PREV - 2 * v, 0) for v in range(C_PREV // 2 + 1))),
        bias_sc=_build_bias(c_rel, c_lo, rel_sc, ok_sc, const_below=-C_CLIP, kmins=(0,))[:, 0],
        wT=_transpose_weights(w_in),
        wout=w_out.astype(bf16),
        gpre=g_pre[:, None, :],
        gpost=g_post[:, None, :],
        ang=a_norm_g[:, :, None],
        awsT=awsT,
        abs=a_bs[:, :, None, :],
        sink=jnp.repeat(b_sinks, LANE, axis=1)[:, None, :],
        aws_s=awsT[:, :, :t_new, :t_new],
        arep=jnp.asarray((r_[:, None] % t_new == np.arange(t_new)[None, :]).astype(np.float32), bf16),
        abias_s=jnp.tile(a_bs[:, :, None, :t_new], (1, 1, 1, reps)),
        akeep=jnp.asarray(((r_[:, None] // t_new == r_[None, :] // t_new)
                           & (r_[:, None] % t_new <= r_[None, :] % t_new)).astype(np.float32)),
    )

    xp = x_prompt
    xs = x_sample.reshape(ns_all * t_new, d)
    caches = tuple(c.transpose(0, 1, 3, 4, 2).reshape(depth, ns_all, -1, c.shape[2])
                   for c in (cache_b_k, cache_b_v, cache_c_k, cache_c_v))

    kv_all = tuple(jnp.zeros((depth, nb, w, keep), f32)
                   for w, keep in ((B_KVW, B_KEEP), (B_KVW, B_KEEP), (C_WIDTH, C_KEEP), (C_WIDTH, C_KEEP)))
    s_all = tuple(jnp.zeros((depth, ns_all * t_new, w), f32)
                  for w in (B_KVW, B_KVW, C_WIDTH, C_WIDTH, A_WIDTH))
    for l in range(depth):
        xp, *kv_all = _prompt_layer(l, xp, p, kv_all)
        xs, *s_all = _sample_layer(l, xs, caches, p, s_all, t_new=t_new, tok_per_step=tok_per_step)

    def heads_of(a, heads):
        return a.reshape(depth, ns_all, t_new, heads, HEAD_DIM)

    def from_fm(a, heads):
        return a.reshape(depth, nb, heads, HEAD_DIM, a.shape[-1]).transpose(0, 1, 4, 2, 3)

    pk, pv, pck, pcv = kv_all
    sk, sv, sck, scv, sav = s_all
    return (xp, xs.reshape(ns_all, t_new, d),
            from_fm(pk, B_KV), from_fm(pv, B_KV), from_fm(pck, C_HEADS), from_fm(pcv, C_HEADS),
            heads_of(sk, B_KV), heads_of(sv, B_KV), heads_of(sck, C_HEADS), heads_of(scv, C_HEADS),
            sav.reshape(depth, ns_all, t_new, A_WIDTH))
```

```python
import functools
import math

import numpy as np
import jax
import jax.numpy as jnp
from jax import lax
from jax.experimental import pallas as pl
from jax.experimental.pallas import tpu as pltpu

bf16 = jnp.bfloat16
f32 = jnp.float32

HEAD_DIM = 64
CHUNK = 64
A_WIDTH, A_GROUPS, A_CHUNK = 256, 4, 128
A_GD = A_WIDTH // A_GROUPS
B_HEADS, B_KV = 8, 2
B_GROUP = B_HEADS // B_KV
B_WIDTH, B_KVW = B_HEADS * HEAD_DIM, B_KV * HEAD_DIM
B_PREV = 2
C_HEADS = 4
C_WIDTH = C_HEADS * HEAD_DIM
C_PREV = 8
C_CLIP = 128
T5_BUCKETS, T5_MAX_DIST = 32, 128
RMS_EPS = 1e-6
NEG_INF = -1e30
QK_SCALE = HEAD_DIM ** -0.5
LOG2E = math.log2(math.e)

LANE = 128
PAIR = 2 * CHUNK
B_WIN = (B_PREV + 2) * CHUNK
C_WIN = (C_PREV + 2) * CHUNK
B_KEEP = B_PREV * CHUNK
C_KEEP = C_PREV * CHUNK
ONES_ROWS = 16
VB_STRIDE = HEAD_DIM + ONES_ROWS
VC_STRIDE = 2 * HEAD_DIM + ONES_ROWS
SEQ_TILE = 512
SAMPLE_STREAMS_PER_BLOCK = 4
VMEM_LIMIT = 56 * 1024 * 1024
TOEPLITZ_UNROLL = 17

_IN_SIZES = [A_WIDTH, A_WIDTH, A_WIDTH, B_WIDTH, B_KVW, B_KVW, B_WIDTH, C_WIDTH, C_WIDTH, C_WIDTH, C_WIDTH]
_IN_OFF = [int(v) for v in np.cumsum([0] + _IN_SIZES)]
(T_AU, T_AV, T_AZ, T_BQ, T_BK, T_BV, T_BZ, T_CQ, T_CK, T_CV, T_CZ, T_END) = _IN_OFF
_T_SECTIONS = ((T_AU, T_BK), (T_BV, T_CK), (T_CV, T_END))
Y_A, Y_B, Y_C = 0, A_WIDTH, A_WIDTH + B_WIDTH

_NT = (((1,), (1,)), ((), ()))
_TN = (((0,), (0,)), ((), ()))


def _t5_bucket(rel):
    half = T5_BUCKETS // 2
    max_exact = half // 2
    ret = jnp.where(rel > 0, half, 0)
    n = jnp.abs(rel)
    nf = jnp.maximum(n, 1).astype(jnp.float32)
    large = max_exact + (jnp.log(nf / max_exact) / math.log(T5_MAX_DIST / max_exact)
                         * (half - max_exact)).astype(jnp.int32)
    large = jnp.minimum(large, half - 1)
    return ret + jnp.where(n < max_exact, n, large)


def _toeplitz_kernel(rows_ref, shift_ref, frev_ref, ok_ref, out_ref, *, n_heads, kmins, n_toeplitz,
                     n_const, const_lane):
    width = frev_ref.shape[-1]
    neg = jnp.full((CHUNK, LANE), NEG_INF, f32)

    def store(rb, h, val):
        r0 = pl.multiple_of(rb * CHUNK, CHUNK)
        val = jnp.where(ok_ref[pl.ds(r0, CHUNK), :] > 0.0, val, NEG_INF)
        for v, kmin in enumerate(kmins):
            out_ref[0, v, pl.ds(r0, CHUNK), h * LANE:(h + 1) * LANE] = (
                val if kmin == 0 else jnp.where(rb >= kmin, val, neg))

    def toeplitz_block(i, carry):
        rolled = pltpu.roll(frev_ref[...], shift_ref[i], 1)
        ring = jnp.concatenate([rolled[:, :LANE], rolled[:, width - LANE:]], axis=1)
        for h in range(n_heads):
            x = jnp.broadcast_to(ring[h:h + 1, :], (CHUNK, 2 * LANE))
            store(rows_ref[i], h, pltpu.roll(x, 0, 1, stride=1, stride_axis=0)[:, :LANE] * LOG2E)
        return carry

    def const_block(i, carry):
        for h in range(n_heads):
            store(rows_ref[n_toeplitz + i], h, consts[h])
        return carry

    def trips(n):
        return max(u for u in range(1, TOEPLITZ_UNROLL + 1) if n % u == 0)

    if n_toeplitz:
        lax.fori_loop(0, n_toeplitz, toeplitz_block, 0, unroll=trips(n_toeplitz))
    if n_const:
        consts = [jnp.broadcast_to(frev_ref[h:h + 1, const_lane:const_lane + 1], (CHUNK, LANE)) * LOG2E
                  for h in range(n_heads)]
        lax.fori_loop(0, n_const, const_block, 0, unroll=trips(n_const))


def _build_bias(f_ext, rel_lo, rel, ok, *, kmins, const_below=None):
    n_layers, n_rel, n_heads = f_ext.shape
    n_rows = rel.shape[0]
    width = -(-n_rel // LANE) * LANE
    jj, ll = np.arange(CHUNK)[:, None], np.arange(LANE)[None, :]
    toeplitz_rows, shifts, const_rows = [], [], []
    for rb in range(n_rows // CHUNK):
        r, o = rel[rb * CHUNK:(rb + 1) * CHUNK], ok[rb * CHUNK:(rb + 1) * CHUNK]
        if not o.any() or (const_below is not None and r[o].max() <= const_below):
            const_rows.append(rb)
            continue
        bases = (r - jj + ll)[o]
        assert (bases == bases[0]).all(), "block is not Toeplitz on its valid entries"
        base = int(bases[0])
        k = (base + jj - ll - rel_lo)[o]
        assert k.min() >= 0 and k.max() < n_rel
        toeplitz_rows.append(rb)
        shifts.append((width - ((n_rel - 1) - (base - rel_lo))) % width)
    const_lane = 0 if const_below is None else (n_rel - 1) - (const_below - rel_lo)
    frev = jnp.pad(jnp.flip(f_ext, axis=1).transpose(0, 2, 1), ((0, 0), (0, 0), (0, width - n_rel)))
    kern = functools.partial(_toeplitz_kernel, n_heads=n_heads, kmins=tuple(kmins),
                             n_toeplitz=len(toeplitz_rows), n_const=len(const_rows), const_lane=const_lane)
    rows_arg = jnp.asarray(np.asarray(toeplitz_rows + const_rows, np.int32))
    return pl.pallas_call(
        kern,
        grid=(n_layers,),
        in_specs=[pl.BlockSpec(memory_space=pltpu.SMEM),
                  pl.BlockSpec(memory_space=pltpu.SMEM),
                  pl.BlockSpec((None, n_heads, width), lambda l: (l, 0, 0)),
                  pl.BlockSpec((n_rows, LANE), lambda l: (0, 0))],
        out_specs=pl.BlockSpec((1, len(kmins), n_rows, n_heads * LANE), lambda l: (l, 0, 0, 0)),
        out_shape=jax.ShapeDtypeStruct((n_layers, len(kmins), n_rows, n_heads * LANE), f32),
        compiler_params=pltpu.CompilerParams(dimension_semantics=("arbitrary",),
                                             vmem_limit_bytes=VMEM_LIMIT),
        name="bias_toeplitz",
    )(rows_arg, jnp.asarray(np.asarray(shifts or [0], np.int32)), frev, jnp.asarray(ok.astype(np.float32)))


def _wT_kernel(w_ref, o_ref):
    o_ref[...] = w_ref[...].T.astype(bf16)


def _transpose_weights(w, cols=8 * LANE):
    n_layers, k, n = w.shape
    return pl.pallas_call(
        _wT_kernel,
        grid=(n_layers, n // cols),
        in_specs=[pl.BlockSpec((None, k, cols), lambda l, c: (l, 0, c))],
        out_specs=pl.BlockSpec((None, cols, k), lambda l, c: (l, c, 0)),
        out_shape=jax.ShapeDtypeStruct((n_layers, n, k), bf16),
        compiler_params=pltpu.CompilerParams(dimension_semantics=("arbitrary", "arbitrary"),
                                             vmem_limit_bytes=VMEM_LIMIT),
        name="weights_T",
    )(w)


def _prompt_maps():
    i = np.arange(PAIR)[None, :]
    qi = i // CHUNK
    jb = np.arange(B_WIN)[:, None]
    rel_b = jb - B_KEEP - i
    ok_b = (jb // CHUNK >= qi) & (jb // CHUNK <= qi + B_PREV)
    jc = np.arange(C_WIN)[:, None]
    rel_c = jc - C_KEEP - i
    ok_c = (jc // CHUNK >= qi) & (jc // CHUNK <= qi + C_PREV)
    return rel_b, ok_b, rel_c, ok_c


def _sample_maps(lc_b, lc_c, t_new):
    ns = SAMPLE_STREAMS_PER_BLOCK
    c = np.arange(LANE)[None, :]
    sq, i = c // t_new, c % t_new

    def one(lc):
        r = np.arange(ns * lc + LANE)[:, None]
        is_new = r >= ns * lc
        sk = np.where(is_new, (r - ns * lc) // t_new, r // lc)
        j = np.where(is_new, lc + (r - ns * lc) % t_new, r % lc)
        return (j - lc - i).astype(np.int32), (sk == sq)

    return one(lc_b) + one(lc_c)


def _rms_rows(x, g):
    ms = jnp.mean(x * x, axis=-1, keepdims=True)
    return x * lax.rsqrt(ms + RMS_EPS) * g


def _rms_cols(xT, g_col):
    ms = jnp.mean(xT * xT, axis=0, keepdims=True)
    return xT * lax.rsqrt(ms + RMS_EPS) * g_col


def _project_T(pt_sc, wT_ref, h):
    for r0, r1 in _T_SECTIONS:
        pt_sc[r0:r1, :] = lax.dot_general(wT_ref[r0:r1, :], h, _NT, preferred_element_type=f32)


def _qblockdiag(qa, qb, lower):
    z = jnp.zeros((HEAD_DIM, LANE), bf16)
    if lower is None:
        return jnp.concatenate([jnp.concatenate([qa, z], axis=1),
                                jnp.concatenate([z, qb], axis=1)], axis=0)
    top = jnp.concatenate([qa, qb], axis=1)
    zz = jnp.zeros((HEAD_DIM, 2 * LANE), bf16)
    return jnp.concatenate([top, zz] if lower == 0 else [zz, top], axis=0)


def _with_ones(vT, group):
    ones = jnp.ones((ONES_ROWS, vT.shape[1]), vT.dtype)
    parts = []
    for r in range(0, vT.shape[0], group):
        parts += [vT[r:r + group, :], ones]
    return jnp.concatenate(parts, axis=0)


def _softmax_pv(s_m, sink, vwin):
    s, m = s_m
    if sink is not None:
        m = jnp.maximum(m, sink)
    e = jnp.exp2(s - m)
    o = jnp.dot(vwin, e.astype(bf16), preferred_element_type=f32)
    nv = vwin.shape[0] - ONES_ROWS
    den = o[nv:nv + 1, :]
    if sink is not None:
        den = den + jnp.exp2(sink - m)
    return o[:nv, :] * (1.0 / den)


def _scores(q_sc, cols, kwin_b, bias_b, kwin_c, bias_c):
    def head_q(r0, hh):
        return q_sc[r0 + hh * HEAD_DIM:r0 + (hh + 1) * HEAD_DIM, cols]

    def with_max(s):
        return s, jnp.max(s, axis=0, keepdims=True)

    s_b, s_c = [], []
    for cb in range(B_HEADS // 2):
        ha, hb = 2 * cb, 2 * cb + 1
        qbd = _qblockdiag(head_q(0, ha), head_q(0, hb), ha // B_GROUP)
        s = jnp.dot(kwin_b, qbd, preferred_element_type=f32)
        s_b.append(with_max(s + bias_b[:, cb * 2 * LANE:(cb + 1) * 2 * LANE]))
    for cb in range(C_HEADS // 2):
        ha, hb = 2 * cb, 2 * cb + 1
        qbd = _qblockdiag(head_q(B_WIDTH, ha), head_q(B_WIDTH, hb), None)
        s = jnp.dot(kwin_c[:, cb * LANE:(cb + 1) * LANE], qbd, preferred_element_type=f32)
        s_c.append(with_max(s + bias_c[:, cb * 2 * LANE:(cb + 1) * 2 * LANE]))
    return s_b, s_c


def _attend(pt_sc, yT_sc, cols, s_b, s_c, vwinT_b, vwinT_c, sink_ref):
    for cb in range(B_HEADS // 2):
        ha, hb = 2 * cb, 2 * cb + 1
        kv = ha // B_GROUP
        sink = sink_ref[:, cb * 2 * LANE:(cb + 1) * 2 * LANE] * LOG2E
        o = _softmax_pv(s_b[cb], sink, vwinT_b[kv * VB_STRIDE:(kv + 1) * VB_STRIDE, :])
        for j, hh in enumerate((ha, hb)):
            gate = pt_sc[T_BZ + hh * HEAD_DIM:T_BZ + (hh + 1) * HEAD_DIM, cols]
            yT_sc[Y_B + hh * HEAD_DIM:Y_B + (hh + 1) * HEAD_DIM, cols] = (
                o[:, j * LANE:(j + 1) * LANE] * gate).astype(bf16)
    for cb in range(C_HEADS // 2):
        ha, hb = 2 * cb, 2 * cb + 1
        o = _softmax_pv(s_c[cb], None, vwinT_c[cb * VC_STRIDE:(cb + 1) * VC_STRIDE, :])
        for j, hh in enumerate((ha, hb)):
            gate = pt_sc[T_CZ + hh * HEAD_DIM:T_CZ + (hh + 1) * HEAD_DIM, cols]
            yT_sc[Y_C + hh * HEAD_DIM:Y_C + (hh + 1) * HEAD_DIM, cols] = (
                o[j * HEAD_DIM:(j + 1) * HEAD_DIM, j * LANE:(j + 1) * LANE] * gate).astype(bf16)


def _prepare_rows(pt_sc, q_sc):
    for r0, r1 in ((T_BZ, T_CQ), (T_CZ, T_END)):
        pt_sc[r0:r1, :] = jax.nn.silu(pt_sc[r0:r1, :])
    q_sc[0:B_WIDTH, :] = (pt_sc[T_BQ:T_BK, :] * (QK_SCALE * LOG2E)).astype(bf16)
    q_sc[B_WIDTH:, :] = (pt_sc[T_CQ:T_CK, :] * (QK_SCALE * LOG2E)).astype(bf16)


def _finish(x, yT_sc, wout_ref, gpost_ref):
    out = lax.dot_general(yT_sc[...], wout_ref[...], _TN, preferred_element_type=f32)
    return x + _rms_rows(out, gpost_ref[...])


def _layer_spec(arr, layer):
    shape = arr.shape[1:]
    return pl.BlockSpec((None,) + shape, lambda *_: (layer,) + (0,) * len(shape),
                        pipeline_mode=pl.Buffered(1))


def _const_spec(arr):
    return pl.BlockSpec(arr.shape, lambda *_: (0,) * arr.ndim, pipeline_mode=pl.Buffered(1))


def _prompt_kernel(x_ref, gpre_ref, gpost_ref, wT_ref, wout_ref,
                   ang_ref, awsT_ref, abs_ref, sink_ref, biasb_ref, biasc_ref,
                   bk_all, bv_all, ck_all, cv_all,
                   xo_ref, bk_ref, bv_ref, ck_ref, cv_ref,
                   pt_sc, kb_ring, kc_ring, vbT_ring, vcT_ring, yT_sc, q_sc,
                   sb_even, mb_even, sc_even, mc_even, sb_odd, mb_odd, sc_odd, mc_odd):
    t = pl.program_id(1)
    ts = SEQ_TILE

    @pl.when(t == 0)
    def _():
        kb_ring[0:B_KEEP, :] = jnp.zeros((B_KEEP, B_KVW), bf16)
        kc_ring[0:C_KEEP, :] = jnp.zeros((C_KEEP, C_WIDTH), bf16)
        vbT_ring[:, 0:B_KEEP] = jnp.zeros((B_KV * VB_STRIDE, B_KEEP), bf16)
        vcT_ring[:, 0:C_KEEP] = jnp.zeros((C_HEADS // 2 * VC_STRIDE, C_KEEP), bf16)
        for g in range(B_KV):
            vbT_ring[g * VB_STRIDE + HEAD_DIM:(g + 1) * VB_STRIDE, :] = jnp.ones((ONES_ROWS, B_KEEP + ts), bf16)
        for g in range(C_HEADS // 2):
            vcT_ring[g * VC_STRIDE + 2 * HEAD_DIM:(g + 1) * VC_STRIDE, :] = jnp.ones((ONES_ROWS, C_KEEP + ts), bf16)

    x = x_ref[0]
    h = _rms_rows(x, gpre_ref[...]).astype(bf16)
    _project_T(pt_sc, wT_ref, h)
    kb_ring[B_KEEP:, :] = lax.dot_general(h, wT_ref[T_BK:T_BV, :], _NT,
                                          preferred_element_type=f32).astype(bf16)
    kc_ring[C_KEEP:, :] = lax.dot_general(h, wT_ref[T_CK:T_CV, :], _NT,
                                          preferred_element_type=f32).astype(bf16)
    for g in range(B_KV):
        vbT_ring[g * VB_STRIDE:g * VB_STRIDE + HEAD_DIM, B_KEEP:] = (
            pt_sc[T_BV + g * HEAD_DIM:T_BV + (g + 1) * HEAD_DIM, :].astype(bf16))
    for g in range(C_HEADS // 2):
        vcT_ring[g * VC_STRIDE:g * VC_STRIDE + 2 * HEAD_DIM, C_KEEP:] = (
            pt_sc[T_CV + g * 2 * HEAD_DIM:T_CV + (g + 1) * 2 * HEAD_DIM, :].astype(bf16))

    _prepare_rows(pt_sc, q_sc)

    vn = _rms_cols(pt_sc[T_AV:T_AV + A_WIDTH, :], ang_ref[...]).astype(bf16)
    n_ac = ts // A_CHUNK
    row = lax.broadcasted_iota(jnp.int32, (A_CHUNK, A_CHUNK), 0)
    col = lax.broadcasted_iota(jnp.int32, (A_CHUNK, A_CHUNK), 1)
    for g in range(A_GROUPS):
        rows = slice(g * A_GD, (g + 1) * A_GD)
        wgt = jnp.where(row <= col, awsT_ref[g], 0.0).astype(bf16)
        lhs = jnp.concatenate([vn[rows, c * A_CHUNK:(c + 1) * A_CHUNK] for c in range(n_ac)], axis=0)
        mix = jnp.dot(lhs, wgt, preferred_element_type=f32) + abs_ref[g]
        for c in range(n_ac):
            cols = slice(c * A_CHUNK, (c + 1) * A_CHUNK)
            u = pt_sc[T_AU + g * A_GD:T_AU + (g + 1) * A_GD, cols]
            z = pt_sc[T_AZ + g * A_GD:T_AZ + (g + 1) * A_GD, cols]
            yT_sc[Y_A + g * A_GD:Y_A + (g + 1) * A_GD, cols] = (
                u * mix[c * A_GD:(c + 1) * A_GD, :] * jax.nn.silu(z)).astype(bf16)

    n_pairs = ts // PAIR

    even, odd = (sb_even, mb_even, sc_even, mc_even), (sb_odd, mb_odd, sc_odd, mc_odd)

    def scores_into(p, bufs):
        sb_ref, mb_ref, sc_ref, mc_ref = bufs
        pg = t * n_pairs + p
        start = p * PAIR if isinstance(p, int) else pl.multiple_of(p * PAIR, PAIR)
        s_b, s_c = _scores(q_sc, pl.ds(start, PAIR),
                           kb_ring[pl.ds(start, B_WIN), :], biasb_ref.at[jnp.minimum(pg, 1)],
                           kc_ring[pl.ds(start, C_WIN), :], biasc_ref.at[jnp.minimum(pg, C_PREV // 2)])
        for cb, (v, m) in enumerate(s_b):
            sb_ref[cb] = v
            mb_ref[cb] = m
        for cb, (v, m) in enumerate(s_c):
            sc_ref[cb] = v
            mc_ref[cb] = m

    def attend_from(p, bufs):
        sb_ref, mb_ref, sc_ref, mc_ref = bufs
        start = p * PAIR if isinstance(p, int) else pl.multiple_of(p * PAIR, PAIR)
        _attend(pt_sc, yT_sc, pl.ds(start, PAIR),
                [(sb_ref[cb], mb_ref[cb]) for cb in range(B_HEADS // 2)],
                [(sc_ref[cb], mc_ref[cb]) for cb in range(C_HEADS // 2)],
                vbT_ring[:, pl.ds(start, B_WIN)], vcT_ring[:, pl.ds(start, C_WIN)], sink_ref)

    scores_into(0, even)

    def finish_rows(p):
        start = p * PAIR if isinstance(p, int) else pl.multiple_of(p * PAIR, PAIR)
        rows = pl.ds(start, PAIR)
        xo_ref[0, rows, :] = _finish(x_ref[0, rows, :], yT_sc.at[:, rows], wout_ref, gpost_ref)

    assert n_pairs == 4

    def first_pairs(q, carry):
        p = 2 * q
        scores_into(p + 1, odd)
        attend_from(p, even)
        scores_into(p + 2, even)
        finish_rows(p)
        attend_from(p + 1, odd)
        return carry

    def last_pairs(q, carry):
        p = 2 * q
        scores_into(p + 1, odd)
        finish_rows(p - 1)
        attend_from(p, even)
        finish_rows(p)
        attend_from(p + 1, odd)
        return carry

    n_full = jnp.minimum(t + n_pairs, n_pairs // 2 - 1)
    lax.fori_loop(0, n_full, first_pairs, 0)
    lax.fori_loop(n_full, n_full + 1, last_pairs, 0)
    finish_rows(n_pairs - 1)

    kb_ring[0:B_KEEP, :] = kb_ring[ts:ts + B_KEEP, :]
    kc_ring[0:C_KEEP, :] = kc_ring[ts:ts + C_KEEP, :]
    vbT_ring[:, 0:B_KEEP] = vbT_ring[:, ts:ts + B_KEEP]
    vcT_ring[:, 0:C_KEEP] = vcT_ring[:, ts:ts + C_KEEP]

    @pl.when(t == pl.num_programs(1) - 1)
    def _():
        h_last = _rms_rows(x_ref[0], gpre_ref[...]).astype(bf16)
        bk_ref[0] = lax.dot_general(wT_ref[T_BK:T_BV, :], h_last[ts - B_KEEP:, :], _NT,
                                    preferred_element_type=f32)
        ck_ref[0] = lax.dot_general(wT_ref[T_CK:T_CV, :], h_last[ts - C_KEEP:, :], _NT,
                                    preferred_element_type=f32)
        bv_ref[0] = pt_sc[T_BV:T_BZ, ts - B_KEEP:]
        cv_ref[0] = pt_sc[T_CV:T_CZ, ts - C_KEEP:]


def _prompt_layer(layer, x, p, kv_all):
    nb, seq, d = x.shape
    ts = SEQ_TILE
    assert seq % ts == 0 and ts == C_KEEP
    kv_spec = lambda keep, width: pl.BlockSpec((None, 1, width, keep), lambda b, t: (layer, b, 0, 0))
    n_in = 11
    return pl.pallas_call(
        _prompt_kernel,
        grid=(nb, seq // ts),
        in_specs=[
            pl.BlockSpec((1, ts, d), lambda b, t: (b, t, 0)),
            _layer_spec(p["gpre"], layer), _layer_spec(p["gpost"], layer),
            _layer_spec(p["wT"], layer),
            _layer_spec(p["wout"], layer),
            _layer_spec(p["ang"], layer), _layer_spec(p["awsT"], layer), _layer_spec(p["abs"], layer),
            _layer_spec(p["sink"], layer), _const_spec(p["bias_pb"]), _layer_spec(p["bias_pc"], layer),
        ] + [pl.BlockSpec(memory_space=pl.ANY)] * len(kv_all),
        out_specs=[
            pl.BlockSpec((1, ts, d), lambda b, t: (b, t, 0)),
            kv_spec(B_KEEP, B_KVW), kv_spec(B_KEEP, B_KVW), kv_spec(C_KEEP, C_WIDTH), kv_spec(C_KEEP, C_WIDTH),
        ],
        out_shape=[jax.ShapeDtypeStruct((nb, seq, d), f32)]
        + [jax.ShapeDtypeStruct(a.shape, a.dtype) for a in kv_all],
        input_output_aliases={n_in + i: 1 + i for i in range(len(kv_all))},
        scratch_shapes=[
            pltpu.VMEM((T_END, ts), f32),
            pltpu.VMEM((B_KEEP + ts, B_KVW), bf16),
            pltpu.VMEM((C_KEEP + ts, C_WIDTH), bf16),
            pltpu.VMEM((B_KV * VB_STRIDE, B_KEEP + ts), bf16),
            pltpu.VMEM((C_HEADS // 2 * VC_STRIDE, C_KEEP + ts), bf16),
            pltpu.VMEM((A_WIDTH + B_WIDTH + C_WIDTH, ts), bf16),
            pltpu.VMEM((B_WIDTH + C_WIDTH, ts), bf16),
        ] + 2 * [
            pltpu.VMEM((B_HEADS // 2, B_WIN, 2 * LANE), f32), pltpu.VMEM((B_HEADS // 2, 1, 2 * LANE), f32),
            pltpu.VMEM((C_HEADS // 2, C_WIN, 2 * LANE), f32), pltpu.VMEM((C_HEADS // 2, 1, 2 * LANE), f32),
        ],
        compiler_params=pltpu.CompilerParams(dimension_semantics=("arbitrary", "arbitrary"),
                                             vmem_limit_bytes=VMEM_LIMIT),
        name="prompt_layer",
    )(x, p["gpre"], p["gpost"], p["wT"], p["wout"], p["ang"], p["awsT"],
      p["abs"], p["sink"], p["bias_pb"], p["bias_pc"], *kv_all)


def _sample_kernel(x_ref, cbk_ref, cbv_ref, cck_ref, ccv_ref, gpre_ref, gpost_ref, wT_ref,
                   wout_ref, ang_ref, aws_ref, arep_ref, akeep_ref, abias_ref,
                   sink_ref, biasb_ref, biasc_ref,
                   bk_all, bv_all, ck_all, cv_all, av_all,
                   xo_ref, bk_ref, bv_ref, ck_ref, cv_ref, av_ref,
                   pt_sc, yT_sc, q_sc):
    ns = SAMPLE_STREAMS_PER_BLOCK
    x = x_ref[...]
    ntok = x.shape[0]
    h = _rms_rows(x, gpre_ref[...]).astype(bf16)
    _project_T(pt_sc, wT_ref, h)
    kvn_b = lax.dot_general(h, wT_ref[T_BK:T_BZ, :], _NT, preferred_element_type=f32)
    kvn_c = lax.dot_general(h, wT_ref[T_CK:T_CZ, :], _NT, preferred_element_type=f32)
    kn_c = kvn_c[:, 0:C_WIDTH]
    bk_ref[...] = kvn_b[:, 0:B_KVW]
    bv_ref[...] = kvn_b[:, B_KVW:]
    ck_ref[...] = kn_c
    cv_ref[...] = kvn_c[:, C_WIDTH:]

    vn = _rms_cols(pt_sc[T_AV:T_AV + A_WIDTH, :], ang_ref[...])
    av_ref[...] = vn.T
    vnb = vn.astype(bf16)
    keep = akeep_ref[...] > 0.0
    rep = arep_ref[...]
    for g in range(A_GROUPS):
        rows = slice(g * A_GD, (g + 1) * A_GD)
        tiled = lax.dot_general(jnp.dot(rep, aws_ref[g].astype(bf16), preferred_element_type=f32).astype(bf16),
                                rep, _NT, preferred_element_type=f32)
        wgt = jnp.where(keep, tiled, 0.0).astype(bf16)
        mix = jnp.dot(vnb[rows, :], wgt, preferred_element_type=f32) + abias_ref[g]
        u = pt_sc[T_AU + g * A_GD:T_AU + (g + 1) * A_GD, :]
        z = pt_sc[T_AZ + g * A_GD:T_AZ + (g + 1) * A_GD, :]
        yT_sc[Y_A + g * A_GD:Y_A + (g + 1) * A_GD, :] = (u * mix * jax.nn.silu(z)).astype(bf16)

    _prepare_rows(pt_sc, q_sc)
    for blk in range(ntok // LANE):
        cols = slice(blk * LANE, (blk + 1) * LANE)
        streams = range(blk * ns, (blk + 1) * ns)
        kwin_b = jnp.concatenate([cbk_ref[s].T.astype(bf16) for s in streams]
                                 + [kvn_b[cols, 0:B_KVW].astype(bf16)], axis=0)
        vwinT_b = jnp.concatenate([cbv_ref[s].astype(bf16) for s in streams]
                                  + [pt_sc[T_BV:T_BV + B_KVW, cols].astype(bf16)], axis=1)
        kwin_c = jnp.concatenate([cck_ref[s].T.astype(bf16) for s in streams]
                                 + [kn_c[cols, :].astype(bf16)], axis=0)
        vwinT_c = jnp.concatenate([ccv_ref[s].astype(bf16) for s in streams]
                                  + [pt_sc[T_CV:T_CV + C_WIDTH, cols].astype(bf16)], axis=1)
        s_b, s_c = _scores(q_sc, cols, kwin_b, biasb_ref, kwin_c, biasc_ref)
        _attend(pt_sc, yT_sc, cols, s_b, s_c, _with_ones(vwinT_b, HEAD_DIM), _with_ones(vwinT_c, 2 * HEAD_DIM),
                sink_ref)

    xo_ref[...] = _finish(x, yT_sc, wout_ref, gpost_ref)


def _sample_layer(layer, x, caches, p, outs_all, *, t_new, tok_per_step):
    ntok, d = x.shape
    spb = tok_per_step // t_new
    cbk, cbv, cck, ccv = caches
    n_in = 17

    def rows(width):
        return pl.BlockSpec((tok_per_step, width), lambda i: (i, 0))

    def layer_rows(a):
        return pl.BlockSpec((None, tok_per_step, a.shape[2]), lambda i: (layer, i, 0))

    def cache(c):
        return pl.BlockSpec((None, spb) + c.shape[2:], lambda i: (layer, i, 0, 0))

    return pl.pallas_call(
        _sample_kernel,
        grid=(ntok // tok_per_step,),
        in_specs=[
            rows(d), cache(cbk), cache(cbv), cache(cck), cache(ccv),
            _layer_spec(p["gpre"], layer), _layer_spec(p["gpost"], layer),
            _layer_spec(p["wT"], layer),
            _layer_spec(p["wout"], layer),
            _layer_spec(p["ang"], layer), _layer_spec(p["aws_s"], layer), _const_spec(p["arep"]),
            _const_spec(p["akeep"]), _layer_spec(p["abias_s"], layer),
            _layer_spec(p["sink"], layer), _const_spec(p["bias_sb"]), _layer_spec(p["bias_sc"], layer),
        ] + [pl.BlockSpec(memory_space=pl.ANY)] * len(outs_all),
        out_specs=[rows(d)] + [layer_rows(a) for a in outs_all],
        out_shape=[jax.ShapeDtypeStruct((ntok, d), f32)]
        + [jax.ShapeDtypeStruct(a.shape, a.dtype) for a in outs_all],
        input_output_aliases={n_in + i: 1 + i for i in range(len(outs_all))},
        scratch_shapes=[
            pltpu.VMEM((T_END, tok_per_step), f32),
            pltpu.VMEM((A_WIDTH + B_WIDTH + C_WIDTH, tok_per_step), bf16),
            pltpu.VMEM((B_WIDTH + C_WIDTH, tok_per_step), bf16),
        ],
        compiler_params=pltpu.CompilerParams(dimension_semantics=("arbitrary",),
                                             vmem_limit_bytes=VMEM_LIMIT),
        name="sample_layer",
    )(x, cbk, cbv, cck, ccv, p["gpre"], p["gpost"], p["wT"], p["wout"],
      p["ang"], p["aws_s"], p["arep"], p["akeep"], p["abias_s"], p["sink"], p["bias_sb"], p["bias_sc"],
      *outs_all)


def kernel(x_prompt, x_sample, cache_b_k, cache_b_v, cache_c_k, cache_c_v, g_pre, g_post, w_in, w_out,
           a_norm_g, a_ws, a_bs, b_sinks, c_rel_bias, t5_bias):
    depth = w_in.shape[0]
    nb, seq, d = x_prompt.shape
    ns_all, t_new, _ = x_sample.shape
    lc_b, lc_c = cache_b_k.shape[2], cache_c_k.shape[2]
    assert lc_b == B_KEEP and lc_c == C_KEEP and LANE % t_new == 0
    assert SAMPLE_STREAMS_PER_BLOCK * t_new == LANE
    tok_per_step = 2 * LANE
    reps = tok_per_step // t_new

    rel_pb, ok_pb, rel_pc, ok_pc = _prompt_maps()
    rel_sb, ok_sb, rel_sc, ok_sc = _sample_maps(lc_b, lc_c, t_new)
    b_lo = int(min(rel_pb.min(), rel_sb.min()))
    b_hi = int(max(rel_pb.max(), rel_sb.max()))
    t5_rel = t5_bias[_t5_bucket(jnp.arange(b_lo, b_hi + 1))][None]
    c_lo = -C_CLIP - LANE + 1
    c_hi = int(max(rel_pc.max(), rel_sc.max()))
    assert c_hi < C_CLIP
    c_rel = jnp.concatenate([jnp.broadcast_to(c_rel_bias[:, :1], (depth, -C_CLIP - c_lo, C_HEADS)),
                             c_rel_bias[:, :c_hi + C_CLIP + 1]], axis=1)
    r_ = np.arange(tok_per_step)
    awsT = a_ws.transpose(0, 1, 3, 2)

    p = dict(
        bias_pb=_build_bias(t5_rel, b_lo, rel_pb, ok_pb, kmins=(B_PREV, 0))[0],
        bias_sb=_build_bias(t5_rel, b_lo, rel_sb, ok_sb, kmins=(0,))[0, 0],
        bias_pc=_build_bias(c_rel, c_lo, rel_pc, ok_pc, const_below=-C_CLIP,
                            kmins=tuple(max(C_PREV - 2 * v, 0) for v in range(C_PREV // 2 + 1))),
        bias_sc=_build_bias(c_rel, c_lo, rel_sc, ok_sc, const_below=-C_CLIP, kmins=(0,))[:, 0],
        wT=_transpose_weights(w_in),
        wout=w_out.astype(bf16),
        gpre=g_pre[:, None, :],
        gpost=g_post[:, None, :],
        ang=a_norm_g[:, :, None],
        awsT=awsT,
        abs=a_bs[:, :, None, :],
        sink=jnp.repeat(b_sinks, LANE, axis=1)[:, None, :],
        aws_s=awsT[:, :, :t_new, :t_new],
        arep=jnp.asarray((r_[:, None] % t_new == np.arange(t_new)[None, :]).astype(np.float32), bf16),
        abias_s=jnp.tile(a_bs[:, :, None, :t_new], (1, 1, 1, reps)),
        akeep=jnp.asarray(((r_[:, None] // t_new == r_[None, :] // t_new)
                           & (r_[:, None] % t_new <= r_[None, :] % t_new)).astype(np.float32)),
    )

    xp = x_prompt
    xs = x_sample.reshape(ns_all * t_new, d)
    caches = tuple(c.transpose(0, 1, 3, 4, 2).reshape(depth, ns_all, -1, c.shape[2])
                   for c in (cache_b_k, cache_b_v, cache_c_k, cache_c_v))

    kv_all = tuple(jnp.zeros((depth, nb, w, keep), f32)
                   for w, keep in ((B_KVW, B_KEEP), (B_KVW, B_KEEP), (C_WIDTH, C_KEEP), (C_WIDTH, C_KEEP)))
    s_all = tuple(jnp.zeros((depth, ns_all * t_new, w), f32)
                  for w in (B_KVW, B_KVW, C_WIDTH, C_WIDTH, A_WIDTH))
    for l in range(depth):
        xp, *kv_all = _prompt_layer(l, xp, p, kv_all)
        xs, *s_all = _sample_layer(l, xs, caches, p, s_all, t_new=t_new, tok_per_step=tok_per_step)

    def heads_of(a, heads):
        return a.reshape(depth, ns_all, t_new, heads, HEAD_DIM)

    def from_fm(a, heads):
        return a.reshape(depth, nb, heads, HEAD_DIM, a.shape[-1]).transpose(0, 1, 4, 2, 3)

    pk, pv, pck, pcv = kv_all
    sk, sv, sck, scv, sav = s_all
    return (xp, xs.reshape(ns_all, t_new, d),
            from_fm(pk, B_KV), from_fm(pv, B_KV), from_fm(pck, C_HEADS), from_fm(pcv, C_HEADS),
            heads_of(sk, B_KV), heads_of(sv, B_KV), heads_of(sck, C_HEADS), heads_of(scv, C_HEADS),
            sav.reshape(depth, ns_all, t_new, A_WIDTH))
```

```python
import functools
import math

import numpy as np
import jax
import jax.numpy as jnp
from jax import lax
from jax.experimental import pallas as pl
from jax.experimental.pallas import tpu as pltpu

bf16 = jnp.bfloat16
f32 = jnp.float32

HEAD_DIM = 64
CHUNK = 64
A_WIDTH, A_GROUPS, A_CHUNK = 256, 4, 128
A_GD = A_WIDTH // A_GROUPS
B_HEADS, B_KV = 8, 2
B_GROUP = B_HEADS // B_KV
B_WIDTH, B_KVW = B_HEADS * HEAD_DIM, B_KV * HEAD_DIM
B_PREV = 2
C_HEADS = 4
C_WIDTH = C_HEADS * HEAD_DIM
C_PREV = 8
C_CLIP = 128
T5_BUCKETS, T5_MAX_DIST = 32, 128
RMS_EPS = 1e-6
NEG_INF = -1e30
QK_SCALE = HEAD_DIM ** -0.5
LOG2E = math.log2(math.e)

LANE = 128
PAIR = 2 * CHUNK
B_WIN = (B_PREV + 2) * CHUNK
C_WIN = (C_PREV + 2) * CHUNK
B_KEEP = B_PREV * CHUNK
C_KEEP = C_PREV * CHUNK
ONES_ROWS = 16
VB_STRIDE = HEAD_DIM + ONES_ROWS
VC_STRIDE = 2 * HEAD_DIM + ONES_ROWS
SEQ_TILE = 512
SAMPLE_STREAMS_PER_BLOCK = 4
VMEM_LIMIT = 56 * 1024 * 1024
TOEPLITZ_UNROLL = 17

_IN_SIZES = [A_WIDTH, A_WIDTH, A_WIDTH, B_WIDTH, B_KVW, B_KVW, B_WIDTH, C_WIDTH, C_WIDTH, C_WIDTH, C_WIDTH]
_IN_OFF = [int(v) for v in np.cumsum([0] + _IN_SIZES)]
(T_AU, T_AV, T_AZ, T_BQ, T_BK, T_BV, T_BZ, T_CQ, T_CK, T_CV, T_CZ, T_END) = _IN_OFF
_T_SECTIONS = ((T_AU, T_BK), (T_BV, T_CK), (T_CV, T_END))
Y_A, Y_B, Y_C = 0, A_WIDTH, A_WIDTH + B_WIDTH

_NT = (((1,), (1,)), ((), ()))
_TN = (((0,), (0,)), ((), ()))


def _t5_bucket(rel):
    half = T5_BUCKETS // 2
    max_exact = half // 2
    ret = jnp.where(rel > 0, half, 0)
    n = jnp.abs(rel)
    nf = jnp.maximum(n, 1).astype(jnp.float32)
    large = max_exact + (jnp.log(nf / max_exact) / math.log(T5_MAX_DIST / max_exact)
                         * (half - max_exact)).astype(jnp.int32)
    large = jnp.minimum(large, half - 1)
    return ret + jnp.where(n < max_exact, n, large)


def _toeplitz_kernel(rows_ref, shift_ref, frev_ref, ok_ref, out_ref, *, n_heads, kmins, n_toeplitz,
                     n_const, const_lane):
    width = frev_ref.shape[-1]
    neg = jnp.full((CHUNK, LANE), NEG_INF, f32)

    def store(rb, h, val):
        r0 = pl.multiple_of(rb * CHUNK, CHUNK)
        val = jnp.where(ok_ref[pl.ds(r0, CHUNK), :] > 0.0, val, NEG_INF)
        for v, kmin in enumerate(kmins):
            out_ref[0, v, pl.ds(r0, CHUNK), h * LANE:(h + 1) * LANE] = (
                val if kmin == 0 else jnp.where(rb >= kmin, val, neg))

    def toeplitz_block(i, carry):
        rolled = pltpu.roll(frev_ref[...], shift_ref[i], 1)
        ring = jnp.concatenate([rolled[:, :LANE], rolled[:, width - LANE:]], axis=1)
        for h in range(n_heads):
            x = jnp.broadcast_to(ring[h:h + 1, :], (CHUNK, 2 * LANE))
            store(rows_ref[i], h, pltpu.roll(x, 0, 1, stride=1, stride_axis=0)[:, :LANE] * LOG2E)
        return carry

    def const_block(i, carry):
        for h in range(n_heads):
            store(rows_ref[n_toeplitz + i], h, consts[h])
        return carry

    def trips(n):
        return max(u for u in range(1, TOEPLITZ_UNROLL + 1) if n % u == 0)

    if n_toeplitz:
        lax.fori_loop(0, n_toeplitz, toeplitz_block, 0, unroll=trips(n_toeplitz))
    if n_const:
        consts = [jnp.broadcast_to(frev_ref[h:h + 1, const_lane:const_lane + 1], (CHUNK, LANE)) * LOG2E
                  for h in range(n_heads)]
        lax.fori_loop(0, n_const, const_block, 0, unroll=trips(n_const))


def _build_bias(f_ext, rel_lo, rel, ok, *, kmins, const_below=None):
    n_layers, n_rel, n_heads = f_ext.shape
    n_rows = rel.shape[0]
    width = -(-n_rel // LANE) * LANE
    jj, ll = np.arange(CHUNK)[:, None], np.arange(LANE)[None, :]
    toeplitz_rows, shifts, const_rows = [], [], []
    for rb in range(n_rows // CHUNK):
        r, o = rel[rb * CHUNK:(rb + 1) * CHUNK], ok[rb * CHUNK:(rb + 1) * CHUNK]
        if not o.any() or (const_below is not None and r[o].max() <= const_below):
            const_rows.append(rb)
            continue
        bases = (r - jj + ll)[o]
        assert (bases == bases[0]).all(), "block is not Toeplitz on its valid entries"
        base = int(bases[0])
        k = (base + jj - ll - rel_lo)[o]
        assert k.min() >= 0 and k.max() < n_rel
        toeplitz_rows.append(rb)
        shifts.append((width - ((n_rel - 1) - (base - rel_lo))) % width)
    const_lane = 0 if const_below is None else (n_rel - 1) - (const_below - rel_lo)
    frev = jnp.pad(jnp.flip(f_ext, axis=1).transpose(0, 2, 1), ((0, 0), (0, 0), (0, width - n_rel)))
    kern = functools.partial(_toeplitz_kernel, n_heads=n_heads, kmins=tuple(kmins),
                             n_toeplitz=len(toeplitz_rows), n_const=len(const_rows), const_lane=const_lane)
    rows_arg = jnp.asarray(np.asarray(toeplitz_rows + const_rows, np.int32))
    return pl.pallas_call(
        kern,
        grid=(n_layers,),
        in_specs=[pl.BlockSpec(memory_space=pltpu.SMEM),
                  pl.BlockSpec(memory_space=pltpu.SMEM),
                  pl.BlockSpec((None, n_heads, width), lambda l: (l, 0, 0)),
                  pl.BlockSpec((n_rows, LANE), lambda l: (0, 0))],
        out_specs=pl.BlockSpec((1, len(kmins), n_rows, n_heads * LANE), lambda l: (l, 0, 0, 0)),
        out_shape=jax.ShapeDtypeStruct((n_layers, len(kmins), n_rows, n_heads * LANE), f32),
        compiler_params=pltpu.CompilerParams(dimension_semantics=("arbitrary",),
                                             vmem_limit_bytes=VMEM_LIMIT),
        name="bias_toeplitz",
    )(rows_arg, jnp.asarray(np.asarray(shifts or [0], np.int32)), frev, jnp.asarray(ok.astype(np.float32)))


def _wT_kernel(w_ref, o_ref):
    o_ref[...] = w_ref[...].T.astype(bf16)


def _transpose_weights(w, cols=8 * LANE):
    n_layers, k, n = w.shape
    return pl.pallas_call(
        _wT_kernel,
        grid=(n_layers, n // cols),
        in_specs=[pl.BlockSpec((None, k, cols), lambda l, c: (l, 0, c))],
        out_specs=pl.BlockSpec((None, cols, k), lambda l, c: (l, c, 0)),
        out_shape=jax.ShapeDtypeStruct((n_layers, n, k), bf16),
        compiler_params=pltpu.CompilerParams(dimension_semantics=("arbitrary", "arbitrary"),
                                             vmem_limit_bytes=VMEM_LIMIT),
        name="weights_T",
    )(w)


def _prompt_maps():
    i = np.arange(PAIR)[None, :]
    qi = i // CHUNK
    jb = np.arange(B_WIN)[:, None]
    rel_b = jb - B_KEEP - i
    ok_b = (jb // CHUNK >= qi) & (jb // CHUNK <= qi + B_PREV)
    jc = np.arange(C_WIN)[:, None]
    rel_c = jc - C_KEEP - i
    ok_c = (jc // CHUNK >= qi) & (jc // CHUNK <= qi + C_PREV)
    return rel_b, ok_b, rel_c, ok_c


def _sample_maps(lc_b, lc_c, t_new):
    ns = SAMPLE_STREAMS_PER_BLOCK
    c = np.arange(LANE)[None, :]
    sq, i = c // t_new, c % t_new

    def one(lc):
        r = np.arange(ns * lc + LANE)[:, None]
        is_new = r >= ns * lc
        sk = np.where(is_new, (r - ns * lc) // t_new, r // lc)
        j = np.where(is_new, lc + (r - ns * lc) % t_new, r % lc)
        return (j - lc - i).astype(np.int32), (sk == sq)

    return one(lc_b) + one(lc_c)


def _rms_rows(x, g):
    ms = jnp.mean(x * x, axis=-1, keepdims=True)
    return x * lax.rsqrt(ms + RMS_EPS) * g


def _rms_cols(xT, g_col):
    ms = jnp.mean(xT * xT, axis=0, keepdims=True)
    return xT * lax.rsqrt(ms + RMS_EPS) * g_col


def _project_T(pt_sc, wT_ref, h):
    for r0, r1 in _T_SECTIONS:
        pt_sc[r0:r1, :] = lax.dot_general(wT_ref[r0:r1, :], h, _NT, preferred_element_type=f32)


def _qblockdiag(qa, qb, lower):
    z = jnp.zeros((HEAD_DIM, LANE), bf16)
    if lower is None:
        return jnp.concatenate([jnp.concatenate([qa, z], axis=1),
                                jnp.concatenate([z, qb], axis=1)], axis=0)
    top = jnp.concatenate([qa, qb], axis=1)
    zz = jnp.zeros((HEAD_DIM, 2 * LANE), bf16)
    return jnp.concatenate([top, zz] if lower == 0 else [zz, top], axis=0)


def _with_ones(vT, group):
    ones = jnp.ones((ONES_ROWS, vT.shape[1]), vT.dtype)
    parts = []
    for r in range(0, vT.shape[0], group):
        parts += [vT[r:r + group, :], ones]
    return jnp.concatenate(parts, axis=0)


def _softmax_pv(s_m, sink, vwin):
    s, m = s_m
    if sink is not None:
        m = jnp.maximum(m, sink)
    e = jnp.exp2(s - m)
    o = jnp.dot(vwin, e.astype(bf16), preferred_element_type=f32)
    nv = vwin.shape[0] - ONES_ROWS
    den = o[nv:nv + 1, :]
    if sink is not None:
        den = den + jnp.exp2(sink - m)
    return o[:nv, :] * (1.0 / den)


def _scores(q_sc, cols, kwin_b, bias_b, kwin_c, bias_c):
    def head_q(r0, hh):
        return q_sc[r0 + hh * HEAD_DIM:r0 + (hh + 1) * HEAD_DIM, cols]

    def with_max(s):
        return s, jnp.max(s, axis=0, keepdims=True)

    s_b, s_c = [], []
    for cb in range(B_HEADS // 2):
        ha, hb = 2 * cb, 2 * cb + 1
        qbd = _qblockdiag(head_q(0, ha), head_q(0, hb), ha // B_GROUP)
        s = jnp.dot(kwin_b, qbd, preferred_element_type=f32)
        s_b.append(with_max(s + bias_b[:, cb * 2 * LANE:(cb + 1) * 2 * LANE]))
    for cb in range(C_HEADS // 2):
        ha, hb = 2 * cb, 2 * cb + 1
        qbd = _qblockdiag(head_q(B_WIDTH, ha), head_q(B_WIDTH, hb), None)
        s = jnp.dot(kwin_c[:, cb * LANE:(cb + 1) * LANE], qbd, preferred_element_type=f32)
        s_c.append(with_max(s + bias_c[:, cb * 2 * LANE:(cb + 1) * 2 * LANE]))
    return s_b, s_c


def _attend(pt_sc, yT_sc, cols, s_b, s_c, vwinT_b, vwinT_c, sink_ref):
    for cb in range(B_HEADS // 2):
        ha, hb = 2 * cb, 2 * cb + 1
        kv = ha // B_GROUP
        sink = sink_ref[:, cb * 2 * LANE:(cb + 1) * 2 * LANE] * LOG2E
        o = _softmax_pv(s_b[cb], sink, vwinT_b[kv * VB_STRIDE:(kv + 1) * VB_STRIDE, :])
        for j, hh in enumerate((ha, hb)):
            gate = pt_sc[T_BZ + hh * HEAD_DIM:T_BZ + (hh + 1) * HEAD_DIM, cols]
            yT_sc[Y_B + hh * HEAD_DIM:Y_B + (hh + 1) * HEAD_DIM, cols] = (
                o[:, j * LANE:(j + 1) * LANE] * gate).astype(bf16)
    for cb in range(C_HEADS // 2):
        ha, hb = 2 * cb, 2 * cb + 1
        o = _softmax_pv(s_c[cb], None, vwinT_c[cb * VC_STRIDE:(cb + 1) * VC_STRIDE, :])
        for j, hh in enumerate((ha, hb)):
            gate = pt_sc[T_CZ + hh * HEAD_DIM:T_CZ + (hh + 1) * HEAD_DIM, cols]
            yT_sc[Y_C + hh * HEAD_DIM:Y_C + (hh + 1) * HEAD_DIM, cols] = (
                o[j * HEAD_DIM:(j + 1) * HEAD_DIM, j * LANE:(j + 1) * LANE] * gate).astype(bf16)


def _prepare_rows(pt_sc, q_sc):
    for r0, r1 in ((T_BZ, T_CQ), (T_CZ, T_END)):
        pt_sc[r0:r1, :] = jax.nn.silu(pt_sc[r0:r1, :])
    q_sc[0:B_WIDTH, :] = (pt_sc[T_BQ:T_BK, :] * (QK_SCALE * LOG2E)).astype(bf16)
    q_sc[B_WIDTH:, :] = (pt_sc[T_CQ:T_CK, :] * (QK_SCALE * LOG2E)).astype(bf16)


def _finish(x, yT_sc, wout_ref, gpost_ref):
    out = lax.dot_general(yT_sc[...], wout_ref[...], _TN, preferred_element_type=f32)
    return x + _rms_rows(out, gpost_ref[...])


def _layer_spec(arr, layer):
    shape = arr.shape[1:]
    return pl.BlockSpec((None,) + shape, lambda *_: (layer,) + (0,) * len(shape),
                        pipeline_mode=pl.Buffered(1))


def _const_spec(arr):
    return pl.BlockSpec(arr.shape, lambda *_: (0,) * arr.ndim, pipeline_mode=pl.Buffered(1))


def _prompt_kernel(x_ref, gpre_ref, gpost_ref, wT_ref, wout_ref,
                   ang_ref, awsT_ref, abs_ref, sink_ref, biasb_ref, biasc_ref,
                   bk_all, bv_all, ck_all, cv_all,
                   xo_ref, bk_ref, bv_ref, ck_ref, cv_ref,
                   pt_sc, kb_ring, kc_ring, vbT_ring, vcT_ring, yT_sc, q_sc,
                   sb_even, mb_even, sc_even, mc_even, sb_odd, mb_odd, sc_odd, mc_odd):
    t = pl.program_id(1)
    ts = SEQ_TILE

    @pl.when(t == 0)
    def _():
        kb_ring[0:B_KEEP, :] = jnp.zeros((B_KEEP, B_KVW), bf16)
        kc_ring[0:C_KEEP, :] = jnp.zeros((C_KEEP, C_WIDTH), bf16)
        vbT_ring[:, 0:B_KEEP] = jnp.zeros((B_KV * VB_STRIDE, B_KEEP), bf16)
        vcT_ring[:, 0:C_KEEP] = jnp.zeros((C_HEADS // 2 * VC_STRIDE, C_KEEP), bf16)
        for g in range(B_KV):
            vbT_ring[g * VB_STRIDE + HEAD_DIM:(g + 1) * VB_STRIDE, :] = jnp.ones((ONES_ROWS, B_KEEP + ts), bf16)
        for g in range(C_HEADS // 2):
            vcT_ring[g * VC_STRIDE + 2 * HEAD_DIM:(g + 1) * VC_STRIDE, :] = jnp.ones((ONES_ROWS, C_KEEP + ts), bf16)

    x = x_ref[0]
    h = _rms_rows(x, gpre_ref[...]).astype(bf16)
    _project_T(pt_sc, wT_ref, h)
    kb_ring[B_KEEP:, :] = lax.dot_general(h, wT_ref[T_BK:T_BV, :], _NT,
                                          preferred_element_type=f32).astype(bf16)
    kc_ring[C_KEEP:, :] = lax.dot_general(h, wT_ref[T_CK:T_CV, :], _NT,
                                          preferred_element_type=f32).astype(bf16)
    for g in range(B_KV):
        vbT_ring[g * VB_STRIDE:g * VB_STRIDE + HEAD_DIM, B_KEEP:] = (
            pt_sc[T_BV + g * HEAD_DIM:T_BV + (g + 1) * HEAD_DIM, :].astype(bf16))
    for g in range(C_HEADS // 2):
        vcT_ring[g * VC_STRIDE:g * VC_STRIDE + 2 * HEAD_DIM, C_KEEP:] = (
            pt_sc[T_CV + g * 2 * HEAD_DIM:T_CV + (g + 1) * 2 * HEAD_DIM, :].astype(bf16))

    _prepare_rows(pt_sc, q_sc)

    vn = _rms_cols(pt_sc[T_AV:T_AV + A_WIDTH, :], ang_ref[...]).astype(bf16)
    n_ac = ts // A_CHUNK
    row = lax.broadcasted_iota(jnp.int32, (A_CHUNK, A_CHUNK), 0)
    col = lax.broadcasted_iota(jnp.int32, (A_CHUNK, A_CHUNK), 1)
    for g in range(A_GROUPS):
        rows = slice(g * A_GD, (g + 1) * A_GD)
        wgt = jnp.where(row <= col, awsT_ref[g], 0.0).astype(bf16)
        lhs = jnp.concatenate([vn[rows, c * A_CHUNK:(c + 1) * A_CHUNK] for c in range(n_ac)], axis=0)
        mix = jnp.dot(lhs, wgt, preferred_element_type=f32) + abs_ref[g]
        for c in range(n_ac):
            cols = slice(c * A_CHUNK, (c + 1) * A_CHUNK)
            u = pt_sc[T_AU + g * A_GD:T_AU + (g + 1) * A_GD, cols]
            z = pt_sc[T_AZ + g * A_GD:T_AZ + (g + 1) * A_GD, cols]
            yT_sc[Y_A + g * A_GD:Y_A + (g + 1) * A_GD, cols] = (
                u * mix[c * A_GD:(c + 1) * A_GD, :] * jax.nn.silu(z)).astype(bf16)

    n_pairs = ts // PAIR

    even, odd = (sb_even, mb_even, sc_even, mc_even), (sb_odd, mb_odd, sc_odd, mc_odd)

    def scores_into(p, bufs):
        sb_ref, mb_ref, sc_ref, mc_ref = bufs
        pg = t * n_pairs + p
        start = p * PAIR if isinstance(p, int) else pl.multiple_of(p * PAIR, PAIR)
        s_b, s_c = _scores(q_sc, pl.ds(start, PAIR),
                           kb_ring[pl.ds(start, B_WIN), :], biasb_ref.at[jnp.minimum(pg, 1)],
                           kc_ring[pl.ds(start, C_WIN), :], biasc_ref.at[jnp.minimum(pg, C_PREV // 2)])
        for cb, (v, m) in enumerate(s_b):
            sb_ref[cb] = v
            mb_ref[cb] = m
        for cb, (v, m) in enumerate(s_c):
            sc_ref[cb] = v
            mc_ref[cb] = m

    def attend_from(p, bufs):
        sb_ref, mb_ref, sc_ref, mc_ref = bufs
        start = p * PAIR if isinstance(p, int) else pl.multiple_of(p * PAIR, PAIR)
        _attend(pt_sc, yT_sc, pl.ds(start, PAIR),
                [(sb_ref[cb], mb_ref[cb]) for cb in range(B_HEADS // 2)],
                [(sc_ref[cb], mc_ref[cb]) for cb in range(C_HEADS // 2)],
                vbT_ring[:, pl.ds(start, B_WIN)], vcT_ring[:, pl.ds(start, C_WIN)], sink_ref)

    scores_into(0, even)

    def finish_rows(p):
        start = p * PAIR if isinstance(p, int) else pl.multiple_of(p * PAIR, PAIR)
        rows = pl.ds(start, PAIR)
        xo_ref[0, rows, :] = _finish(x_ref[0, rows, :], yT_sc.at[:, rows], wout_ref, gpost_ref)

    assert n_pairs == 4

    def first_pairs(q, carry):
        p = 2 * q
        scores_into(p + 1, odd)
        attend_from(p, even)
        scores_into(p + 2, even)
        finish_rows(p)
        attend_from(p + 1, odd)
        return carry

    def last_pairs(q, carry):
        p = 2 * q
        scores_into(p + 1, odd)
        finish_rows(p - 1)
        attend_from(p, even)
        finish_rows(p)
        attend_from(p + 1, odd)
        return carry

    n_full = jnp.minimum(t + n_pairs, n_pairs // 2 - 1)
    lax.fori_loop(0, n_full, first_pairs, 0)
    lax.fori_loop(n_full, n_full + 1, last_pairs, 0)
    finish_rows(n_pairs - 1)

    kb_ring[0:B_KEEP, :] = kb_ring[ts:ts + B_KEEP, :]
    kc_ring[0:C_KEEP, :] = kc_ring[ts:ts + C_KEEP, :]
    vbT_ring[:, 0:B_KEEP] = vbT_ring[:, ts:ts + B_KEEP]
    vcT_ring[:, 0:C_KEEP] = vcT_ring[:, ts:ts + C_KEEP]

    @pl.when(t == pl.num_programs(1) - 1)
    def _():
        h_last = _rms_rows(x_ref[0], gpre_ref[...]).astype(bf16)
        bk_ref[0] = lax.dot_general(wT_ref[T_BK:T_BV, :], h_last[ts - B_KEEP:, :], _NT,
                                    preferred_element_type=f32)
        ck_ref[0] = lax.dot_general(wT_ref[T_CK:T_CV, :], h_last[ts - C_KEEP:, :], _NT,
                                    preferred_element_type=f32)
        bv_ref[0] = pt_sc[T_BV:T_BZ, ts - B_KEEP:]
        cv_ref[0] = pt_sc[T_CV:T_CZ, ts - C_KEEP:]


def _prompt_layer(layer, x, p, kv_all):
    nb, seq, d = x.shape
    ts = SEQ_TILE
    assert seq % ts == 0 and ts == C_KEEP
    kv_spec = lambda keep, width: pl.BlockSpec((None, 1, width, keep), lambda b, t: (layer, b, 0, 0))
    n_in = 11
    return pl.pallas_call(
        _prompt_kernel,
        grid=(nb, seq // ts),
        in_specs=[
            pl.BlockSpec((1, ts, d), lambda b, t: (b, t, 0)),
            _layer_spec(p["gpre"], layer), _layer_spec(p["gpost"], layer),
            _layer_spec(p["wT"], layer),
            _layer_spec(p["wout"], layer),
            _layer_spec(p["ang"], layer), _layer_spec(p["awsT"], layer), _layer_spec(p["abs"], layer),
            _layer_spec(p["sink"], layer), _const_spec(p["bias_pb"]), _layer_spec(p["bias_pc"], layer),
        ] + [pl.BlockSpec(memory_space=pl.ANY)] * len(kv_all),
        out_specs=[
            pl.BlockSpec((1, ts, d), lambda b, t: (b, t, 0)),
            kv_spec(B_KEEP, B_KVW), kv_spec(B_KEEP, B_KVW), kv_spec(C_KEEP, C_WIDTH), kv_spec(C_KEEP, C_WIDTH),
        ],
        out_shape=[jax.ShapeDtypeStruct((nb, seq, d), f32)]
        + [jax.ShapeDtypeStruct(a.shape, a.dtype) for a in kv_all],
        input_output_aliases={n_in + i: 1 + i for i in range(len(kv_all))},
        scratch_shapes=[
            pltpu.VMEM((T_END, ts), f32),
            pltpu.VMEM((B_KEEP + ts, B_KVW), bf16),
            pltpu.VMEM((C_KEEP + ts, C_WIDTH), bf16),
            pltpu.VMEM((B_KV * VB_STRIDE, B_KEEP + ts), bf16),
            pltpu.VMEM((C_HEADS // 2 * VC_STRIDE, C_KEEP + ts), bf16),
            pltpu.VMEM((A_WIDTH + B_WIDTH + C_WIDTH, ts), bf16),
            pltpu.VMEM((B_WIDTH + C_WIDTH, ts), bf16),
        ] + 2 * [
            pltpu.VMEM((B_HEADS // 2, B_WIN, 2 * LANE), f32), pltpu.VMEM((B_HEADS // 2, 1, 2 * LANE), f32),
            pltpu.VMEM((C_HEADS // 2, C_WIN, 2 * LANE), f32), pltpu.VMEM((C_HEADS // 2, 1, 2 * LANE), f32),
        ],
        compiler_params=pltpu.CompilerParams(dimension_semantics=("arbitrary", "arbitrary"),
                                             vmem_limit_bytes=VMEM_LIMIT),
        name="prompt_layer",
    )(x, p["gpre"], p["gpost"], p["wT"], p["wout"], p["ang"], p["awsT"],
      p["abs"], p["sink"], p["bias_pb"], p["bias_pc"], *kv_all)


def _sample_kernel(x_ref, cbk_ref, cbv_ref, cck_ref, ccv_ref, gpre_ref, gpost_ref, wT_ref,
                   wout_ref, ang_ref, aws_ref, arep_ref, akeep_ref, abias_ref,
                   sink_ref, biasb_hbm, biasc_hbm,
                   bk_all, bv_all, ck_all, cv_all, av_all,
                   xo_ref, bk_ref, bv_ref, ck_ref, cv_ref, av_ref,
                   pt_sc, yT_sc, q_sc, biasb_ref, biasc_ref, bias_sem, *, layer):
    ns = SAMPLE_STREAMS_PER_BLOCK
    bias_copies = (pltpu.make_async_copy(biasb_hbm, biasb_ref, bias_sem.at[0]),
                   pltpu.make_async_copy(biasc_hbm.at[layer], biasc_ref, bias_sem.at[1]))

    @pl.when(pl.program_id(0) == 0)
    def _():
        for cp in bias_copies:
            cp.start()

    x = x_ref[...]
    ntok = x.shape[0]
    h = _rms_rows(x, gpre_ref[...]).astype(bf16)
    _project_T(pt_sc, wT_ref, h)
    kvn_b = lax.dot_general(h, wT_ref[T_BK:T_BZ, :], _NT, preferred_element_type=f32)
    kvn_c = lax.dot_general(h, wT_ref[T_CK:T_CZ, :], _NT, preferred_element_type=f32)
    kn_c = kvn_c[:, 0:C_WIDTH]
    bk_ref[...] = kvn_b[:, 0:B_KVW]
    bv_ref[...] = kvn_b[:, B_KVW:]
    ck_ref[...] = kn_c
    cv_ref[...] = kvn_c[:, C_WIDTH:]

    vn = _rms_cols(pt_sc[T_AV:T_AV + A_WIDTH, :], ang_ref[...])
    av_ref[...] = vn.T
    vnb = vn.astype(bf16)
    keep = akeep_ref[...] > 0.0
    rep = arep_ref[...]
    for g in range(A_GROUPS):
        rows = slice(g * A_GD, (g + 1) * A_GD)
        tiled = lax.dot_general(jnp.dot(rep, aws_ref[g].astype(bf16), preferred_element_type=f32).astype(bf16),
                                rep, _NT, preferred_element_type=f32)
        wgt = jnp.where(keep, tiled, 0.0).astype(bf16)
        mix = jnp.dot(vnb[rows, :], wgt, preferred_element_type=f32) + abias_ref[g]
        u = pt_sc[T_AU + g * A_GD:T_AU + (g + 1) * A_GD, :]
        z = pt_sc[T_AZ + g * A_GD:T_AZ + (g + 1) * A_GD, :]
        yT_sc[Y_A + g * A_GD:Y_A + (g + 1) * A_GD, :] = (u * mix * jax.nn.silu(z)).astype(bf16)

    _prepare_rows(pt_sc, q_sc)

    @pl.when(pl.program_id(0) == 0)
    def _():
        for cp in bias_copies:
            cp.wait()

    for blk in range(ntok // LANE):
        cols = slice(blk * LANE, (blk + 1) * LANE)
        streams = range(blk * ns, (blk + 1) * ns)
        kwin_b = jnp.concatenate([cbk_ref[s].T.astype(bf16) for s in streams]
                                 + [kvn_b[cols, 0:B_KVW].astype(bf16)], axis=0)
        vwinT_b = jnp.concatenate([cbv_ref[s].astype(bf16) for s in streams]
                                  + [pt_sc[T_BV:T_BV + B_KVW, cols].astype(bf16)], axis=1)
        kwin_c = jnp.concatenate([cck_ref[s].T.astype(bf16) for s in streams]
                                 + [kn_c[cols, :].astype(bf16)], axis=0)
        vwinT_c = jnp.concatenate([ccv_ref[s].astype(bf16) for s in streams]
                                  + [pt_sc[T_CV:T_CV + C_WIDTH, cols].astype(bf16)], axis=1)
        s_b, s_c = _scores(q_sc, cols, kwin_b, biasb_ref, kwin_c, biasc_ref)
        _attend(pt_sc, yT_sc, cols, s_b, s_c, _with_ones(vwinT_b, HEAD_DIM), _with_ones(vwinT_c, 2 * HEAD_DIM),
                sink_ref)

    xo_ref[...] = _finish(x, yT_sc, wout_ref, gpost_ref)


def _sample_layer(layer, x, caches, p, outs_all, *, t_new, tok_per_step):
    ntok, d = x.shape
    spb = tok_per_step // t_new
    cbk, cbv, cck, ccv = caches
    n_in = 17

    def rows(width):
        return pl.BlockSpec((tok_per_step, width), lambda i: (i, 0))

    def layer_rows(a):
        return pl.BlockSpec((None, tok_per_step, a.shape[2]), lambda i: (layer, i, 0))

    def cache(c):
        return pl.BlockSpec((None, spb) + c.shape[2:], lambda i: (layer, i, 0, 0))

    return pl.pallas_call(
        functools.partial(_sample_kernel, layer=layer),
        grid=(ntok // tok_per_step,),
        in_specs=[
            rows(d), cache(cbk), cache(cbv), cache(cck), cache(ccv),
            _layer_spec(p["gpre"], layer), _layer_spec(p["gpost"], layer),
            _layer_spec(p["wT"], layer),
            _layer_spec(p["wout"], layer),
            _layer_spec(p["ang"], layer), _layer_spec(p["aws_s"], layer), _const_spec(p["arep"]),
            _const_spec(p["akeep"]), _layer_spec(p["abias_s"], layer),
            _layer_spec(p["sink"], layer), pl.BlockSpec(memory_space=pl.ANY), pl.BlockSpec(memory_space=pl.ANY),
        ] + [pl.BlockSpec(memory_space=pl.ANY)] * len(outs_all),
        out_specs=[rows(d)] + [layer_rows(a) for a in outs_all],
        out_shape=[jax.ShapeDtypeStruct((ntok, d), f32)]
        + [jax.ShapeDtypeStruct(a.shape, a.dtype) for a in outs_all],
        input_output_aliases={n_in + i: 1 + i for i in range(len(outs_all))},
        scratch_shapes=[
            pltpu.VMEM((T_END, tok_per_step), f32),
            pltpu.VMEM((A_WIDTH + B_WIDTH + C_WIDTH, tok_per_step), bf16),
            pltpu.VMEM((B_WIDTH + C_WIDTH, tok_per_step), bf16),
            pltpu.VMEM(p["bias_sb"].shape, f32), pltpu.VMEM(p["bias_sc"].shape[1:], f32),
            pltpu.SemaphoreType.DMA((2,)),
        ],
        compiler_params=pltpu.CompilerParams(dimension_semantics=("arbitrary",),
                                             vmem_limit_bytes=VMEM_LIMIT),
        name="sample_layer",
    )(x, cbk, cbv, cck, ccv, p["gpre"], p["gpost"], p["wT"], p["wout"],
      p["ang"], p["aws_s"], p["arep"], p["akeep"], p["abias_s"], p["sink"], p["bias_sb"], p["bias_sc"],
      *outs_all)


def kernel(x_prompt, x_sample, cache_b_k, cache_b_v, cache_c_k, cache_c_v, g_pre, g_post, w_in, w_out,
           a_norm_g, a_ws, a_bs, b_sinks, c_rel_bias, t5_bias):
    depth = w_in.shape[0]
    nb, seq, d = x_prompt.shape
    ns_all, t_new, _ = x_sample.shape
    lc_b, lc_c = cache_b_k.shape[2], cache_c_k.shape[2]
    assert lc_b == B_KEEP and lc_c == C_KEEP and LANE % t_new == 0
    assert SAMPLE_STREAMS_PER_BLOCK * t_new == LANE
    tok_per_step = 2 * LANE
    reps = tok_per_step // t_new

    rel_pb, ok_pb, rel_pc, ok_pc = _prompt_maps()
    rel_sb, ok_sb, rel_sc, ok_sc = _sample_maps(lc_b, lc_c, t_new)
    b_lo = int(min(rel_pb.min(), rel_sb.min()))
    b_hi = int(max(rel_pb.max(), rel_sb.max()))
    t5_rel = t5_bias[_t5_bucket(jnp.arange(b_lo, b_hi + 1))][None]
    c_lo = -C_CLIP - LANE + 1
    c_hi = int(max(rel_pc.max(), rel_sc.max()))
    assert c_hi < C_CLIP
    c_rel = jnp.concatenate([jnp.broadcast_to(c_rel_bias[:, :1], (depth, -C_CLIP - c_lo, C_HEADS)),
                             c_rel_bias[:, :c_hi + C_CLIP + 1]], axis=1)
    r_ = np.arange(tok_per_step)
    awsT = a_ws.transpose(0, 1, 3, 2)

    p = dict(
        bias_pb=_build_bias(t5_rel, b_lo, rel_pb, ok_pb, kmins=(B_PREV, 0))[0],
        bias_sb=_build_bias(t5_rel, b_lo, rel_sb, ok_sb, kmins=(0,))[0, 0],
        bias_pc=_build_bias(c_rel, c_lo, rel_pc, ok_pc, const_below=-C_CLIP,
                            kmins=tuple(max(C_PREV - 2 * v, 0) for v in range(C_PREV // 2 + 1))),
        bias_sc=_build_bias(c_rel, c_lo, rel_sc, ok_sc, const_below=-C_CLIP, kmins=(0,))[:, 0],
        wT=_transpose_weights(w_in),
        wout=w_out.astype(bf16),
        gpre=g_pre[:, None, :],
        gpost=g_post[:, None, :],
        ang=a_norm_g[:, :, None],
        awsT=awsT,
        abs=a_bs[:, :, None, :],
        sink=jnp.repeat(b_sinks, LANE, axis=1)[:, None, :],
        aws_s=awsT[:, :, :t_new, :t_new],
        arep=jnp.asarray((r_[:, None] % t_new == np.arange(t_new)[None, :]).astype(np.float32), bf16),
        abias_s=jnp.tile(a_bs[:, :, None, :t_new], (1, 1, 1, reps)),
        akeep=jnp.asarray(((r_[:, None] // t_new == r_[None, :] // t_new)
                           & (r_[:, None] % t_new <= r_[None, :] % t_new)).astype(np.float32)),
    )

    xp = x_prompt
    xs = x_sample.reshape(ns_all * t_new, d)
    caches = tuple(c.transpose(0, 1, 3, 4, 2).reshape(depth, ns_all, -1, c.shape[2])
                   for c in (cache_b_k, cache_b_v, cache_c_k, cache_c_v))

    kv_all = tuple(jnp.zeros((depth, nb, w, keep), f32)
                   for w, keep in ((B_KVW, B_KEEP), (B_KVW, B_KEEP), (C_WIDTH, C_KEEP), (C_WIDTH, C_KEEP)))
    s_all = tuple(jnp.zeros((depth, ns_all * t_new, w), f32)
                  for w in (B_KVW, B_KVW, C_WIDTH, C_WIDTH, A_WIDTH))
    for l in range(depth):
        xp, *kv_all = _prompt_layer(l, xp, p, kv_all)
        xs, *s_all = _sample_layer(l, xs, caches, p, s_all, t_new=t_new, tok_per_step=tok_per_step)

    def heads_of(a, heads):
        return a.reshape(depth, ns_all, t_new, heads, HEAD_DIM)

    def from_fm(a, heads):
        return a.reshape(depth, nb, heads, HEAD_DIM, a.shape[-1]).transpose(0, 1, 4, 2, 3)

    pk, pv, pck, pcv = kv_all
    sk, sv, sck, scv, sav = s_all
    return (xp, xs.reshape(ns_all, t_new, d),
            from_fm(pk, B_KV), from_fm(pv, B_KV), from_fm(pck, C_HEADS), from_fm(pcv, C_HEADS),
            heads_of(sk, B_KV), heads_of(sv, B_KV), heads_of(sck, C_HEADS), heads_of(scv, C_HEADS),
            sav.reshape(depth, ns_all, t_new, A_WIDTH))
```
